```python
import math
import jax, jax.numpy as jnp
from jax import lax
import numpy as np

D_MODEL = 2048
BATCH = 2
SEQ = 4096
DEPTH = 2

MEM_LEN = 256
N_HEADS = 12
MIX_WIDTH = 3 * D_MODEL // 4
HEAD_DIM = MIX_WIDTH // N_HEADS
DIFF_QK_DIM = HEAD_DIM // 2
N_KV_HEADS = 4
GQA_GROUP = N_HEADS // N_KV_HEADS
KV_WIDTH = N_KV_HEADS * HEAD_DIM
WINDOW = 128
BLOCK = 128
N_MEM_HEADS = 4
MEM_WIDTH = D_MODEL - MIX_WIDTH
MEM_HEAD_DIM = MEM_WIDTH // N_MEM_HEADS
D_FF = 4 * D_MODEL
N_BUCKETS = 32
MAX_DISTANCE = 128
RMS_EPS = 1e-6
NEG_INF = -1e30
N_LAYERS_A = (DEPTH + 1) // 2
N_LAYERS_B = DEPTH // 2
IN_A = 3 * MIX_WIDTH + MEM_WIDTH
IN_B = MIX_WIDTH + 2 * KV_WIDTH + MEM_WIDTH

kernel_name = "hybrid_diffattn_swa_sink_memxattn_encoder"


def rms_norm(x, gain):
    x32 = x.astype(jnp.float32)
    y = x32 * lax.rsqrt(jnp.mean(x32 * x32, axis=-1, keepdims=True) + RMS_EPS)
    return (y * gain.astype(jnp.float32)).astype(x.dtype)


def t5_bucket(rel):
    half = N_BUCKETS // 2
    max_exact = half // 2
    side = jnp.where(rel > 0, half, 0)
    n = jnp.abs(rel)
    n_f = jnp.maximum(n, 1).astype(jnp.float32)
    large = max_exact + (jnp.log(n_f / max_exact) / math.log(MAX_DISTANCE / max_exact)
                         * (half - max_exact)).astype(jnp.int32)
    large = jnp.minimum(large, half - 1)
    return side + jnp.where(n < max_exact, n, large)


def rel_bias_lookup(rel_bias, rel):
    return jnp.take(rel_bias, t5_bucket(rel), axis=0)


def diff_attention(q, k, v, positions, rel_bias, lam, scale):
    B, S = v.shape[:2]
    nb = S // BLOCK
    q_blocks = q.reshape(B, nb, BLOCK, N_HEADS, 2, DIFF_QK_DIM).transpose(1, 0, 2, 3, 4, 5)
    pos_blocks = positions.reshape(B, nb, BLOCK).transpose(1, 0, 2)

    def one_block(args):
        qb, pb = args
        s = jnp.einsum('bqhcd,bkhcd->bhcqk', qb, k).astype(jnp.float32) * scale
        bias = rel_bias_lookup(rel_bias, positions[:, None, :] - pb[:, :, None])
        s = s + jnp.transpose(bias, (0, 3, 1, 2))[:, :, None].astype(jnp.float32)
        p = jax.nn.softmax(s, axis=-1)
        a = p[:, :, 0] - lam * p[:, :, 1]
        return jnp.einsum('bhqk,bkhe->bqhe', a.astype(v.dtype), v)

    out = lax.map(one_block, (q_blocks, pos_blocks))
    return out.transpose(1, 0, 2, 3, 4).reshape(B, S, N_HEADS, HEAD_DIM)


def windowed_gqa_sink(q, k, v, positions, rel_bias, sink, scale):
    B, S = q.shape[:2]
    nb = S // BLOCK

    def neighbourhood(t):
        pad = [(0, 0), (BLOCK, BLOCK)] + [(0, 0)] * (t.ndim - 2)
        tp = jnp.pad(t, pad).reshape((B, nb + 2, BLOCK) + t.shape[2:])
        return jnp.concatenate([tp[:, :-2], tp[:, 1:-1], tp[:, 2:]], axis=2)

    kn, vn, pn = neighbourhood(k), neighbourhood(v), neighbourhood(positions)
    qb = q.reshape(B, nb, BLOCK, N_KV_HEADS, GQA_GROUP, HEAD_DIM)
    s = jnp.einsum('bnqkgd,bnckd->bnkgqc', qb, kn).astype(jnp.float32) * scale
    pq = positions.reshape(B, nb, BLOCK)
    bias = rel_bias_lookup(rel_bias, pn[:, :, None, :] - pq[:, :, :, None])
    bias = bias.reshape(B, nb, BLOCK, 3 * BLOCK, N_KV_HEADS, GQA_GROUP).transpose(0, 1, 4, 5, 2, 3)
    qi = jnp.arange(S).reshape(nb, BLOCK)
    ki = (jnp.arange(nb)[:, None] - 1) * BLOCK + jnp.arange(3 * BLOCK)[None, :]
    valid = ((jnp.abs(ki[:, None, :] - qi[:, :, None]) <= WINDOW)
             & (ki[:, None, :] >= 0) & (ki[:, None, :] < S))
    s = jnp.where(valid[None, :, None, None], s + bias.astype(jnp.float32), NEG_INF)
    sink_col = jnp.broadcast_to(
        sink.astype(jnp.float32).reshape(1, 1, N_KV_HEADS, GQA_GROUP, 1, 1), s.shape[:-1] + (1,))
    p = jax.nn.softmax(jnp.concatenate([s, sink_col], axis=-1), axis=-1)[..., :-1]
    out = jnp.einsum('bnkgqc,bnckd->bnqkgd', p.astype(v.dtype), vn)
    return out.reshape(B, S, N_HEADS, HEAD_DIM)


def memory_attention(q_m, k_m, v_m, scale):
    s = jnp.einsum('bshd,bmhd->bhsm', q_m, k_m).astype(jnp.float32) * scale
    p = jax.nn.softmax(s, axis=-1)
    return jnp.einsum('bhsm,bmhd->bshd', p.astype(v_m.dtype), v_m)


def setup_inputs(seed: int = 0) -> dict:
    key = jax.random.key(seed)
    ks = jax.random.split(key, 26)
    f32 = jnp.float32

    def w(k, shape, fan_in):
        return jax.random.normal(k, shape, f32) * fan_in ** -0.5

    def gain(k, shape):
        return 1.0 + 0.05 * jax.random.normal(k, shape, f32)

    x = jax.random.normal(ks[0], (BATCH, SEQ, D_MODEL), f32)
    mem = jax.random.normal(ks[1], (BATCH, MEM_LEN, D_MODEL), f32)
    offset = jax.random.randint(ks[2], (BATCH, 1), 0, 1024, dtype=jnp.int32)
    positions = offset + jnp.arange(SEQ, dtype=jnp.int32)[None, :]
    return {
        "x": x,
        "mem": mem,
        "positions": positions,
        "rel_bias": 0.5 * jax.random.normal(ks[3], (N_BUCKETS, N_HEADS), f32),
        "norm_attn": gain(ks[4], (DEPTH, D_MODEL)),
        "norm_mem": gain(ks[5], (DEPTH, D_MODEL)),
        "norm_mlp": gain(ks[6], (DEPTH, D_MODEL)),
        "w_in_a": w(ks[7], (N_LAYERS_A, D_MODEL, IN_A), D_MODEL),
        "a_q_norm": gain(ks[8], (N_LAYERS_A, DIFF_QK_DIM)),
        "a_k_norm": gain(ks[9], (N_LAYERS_A, DIFF_QK_DIM)),
        "a_lambda_q1": 0.1 * jax.random.normal(ks[10], (N_LAYERS_A, DIFF_QK_DIM), f32),
        "a_lambda_k1": 0.1 * jax.random.normal(ks[11], (N_LAYERS_A, DIFF_QK_DIM), f32),
        "a_lambda_q2": 0.1 * jax.random.normal(ks[12], (N_LAYERS_A, DIFF_QK_DIM), f32),
        "a_lambda_k2": 0.1 * jax.random.normal(ks[13], (N_LAYERS_A, DIFF_QK_DIM), f32),
        "a_subln": gain(ks[14], (N_LAYERS_A, HEAD_DIM)),
        "w_in_b": w(ks[15], (N_LAYERS_B, D_MODEL, IN_B), D_MODEL),
        "b_q_norm": gain(ks[16], (N_LAYERS_B, HEAD_DIM)),
        "b_k_norm": gain(ks[17], (N_LAYERS_B, HEAD_DIM)),
        "b_sink": 0.5 * jax.random.normal(ks[18], (N_LAYERS_B, N_HEADS), f32),
        "w_mem_kv": w(ks[19], (DEPTH, D_MODEL, 2 * MEM_WIDTH), D_MODEL),
        "m_q_norm": gain(ks[20], (DEPTH, MEM_HEAD_DIM)),
        "m_k_norm": gain(ks[21], (DEPTH, MEM_HEAD_DIM)),
        "w_out": w(ks[22], (DEPTH, MIX_WIDTH + MEM_WIDTH, D_MODEL), MIX_WIDTH + MEM_WIDTH),
        "w_up": w(ks[23], (DEPTH, D_MODEL, D_FF), D_MODEL),
        "w_down": w(ks[24], (DEPTH, D_FF, D_MODEL), D_FF),
    }


def reference(x, mem, positions, rel_bias, norm_attn, norm_mem, norm_mlp, w_in_a, a_q_norm, a_k_norm,
              a_lambda_q1, a_lambda_k1, a_lambda_q2, a_lambda_k2, a_subln, w_in_b, b_q_norm, b_k_norm,
              b_sink, w_mem_kv, m_q_norm, m_k_norm, w_out, w_up, w_down):
    B, S, _ = x.shape
    for i in range(DEPTH):
        h = rms_norm(x, norm_attn[i])
        mn = rms_norm(mem, norm_mem[i])
        mkv = (mn @ w_mem_kv[i]).reshape(B, MEM_LEN, 2, N_MEM_HEADS, MEM_HEAD_DIM)
        k_m = rms_norm(mkv[:, :, 0], m_k_norm[i])
        v_m = mkv[:, :, 1]
        if i % 2 == 0:
            j = i // 2
            proj = h @ w_in_a[j]
            q, k, v, q_m = jnp.split(proj, [MIX_WIDTH, 2 * MIX_WIDTH, 3 * MIX_WIDTH], axis=-1)
            q = rms_norm(q.reshape(B, S, N_HEADS, 2, DIFF_QK_DIM), a_q_norm[j])
            k = rms_norm(k.reshape(B, S, N_HEADS, 2, DIFF_QK_DIM), a_k_norm[j])
            v = v.reshape(B, S, N_HEADS, HEAD_DIM)
            lam_init = 0.8 - 0.6 * math.exp(-0.3 * i)
            lam = (jnp.exp(jnp.sum(a_lambda_q1[j].astype(jnp.float32) * a_lambda_k1[j].astype(jnp.float32)))
                   - jnp.exp(jnp.sum(a_lambda_q2[j].astype(jnp.float32) * a_lambda_k2[j].astype(jnp.float32)))
                   + lam_init)
            o = diff_attention(q, k, v, positions, rel_bias, lam, DIFF_QK_DIM ** -0.5)
            o = rms_norm(o, a_subln[j]) * (1.0 - lam_init)
        else:
            j = i // 2
            proj = h @ w_in_b[j]
            q, k, v, q_m = jnp.split(proj, [MIX_WIDTH, MIX_WIDTH + KV_WIDTH, MIX_WIDTH + 2 * KV_WIDTH], axis=-1)
            q = rms_norm(q.reshape(B, S, N_HEADS, HEAD_DIM), b_q_norm[j])
            k = rms_norm(k.reshape(B, S, N_KV_HEADS, HEAD_DIM), b_k_norm[j])
            v = v.reshape(B, S, N_KV_HEADS, HEAD_DIM)
            o = windowed_gqa_sink(q, k, v, positions, rel_bias, b_sink[j], HEAD_DIM ** -0.5)
        q_m = rms_norm(q_m.reshape(B, S, N_MEM_HEADS, MEM_HEAD_DIM), m_q_norm[i])
        o_m = memory_attention(q_m, k_m, v_m, MEM_HEAD_DIM ** -0.5)
        heads = jnp.concatenate([o.reshape(B, S, MIX_WIDTH), o_m.reshape(B, S, MEM_WIDTH)], axis=-1)
        x = x + heads @ w_out[i]
        h = rms_norm(x, norm_mlp[i])
        x = x + jnp.square(jax.nn.relu(h @ w_up[i])) @ w_down[i]
    return x
```

```python
import functools
import math

import jax
import jax.numpy as jnp
from jax import lax
from jax.experimental import pallas as pl
from jax.experimental.pallas import tpu as pltpu

F32 = jnp.float32
BF16 = jnp.bfloat16
I32 = jnp.int32

D_MODEL = 2048
N_HEADS = 12
HEAD_DIM = 128
DIFF_QK_DIM = 64
N_KV_HEADS = 4
GQA_GROUP = 3
MIX_WIDTH = N_HEADS * HEAD_DIM
KV_WIDTH = N_KV_HEADS * HEAD_DIM
WINDOW = 128
N_MEM_HEADS = 4
MEM_WIDTH = N_MEM_HEADS * HEAD_DIM
MEM_LEN = 256
D_FF = 4 * D_MODEL
N_BUCKETS = 32
MAX_DISTANCE = 128
RMS_EPS = 1e-6
NEG_BIG = -1e30
LOG2E = math.log2(math.e)

HALF_BUCKETS = N_BUCKETS // 2
MAX_EXACT = HALF_BUCKETS // 2
FAR_DIST = 91

LANES = 128
VMEM_LIMIT = 56 * 1024 * 1024

TM_PROJ, TN_PROJ = 1024, 512
TS_PREP = 512
TQ_A, TK_A = 512, 256
TQ_B = 256
KW_B = TQ_B + 2 * WINDOW
TQ_MEM = 512
TM_OUT, TN_OUT = 1024, 1024
TM_MLP, TF_MLP = 1024, 512


def _cparams(sem):
    return pltpu.CompilerParams(dimension_semantics=sem, vmem_limit_bytes=VMEM_LIMIT)


def _smem():
    return pl.BlockSpec(memory_space=pltpu.SMEM)


def _t5_bucket(rel):
    side = jnp.where(rel > 0, HALF_BUCKETS, 0)
    n = jnp.abs(rel)
    n_f = jnp.maximum(n, 1).astype(F32)
    large = MAX_EXACT + (jnp.log(n_f / MAX_EXACT) / math.log(MAX_DISTANCE / MAX_EXACT)
                         * (HALF_BUCKETS - MAX_EXACT)).astype(I32)
    large = jnp.minimum(large, HALF_BUCKETS - 1)
    return side + jnp.where(n < MAX_EXACT, n, large)


def _table_lookup(tbl_row, bucket):
    rows, cols = bucket.shape
    tb = jnp.broadcast_to(tbl_row, (rows, LANES))
    parts = [jnp.take_along_axis(tb, bucket[:, c:c + LANES], axis=1) for c in range(0, cols, LANES)]
    return parts[0] if len(parts) == 1 else jnp.concatenate(parts, axis=1)


def _group_rms_scale(x, group):
    t = x * x
    if group == LANES:
        return lax.rsqrt(jnp.mean(t, axis=-1, keepdims=True) + RMS_EPS)
    lane = lax.broadcasted_iota(I32, x.shape, 1)
    lo = lane < group
    s_lo = jnp.sum(jnp.where(lo, t, 0.0), axis=-1, keepdims=True)
    s_hi = jnp.sum(jnp.where(lo, 0.0, t), axis=-1, keepdims=True)
    return jnp.where(lo, lax.rsqrt(s_lo / group + RMS_EPS), lax.rsqrt(s_hi / group + RMS_EPS))


def _proj_kernel(x_ref, g_ref, w_ref, o_ref, h_ref):
    @pl.when(pl.program_id(1) == 0)
    def _():
        x = x_ref[...]
        r = lax.rsqrt(jnp.mean(x * x, axis=-1, keepdims=True) + RMS_EPS)
        h_ref[...] = ((x * r) * g_ref[...]).astype(BF16)

    o_ref[...] = jnp.dot(h_ref[...], w_ref[...], preferred_element_type=F32)


def _norm_proj(x2d, gain, w_bf16):
    m, d = x2d.shape
    n = w_bf16.shape[1]
    return pl.pallas_call(
        _proj_kernel,
        grid=(m // TM_PROJ, n // TN_PROJ),
        in_specs=[
            pl.BlockSpec((TM_PROJ, d), lambda i, j: (i, 0)),
            pl.BlockSpec((1, d), lambda i, j: (0, 0)),
            pl.BlockSpec((d, TN_PROJ), lambda i, j: (0, j)),
        ],
        out_specs=pl.BlockSpec((TM_PROJ, TN_PROJ), lambda i, j: (i, j)),
        out_shape=jax.ShapeDtypeStruct((m, n), F32),
        scratch_shapes=[pltpu.VMEM((TM_PROJ, d), BF16)],
        compiler_params=_cparams(("parallel", "arbitrary")),
        name="norm_proj",
    )(x2d, gain.reshape(1, d), w_bf16)


def _mem_kv_kernel(mem_ref, g_ref, w_ref, gk_ref, k_ref, v_ref):
    x = mem_ref[0]
    r = lax.rsqrt(jnp.mean(x * x, axis=-1, keepdims=True) + RMS_EPS)
    mn = ((x * r) * g_ref[...]).astype(BF16)
    mkv = jnp.dot(mn, w_ref[...], preferred_element_type=F32)
    for h in range(N_MEM_HEADS):
        kh = mkv[:, h * HEAD_DIM:(h + 1) * HEAD_DIM]
        k_ref[0, :, h * HEAD_DIM:(h + 1) * HEAD_DIM] = ((kh * _group_rms_scale(kh, HEAD_DIM)) * gk_ref[...]).astype(BF16)
    v_ref[0] = mkv[:, MEM_WIDTH:].astype(BF16)


def _mem_kv(mem, gain, w_bf16, gk):
    b = mem.shape[0]
    shp = jax.ShapeDtypeStruct((b, MEM_LEN, MEM_WIDTH), BF16)
    return pl.pallas_call(
        _mem_kv_kernel,
        grid=(b,),
        in_specs=[
            pl.BlockSpec((1, MEM_LEN, D_MODEL), lambda i: (i, 0, 0)),
            pl.BlockSpec((1, D_MODEL), lambda i: (0, 0)),
            pl.BlockSpec((D_MODEL, 2 * MEM_WIDTH), lambda i: (0, 0)),
            pl.BlockSpec((1, HEAD_DIM), lambda i: (0, 0)),
        ],
        out_specs=[pl.BlockSpec((1, MEM_LEN, MEM_WIDTH), lambda i: (i, 0, 0))] * 2,
        out_shape=[shp, shp],
        compiler_params=_cparams(("parallel",)),
        name="mem_kv",
    )(mem, gain.reshape(1, D_MODEL), w_bf16, gk.reshape(1, HEAD_DIM))


def _prep_a_kernel(q_ref, k_ref, v_ref, qm_ref, gq_ref, gk_ref, gm_ref, qt_ref, kn_ref, vt_ref, qmn_ref):
    q_scale = DIFF_QK_DIM ** -0.5 * LOG2E
    m_scale = HEAD_DIM ** -0.5 * LOG2E
    n_kt = TS_PREP // TK_A
    for h in range(N_HEADS):
        sl = slice(h * HEAD_DIM, (h + 1) * HEAD_DIM)
        q = q_ref[0, :, sl]
        qn = ((q * _group_rms_scale(q, DIFF_QK_DIM)) * gq_ref[...]) * q_scale
        qt_ref[0, h] = qn.T.astype(BF16)
        k = k_ref[0, :, sl]
        kn_ref[0, :, sl] = ((k * _group_rms_scale(k, DIFF_QK_DIM)) * gk_ref[...]).astype(BF16)
        vt = v_ref[0, :, sl].T.astype(BF16)
        for t in range(n_kt):
            vt_ref[0, h, t] = vt[:, t * TK_A:(t + 1) * TK_A]
    for h in range(N_MEM_HEADS):
        sl = slice(h * HEAD_DIM, (h + 1) * HEAD_DIM)
        qm = qm_ref[0, :, sl]
        qmn_ref[0, :, sl] = (((qm * _group_rms_scale(qm, HEAD_DIM)) * gm_ref[...]) * m_scale).astype(BF16)


def _prep_a(proj, gq, gk, gm):
    b, s, _ = proj.shape
    n_kt = TS_PREP // TK_A
    gq2 = jnp.concatenate([gq, gq]).reshape(1, HEAD_DIM)
    gk2 = jnp.concatenate([gk, gk]).reshape(1, HEAD_DIM)
    wblk = MIX_WIDTH // MEM_WIDTH
    return pl.pallas_call(
        _prep_a_kernel,
        grid=(b, s // TS_PREP),
        in_specs=[
            pl.BlockSpec((1, TS_PREP, MIX_WIDTH), lambda i, j: (i, j, 0)),
            pl.BlockSpec((1, TS_PREP, MIX_WIDTH), lambda i, j: (i, j, 1)),
            pl.BlockSpec((1, TS_PREP, MIX_WIDTH), lambda i, j: (i, j, 2)),
            pl.BlockSpec((1, TS_PREP, MEM_WIDTH), lambda i, j: (i, j, 3 * wblk)),
            pl.BlockSpec((1, HEAD_DIM), lambda i, j: (0, 0)),
            pl.BlockSpec((1, HEAD_DIM), lambda i, j: (0, 0)),
            pl.BlockSpec((1, HEAD_DIM), lambda i, j: (0, 0)),
        ],
        out_specs=[
            pl.BlockSpec((1, N_HEADS, HEAD_DIM, TS_PREP), lambda i, j: (i, 0, 0, j)),
            pl.BlockSpec((1, TS_PREP, MIX_WIDTH), lambda i, j: (i, j, 0)),
            pl.BlockSpec((1, N_HEADS, n_kt, HEAD_DIM, TK_A), lambda i, j: (i, 0, j, 0, 0)),
            pl.BlockSpec((1, TS_PREP, MEM_WIDTH), lambda i, j: (i, j, 0)),
        ],
        out_shape=[
            jax.ShapeDtypeStruct((b, N_HEADS, HEAD_DIM, s), BF16),
            jax.ShapeDtypeStruct((b, s, MIX_WIDTH), BF16),
            jax.ShapeDtypeStruct((b, N_HEADS, s // TK_A, HEAD_DIM, TK_A), BF16),
            jax.ShapeDtypeStruct((b, s, MEM_WIDTH), BF16),
        ],
        compiler_params=_cparams(("parallel", "parallel")),
        name="prep_a",
    )(proj, proj, proj, proj, gq2, gk2, gm.reshape(1, HEAD_DIM))


def _diff_attn_kernel(qlo_ref, qhi_ref, klo_ref, khi_ref,
                      far_ref, lq1_ref, lk1_ref, lq2_ref, lk2_ref,
                      qt_ref, k_ref, vt_ref, posq_ref, posk_ref, tbl_ref, gsub_ref,
                      o_ref, bucket_ref, m_ref, l_ref, acc_ref, *, lam_init):
    b, iq, h = pl.program_id(0), pl.program_id(1), pl.program_id(2)
    n_kt = k_ref.shape[1] // TK_A
    q_lo, q_hi = qlo_ref[b, iq], qhi_ref[b, iq]

    def tile_is_pos_far(j):
        return klo_ref[b, j] - q_hi >= FAR_DIST

    def tile_is_neg_far(j):
        return khi_ref[b, j] - q_lo <= -FAR_DIST

    @pl.when(h == 0)
    def _():
        def fill(j, c):
            @pl.when(jnp.logical_not(jnp.logical_or(tile_is_pos_far(j), tile_is_neg_far(j))))
            def _():
                pk = posk_ref[0, pl.ds(pl.multiple_of(j * TK_A, TK_A), TK_A), :]
                bucket_ref[j] = _t5_bucket(pk - posq_ref[0])
            return c
        lax.fori_loop(0, n_kt, fill, 0)

    qt = qt_ref[0, 0].astype(F32)
    row = lax.broadcasted_iota(I32, qt.shape, 0)
    q_half = (jnp.where(row < DIFF_QK_DIM, qt, 0.0).astype(BF16),
              jnp.where(row < DIFF_QK_DIM, 0.0, qt).astype(BF16))

    m_ref[...] = jnp.full(m_ref.shape, NEG_BIG, F32)
    l_ref[...] = jnp.zeros(l_ref.shape, F32)
    acc_ref[...] = jnp.zeros(acc_ref.shape, F32)

    def update(j, c, s, bias, off):
        z = s if bias is None else s + bias
        m_old = m_ref[c]
        m_new = jnp.maximum(m_old, jnp.max(z, axis=0, keepdims=True) + off)
        p = jnp.exp2(z - (m_new - off))
        alpha = jnp.exp2(m_old - m_new)
        l_ref[c] = alpha * l_ref[c] + jnp.sum(p, axis=0, keepdims=True)
        acc_ref[c] = alpha * acc_ref[c] + jnp.dot(vt_ref[0, 0, j], p.astype(BF16), preferred_element_type=F32)
        m_ref[c] = m_new

    def kv_step(j, carry):
        k_t = k_ref[0, pl.ds(pl.multiple_of(j * TK_A, TK_A), TK_A), :]
        pos_far, neg_far = tile_is_pos_far(j), tile_is_neg_far(j)
        far = jnp.logical_or(pos_far, neg_far)

        @pl.when(far)
        def _():
            off = jnp.where(pos_far, far_ref[1, h], far_ref[0, h])
            for c in range(2):
                update(j, c, jnp.dot(k_t, q_half[c], preferred_element_type=F32), None, off)

        @pl.when(jnp.logical_not(far))
        def _():
            bias = _table_lookup(tbl_ref[0], bucket_ref[j])
            for c in range(2):
                update(j, c, jnp.dot(k_t, q_half[c], preferred_element_type=F32), bias, 0.0)
        return carry

    lax.fori_loop(0, n_kt, kv_step, 0)

    lam = (jnp.exp(jnp.sum(lq1_ref[...] * lk1_ref[...], axis=-1, keepdims=True))
           - jnp.exp(jnp.sum(lq2_ref[...] * lk2_ref[...], axis=-1, keepdims=True)) + lam_init)
    o_t = acc_ref[0] / l_ref[0] - lam * (acc_ref[1] / l_ref[1])
    r = lax.rsqrt(jnp.mean(o_t * o_t, axis=0, keepdims=True) + RMS_EPS)
    o_t = ((o_t * r) * gsub_ref[...]) * (1.0 - lam_init)
    o_ref[0] = o_t.T.astype(BF16)


def _diff_attention(qt, kn, vt, positions, tbl_log2, far_log2, lam_vecs, gsub, lam_init):
    b, s, _ = kn.shape
    nq, nk = s // TQ_A, s // TK_A
    pq = positions.reshape(b, nq, TQ_A)
    pk = positions.reshape(b, nk, TK_A)
    ranges = (pq.min(-1), pq.max(-1), pk.min(-1), pk.max(-1))
    grid_spec = pltpu.PrefetchScalarGridSpec(
        num_scalar_prefetch=4,
        grid=(b, nq, N_HEADS),
        in_specs=[
            _smem(),
            pl.BlockSpec((1, DIFF_QK_DIM), lambda i, j, h, *_: (0, 0)),
            pl.BlockSpec((1, DIFF_QK_DIM), lambda i, j, h, *_: (0, 0)),
            pl.BlockSpec((1, DIFF_QK_DIM), lambda i, j, h, *_: (0, 0)),
            pl.BlockSpec((1, DIFF_QK_DIM), lambda i, j, h, *_: (0, 0)),
            pl.BlockSpec((1, 1, HEAD_DIM, TQ_A), lambda i, j, h, *_: (i, h, 0, j)),
            pl.BlockSpec((1, s, HEAD_DIM), lambda i, j, h, *_: (i, 0, h)),
            pl.BlockSpec((1, 1, nk, HEAD_DIM, TK_A), lambda i, j, h, *_: (i, h, 0, 0, 0)),
            pl.BlockSpec((1, 1, TQ_A), lambda i, j, h, *_: (i, 0, j)),
            pl.BlockSpec((1, s, 1), lambda i, j, h, *_: (i, 0, 0)),
            pl.BlockSpec((1, 1, LANES), lambda i, j, h, *_: (h, 0, 0)),
            pl.BlockSpec((HEAD_DIM, 1), lambda i, j, h, *_: (0, 0)),
        ],
        out_specs=pl.BlockSpec((1, TQ_A, HEAD_DIM), lambda i, j, h, *_: (i, j, h)),
        scratch_shapes=[
            pltpu.VMEM((nk, TK_A, TQ_A), I32),
            pltpu.VMEM((2, 1, TQ_A), F32),
            pltpu.VMEM((2, 1, TQ_A), F32),
            pltpu.VMEM((2, HEAD_DIM, TQ_A), F32),
        ],
    )
    return pl.pallas_call(
        functools.partial(_diff_attn_kernel, lam_init=lam_init),
        grid_spec=grid_spec,
        out_shape=jax.ShapeDtypeStruct((b, s, MIX_WIDTH), BF16),
        compiler_params=_cparams(("parallel", "parallel", "arbitrary")),
        name="diff_attn",
    )(*ranges, far_log2, *[v.reshape(1, DIFF_QK_DIM) for v in lam_vecs],
      qt, kn, vt, positions.reshape(b, 1, s), positions.reshape(b, s, 1), tbl_log2, gsub.reshape(HEAD_DIM, 1))


def _prep_b_kernel(q_ref, k_ref, v_ref, qm_ref, gq_ref, gk_ref, gm_ref, qn_ref, kn_ref, vn_ref, qmn_ref):
    scale = HEAD_DIM ** -0.5 * LOG2E
    for h in range(N_HEADS):
        sl = slice(h * HEAD_DIM, (h + 1) * HEAD_DIM)
        q = q_ref[0, :, sl]
        qn_ref[0, :, sl] = (((q * _group_rms_scale(q, HEAD_DIM)) * gq_ref[...]) * scale).astype(BF16)
    for h in range(N_KV_HEADS):
        sl = slice(h * HEAD_DIM, (h + 1) * HEAD_DIM)
        k = k_ref[0, :, sl]
        kn_ref[0, :, sl] = ((k * _group_rms_scale(k, HEAD_DIM)) * gk_ref[...]).astype(BF16)
        qm = qm_ref[0, :, sl]
        qmn_ref[0, :, sl] = (((qm * _group_rms_scale(qm, HEAD_DIM)) * gm_ref[...]) * scale).astype(BF16)
    vn_ref[0] = v_ref[0].astype(BF16)


def _prep_b(proj, gq, gk, gm):
    b, s, _ = proj.shape
    kblk = MIX_WIDTH // KV_WIDTH
    g = lambda v: v.reshape(1, HEAD_DIM)
    return pl.pallas_call(
        _prep_b_kernel,
        grid=(b, s // TS_PREP),
        in_specs=[
            pl.BlockSpec((1, TS_PREP, MIX_WIDTH), lambda i, j: (i, j, 0)),
            pl.BlockSpec((1, TS_PREP, KV_WIDTH), lambda i, j: (i, j, kblk)),
            pl.BlockSpec((1, TS_PREP, KV_WIDTH), lambda i, j: (i, j, kblk + 1)),
            pl.BlockSpec((1, TS_PREP, MEM_WIDTH), lambda i, j: (i, j, kblk + 2)),
            pl.BlockSpec((1, HEAD_DIM), lambda i, j: (0, 0)),
            pl.BlockSpec((1, HEAD_DIM), lambda i, j: (0, 0)),
            pl.BlockSpec((1, HEAD_DIM), lambda i, j: (0, 0)),
        ],
        out_specs=[
            pl.BlockSpec((1, TS_PREP, MIX_WIDTH), lambda i, j: (i, j, 0)),
            pl.BlockSpec((1, TS_PREP, KV_WIDTH), lambda i, j: (i, j, 0)),
            pl.BlockSpec((1, TS_PREP, KV_WIDTH), lambda i, j: (i, j, 0)),
            pl.BlockSpec((1, TS_PREP, MEM_WIDTH), lambda i, j: (i, j, 0)),
        ],
        out_shape=[
            jax.ShapeDtypeStruct((b, s, MIX_WIDTH), BF16),
            jax.ShapeDtypeStruct((b, s, KV_WIDTH), BF16),
            jax.ShapeDtypeStruct((b, s, KV_WIDTH), BF16),
            jax.ShapeDtypeStruct((b, s, MEM_WIDTH), BF16),
        ],
        compiler_params=_cparams(("parallel", "parallel")),
        name="prep_b",
    )(proj, proj, proj, proj, g(gq), g(gk), g(gm))


def _win_attn_kernel(sink_ref, q_ref, k_ref, v_ref, posq_ref, posk_ref, tbl_ref, o_ref):
    iq = pl.program_id(1)
    s_len = k_ref.shape[1]
    q0 = iq * TQ_B
    ks = jnp.clip(q0 - WINDOW, 0, s_len - KW_B)
    ks = pl.multiple_of(ks, LANES)

    pk_rows = posk_ref[0, pl.ds(ks // LANES, KW_B // LANES), :]
    pk = jnp.concatenate([pk_rows[t:t + 1, :] for t in range(KW_B // LANES)], axis=1)
    bucket = _t5_bucket(pk - posq_ref[0])
    qi = q0 + lax.broadcasted_iota(I32, (TQ_B, KW_B), 0)
    ki = ks + lax.broadcasted_iota(I32, (TQ_B, KW_B), 1)
    valid = jnp.abs(ki - qi) <= WINDOW

    for kvh in range(N_KV_HEADS):
        ksl = slice(kvh * HEAD_DIM, (kvh + 1) * HEAD_DIM)
        k_w = k_ref[0, pl.ds(ks, KW_B), ksl]
        v_w = v_ref[0, pl.ds(ks, KW_B), ksl]
        for g in range(GQA_GROUP):
            h = kvh * GQA_GROUP + g
            hsl = slice(h * HEAD_DIM, (h + 1) * HEAD_DIM)
            s = lax.dot_general(q_ref[0, :, hsl], k_w, (((1,), (1,)), ((), ())), preferred_element_type=F32)
            z = jnp.where(valid, s + _table_lookup(tbl_ref[h], bucket), NEG_BIG)
            sink = sink_ref[h]
            m = jnp.maximum(jnp.max(z, axis=-1, keepdims=True), sink)
            e = jnp.exp2(z - m)
            denom = jnp.sum(e, axis=-1, keepdims=True) + jnp.exp2(sink - m)
            p = (e / denom).astype(BF16)
            o_ref[0, :, hsl] = jnp.dot(p, v_w, preferred_element_type=F32).astype(BF16)


def _win_attention(qn, kn, vn, positions, tbl_log2, sink_log2):
    b, s, _ = qn.shape
    return pl.pallas_call(
        _win_attn_kernel,
        grid=(b, s // TQ_B),
        in_specs=[
            _smem(),
            pl.BlockSpec((1, TQ_B, MIX_WIDTH), lambda i, j: (i, j, 0)),
            pl.BlockSpec((1, s, KV_WIDTH), lambda i, j: (i, 0, 0)),
            pl.BlockSpec((1, s, KV_WIDTH), lambda i, j: (i, 0, 0)),
            pl.BlockSpec((1, TQ_B, 1), lambda i, j: (i, j, 0)),
            pl.BlockSpec((1, s // LANES, LANES), lambda i, j: (i, 0, 0)),
            pl.BlockSpec((N_HEADS, 1, LANES), lambda i, j: (0, 0, 0)),
        ],
        out_specs=pl.BlockSpec((1, TQ_B, MIX_WIDTH), lambda i, j: (i, j, 0)),
        out_shape=jax.ShapeDtypeStruct((b, s, MIX_WIDTH), BF16),
        compiler_params=_cparams(("parallel", "parallel")),
        name="win_attn",
    )(sink_log2, qn, kn, vn, positions.reshape(b, s, 1), positions.reshape(b, s // LANES, LANES), tbl_log2)


def _mem_attn_kernel(q_ref, k_ref, v_ref, o_ref):
    for h in range(N_MEM_HEADS):
        sl = slice(h * HEAD_DIM, (h + 1) * HEAD_DIM)
        s = lax.dot_general(q_ref[0, :, sl], k_ref[0, :, sl], (((1,), (1,)), ((), ())), preferred_element_type=F32)
        e = jnp.exp2(s - jnp.max(s, axis=-1, keepdims=True))
        p = (e / jnp.sum(e, axis=-1, keepdims=True)).astype(BF16)
        o_ref[0, :, sl] = jnp.dot(p, v_ref[0, :, sl], preferred_element_type=F32).astype(BF16)


def _mem_attention(qmn, k_m, v_m):
    b, s, _ = qmn.shape
    return pl.pallas_call(
        _mem_attn_kernel,
        grid=(b, s // TQ_MEM),
        in_specs=[
            pl.BlockSpec((1, TQ_MEM, MEM_WIDTH), lambda i, j: (i, j, 0)),
            pl.BlockSpec((1, MEM_LEN, MEM_WIDTH), lambda i, j: (i, 0, 0)),
            pl.BlockSpec((1, MEM_LEN, MEM_WIDTH), lambda i, j: (i, 0, 0)),
        ],
        out_specs=pl.BlockSpec((1, TQ_MEM, MEM_WIDTH), lambda i, j: (i, j, 0)),
        out_shape=jax.ShapeDtypeStruct((b, s, MEM_WIDTH), BF16),
        compiler_params=_cparams(("parallel", "parallel")),
        name="mem_attn",
    )(qmn, k_m, v_m)


def _out_proj_kernel(x_ref, o_ref, om_ref, wo_ref, wm_ref, y_ref):
    y_ref[...] = (x_ref[...]
                  + jnp.dot(o_ref[...], wo_ref[...], preferred_element_type=F32)
                  + jnp.dot(om_ref[...], wm_ref[...], preferred_element_type=F32))


def _out_proj(x2d, o2d, om2d, w_bf16):
    m, d = x2d.shape
    return pl.pallas_call(
        _out_proj_kernel,
        grid=(m // TM_OUT, d // TN_OUT),
        in_specs=[
            pl.BlockSpec((TM_OUT, TN_OUT), lambda i, j: (i, j)),
            pl.BlockSpec((TM_OUT, MIX_WIDTH), lambda i, j: (i, 0)),
            pl.BlockSpec((TM_OUT, MEM_WIDTH), lambda i, j: (i, 0)),
            pl.BlockSpec((MIX_WIDTH, TN_OUT), lambda i, j: (0, j)),
            pl.BlockSpec((MEM_WIDTH, TN_OUT), lambda i, j: (MIX_WIDTH // MEM_WIDTH, j)),
        ],
        out_specs=pl.BlockSpec((TM_OUT, TN_OUT), lambda i, j: (i, j)),
        out_shape=jax.ShapeDtypeStruct((m, d), F32),
        compiler_params=_cparams(("parallel", "parallel")),
        name="out_proj",
    )(x2d, o2d, om2d, w_bf16, w_bf16)


def _mlp_kernel(x_ref, g_ref, wu_ref, wd_ref, y_ref, h_ref):
    f = pl.program_id(1)

    @pl.when(f == 0)
    def _():
        x = x_ref[...]
        r = lax.rsqrt(jnp.mean(x * x, axis=-1, keepdims=True) + RMS_EPS)
        h_ref[...] = ((x * r) * g_ref[...]).astype(BF16)
        y_ref[...] = x

    u = jnp.maximum(jnp.dot(h_ref[...], wu_ref[...], preferred_element_type=F32), 0.0)
    y_ref[...] += jnp.dot((u * u).astype(BF16), wd_ref[...], preferred_element_type=F32)


def _mlp(x2d, gain, wu_bf16, wd_bf16):
    m, d = x2d.shape
    return pl.pallas_call(
        _mlp_kernel,
        grid=(m // TM_MLP, D_FF // TF_MLP),
        in_specs=[
            pl.BlockSpec((TM_MLP, d), lambda i, f: (i, 0)),
            pl.BlockSpec((1, d), lambda i, f: (0, 0)),
            pl.BlockSpec((d, TF_MLP), lambda i, f: (0, f)),
            pl.BlockSpec((TF_MLP, d), lambda i, f: (f, 0)),
        ],
        out_specs=pl.BlockSpec((TM_MLP, d), lambda i, f: (i, 0)),
        out_shape=jax.ShapeDtypeStruct((m, d), F32),
        scratch_shapes=[pltpu.VMEM((TM_MLP, d), BF16)],
        compiler_params=_cparams(("parallel", "arbitrary")),
        name="mlp",
    )(x2d, gain.reshape(1, d), wu_bf16, wd_bf16)


def _bias_tables(rel_bias):
    t = (rel_bias.astype(F32) * LOG2E).T
    tbl = jnp.zeros((N_HEADS, 1, LANES), F32).at[:, 0, :N_BUCKETS].set(t)
    far = jnp.stack([t[:, HALF_BUCKETS - 1], t[:, N_BUCKETS - 1]])
    return tbl, far


def kernel(x, mem, positions, rel_bias, norm_attn, norm_mem, norm_mlp, w_in_a, a_q_norm, a_k_norm, a_lambda_q1, a_lambda_k1, a_lambda_q2, a_lambda_k2, a_subln, w_in_b, b_q_norm, b_k_norm, b_sink, w_mem_kv, m_q_norm, m_k_norm, w_out, w_up, w_down):
    b, s, d = x.shape
    depth = norm_attn.shape[0]
    tbl_log2, far_log2 = _bias_tables(rel_bias)
    x2d = x.reshape(b * s, d)
    for i in range(depth):
        j = i // 2
        k_m, v_m = _mem_kv(mem, norm_mem[i], w_mem_kv[i].astype(BF16), m_k_norm[i])
        if i % 2 == 0:
            proj = _norm_proj(x2d, norm_attn[i], w_in_a[j].astype(BF16)).reshape(b, s, -1)
            qt, kn, vt, qmn = _prep_a(proj, a_q_norm[j], a_k_norm[j], m_q_norm[i])
            lam_init = 0.8 - 0.6 * math.exp(-0.3 * i)
            o = _diff_attention(qt, kn, vt, positions, tbl_log2, far_log2,
                                (a_lambda_q1[j], a_lambda_k1[j], a_lambda_q2[j], a_lambda_k2[j]), a_subln[j], lam_init)
        else:
            proj = _norm_proj(x2d, norm_attn[i], w_in_b[j].astype(BF16)).reshape(b, s, -1)
            qn, kn, vn, qmn = _prep_b(proj, b_q_norm[j], b_k_norm[j], m_q_norm[i])
            o = _win_attention(qn, kn, vn, positions, tbl_log2, b_sink[j].astype(F32) * LOG2E)
        o_m = _mem_attention(qmn, k_m, v_m)
        x2d = _out_proj(x2d, o.reshape(b * s, MIX_WIDTH), o_m.reshape(b * s, MEM_WIDTH), w_out[i].astype(BF16))
        x2d = _mlp(x2d, norm_mlp[i], w_up[i].astype(BF16), w_down[i].astype(BF16))
    return x2d.reshape(b, s, d)
```

```python
import functools
import math

import jax
import jax.numpy as jnp
from jax import lax
from jax.experimental import pallas as pl
from jax.experimental.pallas import tpu as pltpu

F32 = jnp.float32
BF16 = jnp.bfloat16
I32 = jnp.int32

D_MODEL = 2048
N_HEADS = 12
HEAD_DIM = 128
DIFF_QK_DIM = 64
N_KV_HEADS = 4
GQA_GROUP = 3
MIX_WIDTH = N_HEADS * HEAD_DIM
KV_WIDTH = N_KV_HEADS * HEAD_DIM
WINDOW = 128
N_MEM_HEADS = 4
MEM_WIDTH = N_MEM_HEADS * HEAD_DIM
MEM_LEN = 256
D_FF = 4 * D_MODEL
N_BUCKETS = 32
MAX_DISTANCE = 128
RMS_EPS = 1e-6
NEG_BIG = -1e30
LOG2E = math.log2(math.e)

HALF_BUCKETS = N_BUCKETS // 2
MAX_EXACT = HALF_BUCKETS // 2
FAR_DIST = 91

ONE_COL = DIFF_QK_DIM
V_ROWS = HEAD_DIM + 16
SHIFT_LIMIT = 50.0

LANES = 128
VMEM_LIMIT = 56 * 1024 * 1024

TM_PROJ, TN_PROJ = 1024, 512
TS_PREP = 512
TQ_A, TK_A = 512, 512
TQ_B = 256
KW_B = TQ_B + 2 * WINDOW
TQ_MEM = 512
TM_OUT, TN_OUT = 1024, 1024
TM_MLP, TF_MLP = 1024, 512


def _cparams(sem):
    return pltpu.CompilerParams(dimension_semantics=sem, vmem_limit_bytes=VMEM_LIMIT)


def _smem():
    return pl.BlockSpec(memory_space=pltpu.SMEM)


def _t5_bucket(rel):
    side = jnp.where(rel > 0, HALF_BUCKETS, 0)
    n = jnp.abs(rel)
    n_f = jnp.maximum(n, 1).astype(F32)
    large = MAX_EXACT + (jnp.log(n_f / MAX_EXACT) / math.log(MAX_DISTANCE / MAX_EXACT)
                         * (HALF_BUCKETS - MAX_EXACT)).astype(I32)
    large = jnp.minimum(large, HALF_BUCKETS - 1)
    return side + jnp.where(n < MAX_EXACT, n, large)


def _table_lookup(tbl_row, bucket):
    rows, cols = bucket.shape
    tb = jnp.broadcast_to(tbl_row, (rows, LANES))
    parts = [jnp.take_along_axis(tb, bucket[:, c:c + LANES], axis=1) for c in range(0, cols, LANES)]
    return parts[0] if len(parts) == 1 else jnp.concatenate(parts, axis=1)


def _group_rms_scale(x, group):
    t = x * x
    if group == LANES:
        return lax.rsqrt(jnp.mean(t, axis=-1, keepdims=True) + RMS_EPS)
    lane = lax.broadcasted_iota(I32, x.shape, 1)
    lo = lane < group
    s_lo = jnp.sum(jnp.where(lo, t, 0.0), axis=-1, keepdims=True)
    s_hi = jnp.sum(jnp.where(lo, 0.0, t), axis=-1, keepdims=True)
    return jnp.where(lo, lax.rsqrt(s_lo / group + RMS_EPS), lax.rsqrt(s_hi / group + RMS_EPS))


def _proj_kernel(x_ref, g_ref, w_ref, o_ref, h_ref):
    @pl.when(pl.program_id(1) == 0)
    def _():
        x = x_ref[...]
        r = lax.rsqrt(jnp.mean(x * x, axis=-1, keepdims=True) + RMS_EPS)
        h_ref[...] = ((x * r) * g_ref[...]).astype(BF16)

    o_ref[...] = jnp.dot(h_ref[...], w_ref[...], preferred_element_type=F32)


def _norm_proj(x2d, gain, w_bf16):
    m, d = x2d.shape
    n = w_bf16.shape[1]
    return pl.pallas_call(
        _proj_kernel,
        grid=(m // TM_PROJ, n // TN_PROJ),
        in_specs=[
            pl.BlockSpec((TM_PROJ, d), lambda i, j: (i, 0)),
            pl.BlockSpec((1, d), lambda i, j: (0, 0)),
            pl.BlockSpec((d, TN_PROJ), lambda i, j: (0, j)),
        ],
        out_specs=pl.BlockSpec((TM_PROJ, TN_PROJ), lambda i, j: (i, j)),
        out_shape=jax.ShapeDtypeStruct((m, n), F32),
        scratch_shapes=[pltpu.VMEM((TM_PROJ, d), BF16)],
        compiler_params=_cparams(("parallel", "arbitrary")),
        name="norm_proj",
    )(x2d, gain.reshape(1, d), w_bf16)


def _mem_kv_kernel(mem_ref, g_ref, w_ref, gk_ref, k_ref, v_ref):
    x = mem_ref[0]
    r = lax.rsqrt(jnp.mean(x * x, axis=-1, keepdims=True) + RMS_EPS)
    mn = ((x * r) * g_ref[...]).astype(BF16)
    mkv = jnp.dot(mn, w_ref[...], preferred_element_type=F32)
    for h in range(N_MEM_HEADS):
        kh = mkv[:, h * HEAD_DIM:(h + 1) * HEAD_DIM]
        k_ref[0, :, h * HEAD_DIM:(h + 1) * HEAD_DIM] = ((kh * _group_rms_scale(kh, HEAD_DIM)) * gk_ref[...]).astype(BF16)
    v_ref[0] = mkv[:, MEM_WIDTH:].astype(BF16)


def _mem_kv(mem, gain, w_bf16, gk):
    b = mem.shape[0]
    shp = jax.ShapeDtypeStruct((b, MEM_LEN, MEM_WIDTH), BF16)
    return pl.pallas_call(
        _mem_kv_kernel,
        grid=(b,),
        in_specs=[
            pl.BlockSpec((1, MEM_LEN, D_MODEL), lambda i: (i, 0, 0)),
            pl.BlockSpec((1, D_MODEL), lambda i: (0, 0)),
            pl.BlockSpec((D_MODEL, 2 * MEM_WIDTH), lambda i: (0, 0)),
            pl.BlockSpec((1, HEAD_DIM), lambda i: (0, 0)),
        ],
        out_specs=[pl.BlockSpec((1, MEM_LEN, MEM_WIDTH), lambda i: (i, 0, 0))] * 2,
        out_shape=[shp, shp],
        compiler_params=_cparams(("parallel",)),
        name="mem_kv",
    )(mem, gain.reshape(1, D_MODEL), w_bf16, gk.reshape(1, HEAD_DIM))


def _prep_a_kernel(negm_ref, q_ref, k_ref, v_ref, qm_ref, gq_ref, gk_ref, gm_ref, qx_ref, kx_ref, vtx_ref, qmn_ref):
    q_scale = DIFF_QK_DIM ** -0.5 * LOG2E
    m_scale = HEAD_DIM ** -0.5 * LOG2E
    lane = lax.broadcasted_iota(I32, (TS_PREP, HEAD_DIM), 1)
    neg_shift = negm_ref[0]

    def extend(x, extra):
        return jnp.where(lane < DIFF_QK_DIM, x, jnp.where(lane == ONE_COL, extra, 0.0))

    ones_rows = jnp.where(lax.broadcasted_iota(I32, (V_ROWS - HEAD_DIM, TS_PREP), 0) == 0, 1.0, 0.0).astype(BF16)
    for h in range(N_HEADS):
        sl = slice(h * HEAD_DIM, (h + 1) * HEAD_DIM)
        q = q_ref[0, :, sl]
        qn = ((q * _group_rms_scale(q, DIFF_QK_DIM)) * gq_ref[...]) * q_scale
        k = k_ref[0, :, sl]
        kn = (k * _group_rms_scale(k, DIFF_QK_DIM)) * gk_ref[...]
        for c in range(2):
            qc = qn if c == 0 else pltpu.roll(qn, DIFF_QK_DIM, axis=1)
            kc = kn if c == 0 else pltpu.roll(kn, DIFF_QK_DIM, axis=1)
            qx_ref[0, 2 * h + c] = extend(qc, neg_shift).T.astype(BF16)
            kx_ref[0, 2 * h + c] = extend(kc, 1.0).astype(BF16)
        vt = v_ref[0, :, sl].T.astype(BF16)
        vtx_ref[0, h, 0] = jnp.concatenate([vt, ones_rows], axis=0)
    for h in range(N_MEM_HEADS):
        sl = slice(h * HEAD_DIM, (h + 1) * HEAD_DIM)
        qm = qm_ref[0, :, sl]
        qmn_ref[0, :, sl] = (((qm * _group_rms_scale(qm, HEAD_DIM)) * gm_ref[...]) * m_scale).astype(BF16)


def _prep_a(proj, gq, gk, gm, neg_shift):
    assert TS_PREP == TK_A
    b, s, _ = proj.shape
    gq2 = jnp.concatenate([gq, gq]).reshape(1, HEAD_DIM)
    gk2 = jnp.concatenate([gk, gk]).reshape(1, HEAD_DIM)
    wblk = MIX_WIDTH // MEM_WIDTH
    return pl.pallas_call(
        _prep_a_kernel,
        grid=(b, s // TS_PREP),
        in_specs=[
            _smem(),
            pl.BlockSpec((1, TS_PREP, MIX_WIDTH), lambda i, j: (i, j, 0)),
            pl.BlockSpec((1, TS_PREP, MIX_WIDTH), lambda i, j: (i, j, 1)),
            pl.BlockSpec((1, TS_PREP, MIX_WIDTH), lambda i, j: (i, j, 2)),
            pl.BlockSpec((1, TS_PREP, MEM_WIDTH), lambda i, j: (i, j, 3 * wblk)),
            pl.BlockSpec((1, HEAD_DIM), lambda i, j: (0, 0)),
            pl.BlockSpec((1, HEAD_DIM), lambda i, j: (0, 0)),
            pl.BlockSpec((1, HEAD_DIM), lambda i, j: (0, 0)),
        ],
        out_specs=[
            pl.BlockSpec((1, 2 * N_HEADS, HEAD_DIM, TS_PREP), lambda i, j: (i, 0, 0, j)),
            pl.BlockSpec((1, 2 * N_HEADS, TS_PREP, HEAD_DIM), lambda i, j: (i, 0, j, 0)),
            pl.BlockSpec((1, N_HEADS, 1, V_ROWS, TK_A), lambda i, j: (i, 0, j, 0, 0)),
            pl.BlockSpec((1, TS_PREP, MEM_WIDTH), lambda i, j: (i, j, 0)),
        ],
        out_shape=[
            jax.ShapeDtypeStruct((b, 2 * N_HEADS, HEAD_DIM, s), BF16),
            jax.ShapeDtypeStruct((b, 2 * N_HEADS, s, HEAD_DIM), BF16),
            jax.ShapeDtypeStruct((b, N_HEADS, s // TK_A, V_ROWS, TK_A), BF16),
            jax.ShapeDtypeStruct((b, s, MEM_WIDTH), BF16),
        ],
        compiler_params=_cparams(("parallel", "parallel")),
        name="prep_a",
    )(neg_shift.reshape(1), proj, proj, proj, proj, gq2, gk2, gm.reshape(1, HEAD_DIM))


def _diff_attn_kernel(qlo_ref, qhi_ref, klo_ref, khi_ref,
                      far_ref, lq1_ref, lk1_ref, lq2_ref, lk2_ref,
                      qx_ref, kx_ref, vtx_ref, posq_ref, posk_ref, tbl_ref, gsub_ref,
                      o_ref, acc_ref, *m_scratch, lam_init, bounded):
    b, iq, j = pl.program_id(0), pl.program_id(1), pl.program_id(2)
    pos_far = klo_ref[b, j] - qhi_ref[b, iq] >= FAR_DIST
    neg_far = khi_ref[b, j] - qlo_ref[b, iq] <= -FAR_DIST
    far = jnp.logical_or(pos_far, neg_far)

    @pl.when(j == 0)
    def _():
        acc_ref[...] = jnp.zeros(acc_ref.shape, F32)
        if not bounded:
            m_scratch[0][...] = jnp.full(m_scratch[0].shape, NEG_BIG, F32)

    def accumulate(h, c, bias, far_const):
        hc = 2 * h + c
        s = jnp.dot(kx_ref[0, hc], qx_ref[0, hc], preferred_element_type=F32)
        if bias is not None:
            s = s + bias
        if bounded:
            pv = jnp.dot(vtx_ref[0, h, 0], jnp.exp2(s).astype(BF16), preferred_element_type=F32)
            acc_ref[hc] += pv if far_const is None else far_const * pv
        else:
            m_ref = m_scratch[0]
            off = 0.0 if far_const is None else far_const
            m_old = m_ref[hc]
            m_new = jnp.maximum(m_old, jnp.max(s, axis=0, keepdims=True) + off)
            p = jnp.exp2(s - (m_new - off)).astype(BF16)
            acc_ref[hc] = (jnp.exp2(m_old - m_new) * acc_ref[hc]
                           + jnp.dot(vtx_ref[0, h, 0], p, preferred_element_type=F32))
            m_ref[hc] = m_new

    @pl.when(far)
    def _():
        for h in range(N_HEADS):
            far_const = jnp.where(pos_far, far_ref[1, h], far_ref[0, h])
            for c in range(2):
                accumulate(h, c, None, far_const)

    @pl.when(jnp.logical_not(far))
    def _():
        bucket = _t5_bucket(posk_ref[0] - posq_ref[0])
        for h in range(N_HEADS):
            bias = _table_lookup(tbl_ref[h], bucket)
            for c in range(2):
                accumulate(h, c, bias, None)

    @pl.when(j == pl.num_programs(2) - 1)
    def _():
        lam = (jnp.exp(jnp.sum(lq1_ref[...] * lk1_ref[...], axis=-1, keepdims=True))
               - jnp.exp(jnp.sum(lq2_ref[...] * lk2_ref[...], axis=-1, keepdims=True)) + lam_init)
        for h in range(N_HEADS):
            a0, a1 = acc_ref[2 * h], acc_ref[2 * h + 1]
            o_t = (a0[:HEAD_DIM] / a0[HEAD_DIM:HEAD_DIM + 1]
                   - lam * (a1[:HEAD_DIM] / a1[HEAD_DIM:HEAD_DIM + 1]))
            r = lax.rsqrt(jnp.mean(o_t * o_t, axis=0, keepdims=True) + RMS_EPS)
            o_t = ((o_t * r) * gsub_ref[...]) * (1.0 - lam_init)
            o_ref[0, :, h * HEAD_DIM:(h + 1) * HEAD_DIM] = o_t.T.astype(BF16)


def _diff_attention(qx, kx, vtx, positions, tbl_log2, far_consts, lam_vecs, gsub, lam_init, bounded):
    b, _, s, _ = kx.shape
    nq, nk = s // TQ_A, s // TK_A
    pq = positions.reshape(b, nq, TQ_A)
    pk = positions.reshape(b, nk, TK_A)
    ranges = (pq.min(-1), pq.max(-1), pk.min(-1), pk.max(-1))
    lam_spec = pl.BlockSpec((1, DIFF_QK_DIM), lambda i, q, k, *_: (0, 0))
    scratch = [pltpu.VMEM((2 * N_HEADS, V_ROWS, TQ_A), F32)]
    if not bounded:
        scratch.append(pltpu.VMEM((2 * N_HEADS, 1, TQ_A), F32))
    grid_spec = pltpu.PrefetchScalarGridSpec(
        num_scalar_prefetch=4,
        grid=(b, nq, nk),
        in_specs=[
            _smem(), lam_spec, lam_spec, lam_spec, lam_spec,
            pl.BlockSpec((1, 2 * N_HEADS, HEAD_DIM, TQ_A), lambda i, q, k, *_: (i, 0, 0, q)),
            pl.BlockSpec((1, 2 * N_HEADS, TK_A, HEAD_DIM), lambda i, q, k, *_: (i, 0, k, 0)),
            pl.BlockSpec((1, N_HEADS, 1, V_ROWS, TK_A), lambda i, q, k, *_: (i, 0, k, 0, 0)),
            pl.BlockSpec((1, 1, TQ_A), lambda i, q, k, *_: (i, 0, q)),
            pl.BlockSpec((1, TK_A, 1), lambda i, q, k, *_: (i, k, 0)),
            pl.BlockSpec((N_HEADS, 1, LANES), lambda i, q, k, *_: (0, 0, 0)),
            pl.BlockSpec((HEAD_DIM, 1), lambda i, q, k, *_: (0, 0)),
        ],
        out_specs=pl.BlockSpec((1, TQ_A, MIX_WIDTH), lambda i, q, k, *_: (i, q, 0)),
        scratch_shapes=scratch,
    )
    return pl.pallas_call(
        functools.partial(_diff_attn_kernel, lam_init=lam_init, bounded=bounded),
        grid_spec=grid_spec,
        out_shape=jax.ShapeDtypeStruct((b, s, MIX_WIDTH), BF16),
        compiler_params=_cparams(("parallel", "parallel", "arbitrary")),
        name="diff_attn_bounded" if bounded else "diff_attn_running_max",
    )(*ranges, far_consts, *[v.reshape(1, DIFF_QK_DIM) for v in lam_vecs],
      qx, kx, vtx, positions.reshape(b, 1, s), positions.reshape(b, s, 1), tbl_log2, gsub.reshape(HEAD_DIM, 1))


def _prep_b_kernel(q_ref, k_ref, v_ref, qm_ref, gq_ref, gk_ref, gm_ref, qn_ref, kn_ref, vn_ref, qmn_ref):
    scale = HEAD_DIM ** -0.5 * LOG2E
    for h in range(N_HEADS):
        sl = slice(h * HEAD_DIM, (h + 1) * HEAD_DIM)
        q = q_ref[0, :, sl]
        qn_ref[0, :, sl] = (((q * _group_rms_scale(q, HEAD_DIM)) * gq_ref[...]) * scale).astype(BF16)
    for h in range(N_KV_HEADS):
        sl = slice(h * HEAD_DIM, (h + 1) * HEAD_DIM)
        k = k_ref[0, :, sl]
        kn_ref[0, :, sl] = ((k * _group_rms_scale(k, HEAD_DIM)) * gk_ref[...]).astype(BF16)
        qm = qm_ref[0, :, sl]
        qmn_ref[0, :, sl] = (((qm * _group_rms_scale(qm, HEAD_DIM)) * gm_ref[...]) * scale).astype(BF16)
    vn_ref[0] = v_ref[0].astype(BF16)


def _prep_b(proj, gq, gk, gm):
    b, s, _ = proj.shape
    kblk = MIX_WIDTH // KV_WIDTH
    g = lambda v: v.reshape(1, HEAD_DIM)
    return pl.pallas_call(
        _prep_b_kernel,
        grid=(b, s // TS_PREP),
        in_specs=[
            pl.BlockSpec((1, TS_PREP, MIX_WIDTH), lambda i, j: (i, j, 0)),
            pl.BlockSpec((1, TS_PREP, KV_WIDTH), lambda i, j: (i, j, kblk)),
            pl.BlockSpec((1, TS_PREP, KV_WIDTH), lambda i, j: (i, j, kblk + 1)),
            pl.BlockSpec((1, TS_PREP, MEM_WIDTH), lambda i, j: (i, j, kblk + 2)),
            pl.BlockSpec((1, HEAD_DIM), lambda i, j: (0, 0)),
            pl.BlockSpec((1, HEAD_DIM), lambda i, j: (0, 0)),
            pl.BlockSpec((1, HEAD_DIM), lambda i, j: (0, 0)),
        ],
        out_specs=[
            pl.BlockSpec((1, TS_PREP, MIX_WIDTH), lambda i, j: (i, j, 0)),
            pl.BlockSpec((1, TS_PREP, KV_WIDTH), lambda i, j: (i, j, 0)),
            pl.BlockSpec((1, TS_PREP, KV_WIDTH), lambda i, j: (i, j, 0)),
            pl.BlockSpec((1, TS_PREP, MEM_WIDTH), lambda i, j: (i, j, 0)),
        ],
        out_shape=[
            jax.ShapeDtypeStruct((b, s, MIX_WIDTH), BF16),
            jax.ShapeDtypeStruct((b, s, KV_WIDTH), BF16),
            jax.ShapeDtypeStruct((b, s, KV_WIDTH), BF16),
            jax.ShapeDtypeStruct((b, s, MEM_WIDTH), BF16),
        ],
        compiler_params=_cparams(("parallel", "parallel")),
        name="prep_b",
    )(proj, proj, proj, proj, g(gq), g(gk), g(gm))


def _win_attn_kernel(sink_ref, q_ref, k_ref, v_ref, posq_ref, posk_ref, tbl_ref, o_ref):
    iq = pl.program_id(1)
    s_len = k_ref.shape[1]
    q0 = iq * TQ_B
    ks = jnp.clip(q0 - WINDOW, 0, s_len - KW_B)
    ks = pl.multiple_of(ks, LANES)

    pk_rows = posk_ref[0, pl.ds(ks // LANES, KW_B // LANES), :]
    pk = jnp.concatenate([pk_rows[t:t + 1, :] for t in range(KW_B // LANES)], axis=1)
    bucket = _t5_bucket(pk - posq_ref[0])
    qi = q0 + lax.broadcasted_iota(I32, (TQ_B, KW_B), 0)
    ki = ks + lax.broadcasted_iota(I32, (TQ_B, KW_B), 1)
    valid = jnp.abs(ki - qi) <= WINDOW

    for kvh in range(N_KV_HEADS):
        ksl = slice(kvh * HEAD_DIM, (kvh + 1) * HEAD_DIM)
        k_w = k_ref[0, pl.ds(ks, KW_B), ksl]
        v_w = v_ref[0, pl.ds(ks, KW_B), ksl]
        for g in range(GQA_GROUP):
            h = kvh * GQA_GROUP + g
            hsl = slice(h * HEAD_DIM, (h + 1) * HEAD_DIM)
            s = lax.dot_general(q_ref[0, :, hsl], k_w, (((1,), (1,)), ((), ())), preferred_element_type=F32)
            z = jnp.where(valid, s + _table_lookup(tbl_ref[h], bucket), NEG_BIG)
            sink = sink_ref[h]
            m = jnp.maximum(jnp.max(z, axis=-1, keepdims=True), sink)
            e = jnp.exp2(z - m)
            denom = jnp.sum(e, axis=-1, keepdims=True) + jnp.exp2(sink - m)
            p = (e / denom).astype(BF16)
            o_ref[0, :, hsl] = jnp.dot(p, v_w, preferred_element_type=F32).astype(BF16)


def _win_attention(qn, kn, vn, positions, tbl_log2, sink_log2):
    b, s, _ = qn.shape
    return pl.pallas_call(
        _win_attn_kernel,
        grid=(b, s // TQ_B),
        in_specs=[
            _smem(),
            pl.BlockSpec((1, TQ_B, MIX_WIDTH), lambda i, j: (i, j, 0)),
            pl.BlockSpec((1, s, KV_WIDTH), lambda i, j: (i, 0, 0)),
            pl.BlockSpec((1, s, KV_WIDTH), lambda i, j: (i, 0, 0)),
            pl.BlockSpec((1, TQ_B, 1), lambda i, j: (i, j, 0)),
            pl.BlockSpec((1, s // LANES, LANES), lambda i, j: (i, 0, 0)),
            pl.BlockSpec((N_HEADS, 1, LANES), lambda i, j: (0, 0, 0)),
        ],
        out_specs=pl.BlockSpec((1, TQ_B, MIX_WIDTH), lambda i, j: (i, j, 0)),
        out_shape=jax.ShapeDtypeStruct((b, s, MIX_WIDTH), BF16),
        compiler_params=_cparams(("parallel", "parallel")),
        name="win_attn",
    )(sink_log2, qn, kn, vn, positions.reshape(b, s, 1), positions.reshape(b, s // LANES, LANES), tbl_log2)


def _mem_attn_kernel(q_ref, k_ref, v_ref, o_ref):
    for h in range(N_MEM_HEADS):
        sl = slice(h * HEAD_DIM, (h + 1) * HEAD_DIM)
        s = lax.dot_general(q_ref[0, :, sl], k_ref[0, :, sl], (((1,), (1,)), ((), ())), preferred_element_type=F32)
        e = jnp.exp2(s - jnp.max(s, axis=-1, keepdims=True))
        p = (e / jnp.sum(e, axis=-1, keepdims=True)).astype(BF16)
        o_ref[0, :, sl] = jnp.dot(p, v_ref[0, :, sl], preferred_element_type=F32).astype(BF16)


def _mem_attention(qmn, k_m, v_m):
    b, s, _ = qmn.shape
    return pl.pallas_call(
        _mem_attn_kernel,
        grid=(b, s // TQ_MEM),
        in_specs=[
            pl.BlockSpec((1, TQ_MEM, MEM_WIDTH), lambda i, j: (i, j, 0)),
            pl.BlockSpec((1, MEM_LEN, MEM_WIDTH), lambda i, j: (i, 0, 0)),
            pl.BlockSpec((1, MEM_LEN, MEM_WIDTH), lambda i, j: (i, 0, 0)),
        ],
        out_specs=pl.BlockSpec((1, TQ_MEM, MEM_WIDTH), lambda i, j: (i, j, 0)),
        out_shape=jax.ShapeDtypeStruct((b, s, MEM_WIDTH), BF16),
        compiler_params=_cparams(("parallel", "parallel")),
        name="mem_attn",
    )(qmn, k_m, v_m)


def _out_proj_kernel(x_ref, o_ref, om_ref, wo_ref, wm_ref, y_ref):
    y_ref[...] = (x_ref[...]
                  + jnp.dot(o_ref[...], wo_ref[...], preferred_element_type=F32)
                  + jnp.dot(om_ref[...], wm_ref[...], preferred_element_type=F32))


def _out_proj(x2d, o2d, om2d, w_bf16):
    m, d = x2d.shape
    return pl.pallas_call(
        _out_proj_kernel,
        grid=(m // TM_OUT, d // TN_OUT),
        in_specs=[
            pl.BlockSpec((TM_OUT, TN_OUT), lambda i, j: (i, j)),
            pl.BlockSpec((TM_OUT, MIX_WIDTH), lambda i, j: (i, 0)),
            pl.BlockSpec((TM_OUT, MEM_WIDTH), lambda i, j: (i, 0)),
            pl.BlockSpec((MIX_WIDTH, TN_OUT), lambda i, j: (0, j)),
            pl.BlockSpec((MEM_WIDTH, TN_OUT), lambda i, j: (MIX_WIDTH // MEM_WIDTH, j)),
        ],
        out_specs=pl.BlockSpec((TM_OUT, TN_OUT), lambda i, j: (i, j)),
        out_shape=jax.ShapeDtypeStruct((m, d), F32),
        compiler_params=_cparams(("parallel", "parallel")),
        name="out_proj",
    )(x2d, o2d, om2d, w_bf16, w_bf16)


def _mlp_kernel(x_ref, g_ref, wu_ref, wd_ref, y_ref, h_ref):
    f = pl.program_id(1)

    @pl.when(f == 0)
    def _():
        x = x_ref[...]
        r = lax.rsqrt(jnp.mean(x * x, axis=-1, keepdims=True) + RMS_EPS)
        h_ref[...] = ((x * r) * g_ref[...]).astype(BF16)
        y_ref[...] = x

    u = jnp.maximum(jnp.dot(h_ref[...], wu_ref[...], preferred_element_type=F32), 0.0)
    y_ref[...] += jnp.dot((u * u).astype(BF16), wd_ref[...], preferred_element_type=F32)


def _mlp(x2d, gain, wu_bf16, wd_bf16):
    m, d = x2d.shape
    return pl.pallas_call(
        _mlp_kernel,
        grid=(m // TM_MLP, D_FF // TF_MLP),
        in_specs=[
            pl.BlockSpec((TM_MLP, d), lambda i, f: (i, 0)),
            pl.BlockSpec((1, d), lambda i, f: (0, 0)),
            pl.BlockSpec((d, TF_MLP), lambda i, f: (0, f)),
            pl.BlockSpec((TF_MLP, d), lambda i, f: (f, 0)),
        ],
        out_specs=pl.BlockSpec((TM_MLP, d), lambda i, f: (i, 0)),
        out_shape=jax.ShapeDtypeStruct((m, d), F32),
        scratch_shapes=[pltpu.VMEM((TM_MLP, d), BF16)],
        compiler_params=_cparams(("parallel", "arbitrary")),
        name="mlp",
    )(x2d, gain.reshape(1, d), wu_bf16, wd_bf16)


def _bias_tables(rel_bias):
    t = (rel_bias.astype(F32) * LOG2E).T
    tbl = jnp.zeros((N_HEADS, 1, LANES), F32).at[:, 0, :N_BUCKETS].set(t)
    far = jnp.stack([t[:, HALF_BUCKETS - 1], t[:, N_BUCKETS - 1]])
    return tbl, far


def _diff_logit_shift(gq, gk, tbl_log2):
    qk = 1.01 * DIFF_QK_DIM * jnp.max(jnp.abs(gq)) * jnp.max(jnp.abs(gk)) * (DIFF_QK_DIM ** -0.5 * LOG2E)
    return jnp.ceil(qk + jnp.max(jnp.abs(tbl_log2))).astype(F32)


def kernel(x, mem, positions, rel_bias, norm_attn, norm_mem, norm_mlp, w_in_a, a_q_norm, a_k_norm, a_lambda_q1, a_lambda_k1, a_lambda_q2, a_lambda_k2, a_subln, w_in_b, b_q_norm, b_k_norm, b_sink, w_mem_kv, m_q_norm, m_k_norm, w_out, w_up, w_down):
    b, s, d = x.shape
    depth = norm_attn.shape[0]
    tbl_log2, far_log2 = _bias_tables(rel_bias)
    x2d = x.reshape(b * s, d)
    for i in range(depth):
        j = i // 2
        k_m, v_m = _mem_kv(mem, norm_mem[i], w_mem_kv[i].astype(BF16), m_k_norm[i])
        if i % 2 == 0:
            proj = _norm_proj(x2d, norm_attn[i], w_in_a[j].astype(BF16)).reshape(b, s, -1)
            shift = _diff_logit_shift(a_q_norm[j], a_k_norm[j], tbl_log2)
            bounded = shift <= SHIFT_LIMIT
            qx, kx, vtx, qmn = _prep_a(proj, a_q_norm[j], a_k_norm[j], m_q_norm[i], jnp.where(bounded, -shift, 0.0))
            lam_init = 0.8 - 0.6 * math.exp(-0.3 * i)
            attn = functools.partial(
                _diff_attention, qx, kx, vtx, positions, tbl_log2,
                lam_vecs=(a_lambda_q1[j], a_lambda_k1[j], a_lambda_q2[j], a_lambda_k2[j]), gsub=a_subln[j],
                lam_init=lam_init)
            o = lax.cond(bounded,
                         lambda: attn(far_consts=jnp.exp2(far_log2), bounded=True),
                         lambda: attn(far_consts=far_log2, bounded=False))
        else:
            proj = _norm_proj(x2d, norm_attn[i], w_in_b[j].astype(BF16)).reshape(b, s, -1)
            qn, kn, vn, qmn = _prep_b(proj, b_q_norm[j], b_k_norm[j], m_q_norm[i])
            o = _win_attention(qn, kn, vn, positions, tbl_log2, b_sink[j].astype(F32) * LOG2E)
        o_m = _mem_attention(qmn, k_m, v_m)
        x2d = _out_proj(x2d, o.reshape(b * s, MIX_WIDTH), o_m.reshape(b * s, MEM_WIDTH), w_out[i].astype(BF16))
        x2d = _mlp(x2d, norm_mlp[i], w_up[i].astype(BF16), w_down[i].astype(BF16))
    return x2d.reshape(b, s, d)
```

```python
import functools
import math

import jax
import jax.numpy as jnp
from jax import lax
from jax.experimental import pallas as pl
from jax.experimental.pallas import tpu as pltpu

F32 = jnp.float32
BF16 = jnp.bfloat16
I32 = jnp.int32

D_MODEL = 2048
N_HEADS = 12
HEAD_DIM = 128
DIFF_QK_DIM = 64
N_KV_HEADS = 4
GQA_GROUP = 3
MIX_WIDTH = N_HEADS * HEAD_DIM
KV_WIDTH = N_KV_HEADS * HEAD_DIM
WINDOW = 128
N_MEM_HEADS = 4
MEM_WIDTH = N_MEM_HEADS * HEAD_DIM
MEM_LEN = 256
D_FF = 4 * D_MODEL
N_BUCKETS = 32
MAX_DISTANCE = 128
RMS_EPS = 1e-6
NEG_BIG = -1e30
LOG2E = math.log2(math.e)

HALF_BUCKETS = N_BUCKETS // 2
MAX_EXACT = HALF_BUCKETS // 2
FAR_DIST = 91

ONE_COL = DIFF_QK_DIM
V_ROWS = HEAD_DIM + 16
SHIFT_LIMIT = 50.0

LANES = 128
VMEM_LIMIT = 56 * 1024 * 1024

TM_PROJ, TN_PROJ = 1024, 512
TS_PREP = 512
TQ_A, TK_A = 512, 512
TQ_B = 256
KW_B = TQ_B + 2 * WINDOW
TQ_MEM = 512
TM_OUT, TN_OUT = 1024, 1024
TM_MLP, TF_MLP = 1024, 512


def _cparams(sem):
    return pltpu.CompilerParams(dimension_semantics=sem, vmem_limit_bytes=VMEM_LIMIT)


def _smem():
    return pl.BlockSpec(memory_space=pltpu.SMEM)


def _t5_bucket(rel):
    side = jnp.where(rel > 0, HALF_BUCKETS, 0)
    n = jnp.abs(rel)
    n_f = jnp.maximum(n, 1).astype(F32)
    large = MAX_EXACT + (jnp.log(n_f / MAX_EXACT) / math.log(MAX_DISTANCE / MAX_EXACT)
                         * (HALF_BUCKETS - MAX_EXACT)).astype(I32)
    large = jnp.minimum(large, HALF_BUCKETS - 1)
    return side + jnp.where(n < MAX_EXACT, n, large)


def _table_lookup(tbl_row, bucket):
    rows, cols = bucket.shape
    tb = jnp.broadcast_to(tbl_row, (rows, LANES))
    parts = [jnp.take_along_axis(tb, bucket[:, c:c + LANES], axis=1) for c in range(0, cols, LANES)]
    return parts[0] if len(parts) == 1 else jnp.concatenate(parts, axis=1)


def _group_rms_scale(x, group):
    t = x * x
    if group == LANES:
        return lax.rsqrt(jnp.mean(t, axis=-1, keepdims=True) + RMS_EPS)
    lane = lax.broadcasted_iota(I32, x.shape, 1)
    lo = lane < group
    s_lo = jnp.sum(jnp.where(lo, t, 0.0), axis=-1, keepdims=True)
    s_hi = jnp.sum(jnp.where(lo, 0.0, t), axis=-1, keepdims=True)
    return jnp.where(lo, lax.rsqrt(s_lo / group + RMS_EPS), lax.rsqrt(s_hi / group + RMS_EPS))


def _proj_kernel(x_ref, g_ref, w_ref, o_ref, h_ref):
    @pl.when(pl.program_id(1) == 0)
    def _():
        x = x_ref[...]
        r = lax.rsqrt(jnp.mean(x * x, axis=-1, keepdims=True) + RMS_EPS)
        h_ref[...] = ((x * r) * g_ref[...]).astype(BF16)

    o_ref[...] = jnp.dot(h_ref[...], w_ref[...], preferred_element_type=F32)


def _norm_proj(x2d, gain, w_bf16):
    m, d = x2d.shape
    n = w_bf16.shape[1]
    return pl.pallas_call(
        _proj_kernel,
        grid=(m // TM_PROJ, n // TN_PROJ),
        in_specs=[
            pl.BlockSpec((TM_PROJ, d), lambda i, j: (i, 0)),
            pl.BlockSpec((1, d), lambda i, j: (0, 0)),
            pl.BlockSpec((d, TN_PROJ), lambda i, j: (0, j)),
        ],
        out_specs=pl.BlockSpec((TM_PROJ, TN_PROJ), lambda i, j: (i, j)),
        out_shape=jax.ShapeDtypeStruct((m, n), F32),
        scratch_shapes=[pltpu.VMEM((TM_PROJ, d), BF16)],
        compiler_params=_cparams(("parallel", "arbitrary")),
        name="norm_proj",
    )(x2d, gain.reshape(1, d), w_bf16)


def _mem_kv_kernel(mem_ref, g_ref, w_ref, gk_ref, k_ref, v_ref):
    x = mem_ref[0]
    r = lax.rsqrt(jnp.mean(x * x, axis=-1, keepdims=True) + RMS_EPS)
    mn = ((x * r) * g_ref[...]).astype(BF16)
    mkv = jnp.dot(mn, w_ref[...], preferred_element_type=F32)
    for h in range(N_MEM_HEADS):
        kh = mkv[:, h * HEAD_DIM:(h + 1) * HEAD_DIM]
        k_ref[0, :, h * HEAD_DIM:(h + 1) * HEAD_DIM] = ((kh * _group_rms_scale(kh, HEAD_DIM)) * gk_ref[...]).astype(BF16)
    v_ref[0] = mkv[:, MEM_WIDTH:].astype(BF16)


def _mem_kv(mem, gain, w_bf16, gk):
    b = mem.shape[0]
    shp = jax.ShapeDtypeStruct((b, MEM_LEN, MEM_WIDTH), BF16)
    return pl.pallas_call(
        _mem_kv_kernel,
        grid=(b,),
        in_specs=[
            pl.BlockSpec((1, MEM_LEN, D_MODEL), lambda i: (i, 0, 0)),
            pl.BlockSpec((1, D_MODEL), lambda i: (0, 0)),
            pl.BlockSpec((D_MODEL, 2 * MEM_WIDTH), lambda i: (0, 0)),
            pl.BlockSpec((1, HEAD_DIM), lambda i: (0, 0)),
        ],
        out_specs=[pl.BlockSpec((1, MEM_LEN, MEM_WIDTH), lambda i: (i, 0, 0))] * 2,
        out_shape=[shp, shp],
        compiler_params=_cparams(("parallel",)),
        name="mem_kv",
    )(mem, gain.reshape(1, D_MODEL), w_bf16, gk.reshape(1, HEAD_DIM))


def _prep_a_kernel(negm_ref, q_ref, k_ref, v_ref, qm_ref, gq_ref, gk_ref, gm_ref, qx_ref, kx_ref, vtx_ref, qmn_ref):
    q_scale = DIFF_QK_DIM ** -0.5 * LOG2E
    m_scale = HEAD_DIM ** -0.5 * LOG2E
    lane = lax.broadcasted_iota(I32, (TS_PREP, HEAD_DIM), 1)
    neg_shift = negm_ref[0]

    def extend(x, extra):
        return jnp.where(lane < DIFF_QK_DIM, x, jnp.where(lane == ONE_COL, extra, 0.0))

    ones_rows = jnp.where(lax.broadcasted_iota(I32, (V_ROWS - HEAD_DIM, TS_PREP), 0) == 0, 1.0, 0.0).astype(BF16)
    for h in range(N_HEADS):
        sl = slice(h * HEAD_DIM, (h + 1) * HEAD_DIM)
        q = q_ref[0, :, sl]
        qn = ((q * _group_rms_scale(q, DIFF_QK_DIM)) * gq_ref[...]) * q_scale
        k = k_ref[0, :, sl]
        kn = (k * _group_rms_scale(k, DIFF_QK_DIM)) * gk_ref[...]
        for c in range(2):
            qc = qn if c == 0 else pltpu.roll(qn, DIFF_QK_DIM, axis=1)
            kc = kn if c == 0 else pltpu.roll(kn, DIFF_QK_DIM, axis=1)
            qx_ref[0, 2 * h + c] = extend(qc, neg_shift).T.astype(BF16)
            kx_ref[0, 2 * h + c] = extend(kc, 1.0).astype(BF16)
        vt = v_ref[0, :, sl].T.astype(BF16)
        vtx_ref[0, h, 0] = jnp.concatenate([vt, ones_rows], axis=0)
    for h in range(N_MEM_HEADS):
        sl = slice(h * HEAD_DIM, (h + 1) * HEAD_DIM)
        qm = qm_ref[0, :, sl]
        qmn_ref[0, :, sl] = (((qm * _group_rms_scale(qm, HEAD_DIM)) * gm_ref[...]) * m_scale).astype(BF16)


def _prep_a(proj, gq, gk, gm, neg_shift):
    assert TS_PREP == TK_A
    b, s, _ = proj.shape
    gq2 = jnp.concatenate([gq, gq]).reshape(1, HEAD_DIM)
    gk2 = jnp.concatenate([gk, gk]).reshape(1, HEAD_DIM)
    wblk = MIX_WIDTH // MEM_WIDTH
    return pl.pallas_call(
        _prep_a_kernel,
        grid=(b, s // TS_PREP),
        in_specs=[
            _smem(),
            pl.BlockSpec((1, TS_PREP, MIX_WIDTH), lambda i, j: (i, j, 0)),
            pl.BlockSpec((1, TS_PREP, MIX_WIDTH), lambda i, j: (i, j, 1)),
            pl.BlockSpec((1, TS_PREP, MIX_WIDTH), lambda i, j: (i, j, 2)),
            pl.BlockSpec((1, TS_PREP, MEM_WIDTH), lambda i, j: (i, j, 3 * wblk)),
            pl.BlockSpec((1, HEAD_DIM), lambda i, j: (0, 0)),
            pl.BlockSpec((1, HEAD_DIM), lambda i, j: (0, 0)),
            pl.BlockSpec((1, HEAD_DIM), lambda i, j: (0, 0)),
        ],
        out_specs=[
            pl.BlockSpec((1, 2 * N_HEADS, HEAD_DIM, TS_PREP), lambda i, j: (i, 0, 0, j)),
            pl.BlockSpec((1, 2 * N_HEADS, TS_PREP, HEAD_DIM), lambda i, j: (i, 0, j, 0)),
            pl.BlockSpec((1, N_HEADS, 1, V_ROWS, TK_A), lambda i, j: (i, 0, j, 0, 0)),
            pl.BlockSpec((1, TS_PREP, MEM_WIDTH), lambda i, j: (i, j, 0)),
        ],
        out_shape=[
            jax.ShapeDtypeStruct((b, 2 * N_HEADS, HEAD_DIM, s), BF16),
            jax.ShapeDtypeStruct((b, 2 * N_HEADS, s, HEAD_DIM), BF16),
            jax.ShapeDtypeStruct((b, N_HEADS, s // TK_A, V_ROWS, TK_A), BF16),
            jax.ShapeDtypeStruct((b, s, MEM_WIDTH), BF16),
        ],
        compiler_params=_cparams(("parallel", "parallel")),
        name="prep_a",
    )(neg_shift.reshape(1), proj, proj, proj, proj, gq2, gk2, gm.reshape(1, HEAD_DIM))


def _diff_attn_kernel(qlo_ref, qhi_ref, klo_ref, khi_ref,
                      far_ref, lq1_ref, lk1_ref, lq2_ref, lk2_ref,
                      qx_ref, kx_ref, vtx_ref, posq_ref, posk_ref, tbl_ref, gsub_ref,
                      o_ref, acc_ref, *m_scratch, lam_init, bounded):
    b, iq, j = pl.program_id(0), pl.program_id(1), pl.program_id(2)
    pos_far = klo_ref[b, j] - qhi_ref[b, iq] >= FAR_DIST
    neg_far = khi_ref[b, j] - qlo_ref[b, iq] <= -FAR_DIST
    far = jnp.logical_or(pos_far, neg_far)

    @pl.when(j == 0)
    def _():
        acc_ref[...] = jnp.zeros(acc_ref.shape, F32)
        if not bounded:
            m_scratch[0][...] = jnp.full(m_scratch[0].shape, NEG_BIG, F32)

    def logits(hc):
        return jnp.dot(kx_ref[0, hc], qx_ref[0, hc], preferred_element_type=F32)

    def accumulate(hc, s, far_const):
        h = hc // 2
        if bounded:
            pv = jnp.dot(vtx_ref[0, h, 0], jnp.exp2(s).astype(BF16), preferred_element_type=F32)
            acc_ref[hc] += pv if far_const is None else far_const * pv
        else:
            m_ref = m_scratch[0]
            off = 0.0 if far_const is None else far_const
            m_old = m_ref[hc]
            m_new = jnp.maximum(m_old, jnp.max(s, axis=0, keepdims=True) + off)
            p = jnp.exp2(s - (m_new - off)).astype(BF16)
            acc_ref[hc] = (jnp.exp2(m_old - m_new) * acc_ref[hc]
                           + jnp.dot(vtx_ref[0, h, 0], p, preferred_element_type=F32))
            m_ref[hc] = m_new

    def all_heads(bias_of_head, far_const_of_head):
        s_next = logits(0)
        bias_next = None if bias_of_head is None else bias_of_head(0)
        for hc in range(2 * N_HEADS):
            h, c = divmod(hc, 2)
            s = s_next
            if hc + 1 < 2 * N_HEADS:
                s_next = logits(hc + 1)
            if bias_of_head is not None:
                if c == 0:
                    bias = bias_next
                    if h + 1 < N_HEADS:
                        bias_next = bias_of_head(h + 1)
                s = s + bias
            accumulate(hc, s, None if far_const_of_head is None else far_const_of_head(h))

    @pl.when(far)
    def _():
        all_heads(None, lambda h: jnp.where(pos_far, far_ref[1, h], far_ref[0, h]))

    @pl.when(jnp.logical_not(far))
    def _():
        bucket = []

        def bias_of_head(h):
            if not bucket:
                bucket.append(_t5_bucket(posk_ref[0] - posq_ref[0]))
            return _table_lookup(tbl_ref[h], bucket[0])

        all_heads(bias_of_head, None)

    @pl.when(j == pl.num_programs(2) - 1)
    def _():
        lam = (jnp.exp(jnp.sum(lq1_ref[...] * lk1_ref[...], axis=-1, keepdims=True))
               - jnp.exp(jnp.sum(lq2_ref[...] * lk2_ref[...], axis=-1, keepdims=True)) + lam_init)
        for h in range(N_HEADS):
            a0, a1 = acc_ref[2 * h], acc_ref[2 * h + 1]
            o_t = (a0[:HEAD_DIM] / a0[HEAD_DIM:HEAD_DIM + 1]
                   - lam * (a1[:HEAD_DIM] / a1[HEAD_DIM:HEAD_DIM + 1]))
            r = lax.rsqrt(jnp.mean(o_t * o_t, axis=0, keepdims=True) + RMS_EPS)
            o_t = ((o_t * r) * gsub_ref[...]) * (1.0 - lam_init)
            o_ref[0, :, h * HEAD_DIM:(h + 1) * HEAD_DIM] = o_t.T.astype(BF16)


def _diff_attention(qx, kx, vtx, positions, tbl_log2, far_consts, lam_vecs, gsub, lam_init, bounded):
    b, _, s, _ = kx.shape
    nq, nk = s // TQ_A, s // TK_A
    pq = positions.reshape(b, nq, TQ_A)
    pk = positions.reshape(b, nk, TK_A)
    ranges = (pq.min(-1), pq.max(-1), pk.min(-1), pk.max(-1))
    lam_spec = pl.BlockSpec((1, DIFF_QK_DIM), lambda i, q, k, *_: (0, 0))
    scratch = [pltpu.VMEM((2 * N_HEADS, V_ROWS, TQ_A), F32)]
    if not bounded:
        scratch.append(pltpu.VMEM((2 * N_HEADS, 1, TQ_A), F32))
    grid_spec = pltpu.PrefetchScalarGridSpec(
        num_scalar_prefetch=4,
        grid=(b, nq, nk),
        in_specs=[
            _smem(), lam_spec, lam_spec, lam_spec, lam_spec,
            pl.BlockSpec((1, 2 * N_HEADS, HEAD_DIM, TQ_A), lambda i, q, k, *_: (i, 0, 0, q)),
            pl.BlockSpec((1, 2 * N_HEADS, TK_A, HEAD_DIM), lambda i, q, k, *_: (i, 0, k, 0)),
            pl.BlockSpec((1, N_HEADS, 1, V_ROWS, TK_A), lambda i, q, k, *_: (i, 0, k, 0, 0)),
            pl.BlockSpec((1, 1, TQ_A), lambda i, q, k, *_: (i, 0, q)),
            pl.BlockSpec((1, TK_A, 1), lambda i, q, k, *_: (i, k, 0)),
            pl.BlockSpec((N_HEADS, 1, LANES), lambda i, q, k, *_: (0, 0, 0)),
            pl.BlockSpec((HEAD_DIM, 1), lambda i, q, k, *_: (0, 0)),
        ],
        out_specs=pl.BlockSpec((1, TQ_A, MIX_WIDTH), lambda i, q, k, *_: (i, q, 0)),
        scratch_shapes=scratch,
    )
    return pl.pallas_call(
        functools.partial(_diff_attn_kernel, lam_init=lam_init, bounded=bounded),
        grid_spec=grid_spec,
        out_shape=jax.ShapeDtypeStruct((b, s, MIX_WIDTH), BF16),
        compiler_params=_cparams(("parallel", "parallel", "arbitrary")),
        name="diff_attn_bounded" if bounded else "diff_attn_running_max",
    )(*ranges, far_consts, *[v.reshape(1, DIFF_QK_DIM) for v in lam_vecs],
      qx, kx, vtx, positions.reshape(b, 1, s), positions.reshape(b, s, 1), tbl_log2, gsub.reshape(HEAD_DIM, 1))


def _prep_b_kernel(q_ref, k_ref, v_ref, qm_ref, gq_ref, gk_ref, gm_ref, qn_ref, kn_ref, vn_ref, qmn_ref):
    scale = HEAD_DIM ** -0.5 * LOG2E
    for h in range(N_HEADS):
        sl = slice(h * HEAD_DIM, (h + 1) * HEAD_DIM)
        q = q_ref[0, :, sl]
        qn_ref[0, :, sl] = (((q * _group_rms_scale(q, HEAD_DIM)) * gq_ref[...]) * scale).astype(BF16)
    for h in range(N_KV_HEADS):
        sl = slice(h * HEAD_DIM, (h + 1) * HEAD_DIM)
        k = k_ref[0, :, sl]
        kn_ref[0, :, sl] = ((k * _group_rms_scale(k, HEAD_DIM)) * gk_ref[...]).astype(BF16)
        qm = qm_ref[0, :, sl]
        qmn_ref[0, :, sl] = (((qm * _group_rms_scale(qm, HEAD_DIM)) * gm_ref[...]) * scale).astype(BF16)
    vn_ref[0] = v_ref[0].astype(BF16)


def _prep_b(proj, gq, gk, gm):
    b, s, _ = proj.shape
    kblk = MIX_WIDTH // KV_WIDTH
    g = lambda v: v.reshape(1, HEAD_DIM)
    return pl.pallas_call(
        _prep_b_kernel,
        grid=(b, s // TS_PREP),
        in_specs=[
            pl.BlockSpec((1, TS_PREP, MIX_WIDTH), lambda i, j: (i, j, 0)),
            pl.BlockSpec((1, TS_PREP, KV_WIDTH), lambda i, j: (i, j, kblk)),
            pl.BlockSpec((1, TS_PREP, KV_WIDTH), lambda i, j: (i, j, kblk + 1)),
            pl.BlockSpec((1, TS_PREP, MEM_WIDTH), lambda i, j: (i, j, kblk + 2)),
            pl.BlockSpec((1, HEAD_DIM), lambda i, j: (0, 0)),
            pl.BlockSpec((1, HEAD_DIM), lambda i, j: (0, 0)),
            pl.BlockSpec((1, HEAD_DIM), lambda i, j: (0, 0)),
        ],
        out_specs=[
            pl.BlockSpec((1, TS_PREP, MIX_WIDTH), lambda i, j: (i, j, 0)),
            pl.BlockSpec((1, TS_PREP, KV_WIDTH), lambda i, j: (i, j, 0)),
            pl.BlockSpec((1, TS_PREP, KV_WIDTH), lambda i, j: (i, j, 0)),
            pl.BlockSpec((1, TS_PREP, MEM_WIDTH), lambda i, j: (i, j, 0)),
        ],
        out_shape=[
            jax.ShapeDtypeStruct((b, s, MIX_WIDTH), BF16),
            jax.ShapeDtypeStruct((b, s, KV_WIDTH), BF16),
            jax.ShapeDtypeStruct((b, s, KV_WIDTH), BF16),
            jax.ShapeDtypeStruct((b, s, MEM_WIDTH), BF16),
        ],
        compiler_params=_cparams(("parallel", "parallel")),
        name="prep_b",
    )(proj, proj, proj, proj, g(gq), g(gk), g(gm))


def _win_attn_kernel(sink_ref, q_ref, k_ref, v_ref, posq_ref, posk_ref, tbl_ref, o_ref):
    iq = pl.program_id(1)
    s_len = k_ref.shape[1]
    q0 = iq * TQ_B
    ks = jnp.clip(q0 - WINDOW, 0, s_len - KW_B)
    ks = pl.multiple_of(ks, LANES)

    pk_rows = posk_ref[0, pl.ds(ks // LANES, KW_B // LANES), :]
    pk = jnp.concatenate([pk_rows[t:t + 1, :] for t in range(KW_B // LANES)], axis=1)
    bucket = _t5_bucket(pk - posq_ref[0])
    qi = q0 + lax.broadcasted_iota(I32, (TQ_B, KW_B), 0)
    ki = ks + lax.broadcasted_iota(I32, (TQ_B, KW_B), 1)
    valid = jnp.abs(ki - qi) <= WINDOW

    for kvh in range(N_KV_HEADS):
        ksl = slice(kvh * HEAD_DIM, (kvh + 1) * HEAD_DIM)
        k_w = k_ref[0, pl.ds(ks, KW_B), ksl]
        v_w = v_ref[0, pl.ds(ks, KW_B), ksl]
        for g in range(GQA_GROUP):
            h = kvh * GQA_GROUP + g
            hsl = slice(h * HEAD_DIM, (h + 1) * HEAD_DIM)
            s = lax.dot_general(q_ref[0, :, hsl], k_w, (((1,), (1,)), ((), ())), preferred_element_type=F32)
            z = jnp.where(valid, s + _table_lookup(tbl_ref[h], bucket), NEG_BIG)
            sink = sink_ref[h]
            m = jnp.maximum(jnp.max(z, axis=-1, keepdims=True), sink)
            e = jnp.exp2(z - m)
            denom = jnp.sum(e, axis=-1, keepdims=True) + jnp.exp2(sink - m)
            p = (e / denom).astype(BF16)
            o_ref[0, :, hsl] = jnp.dot(p, v_w, preferred_element_type=F32).astype(BF16)


def _win_attention(qn, kn, vn, positions, tbl_log2, sink_log2):
    b, s, _ = qn.shape
    return pl.pallas_call(
        _win_attn_kernel,
        grid=(b, s // TQ_B),
        in_specs=[
            _smem(),
            pl.BlockSpec((1, TQ_B, MIX_WIDTH), lambda i, j: (i, j, 0)),
            pl.BlockSpec((1, s, KV_WIDTH), lambda i, j: (i, 0, 0)),
            pl.BlockSpec((1, s, KV_WIDTH), lambda i, j: (i, 0, 0)),
            pl.BlockSpec((1, TQ_B, 1), lambda i, j: (i, j, 0)),
            pl.BlockSpec((1, s // LANES, LANES), lambda i, j: (i, 0, 0)),
            pl.BlockSpec((N_HEADS, 1, LANES), lambda i, j: (0, 0, 0)),
        ],
        out_specs=pl.BlockSpec((1, TQ_B, MIX_WIDTH), lambda i, j: (i, j, 0)),
        out_shape=jax.ShapeDtypeStruct((b, s, MIX_WIDTH), BF16),
        compiler_params=_cparams(("parallel", "parallel")),
        name="win_attn",
    )(sink_log2, qn, kn, vn, positions.reshape(b, s, 1), positions.reshape(b, s // LANES, LANES), tbl_log2)


def _mem_attn_kernel(q_ref, k_ref, v_ref, o_ref):
    for h in range(N_MEM_HEADS):
        sl = slice(h * HEAD_DIM, (h + 1) * HEAD_DIM)
        s = lax.dot_general(q_ref[0, :, sl], k_ref[0, :, sl], (((1,), (1,)), ((), ())), preferred_element_type=F32)
        e = jnp.exp2(s - jnp.max(s, axis=-1, keepdims=True))
        p = (e / jnp.sum(e, axis=-1, keepdims=True)).astype(BF16)
        o_ref[0, :, sl] = jnp.dot(p, v_ref[0, :, sl], preferred_element_type=F32).astype(BF16)


def _mem_attention(qmn, k_m, v_m):
    b, s, _ = qmn.shape
    return pl.pallas_call(
        _mem_attn_kernel,
        grid=(b, s // TQ_MEM),
        in_specs=[
            pl.BlockSpec((1, TQ_MEM, MEM_WIDTH), lambda i, j: (i, j, 0)),
            pl.BlockSpec((1, MEM_LEN, MEM_WIDTH), lambda i, j: (i, 0, 0)),
            pl.BlockSpec((1, MEM_LEN, MEM_WIDTH), lambda i, j: (i, 0, 0)),
        ],
        out_specs=pl.BlockSpec((1, TQ_MEM, MEM_WIDTH), lambda i, j: (i, j, 0)),
        out_shape=jax.ShapeDtypeStruct((b, s, MEM_WIDTH), BF16),
        compiler_params=_cparams(("parallel", "parallel")),
        name="mem_attn",
    )(qmn, k_m, v_m)


def _out_proj_kernel(x_ref, o_ref, om_ref, wo_ref, wm_ref, y_ref):
    y_ref[...] = (x_ref[...]
                  + jnp.dot(o_ref[...], wo_ref[...], preferred_element_type=F32)
                  + jnp.dot(om_ref[...], wm_ref[...], preferred_element_type=F32))


def _out_proj(x2d, o2d, om2d, w_bf16):
    m, d = x2d.shape
    return pl.pallas_call(
        _out_proj_kernel,
        grid=(m // TM_OUT, d // TN_OUT),
        in_specs=[
            pl.BlockSpec((TM_OUT, TN_OUT), lambda i, j: (i, j)),
            pl.BlockSpec((TM_OUT, MIX_WIDTH), lambda i, j: (i, 0)),
            pl.BlockSpec((TM_OUT, MEM_WIDTH), lambda i, j: (i, 0)),
            pl.BlockSpec((MIX_WIDTH, TN_OUT), lambda i, j: (0, j)),
            pl.BlockSpec((MEM_WIDTH, TN_OUT), lambda i, j: (MIX_WIDTH // MEM_WIDTH, j)),
        ],
        out_specs=pl.BlockSpec((TM_OUT, TN_OUT), lambda i, j: (i, j)),
        out_shape=jax.ShapeDtypeStruct((m, d), F32),
        compiler_params=_cparams(("parallel", "parallel")),
        name="out_proj",
    )(x2d, o2d, om2d, w_bf16, w_bf16)


def _mlp_kernel(x_ref, g_ref, wu_ref, wd_ref, y_ref, h_ref):
    f = pl.program_id(1)

    @pl.when(f == 0)
    def _():
        x = x_ref[...]
        r = lax.rsqrt(jnp.mean(x * x, axis=-1, keepdims=True) + RMS_EPS)
        h_ref[...] = ((x * r) * g_ref[...]).astype(BF16)
        y_ref[...] = x

    u = jnp.maximum(jnp.dot(h_ref[...], wu_ref[...], preferred_element_type=F32), 0.0)
    y_ref[...] += jnp.dot((u * u).astype(BF16), wd_ref[...], preferred_element_type=F32)


def _mlp(x2d, gain, wu_bf16, wd_bf16):
    m, d = x2d.shape
    return pl.pallas_call(
        _mlp_kernel,
        grid=(m // TM_MLP, D_FF // TF_MLP),
        in_specs=[
            pl.BlockSpec((TM_MLP, d), lambda i, f: (i, 0)),
            pl.BlockSpec((1, d), lambda i, f: (0, 0)),
            pl.BlockSpec((d, TF_MLP), lambda i, f: (0, f)),
            pl.BlockSpec((TF_MLP, d), lambda i, f: (f, 0)),
        ],
        out_specs=pl.BlockSpec((TM_MLP, d), lambda i, f: (i, 0)),
        out_shape=jax.ShapeDtypeStruct((m, d), F32),
        scratch_shapes=[pltpu.VMEM((TM_MLP, d), BF16)],
        compiler_params=_cparams(("parallel", "arbitrary")),
        name="mlp",
    )(x2d, gain.reshape(1, d), wu_bf16, wd_bf16)


def _bias_tables(rel_bias):
    t = (rel_bias.astype(F32) * LOG2E).T
    tbl = jnp.zeros((N_HEADS, 1, LANES), F32).at[:, 0, :N_BUCKETS].set(t)
    far = jnp.stack([t[:, HALF_BUCKETS - 1], t[:, N_BUCKETS - 1]])
    return tbl, far


def _diff_logit_shift(gq, gk, tbl_log2):
    qk = 1.01 * DIFF_QK_DIM * jnp.max(jnp.abs(gq)) * jnp.max(jnp.abs(gk)) * (DIFF_QK_DIM ** -0.5 * LOG2E)
    return jnp.ceil(qk + jnp.max(jnp.abs(tbl_log2))).astype(F32)


def kernel(x, mem, positions, rel_bias, norm_attn, norm_mem, norm_mlp, w_in_a, a_q_norm, a_k_norm, a_lambda_q1, a_lambda_k1, a_lambda_q2, a_lambda_k2, a_subln, w_in_b, b_q_norm, b_k_norm, b_sink, w_mem_kv, m_q_norm, m_k_norm, w_out, w_up, w_down):
    b, s, d = x.shape
    depth = norm_attn.shape[0]
    tbl_log2, far_log2 = _bias_tables(rel_bias)
    x2d = x.reshape(b * s, d)
    for i in range(depth):
        j = i // 2
        k_m, v_m = _mem_kv(mem, norm_mem[i], w_mem_kv[i].astype(BF16), m_k_norm[i])
        if i % 2 == 0:
            proj = _norm_proj(x2d, norm_attn[i], w_in_a[j].astype(BF16)).reshape(b, s, -1)
            shift = _diff_logit_shift(a_q_norm[j], a_k_norm[j], tbl_log2)
            bounded = shift <= SHIFT_LIMIT
            qx, kx, vtx, qmn = _prep_a(proj, a_q_norm[j], a_k_norm[j], m_q_norm[i], jnp.where(bounded, -shift, 0.0))
            lam_init = 0.8 - 0.6 * math.exp(-0.3 * i)
            attn = functools.partial(
                _diff_attention, qx, kx, vtx, positions, tbl_log2,
                lam_vecs=(a_lambda_q1[j], a_lambda_k1[j], a_lambda_q2[j], a_lambda_k2[j]), gsub=a_subln[j],
                lam_init=lam_init)
            o = lax.cond(bounded,
                         lambda: attn(far_consts=jnp.exp2(far_log2), bounded=True),
                         lambda: attn(far_consts=far_log2, bounded=False))
        else:
            proj = _norm_proj(x2d, norm_attn[i], w_in_b[j].astype(BF16)).reshape(b, s, -1)
            qn, kn, vn, qmn = _prep_b(proj, b_q_norm[j], b_k_norm[j], m_q_norm[i])
            o = _win_attention(qn, kn, vn, positions, tbl_log2, b_sink[j].astype(F32) * LOG2E)
        o_m = _mem_attention(qmn, k_m, v_m)
        x2d = _out_proj(x2d, o.reshape(b * s, MIX_WIDTH), o_m.reshape(b * s, MEM_WIDTH), w_out[i].astype(BF16))
        x2d = _mlp(x2d, norm_mlp[i], w_up[i].astype(BF16), w_down[i].astype(BF16))
    return x2d.reshape(b, s, d)
```

```python
import functools
import math

import jax
import jax.numpy as jnp
from jax import lax
from jax.experimental import pallas as pl
from jax.experimental.pallas import tpu as pltpu

F32 = jnp.float32
BF16 = jnp.bfloat16
I32 = jnp.int32

D_MODEL = 2048
N_HEADS = 12
HEAD_DIM = 128
DIFF_QK_DIM = 64
N_KV_HEADS = 4
GQA_GROUP = 3
MIX_WIDTH = N_HEADS * HEAD_DIM
KV_WIDTH = N_KV_HEADS * HEAD_DIM
WINDOW = 128
N_MEM_HEADS = 4
MEM_WIDTH = N_MEM_HEADS * HEAD_DIM
MEM_LEN = 256
D_FF = 4 * D_MODEL
N_BUCKETS = 32
MAX_DISTANCE = 128
RMS_EPS = 1e-6
NEG_BIG = -1e30
LOG2E = math.log2(math.e)

HALF_BUCKETS = N_BUCKETS // 2
MAX_EXACT = HALF_BUCKETS // 2
FAR_DIST = 91

ONE_COL = DIFF_QK_DIM
V_ROWS = HEAD_DIM + 16
SHIFT_LIMIT = 50.0

LANES = 128
VMEM_LIMIT = 56 * 1024 * 1024

TM_PROJ, TN_PROJ = 1024, 512
TS_PREP = 512
TQ_A, TK_A = 512, 512
TQ_B = 256
KW_B = TQ_B + 2 * WINDOW
TQ_MEM = 512
TM_OUT, TN_OUT = 1024, 1024
TM_MLP, TF_MLP = 1024, 512


def _cparams(sem):
    return pltpu.CompilerParams(dimension_semantics=sem, vmem_limit_bytes=VMEM_LIMIT)


def _smem():
    return pl.BlockSpec(memory_space=pltpu.SMEM)


def _t5_bucket(rel):
    side = jnp.where(rel > 0, HALF_BUCKETS, 0)
    n = jnp.abs(rel)
    n_f = jnp.maximum(n, 1).astype(F32)
    large = MAX_EXACT + (jnp.log(n_f / MAX_EXACT) / math.log(MAX_DISTANCE / MAX_EXACT)
                         * (HALF_BUCKETS - MAX_EXACT)).astype(I32)
    large = jnp.minimum(large, HALF_BUCKETS - 1)
    return side + jnp.where(n < MAX_EXACT, n, large)


def _table_lookup(tbl_row, bucket):
    rows, cols = bucket.shape
    tb = jnp.broadcast_to(tbl_row, (rows, LANES))
    parts = [jnp.take_along_axis(tb, bucket[:, c:c + LANES], axis=1) for c in range(0, cols, LANES)]
    return parts[0] if len(parts) == 1 else jnp.concatenate(parts, axis=1)


def _group_rms_scale(x, group):
    t = x * x
    if group == LANES:
        return lax.rsqrt(jnp.mean(t, axis=-1, keepdims=True) + RMS_EPS)
    lane = lax.broadcasted_iota(I32, x.shape, 1)
    lo = lane < group
    s_lo = jnp.sum(jnp.where(lo, t, 0.0), axis=-1, keepdims=True)
    s_hi = jnp.sum(jnp.where(lo, 0.0, t), axis=-1, keepdims=True)
    return jnp.where(lo, lax.rsqrt(s_lo / group + RMS_EPS), lax.rsqrt(s_hi / group + RMS_EPS))


def _proj_kernel(x_ref, g_ref, w_ref, o_ref, h_ref):
    @pl.when(pl.program_id(1) == 0)
    def _():
        x = x_ref[...]
        r = lax.rsqrt(jnp.mean(x * x, axis=-1, keepdims=True) + RMS_EPS)
        h_ref[...] = ((x * r) * g_ref[...]).astype(BF16)

    o_ref[...] = jnp.dot(h_ref[...], w_ref[...].astype(BF16), preferred_element_type=F32)


def _norm_proj(x2d, gain, w_stack, layer):
    m, d = x2d.shape
    n = w_stack.shape[2]
    return pl.pallas_call(
        _proj_kernel,
        grid=(m // TM_PROJ, n // TN_PROJ),
        in_specs=[
            pl.BlockSpec((TM_PROJ, d), lambda i, j: (i, 0)),
            pl.BlockSpec((1, d), lambda i, j: (0, 0)),
            pl.BlockSpec((None, d, TN_PROJ), lambda i, j: (layer, 0, j)),
        ],
        out_specs=pl.BlockSpec((TM_PROJ, TN_PROJ), lambda i, j: (i, j)),
        out_shape=jax.ShapeDtypeStruct((m, n), F32),
        scratch_shapes=[pltpu.VMEM((TM_PROJ, d), BF16)],
        compiler_params=_cparams(("parallel", "arbitrary")),
        name="norm_proj",
    )(x2d, gain.reshape(1, d), w_stack)


def _mem_kv_kernel(mem_ref, g_ref, w_ref, gk_ref, k_ref, v_ref):
    x = mem_ref[0]
    r = lax.rsqrt(jnp.mean(x * x, axis=-1, keepdims=True) + RMS_EPS)
    mn = ((x * r) * g_ref[...]).astype(BF16)
    mkv = jnp.dot(mn, w_ref[...].astype(BF16), preferred_element_type=F32)
    for h in range(N_MEM_HEADS):
        kh = mkv[:, h * HEAD_DIM:(h + 1) * HEAD_DIM]
        k_ref[0, :, h * HEAD_DIM:(h + 1) * HEAD_DIM] = ((kh * _group_rms_scale(kh, HEAD_DIM)) * gk_ref[...]).astype(BF16)
    v_ref[0] = mkv[:, MEM_WIDTH:].astype(BF16)


def _mem_kv(mem, gain, w_stack, layer, gk):
    b = mem.shape[0]
    shp = jax.ShapeDtypeStruct((b, MEM_LEN, MEM_WIDTH), BF16)
    return pl.pallas_call(
        _mem_kv_kernel,
        grid=(b,),
        in_specs=[
            pl.BlockSpec((1, MEM_LEN, D_MODEL), lambda i: (i, 0, 0)),
            pl.BlockSpec((1, D_MODEL), lambda i: (0, 0)),
            pl.BlockSpec((None, D_MODEL, 2 * MEM_WIDTH), lambda i: (layer, 0, 0)),
            pl.BlockSpec((1, HEAD_DIM), lambda i: (0, 0)),
        ],
        out_specs=[pl.BlockSpec((1, MEM_LEN, MEM_WIDTH), lambda i: (i, 0, 0))] * 2,
        out_shape=[shp, shp],
        compiler_params=_cparams(("parallel",)),
        name="mem_kv",
    )(mem, gain.reshape(1, D_MODEL), w_stack, gk.reshape(1, HEAD_DIM))


def _prep_a_kernel(negm_ref, q_ref, k_ref, v_ref, qm_ref, gq_ref, gk_ref, gm_ref, qx_ref, kx_ref, vtx_ref, qmn_ref):
    q_scale = DIFF_QK_DIM ** -0.5 * LOG2E
    m_scale = HEAD_DIM ** -0.5 * LOG2E
    lane = lax.broadcasted_iota(I32, (TS_PREP, HEAD_DIM), 1)
    neg_shift = negm_ref[0]

    def extend(x, extra):
        return jnp.where(lane < DIFF_QK_DIM, x, jnp.where(lane == ONE_COL, extra, 0.0))

    ones_rows = jnp.where(lax.broadcasted_iota(I32, (V_ROWS - HEAD_DIM, TS_PREP), 0) == 0, 1.0, 0.0).astype(BF16)
    for h in range(N_HEADS):
        sl = slice(h * HEAD_DIM, (h + 1) * HEAD_DIM)
        q = q_ref[0, :, sl]
        qn = ((q * _group_rms_scale(q, DIFF_QK_DIM)) * gq_ref[...]) * q_scale
        k = k_ref[0, :, sl]
        kn = (k * _group_rms_scale(k, DIFF_QK_DIM)) * gk_ref[...]
        for c in range(2):
            qc = qn if c == 0 else pltpu.roll(qn, DIFF_QK_DIM, axis=1)
            kc = kn if c == 0 else pltpu.roll(kn, DIFF_QK_DIM, axis=1)
            qx_ref[0, 2 * h + c] = extend(qc, neg_shift).T.astype(BF16)
            kx_ref[0, 2 * h + c] = extend(kc, 1.0).astype(BF16)
        vt = v_ref[0, :, sl].T.astype(BF16)
        vtx_ref[0, h, 0] = jnp.concatenate([vt, ones_rows], axis=0)
    for h in range(N_MEM_HEADS):
        sl = slice(h * HEAD_DIM, (h + 1) * HEAD_DIM)
        qm = qm_ref[0, :, sl]
        qmn_ref[0, :, sl] = (((qm * _group_rms_scale(qm, HEAD_DIM)) * gm_ref[...]) * m_scale).astype(BF16)


def _prep_a(proj, gq, gk, gm, neg_shift):
    assert TS_PREP == TK_A
    b, s, _ = proj.shape
    gq2 = jnp.concatenate([gq, gq]).reshape(1, HEAD_DIM)
    gk2 = jnp.concatenate([gk, gk]).reshape(1, HEAD_DIM)
    wblk = MIX_WIDTH // MEM_WIDTH
    return pl.pallas_call(
        _prep_a_kernel,
        grid=(b, s // TS_PREP),
        in_specs=[
            _smem(),
            pl.BlockSpec((1, TS_PREP, MIX_WIDTH), lambda i, j: (i, j, 0)),
            pl.BlockSpec((1, TS_PREP, MIX_WIDTH), lambda i, j: (i, j, 1)),
            pl.BlockSpec((1, TS_PREP, MIX_WIDTH), lambda i, j: (i, j, 2)),
            pl.BlockSpec((1, TS_PREP, MEM_WIDTH), lambda i, j: (i, j, 3 * wblk)),
            pl.BlockSpec((1, HEAD_DIM), lambda i, j: (0, 0)),
            pl.BlockSpec((1, HEAD_DIM), lambda i, j: (0, 0)),
            pl.BlockSpec((1, HEAD_DIM), lambda i, j: (0, 0)),
        ],
        out_specs=[
            pl.BlockSpec((1, 2 * N_HEADS, HEAD_DIM, TS_PREP), lambda i, j: (i, 0, 0, j)),
            pl.BlockSpec((1, 2 * N_HEADS, TS_PREP, HEAD_DIM), lambda i, j: (i, 0, j, 0)),
            pl.BlockSpec((1, N_HEADS, 1, V_ROWS, TK_A), lambda i, j: (i, 0, j, 0, 0)),
            pl.BlockSpec((1, TS_PREP, MEM_WIDTH), lambda i, j: (i, j, 0)),
        ],
        out_shape=[
            jax.ShapeDtypeStruct((b, 2 * N_HEADS, HEAD_DIM, s), BF16),
            jax.ShapeDtypeStruct((b, 2 * N_HEADS, s, HEAD_DIM), BF16),
            jax.ShapeDtypeStruct((b, N_HEADS, s // TK_A, V_ROWS, TK_A), BF16),
            jax.ShapeDtypeStruct((b, s, MEM_WIDTH), BF16),
        ],
        compiler_params=_cparams(("parallel", "parallel")),
        name="prep_a",
    )(neg_shift.reshape(1), proj, proj, proj, proj, gq2, gk2, gm.reshape(1, HEAD_DIM))


def _diff_attn_kernel(qlo_ref, qhi_ref, klo_ref, khi_ref,
                      far_ref, lq1_ref, lk1_ref, lq2_ref, lk2_ref,
                      qx_ref, kx_ref, vtx_ref, posq_ref, posk_ref, tbl_ref, gsub_ref,
                      o_ref, acc_ref, *m_scratch, lam_init, bounded):
    b, iq, j = pl.program_id(0), pl.program_id(1), pl.program_id(2)
    pos_far = klo_ref[b, j] - qhi_ref[b, iq] >= FAR_DIST
    neg_far = khi_ref[b, j] - qlo_ref[b, iq] <= -FAR_DIST
    far = jnp.logical_or(pos_far, neg_far)

    @pl.when(j == 0)
    def _():
        acc_ref[...] = jnp.zeros(acc_ref.shape, F32)
        if not bounded:
            m_scratch[0][...] = jnp.full(m_scratch[0].shape, NEG_BIG, F32)

    def logits(hc):
        return jnp.dot(kx_ref[0, hc], qx_ref[0, hc], preferred_element_type=F32)

    def accumulate(hc, s, far_const):
        h = hc // 2
        if bounded:
            pv = jnp.dot(vtx_ref[0, h, 0], jnp.exp2(s).astype(BF16), preferred_element_type=F32)
            acc_ref[hc] += pv if far_const is None else far_const * pv
        else:
            m_ref = m_scratch[0]
            off = 0.0 if far_const is None else far_const
            m_old = m_ref[hc]
            m_new = jnp.maximum(m_old, jnp.max(s, axis=0, keepdims=True) + off)
            p = jnp.exp2(s - (m_new - off)).astype(BF16)
            acc_ref[hc] = (jnp.exp2(m_old - m_new) * acc_ref[hc]
                           + jnp.dot(vtx_ref[0, h, 0], p, preferred_element_type=F32))
            m_ref[hc] = m_new

    def all_heads(bias_of_head, far_const_of_head):
        s_next = logits(0)
        bias_next = None if bias_of_head is None else bias_of_head(0)
        for hc in range(2 * N_HEADS):
            h, c = divmod(hc, 2)
            s = s_next
            if hc + 1 < 2 * N_HEADS:
                s_next = logits(hc + 1)
            if bias_of_head is not None:
                if c == 0:
                    bias = bias_next
                    if h + 1 < N_HEADS:
                        bias_next = bias_of_head(h + 1)
                s = s + bias
            accumulate(hc, s, None if far_const_of_head is None else far_const_of_head(h))

    @pl.when(far)
    def _():
        all_heads(None, lambda h: jnp.where(pos_far, far_ref[1, h], far_ref[0, h]))

    @pl.when(jnp.logical_not(far))
    def _():
        bucket = []

        def bias_of_head(h):
            if not bucket:
                bucket.append(_t5_bucket(posk_ref[0] - posq_ref[0]))
            return _table_lookup(tbl_ref[h], bucket[0])

        all_heads(bias_of_head, None)

    @pl.when(j == pl.num_programs(2) - 1)
    def _():
        lam = (jnp.exp(jnp.sum(lq1_ref[...] * lk1_ref[...], axis=-1, keepdims=True))
               - jnp.exp(jnp.sum(lq2_ref[...] * lk2_ref[...], axis=-1, keepdims=True)) + lam_init)
        for h in range(N_HEADS):
            a0, a1 = acc_ref[2 * h], acc_ref[2 * h + 1]
            o_t = (a0[:HEAD_DIM] / a0[HEAD_DIM:HEAD_DIM + 1]
                   - lam * (a1[:HEAD_DIM] / a1[HEAD_DIM:HEAD_DIM + 1]))
            r = lax.rsqrt(jnp.mean(o_t * o_t, axis=0, keepdims=True) + RMS_EPS)
            o_t = ((o_t * r) * gsub_ref[...]) * (1.0 - lam_init)
            o_ref[0, :, h * HEAD_DIM:(h + 1) * HEAD_DIM] = o_t.T.astype(BF16)


def _diff_attention(qx, kx, vtx, positions, tbl_log2, far_consts, lam_vecs, gsub, lam_init, bounded):
    b, _, s, _ = kx.shape
    nq, nk = s // TQ_A, s // TK_A
    pq = positions.reshape(b, nq, TQ_A)
    pk = positions.reshape(b, nk, TK_A)
    ranges = (pq.min(-1), pq.max(-1), pk.min(-1), pk.max(-1))
    lam_spec = pl.BlockSpec((1, DIFF_QK_DIM), lambda i, q, k, *_: (0, 0))
    scratch = [pltpu.VMEM((2 * N_HEADS, V_ROWS, TQ_A), F32)]
    if not bounded:
        scratch.append(pltpu.VMEM((2 * N_HEADS, 1, TQ_A), F32))
    grid_spec = pltpu.PrefetchScalarGridSpec(
        num_scalar_prefetch=4,
        grid=(b, nq, nk),
        in_specs=[
            _smem(), lam_spec, lam_spec, lam_spec, lam_spec,
            pl.BlockSpec((1, 2 * N_HEADS, HEAD_DIM, TQ_A), lambda i, q, k, *_: (i, 0, 0, q)),
            pl.BlockSpec((1, 2 * N_HEADS, TK_A, HEAD_DIM), lambda i, q, k, *_: (i, 0, k, 0)),
            pl.BlockSpec((1, N_HEADS, 1, V_ROWS, TK_A), lambda i, q, k, *_: (i, 0, k, 0, 0)),
            pl.BlockSpec((1, 1, TQ_A), lambda i, q, k, *_: (i, 0, q)),
            pl.BlockSpec((1, TK_A, 1), lambda i, q, k, *_: (i, k, 0)),
            pl.BlockSpec((N_HEADS, 1, LANES), lambda i, q, k, *_: (0, 0, 0)),
            pl.BlockSpec((HEAD_DIM, 1), lambda i, q, k, *_: (0, 0)),
        ],
        out_specs=pl.BlockSpec((1, TQ_A, MIX_WIDTH), lambda i, q, k, *_: (i, q, 0)),
        scratch_shapes=scratch,
    )
    return pl.pallas_call(
        functools.partial(_diff_attn_kernel, lam_init=lam_init, bounded=bounded),
        grid_spec=grid_spec,
        out_shape=jax.ShapeDtypeStruct((b, s, MIX_WIDTH), BF16),
        compiler_params=_cparams(("parallel", "parallel", "arbitrary")),
        name="diff_attn_bounded" if bounded else "diff_attn_running_max",
    )(*ranges, far_consts, *[v.reshape(1, DIFF_QK_DIM) for v in lam_vecs],
      qx, kx, vtx, positions.reshape(b, 1, s), positions.reshape(b, s, 1), tbl_log2, gsub.reshape(HEAD_DIM, 1))


def _prep_b_kernel(q_ref, k_ref, v_ref, qm_ref, gq_ref, gk_ref, gm_ref, qn_ref, kn_ref, vn_ref, qmn_ref):
    scale = HEAD_DIM ** -0.5 * LOG2E
    for h in range(N_HEADS):
        sl = slice(h * HEAD_DIM, (h + 1) * HEAD_DIM)
        q = q_ref[0, :, sl]
        qn_ref[0, :, sl] = (((q * _group_rms_scale(q, HEAD_DIM)) * gq_ref[...]) * scale).astype(BF16)
    for h in range(N_KV_HEADS):
        sl = slice(h * HEAD_DIM, (h + 1) * HEAD_DIM)
        k = k_ref[0, :, sl]
        kn_ref[0, :, sl] = ((k * _group_rms_scale(k, HEAD_DIM)) * gk_ref[...]).astype(BF16)
        qm = qm_ref[0, :, sl]
        qmn_ref[0, :, sl] = (((qm * _group_rms_scale(qm, HEAD_DIM)) * gm_ref[...]) * scale).astype(BF16)
    vn_ref[0] = v_ref[0].astype(BF16)


def _prep_b(proj, gq, gk, gm):
    b, s, _ = proj.shape
    kblk = MIX_WIDTH // KV_WIDTH
    g = lambda v: v.reshape(1, HEAD_DIM)
    return pl.pallas_call(
        _prep_b_kernel,
        grid=(b, s // TS_PREP),
        in_specs=[
            pl.BlockSpec((1, TS_PREP, MIX_WIDTH), lambda i, j: (i, j, 0)),
            pl.BlockSpec((1, TS_PREP, KV_WIDTH), lambda i, j: (i, j, kblk)),
            pl.BlockSpec((1, TS_PREP, KV_WIDTH), lambda i, j: (i, j, kblk + 1)),
            pl.BlockSpec((1, TS_PREP, MEM_WIDTH), lambda i, j: (i, j, kblk + 2)),
            pl.BlockSpec((1, HEAD_DIM), lambda i, j: (0, 0)),
            pl.BlockSpec((1, HEAD_DIM), lambda i, j: (0, 0)),
            pl.BlockSpec((1, HEAD_DIM), lambda i, j: (0, 0)),
        ],
        out_specs=[
            pl.BlockSpec((1, TS_PREP, MIX_WIDTH), lambda i, j: (i, j, 0)),
            pl.BlockSpec((1, TS_PREP, KV_WIDTH), lambda i, j: (i, j, 0)),
            pl.BlockSpec((1, TS_PREP, KV_WIDTH), lambda i, j: (i, j, 0)),
            pl.BlockSpec((1, TS_PREP, MEM_WIDTH), lambda i, j: (i, j, 0)),
        ],
        out_shape=[
            jax.ShapeDtypeStruct((b, s, MIX_WIDTH), BF16),
            jax.ShapeDtypeStruct((b, s, KV_WIDTH), BF16),
            jax.ShapeDtypeStruct((b, s, KV_WIDTH), BF16),
            jax.ShapeDtypeStruct((b, s, MEM_WIDTH), BF16),
        ],
        compiler_params=_cparams(("parallel", "parallel")),
        name="prep_b",
    )(proj, proj, proj, proj, g(gq), g(gk), g(gm))


def _win_attn_kernel(sink_ref, q_ref, k_ref, v_ref, posq_ref, posk_ref, tbl_ref, o_ref):
    iq = pl.program_id(1)
    s_len = k_ref.shape[1]
    q0 = iq * TQ_B
    ks = jnp.clip(q0 - WINDOW, 0, s_len - KW_B)
    ks = pl.multiple_of(ks, LANES)

    pk_rows = posk_ref[0, pl.ds(ks // LANES, KW_B // LANES), :]
    pk = jnp.concatenate([pk_rows[t:t + 1, :] for t in range(KW_B // LANES)], axis=1)
    bucket = _t5_bucket(pk - posq_ref[0])
    qi = q0 + lax.broadcasted_iota(I32, (TQ_B, KW_B), 0)
    ki = ks + lax.broadcasted_iota(I32, (TQ_B, KW_B), 1)
    valid = jnp.abs(ki - qi) <= WINDOW

    for kvh in range(N_KV_HEADS):
        ksl = slice(kvh * HEAD_DIM, (kvh + 1) * HEAD_DIM)
        k_w = k_ref[0, pl.ds(ks, KW_B), ksl]
        v_w = v_ref[0, pl.ds(ks, KW_B), ksl]
        for g in range(GQA_GROUP):
            h = kvh * GQA_GROUP + g
            hsl = slice(h * HEAD_DIM, (h + 1) * HEAD_DIM)
            s = lax.dot_general(q_ref[0, :, hsl], k_w, (((1,), (1,)), ((), ())), preferred_element_type=F32)
            z = jnp.where(valid, s + _table_lookup(tbl_ref[h], bucket), NEG_BIG)
            sink = sink_ref[h]
            m = jnp.maximum(jnp.max(z, axis=-1, keepdims=True), sink)
            e = jnp.exp2(z - m)
            denom = jnp.sum(e, axis=-1, keepdims=True) + jnp.exp2(sink - m)
            p = (e / denom).astype(BF16)
            o_ref[0, :, hsl] = jnp.dot(p, v_w, preferred_element_type=F32).astype(BF16)


def _win_attention(qn, kn, vn, positions, tbl_log2, sink_log2):
    b, s, _ = qn.shape
    return pl.pallas_call(
        _win_attn_kernel,
        grid=(b, s // TQ_B),
        in_specs=[
            _smem(),
            pl.BlockSpec((1, TQ_B, MIX_WIDTH), lambda i, j: (i, j, 0)),
            pl.BlockSpec((1, s, KV_WIDTH), lambda i, j: (i, 0, 0)),
            pl.BlockSpec((1, s, KV_WIDTH), lambda i, j: (i, 0, 0)),
            pl.BlockSpec((1, TQ_B, 1), lambda i, j: (i, j, 0)),
            pl.BlockSpec((1, s // LANES, LANES), lambda i, j: (i, 0, 0)),
            pl.BlockSpec((N_HEADS, 1, LANES), lambda i, j: (0, 0, 0)),
        ],
        out_specs=pl.BlockSpec((1, TQ_B, MIX_WIDTH), lambda i, j: (i, j, 0)),
        out_shape=jax.ShapeDtypeStruct((b, s, MIX_WIDTH), BF16),
        compiler_params=_cparams(("parallel", "parallel")),
        name="win_attn",
    )(sink_log2, qn, kn, vn, positions.reshape(b, s, 1), positions.reshape(b, s // LANES, LANES), tbl_log2)


def _mem_attn_kernel(q_ref, k_ref, v_ref, o_ref):
    for h in range(N_MEM_HEADS):
        sl = slice(h * HEAD_DIM, (h + 1) * HEAD_DIM)
        s = lax.dot_general(q_ref[0, :, sl], k_ref[0, :, sl], (((1,), (1,)), ((), ())), preferred_element_type=F32)
        e = jnp.exp2(s - jnp.max(s, axis=-1, keepdims=True))
        p = (e / jnp.sum(e, axis=-1, keepdims=True)).astype(BF16)
        o_ref[0, :, sl] = jnp.dot(p, v_ref[0, :, sl], preferred_element_type=F32).astype(BF16)


def _mem_attention(qmn, k_m, v_m):
    b, s, _ = qmn.shape
    return pl.pallas_call(
        _mem_attn_kernel,
        grid=(b, s // TQ_MEM),
        in_specs=[
            pl.BlockSpec((1, TQ_MEM, MEM_WIDTH), lambda i, j: (i, j, 0)),
            pl.BlockSpec((1, MEM_LEN, MEM_WIDTH), lambda i, j: (i, 0, 0)),
            pl.BlockSpec((1, MEM_LEN, MEM_WIDTH), lambda i, j: (i, 0, 0)),
        ],
        out_specs=pl.BlockSpec((1, TQ_MEM, MEM_WIDTH), lambda i, j: (i, j, 0)),
        out_shape=jax.ShapeDtypeStruct((b, s, MEM_WIDTH), BF16),
        compiler_params=_cparams(("parallel", "parallel")),
        name="mem_attn",
    )(qmn, k_m, v_m)


def _out_proj_kernel(x_ref, o_ref, om_ref, wo_ref, wm_ref, y_ref):
    y_ref[...] = (x_ref[...]
                  + jnp.dot(o_ref[...], wo_ref[...].astype(BF16), preferred_element_type=F32)
                  + jnp.dot(om_ref[...], wm_ref[...].astype(BF16), preferred_element_type=F32))


def _out_proj(x2d, o2d, om2d, w_stack, layer):
    m, d = x2d.shape
    return pl.pallas_call(
        _out_proj_kernel,
        grid=(m // TM_OUT, d // TN_OUT),
        in_specs=[
            pl.BlockSpec((TM_OUT, TN_OUT), lambda i, j: (i, j)),
            pl.BlockSpec((TM_OUT, MIX_WIDTH), lambda i, j: (i, 0)),
            pl.BlockSpec((TM_OUT, MEM_WIDTH), lambda i, j: (i, 0)),
            pl.BlockSpec((None, MIX_WIDTH, TN_OUT), lambda i, j: (layer, 0, j)),
            pl.BlockSpec((None, MEM_WIDTH, TN_OUT), lambda i, j: (layer, MIX_WIDTH // MEM_WIDTH, j)),
        ],
        out_specs=pl.BlockSpec((TM_OUT, TN_OUT), lambda i, j: (i, j)),
        out_shape=jax.ShapeDtypeStruct((m, d), F32),
        compiler_params=_cparams(("parallel", "parallel")),
        name="out_proj",
    )(x2d, o2d, om2d, w_stack, w_stack)


def _mlp_kernel(x_ref, g_ref, wu_ref, wd_ref, y_ref, h_ref):
    f = pl.program_id(1)

    @pl.when(f == 0)
    def _():
        x = x_ref[...]
        r = lax.rsqrt(jnp.mean(x * x, axis=-1, keepdims=True) + RMS_EPS)
        h_ref[...] = ((x * r) * g_ref[...]).astype(BF16)
        y_ref[...] = x

    u = jnp.maximum(jnp.dot(h_ref[...], wu_ref[...].astype(BF16), preferred_element_type=F32), 0.0)
    y_ref[...] += jnp.dot((u * u).astype(BF16), wd_ref[...].astype(BF16), preferred_element_type=F32)


def _mlp(x2d, gain, wu_stack, wd_stack, layer):
    m, d = x2d.shape
    return pl.pallas_call(
        _mlp_kernel,
        grid=(m // TM_MLP, D_FF // TF_MLP),
        in_specs=[
            pl.BlockSpec((TM_MLP, d), lambda i, f: (i, 0), pipeline_mode=pl.Buffered(1)),
            pl.BlockSpec((1, d), lambda i, f: (0, 0)),
            pl.BlockSpec((None, d, TF_MLP), lambda i, f: (layer, 0, f)),
            pl.BlockSpec((None, TF_MLP, d), lambda i, f: (layer, f, 0)),
        ],
        out_specs=pl.BlockSpec((TM_MLP, d), lambda i, f: (i, 0)),
        out_shape=jax.ShapeDtypeStruct((m, d), F32),
        scratch_shapes=[pltpu.VMEM((TM_MLP, d), BF16)],
        compiler_params=_cparams(("parallel", "arbitrary")),
        name="mlp",
    )(x2d, gain.reshape(1, d), wu_stack, wd_stack)


def _bias_tables(rel_bias):
    t = (rel_bias.astype(F32) * LOG2E).T
    tbl = jnp.zeros((N_HEADS, 1, LANES), F32).at[:, 0, :N_BUCKETS].set(t)
    far = jnp.stack([t[:, HALF_BUCKETS - 1], t[:, N_BUCKETS - 1]])
    return tbl, far


def _diff_logit_shift(gq, gk, tbl_log2):
    qk = 1.01 * DIFF_QK_DIM * jnp.max(jnp.abs(gq)) * jnp.max(jnp.abs(gk)) * (DIFF_QK_DIM ** -0.5 * LOG2E)
    return jnp.ceil(qk + jnp.max(jnp.abs(tbl_log2))).astype(F32)


def kernel(x, mem, positions, rel_bias, norm_attn, norm_mem, norm_mlp, w_in_a, a_q_norm, a_k_norm, a_lambda_q1, a_lambda_k1, a_lambda_q2, a_lambda_k2, a_subln, w_in_b, b_q_norm, b_k_norm, b_sink, w_mem_kv, m_q_norm, m_k_norm, w_out, w_up, w_down):
    b, s, d = x.shape
    depth = norm_attn.shape[0]
    tbl_log2, far_log2 = _bias_tables(rel_bias)
    x2d = x.reshape(b * s, d)
    for i in range(depth):
        j = i // 2
        k_m, v_m = _mem_kv(mem, norm_mem[i], w_mem_kv, i, m_k_norm[i])
        if i % 2 == 0:
            proj = _norm_proj(x2d, norm_attn[i], w_in_a, j).reshape(b, s, -1)
            shift = _diff_logit_shift(a_q_norm[j], a_k_norm[j], tbl_log2)
            bounded = shift <= SHIFT_LIMIT
            qx, kx, vtx, qmn = _prep_a(proj, a_q_norm[j], a_k_norm[j], m_q_norm[i], jnp.where(bounded, -shift, 0.0))
            lam_init = 0.8 - 0.6 * math.exp(-0.3 * i)
            attn = functools.partial(
                _diff_attention, qx, kx, vtx, positions, tbl_log2,
                lam_vecs=(a_lambda_q1[j], a_lambda_k1[j], a_lambda_q2[j], a_lambda_k2[j]), gsub=a_subln[j],
                lam_init=lam_init)
            o = lax.cond(bounded,
                         lambda: attn(far_consts=jnp.exp2(far_log2), bounded=True),
                         lambda: attn(far_consts=far_log2, bounded=False))
        else:
            proj = _norm_proj(x2d, norm_attn[i], w_in_b, j).reshape(b, s, -1)
            qn, kn, vn, qmn = _prep_b(proj, b_q_norm[j], b_k_norm[j], m_q_norm[i])
            o = _win_attention(qn, kn, vn, positions, tbl_log2, b_sink[j].astype(F32) * LOG2E)
        o_m = _mem_attention(qmn, k_m, v_m)
        x2d = _out_proj(x2d, o.reshape(b * s, MIX_WIDTH), o_m.reshape(b * s, MEM_WIDTH), w_out, i)
        x2d = _mlp(x2d, norm_mlp[i], w_up, w_down, i)
    return x2d.reshape(b, s, d)
```

```python
import functools
import math

import jax
import jax.numpy as jnp
from jax import lax
from jax.experimental import pallas as pl
from jax.experimental.pallas import tpu as pltpu

F32 = jnp.float32
BF16 = jnp.bfloat16
I32 = jnp.int32

D_MODEL = 2048
N_HEADS = 12
HEAD_DIM = 128
DIFF_QK_DIM = 64
N_KV_HEADS = 4
GQA_GROUP = 3
MIX_WIDTH = N_HEADS * HEAD_DIM
KV_WIDTH = N_KV_HEADS * HEAD_DIM
WINDOW = 128
N_MEM_HEADS = 4
MEM_WIDTH = N_MEM_HEADS * HEAD_DIM
MEM_LEN = 256
D_FF = 4 * D_MODEL
N_BUCKETS = 32
MAX_DISTANCE = 128
RMS_EPS = 1e-6
NEG_BIG = -1e30
LOG2E = math.log2(math.e)

HALF_BUCKETS = N_BUCKETS // 2
MAX_EXACT = HALF_BUCKETS // 2
FAR_DIST = 91

ONE_COL = DIFF_QK_DIM
V_ROWS = HEAD_DIM + 16
SHIFT_LIMIT = 50.0

LANES = 128
VMEM_LIMIT = 56 * 1024 * 1024

TM_PROJ, TN_PROJ = 1024, 512
TS_PREP = 512
TQ_A, TK_A = 512, 512
TQ_B = 512
TQ_MEM = 512
TM_OUT, TN_OUT = 1024, 1024
TM_MLP, TF_MLP = 1024, 512


def _cparams(sem):
    return pltpu.CompilerParams(dimension_semantics=sem, vmem_limit_bytes=VMEM_LIMIT)


def _smem():
    return pl.BlockSpec(memory_space=pltpu.SMEM)


def _t5_bucket(rel):
    side = jnp.where(rel > 0, HALF_BUCKETS, 0)
    n = jnp.abs(rel)
    n_f = jnp.maximum(n, 1).astype(F32)
    large = MAX_EXACT + (jnp.log(n_f / MAX_EXACT) / math.log(MAX_DISTANCE / MAX_EXACT)
                         * (HALF_BUCKETS - MAX_EXACT)).astype(I32)
    large = jnp.minimum(large, HALF_BUCKETS - 1)
    return side + jnp.where(n < MAX_EXACT, n, large)


def _table_lookup(tbl_row, bucket):
    rows, cols = bucket.shape
    tb = jnp.broadcast_to(tbl_row, (rows, LANES))
    parts = [jnp.take_along_axis(tb, bucket[:, c:c + LANES], axis=1) for c in range(0, cols, LANES)]
    return parts[0] if len(parts) == 1 else jnp.concatenate(parts, axis=1)


def _group_rms_scale(x, group):
    t = x * x
    if group == LANES:
        return lax.rsqrt(jnp.mean(t, axis=-1, keepdims=True) + RMS_EPS)
    lane = lax.broadcasted_iota(I32, x.shape, 1)
    lo = lane < group
    s_lo = jnp.sum(jnp.where(lo, t, 0.0), axis=-1, keepdims=True)
    s_hi = jnp.sum(jnp.where(lo, 0.0, t), axis=-1, keepdims=True)
    return jnp.where(lo, lax.rsqrt(s_lo / group + RMS_EPS), lax.rsqrt(s_hi / group + RMS_EPS))


def _proj_kernel(x_ref, g_ref, w_ref, o_ref, h_ref):
    @pl.when(pl.program_id(1) == 0)
    def _():
        x = x_ref[...]
        r = lax.rsqrt(jnp.mean(x * x, axis=-1, keepdims=True) + RMS_EPS)
        h_ref[...] = ((x * r) * g_ref[...]).astype(BF16)

    o_ref[...] = jnp.dot(h_ref[...], w_ref[...].astype(BF16), preferred_element_type=F32)


def _norm_proj(x2d, gain, w_stack, layer):
    m, d = x2d.shape
    n = w_stack.shape[2]
    return pl.pallas_call(
        _proj_kernel,
        grid=(m // TM_PROJ, n // TN_PROJ),
        in_specs=[
            pl.BlockSpec((TM_PROJ, d), lambda i, j: (i, 0)),
            pl.BlockSpec((1, d), lambda i, j: (0, 0)),
            pl.BlockSpec((None, d, TN_PROJ), lambda i, j: (layer, 0, j)),
        ],
        out_specs=pl.BlockSpec((TM_PROJ, TN_PROJ), lambda i, j: (i, j)),
        out_shape=jax.ShapeDtypeStruct((m, n), F32),
        scratch_shapes=[pltpu.VMEM((TM_PROJ, d), BF16)],
        compiler_params=_cparams(("parallel", "arbitrary")),
        name="norm_proj",
    )(x2d, gain.reshape(1, d), w_stack)


def _mem_kv_kernel(mem_ref, g_ref, w_ref, gk_ref, k_ref, v_ref):
    x = mem_ref[0]
    r = lax.rsqrt(jnp.mean(x * x, axis=-1, keepdims=True) + RMS_EPS)
    mn = ((x * r) * g_ref[...]).astype(BF16)
    mkv = jnp.dot(mn, w_ref[...].astype(BF16), preferred_element_type=F32)
    for h in range(N_MEM_HEADS):
        kh = mkv[:, h * HEAD_DIM:(h + 1) * HEAD_DIM]
        k_ref[0, :, h * HEAD_DIM:(h + 1) * HEAD_DIM] = ((kh * _group_rms_scale(kh, HEAD_DIM)) * gk_ref[...]).astype(BF16)
    v_ref[0] = mkv[:, MEM_WIDTH:].astype(BF16)


def _mem_kv(mem, gain, w_stack, layer, gk):
    b = mem.shape[0]
    shp = jax.ShapeDtypeStruct((b, MEM_LEN, MEM_WIDTH), BF16)
    return pl.pallas_call(
        _mem_kv_kernel,
        grid=(b,),
        in_specs=[
            pl.BlockSpec((1, MEM_LEN, D_MODEL), lambda i: (i, 0, 0)),
            pl.BlockSpec((1, D_MODEL), lambda i: (0, 0)),
            pl.BlockSpec((None, D_MODEL, 2 * MEM_WIDTH), lambda i: (layer, 0, 0)),
            pl.BlockSpec((1, HEAD_DIM), lambda i: (0, 0)),
        ],
        out_specs=[pl.BlockSpec((1, MEM_LEN, MEM_WIDTH), lambda i: (i, 0, 0))] * 2,
        out_shape=[shp, shp],
        compiler_params=_cparams(("parallel",)),
        name="mem_kv",
    )(mem, gain.reshape(1, D_MODEL), w_stack, gk.reshape(1, HEAD_DIM))


def _prep_a_kernel(negm_ref, q_ref, k_ref, v_ref, qm_ref, gq_ref, gk_ref, gm_ref, qx_ref, kx_ref, vtx_ref, qmn_ref):
    q_scale = DIFF_QK_DIM ** -0.5 * LOG2E
    m_scale = HEAD_DIM ** -0.5 * LOG2E
    lane = lax.broadcasted_iota(I32, (TS_PREP, HEAD_DIM), 1)
    neg_shift = negm_ref[0]

    def extend(x, extra):
        return jnp.where(lane < DIFF_QK_DIM, x, jnp.where(lane == ONE_COL, extra, 0.0))

    ones_rows = jnp.where(lax.broadcasted_iota(I32, (V_ROWS - HEAD_DIM, TS_PREP), 0) == 0, 1.0, 0.0).astype(BF16)
    for h in range(N_HEADS):
        sl = slice(h * HEAD_DIM, (h + 1) * HEAD_DIM)
        q = q_ref[0, :, sl]
        qn = ((q * _group_rms_scale(q, DIFF_QK_DIM)) * gq_ref[...]) * q_scale
        k = k_ref[0, :, sl]
        kn = (k * _group_rms_scale(k, DIFF_QK_DIM)) * gk_ref[...]
        for c in range(2):
            qc = qn if c == 0 else pltpu.roll(qn, DIFF_QK_DIM, axis=1)
            kc = kn if c == 0 else pltpu.roll(kn, DIFF_QK_DIM, axis=1)
            qx_ref[0, 2 * h + c] = extend(qc, neg_shift).T.astype(BF16)
            kx_ref[0, 2 * h + c] = extend(kc, 1.0).astype(BF16)
        vt = v_ref[0, :, sl].T.astype(BF16)
        vtx_ref[0, h, 0] = jnp.concatenate([vt, ones_rows], axis=0)
    for h in range(N_MEM_HEADS):
        sl = slice(h * HEAD_DIM, (h + 1) * HEAD_DIM)
        qm = qm_ref[0, :, sl]
        qmn_ref[0, :, sl] = (((qm * _group_rms_scale(qm, HEAD_DIM)) * gm_ref[...]) * m_scale).astype(BF16)


def _prep_a(proj, gq, gk, gm, neg_shift):
    assert TS_PREP == TK_A
    b, s, _ = proj.shape
    gq2 = jnp.concatenate([gq, gq]).reshape(1, HEAD_DIM)
    gk2 = jnp.concatenate([gk, gk]).reshape(1, HEAD_DIM)
    wblk = MIX_WIDTH // MEM_WIDTH
    return pl.pallas_call(
        _prep_a_kernel,
        grid=(b, s // TS_PREP),
        in_specs=[
            _smem(),
            pl.BlockSpec((1, TS_PREP, MIX_WIDTH), lambda i, j: (i, j, 0)),
            pl.BlockSpec((1, TS_PREP, MIX_WIDTH), lambda i, j: (i, j, 1)),
            pl.BlockSpec((1, TS_PREP, MIX_WIDTH), lambda i, j: (i, j, 2)),
            pl.BlockSpec((1, TS_PREP, MEM_WIDTH), lambda i, j: (i, j, 3 * wblk)),
            pl.BlockSpec((1, HEAD_DIM), lambda i, j: (0, 0)),
            pl.BlockSpec((1, HEAD_DIM), lambda i, j: (0, 0)),
            pl.BlockSpec((1, HEAD_DIM), lambda i, j: (0, 0)),
        ],
        out_specs=[
            pl.BlockSpec((1, 2 * N_HEADS, HEAD_DIM, TS_PREP), lambda i, j: (i, 0, 0, j)),
            pl.BlockSpec((1, 2 * N_HEADS, TS_PREP, HEAD_DIM), lambda i, j: (i, 0, j, 0)),
            pl.BlockSpec((1, N_HEADS, 1, V_ROWS, TK_A), lambda i, j: (i, 0, j, 0, 0)),
            pl.BlockSpec((1, TS_PREP, MEM_WIDTH), lambda i, j: (i, j, 0)),
        ],
        out_shape=[
            jax.ShapeDtypeStruct((b, 2 * N_HEADS, HEAD_DIM, s), BF16),
            jax.ShapeDtypeStruct((b, 2 * N_HEADS, s, HEAD_DIM), BF16),
            jax.ShapeDtypeStruct((b, N_HEADS, s // TK_A, V_ROWS, TK_A), BF16),
            jax.ShapeDtypeStruct((b, s, MEM_WIDTH), BF16),
        ],
        compiler_params=_cparams(("parallel", "parallel")),
        name="prep_a",
    )(neg_shift.reshape(1), proj, proj, proj, proj, gq2, gk2, gm.reshape(1, HEAD_DIM))


def _diff_attn_kernel(qlo_ref, qhi_ref, klo_ref, khi_ref,
                      far_ref, lq1_ref, lk1_ref, lq2_ref, lk2_ref,
                      qx_ref, kx_ref, vtx_ref, posq_ref, posk_ref, tbl_ref, gsub_ref,
                      o_ref, acc_ref, *m_scratch, lam_init, bounded):
    b, iq, j = pl.program_id(0), pl.program_id(1), pl.program_id(2)
    pos_far = klo_ref[b, j] - qhi_ref[b, iq] >= FAR_DIST
    neg_far = khi_ref[b, j] - qlo_ref[b, iq] <= -FAR_DIST
    far = jnp.logical_or(pos_far, neg_far)

    @pl.when(j == 0)
    def _():
        acc_ref[...] = jnp.zeros(acc_ref.shape, F32)
        if not bounded:
            m_scratch[0][...] = jnp.full(m_scratch[0].shape, NEG_BIG, F32)

    def logits(hc):
        return jnp.dot(kx_ref[0, hc], qx_ref[0, hc], preferred_element_type=F32)

    def accumulate(hc, s, far_const):
        h = hc // 2
        if bounded:
            pv = jnp.dot(vtx_ref[0, h, 0], jnp.exp2(s).astype(BF16), preferred_element_type=F32)
            acc_ref[hc] += pv if far_const is None else far_const * pv
        else:
            m_ref = m_scratch[0]
            off = 0.0 if far_const is None else far_const
            m_old = m_ref[hc]
            m_new = jnp.maximum(m_old, jnp.max(s, axis=0, keepdims=True) + off)
            p = jnp.exp2(s - (m_new - off)).astype(BF16)
            acc_ref[hc] = (jnp.exp2(m_old - m_new) * acc_ref[hc]
                           + jnp.dot(vtx_ref[0, h, 0], p, preferred_element_type=F32))
            m_ref[hc] = m_new

    def all_heads(bias_of_head, far_const_of_head):
        s_next = logits(0)
        bias_next = None if bias_of_head is None else bias_of_head(0)
        for hc in range(2 * N_HEADS):
            h, c = divmod(hc, 2)
            s = s_next
            if hc + 1 < 2 * N_HEADS:
                s_next = logits(hc + 1)
            if bias_of_head is not None:
                if c == 0:
                    bias = bias_next
                    if h + 1 < N_HEADS:
                        bias_next = bias_of_head(h + 1)
                s = s + bias
            accumulate(hc, s, None if far_const_of_head is None else far_const_of_head(h))

    @pl.when(far)
    def _():
        all_heads(None, lambda h: jnp.where(pos_far, far_ref[1, h], far_ref[0, h]))

    @pl.when(jnp.logical_not(far))
    def _():
        bucket = []

        def bias_of_head(h):
            if not bucket:
                bucket.append(_t5_bucket(posk_ref[0] - posq_ref[0]))
            return _table_lookup(tbl_ref[h], bucket[0])

        all_heads(bias_of_head, None)

    @pl.when(j == pl.num_programs(2) - 1)
    def _():
        lam = (jnp.exp(jnp.sum(lq1_ref[...] * lk1_ref[...], axis=-1, keepdims=True))
               - jnp.exp(jnp.sum(lq2_ref[...] * lk2_ref[...], axis=-1, keepdims=True)) + lam_init)
        for h in range(N_HEADS):
            a0, a1 = acc_ref[2 * h], acc_ref[2 * h + 1]
            o_t = (a0[:HEAD_DIM] / a0[HEAD_DIM:HEAD_DIM + 1]
                   - lam * (a1[:HEAD_DIM] / a1[HEAD_DIM:HEAD_DIM + 1]))
            r = lax.rsqrt(jnp.mean(o_t * o_t, axis=0, keepdims=True) + RMS_EPS)
            o_t = ((o_t * r) * gsub_ref[...]) * (1.0 - lam_init)
            o_ref[0, :, h * HEAD_DIM:(h + 1) * HEAD_DIM] = o_t.T.astype(BF16)


def _diff_attention(qx, kx, vtx, positions, tbl_log2, far_consts, lam_vecs, gsub, lam_init, bounded):
    b, _, s, _ = kx.shape
    nq, nk = s // TQ_A, s // TK_A
    pq = positions.reshape(b, nq, TQ_A)
    pk = positions.reshape(b, nk, TK_A)
    ranges = (pq.min(-1), pq.max(-1), pk.min(-1), pk.max(-1))
    lam_spec = pl.BlockSpec((1, DIFF_QK_DIM), lambda i, q, k, *_: (0, 0))
    scratch = [pltpu.VMEM((2 * N_HEADS, V_ROWS, TQ_A), F32)]
    if not bounded:
        scratch.append(pltpu.VMEM((2 * N_HEADS, 1, TQ_A), F32))
    grid_spec = pltpu.PrefetchScalarGridSpec(
        num_scalar_prefetch=4,
        grid=(b, nq, nk),
        in_specs=[
            _smem(), lam_spec, lam_spec, lam_spec, lam_spec,
            pl.BlockSpec((1, 2 * N_HEADS, HEAD_DIM, TQ_A), lambda i, q, k, *_: (i, 0, 0, q)),
            pl.BlockSpec((1, 2 * N_HEADS, TK_A, HEAD_DIM), lambda i, q, k, *_: (i, 0, k, 0)),
            pl.BlockSpec((1, N_HEADS, 1, V_ROWS, TK_A), lambda i, q, k, *_: (i, 0, k, 0, 0)),
            pl.BlockSpec((1, 1, TQ_A), lambda i, q, k, *_: (i, 0, q)),
            pl.BlockSpec((1, TK_A, 1), lambda i, q, k, *_: (i, k, 0)),
            pl.BlockSpec((N_HEADS, 1, LANES), lambda i, q, k, *_: (0, 0, 0)),
            pl.BlockSpec((HEAD_DIM, 1), lambda i, q, k, *_: (0, 0)),
        ],
        out_specs=pl.BlockSpec((1, TQ_A, MIX_WIDTH), lambda i, q, k, *_: (i, q, 0)),
        scratch_shapes=scratch,
    )
    return pl.pallas_call(
        functools.partial(_diff_attn_kernel, lam_init=lam_init, bounded=bounded),
        grid_spec=grid_spec,
        out_shape=jax.ShapeDtypeStruct((b, s, MIX_WIDTH), BF16),
        compiler_params=_cparams(("parallel", "parallel", "arbitrary")),
        name="diff_attn_bounded" if bounded else "diff_attn_running_max",
    )(*ranges, far_consts, *[v.reshape(1, DIFF_QK_DIM) for v in lam_vecs],
      qx, kx, vtx, positions.reshape(b, 1, s), positions.reshape(b, s, 1), tbl_log2, gsub.reshape(HEAD_DIM, 1))


def _prep_b_kernel(q_ref, k_ref, v_ref, qm_ref, gq_ref, gk_ref, gm_ref, qt_ref, kn_ref, vt_ref, qmn_ref):
    scale = HEAD_DIM ** -0.5 * LOG2E
    n_blk = TS_PREP // WINDOW
    ones_rows = jnp.where(lax.broadcasted_iota(I32, (V_ROWS - HEAD_DIM, TS_PREP), 0) == 0, 1.0, 0.0).astype(BF16)
    for h in range(N_HEADS):
        g, hg = divmod(h, GQA_GROUP)
        q = q_ref[0, :, h * HEAD_DIM:(h + 1) * HEAD_DIM]
        qt = (((q * _group_rms_scale(q, HEAD_DIM)) * gq_ref[...]) * scale).T.astype(BF16)
        for n in range(n_blk):
            qt_ref[0, n, g, :, hg * WINDOW:(hg + 1) * WINDOW] = qt[:, n * WINDOW:(n + 1) * WINDOW]
    for g in range(N_KV_HEADS):
        sl = slice(g * HEAD_DIM, (g + 1) * HEAD_DIM)
        k = k_ref[0, :, sl]
        kn_ref[0, g] = ((k * _group_rms_scale(k, HEAD_DIM)) * gk_ref[...]).astype(BF16)
        vt = jnp.concatenate([v_ref[0, :, sl].T.astype(BF16), ones_rows], axis=0)
        for n in range(n_blk):
            vt_ref[0, g, n] = vt[:, n * WINDOW:(n + 1) * WINDOW]
        qm = qm_ref[0, :, sl]
        qmn_ref[0, :, sl] = (((qm * _group_rms_scale(qm, HEAD_DIM)) * gm_ref[...]) * scale).astype(BF16)


def _prep_b(proj, gq, gk, gm):
    b, s, _ = proj.shape
    kblk = MIX_WIDTH // KV_WIDTH
    n_blk = TS_PREP // WINDOW
    g = lambda v: v.reshape(1, HEAD_DIM)
    return pl.pallas_call(
        _prep_b_kernel,
        grid=(b, s // TS_PREP),
        in_specs=[
            pl.BlockSpec((1, TS_PREP, MIX_WIDTH), lambda i, j: (i, j, 0)),
            pl.BlockSpec((1, TS_PREP, KV_WIDTH), lambda i, j: (i, j, kblk)),
            pl.BlockSpec((1, TS_PREP, KV_WIDTH), lambda i, j: (i, j, kblk + 1)),
            pl.BlockSpec((1, TS_PREP, MEM_WIDTH), lambda i, j: (i, j, kblk + 2)),
            pl.BlockSpec((1, HEAD_DIM), lambda i, j: (0, 0)),
            pl.BlockSpec((1, HEAD_DIM), lambda i, j: (0, 0)),
            pl.BlockSpec((1, HEAD_DIM), lambda i, j: (0, 0)),
        ],
        out_specs=[
            pl.BlockSpec((1, n_blk, N_KV_HEADS, HEAD_DIM, GQA_GROUP * WINDOW), lambda i, j: (i, j, 0, 0, 0)),
            pl.BlockSpec((1, N_KV_HEADS, TS_PREP, HEAD_DIM), lambda i, j: (i, 0, j, 0)),
            pl.BlockSpec((1, N_KV_HEADS, n_blk, V_ROWS, WINDOW), lambda i, j: (i, 0, j, 0, 0)),
            pl.BlockSpec((1, TS_PREP, MEM_WIDTH), lambda i, j: (i, j, 0)),
        ],
        out_shape=[
            jax.ShapeDtypeStruct((b, s // WINDOW, N_KV_HEADS, HEAD_DIM, GQA_GROUP * WINDOW), BF16),
            jax.ShapeDtypeStruct((b, N_KV_HEADS, s, HEAD_DIM), BF16),
            jax.ShapeDtypeStruct((b, N_KV_HEADS, s // WINDOW, V_ROWS, WINDOW), BF16),
            jax.ShapeDtypeStruct((b, s, MEM_WIDTH), BF16),
        ],
        compiler_params=_cparams(("parallel", "parallel")),
        name="prep_b",
    )(proj, proj, proj, proj, g(gq), g(gk), g(gm))


def _win_attn_kernel(sink_ref, qt_ref, k_ref, vt_ref, posq_ref, posk_ref, tbl_ref, o_ref):
    n_blocks = posq_ref.shape[1]
    n_sub = o_ref.shape[1] // WINDOW
    kw = 3 * WINDOW
    lane3 = lax.broadcasted_iota(I32, (1, GQA_GROUP * WINDOW), 1)

    def window(nl):
        n = pl.program_id(1) * n_sub + nl
        nb = jnp.clip(n - 1, 0, n_blocks - 3)
        return n, nb, pl.multiple_of(nb * WINDOW, WINDOW)

    def logits(nl, g):
        start = window(nl)[2]
        return jnp.dot(k_ref[0, g, pl.ds(start, kw), :], qt_ref[0, nl, g], preferred_element_type=F32)

    def masked_bucket(nl):
        n, _, start = window(nl)
        rel = posk_ref[0, pl.ds(start, kw), :] - posq_ref[0, pl.ds(n, 1), :]
        ki = start + lax.broadcasted_iota(I32, (kw, WINDOW), 0)
        qi = n * WINDOW + lax.broadcasted_iota(I32, (kw, WINDOW), 1)
        return jnp.where(jnp.abs(ki - qi) <= WINDOW, _t5_bucket(rel), N_BUCKETS)

    def finish(nl, g, s, bucket):
        nb = window(nl)[1]
        heads = range(g * GQA_GROUP, (g + 1) * GQA_GROUP)
        z = s + jnp.concatenate([_table_lookup(tbl_ref[h], bucket) for h in heads], axis=1)
        sink = jnp.where(lane3 < WINDOW, sink_ref[heads[0]],
                         jnp.where(lane3 < 2 * WINDOW, sink_ref[heads[1]], sink_ref[heads[2]]))
        m = jnp.maximum(jnp.max(z, axis=0, keepdims=True), sink)
        p = jnp.exp2(z - m).astype(BF16)
        vt = jnp.concatenate([vt_ref[0, g, nb + t] for t in range(3)], axis=1)
        acc = jnp.dot(vt, p, preferred_element_type=F32)
        o_t = acc[:HEAD_DIM] / (acc[HEAD_DIM:HEAD_DIM + 1] + jnp.exp2(sink - m))
        for hg, h in enumerate(heads):
            o_ref[0, nl * WINDOW:(nl + 1) * WINDOW, h * HEAD_DIM:(h + 1) * HEAD_DIM] = (
                o_t[:, hg * WINDOW:(hg + 1) * WINDOW].T.astype(BF16))

    chains = [(nl, g) for nl in range(n_sub) for g in range(N_KV_HEADS)]
    s_next = logits(*chains[0])
    bucket = None
    for idx, (nl, g) in enumerate(chains):
        s = s_next
        if idx + 1 < len(chains):
            s_next = logits(*chains[idx + 1])
        if g == 0:
            bucket = masked_bucket(nl)
        finish(nl, g, s, bucket)


def _win_attention(qt, kn, vt, positions, tbl_log2, sink_log2):
    b, _, s, _ = kn.shape
    n_blocks = s // WINDOW
    n_sub = TQ_B // WINDOW
    return pl.pallas_call(
        _win_attn_kernel,
        grid=(b, s // TQ_B),
        in_specs=[
            _smem(),
            pl.BlockSpec((1, n_sub, N_KV_HEADS, HEAD_DIM, GQA_GROUP * WINDOW), lambda i, j: (i, j, 0, 0, 0)),
            pl.BlockSpec((1, N_KV_HEADS, s, HEAD_DIM), lambda i, j: (i, 0, 0, 0)),
            pl.BlockSpec((1, N_KV_HEADS, n_blocks, V_ROWS, WINDOW), lambda i, j: (i, 0, 0, 0, 0)),
            pl.BlockSpec((1, n_blocks, WINDOW), lambda i, j: (i, 0, 0)),
            pl.BlockSpec((1, s, 1), lambda i, j: (i, 0, 0)),
            pl.BlockSpec((N_HEADS, 1, LANES), lambda i, j: (0, 0, 0)),
        ],
        out_specs=pl.BlockSpec((1, TQ_B, MIX_WIDTH), lambda i, j: (i, j, 0)),
        out_shape=jax.ShapeDtypeStruct((b, s, MIX_WIDTH), BF16),
        compiler_params=_cparams(("parallel", "parallel")),
        name="win_attn",
    )(sink_log2, qt, kn, vt, positions.reshape(b, n_blocks, WINDOW), positions.reshape(b, s, 1), tbl_log2)


def _mem_attn_kernel(q_ref, k_ref, v_ref, o_ref):
    for h in range(N_MEM_HEADS):
        sl = slice(h * HEAD_DIM, (h + 1) * HEAD_DIM)
        s = lax.dot_general(q_ref[0, :, sl], k_ref[0, :, sl], (((1,), (1,)), ((), ())), preferred_element_type=F32)
        e = jnp.exp2(s - jnp.max(s, axis=-1, keepdims=True))
        p = (e / jnp.sum(e, axis=-1, keepdims=True)).astype(BF16)
        o_ref[0, :, sl] = jnp.dot(p, v_ref[0, :, sl], preferred_element_type=F32).astype(BF16)


def _mem_attention(qmn, k_m, v_m):
    b, s, _ = qmn.shape
    return pl.pallas_call(
        _mem_attn_kernel,
        grid=(b, s // TQ_MEM),
        in_specs=[
            pl.BlockSpec((1, TQ_MEM, MEM_WIDTH), lambda i, j: (i, j, 0)),
            pl.BlockSpec((1, MEM_LEN, MEM_WIDTH), lambda i, j: (i, 0, 0)),
            pl.BlockSpec((1, MEM_LEN, MEM_WIDTH), lambda i, j: (i, 0, 0)),
        ],
        out_specs=pl.BlockSpec((1, TQ_MEM, MEM_WIDTH), lambda i, j: (i, j, 0)),
        out_shape=jax.ShapeDtypeStruct((b, s, MEM_WIDTH), BF16),
        compiler_params=_cparams(("parallel", "parallel")),
        name="mem_attn",
    )(qmn, k_m, v_m)


def _out_proj_kernel(x_ref, o_ref, om_ref, wo_ref, wm_ref, y_ref):
    y_ref[...] = (x_ref[...]
                  + jnp.dot(o_ref[...], wo_ref[...].astype(BF16), preferred_element_type=F32)
                  + jnp.dot(om_ref[...], wm_ref[...].astype(BF16), preferred_element_type=F32))


def _out_proj(x2d, o2d, om2d, w_stack, layer):
    m, d = x2d.shape
    return pl.pallas_call(
        _out_proj_kernel,
        grid=(m // TM_OUT, d // TN_OUT),
        in_specs=[
            pl.BlockSpec((TM_OUT, TN_OUT), lambda i, j: (i, j)),
            pl.BlockSpec((TM_OUT, MIX_WIDTH), lambda i, j: (i, 0)),
            pl.BlockSpec((TM_OUT, MEM_WIDTH), lambda i, j: (i, 0)),
            pl.BlockSpec((None, MIX_WIDTH, TN_OUT), lambda i, j: (layer, 0, j)),
            pl.BlockSpec((None, MEM_WIDTH, TN_OUT), lambda i, j: (layer, MIX_WIDTH // MEM_WIDTH, j)),
        ],
        out_specs=pl.BlockSpec((TM_OUT, TN_OUT), lambda i, j: (i, j)),
        out_shape=jax.ShapeDtypeStruct((m, d), F32),
        compiler_params=_cparams(("parallel", "parallel")),
        name="out_proj",
    )(x2d, o2d, om2d, w_stack, w_stack)


def _mlp_kernel(x_ref, g_ref, wu_ref, wd_ref, y_ref, h_ref):
    f = pl.program_id(1)

    @pl.when(f == 0)
    def _():
        x = x_ref[...]
        r = lax.rsqrt(jnp.mean(x * x, axis=-1, keepdims=True) + RMS_EPS)
        h_ref[...] = ((x * r) * g_ref[...]).astype(BF16)
        y_ref[...] = x

    u = jnp.maximum(jnp.dot(h_ref[...], wu_ref[...].astype(BF16), preferred_element_type=F32), 0.0)
    y_ref[...] += jnp.dot((u * u).astype(BF16), wd_ref[...].astype(BF16), preferred_element_type=F32)


def _mlp(x2d, gain, wu_stack, wd_stack, layer):
    m, d = x2d.shape
    return pl.pallas_call(
        _mlp_kernel,
        grid=(m // TM_MLP, D_FF // TF_MLP),
        in_specs=[
            pl.BlockSpec((TM_MLP, d), lambda i, f: (i, 0), pipeline_mode=pl.Buffered(1)),
            pl.BlockSpec((1, d), lambda i, f: (0, 0)),
            pl.BlockSpec((None, d, TF_MLP), lambda i, f: (layer, 0, f)),
            pl.BlockSpec((None, TF_MLP, d), lambda i, f: (layer, f, 0)),
        ],
        out_specs=pl.BlockSpec((TM_MLP, d), lambda i, f: (i, 0)),
        out_shape=jax.ShapeDtypeStruct((m, d), F32),
        scratch_shapes=[pltpu.VMEM((TM_MLP, d), BF16)],
        compiler_params=_cparams(("parallel", "arbitrary")),
        name="mlp",
    )(x2d, gain.reshape(1, d), wu_stack, wd_stack)


def _bias_tables(rel_bias):
    t = (rel_bias.astype(F32) * LOG2E).T
    tbl = jnp.zeros((N_HEADS, 1, LANES), F32).at[:, 0, :N_BUCKETS].set(t).at[:, 0, N_BUCKETS].set(NEG_BIG)
    far = jnp.stack([t[:, HALF_BUCKETS - 1], t[:, N_BUCKETS - 1]])
    return tbl, far


def _diff_logit_shift(gq, gk, rel_bias):
    qk = 1.01 * DIFF_QK_DIM * jnp.max(jnp.abs(gq)) * jnp.max(jnp.abs(gk)) * (DIFF_QK_DIM ** -0.5 * LOG2E)
    return jnp.ceil(qk + jnp.max(jnp.abs(rel_bias)) * LOG2E).astype(F32)


def kernel(x, mem, positions, rel_bias, norm_attn, norm_mem, norm_mlp, w_in_a, a_q_norm, a_k_norm, a_lambda_q1, a_lambda_k1, a_lambda_q2, a_lambda_k2, a_subln, w_in_b, b_q_norm, b_k_norm, b_sink, w_mem_kv, m_q_norm, m_k_norm, w_out, w_up, w_down):
    b, s, d = x.shape
    depth = norm_attn.shape[0]
    tbl_log2, far_log2 = _bias_tables(rel_bias)
    x2d = x.reshape(b * s, d)
    for i in range(depth):
        j = i // 2
        k_m, v_m = _mem_kv(mem, norm_mem[i], w_mem_kv, i, m_k_norm[i])
        if i % 2 == 0:
            proj = _norm_proj(x2d, norm_attn[i], w_in_a, j).reshape(b, s, -1)
            shift = _diff_logit_shift(a_q_norm[j], a_k_norm[j], rel_bias)
            bounded = shift <= SHIFT_LIMIT
            qx, kx, vtx, qmn = _prep_a(proj, a_q_norm[j], a_k_norm[j], m_q_norm[i], jnp.where(bounded, -shift, 0.0))
            lam_init = 0.8 - 0.6 * math.exp(-0.3 * i)
            attn = functools.partial(
                _diff_attention, qx, kx, vtx, positions, tbl_log2,
                lam_vecs=(a_lambda_q1[j], a_lambda_k1[j], a_lambda_q2[j], a_lambda_k2[j]), gsub=a_subln[j],
                lam_init=lam_init)
            o = lax.cond(bounded,
                         lambda: attn(far_consts=jnp.exp2(far_log2), bounded=True),
                         lambda: attn(far_consts=far_log2, bounded=False))
        else:
            proj = _norm_proj(x2d, norm_attn[i], w_in_b, j).reshape(b, s, -1)
            qt, kn, vt, qmn = _prep_b(proj, b_q_norm[j], b_k_norm[j], m_q_norm[i])
            o = _win_attention(qt, kn, vt, positions, tbl_log2, b_sink[j].astype(F32) * LOG2E)
        o_m = _mem_attention(qmn, k_m, v_m)
        x2d = _out_proj(x2d, o.reshape(b * s, MIX_WIDTH), o_m.reshape(b * s, MEM_WIDTH), w_out, i)
        x2d = _mlp(x2d, norm_mlp[i], w_up, w_down, i)
    return x2d.reshape(b, s, d)
```

```python
import functools
import math

import jax
import jax.numpy as jnp
from jax import lax
from jax.experimental import pallas as pl
from jax.experimental.pallas import tpu as pltpu

F32 = jnp.float32
BF16 = jnp.bfloat16
I32 = jnp.int32

D_MODEL = 2048
N_HEADS = 12
HEAD_DIM = 128
DIFF_QK_DIM = 64
N_KV_HEADS = 4
GQA_GROUP = 3
MIX_WIDTH = N_HEADS * HEAD_DIM
KV_WIDTH = N_KV_HEADS * HEAD_DIM
WINDOW = 128
N_MEM_HEADS = 4
MEM_WIDTH = N_MEM_HEADS * HEAD_DIM
MEM_LEN = 256
D_FF = 4 * D_MODEL
N_BUCKETS = 32
MAX_DISTANCE = 128
RMS_EPS = 1e-6
NEG_BIG = -1e30
LOG2E = math.log2(math.e)

HALF_BUCKETS = N_BUCKETS // 2
MAX_EXACT = HALF_BUCKETS // 2
FAR_DIST = 91

ONE_COL = DIFF_QK_DIM
V_ROWS = HEAD_DIM + 16
SHIFT_LIMIT = 50.0

LANES = 128
VMEM_LIMIT = 56 * 1024 * 1024

TM_PROJ, TN_PROJ = 1024, 512
TS_PREP = 512
TQ_A, TK_A = 512, 512
TQ_B = 512
TQ_MEM = 512
TM_OUT, TN_OUT = 1024, 1024
TM_MLP, TF_MLP = 1024, 512


def _cparams(sem):
    return pltpu.CompilerParams(dimension_semantics=sem, vmem_limit_bytes=VMEM_LIMIT)


def _smem():
    return pl.BlockSpec(memory_space=pltpu.SMEM)


def _t5_bucket(rel):
    side = jnp.where(rel > 0, HALF_BUCKETS, 0)
    n = jnp.abs(rel)
    n_f = jnp.maximum(n, 1).astype(F32)
    large = MAX_EXACT + (jnp.log(n_f / MAX_EXACT) / math.log(MAX_DISTANCE / MAX_EXACT)
                         * (HALF_BUCKETS - MAX_EXACT)).astype(I32)
    large = jnp.minimum(large, HALF_BUCKETS - 1)
    return side + jnp.where(n < MAX_EXACT, n, large)


def _table_lookup(tbl_row, bucket):
    rows, cols = bucket.shape
    tb = jnp.broadcast_to(tbl_row, (rows, LANES))
    parts = [jnp.take_along_axis(tb, bucket[:, c:c + LANES], axis=1) for c in range(0, cols, LANES)]
    return parts[0] if len(parts) == 1 else jnp.concatenate(parts, axis=1)


def _group_rms_scale(x, group):
    t = x * x
    if group == LANES:
        return lax.rsqrt(jnp.mean(t, axis=-1, keepdims=True) + RMS_EPS)
    lane = lax.broadcasted_iota(I32, x.shape, 1)
    lo = lane < group
    s_lo = jnp.sum(jnp.where(lo, t, 0.0), axis=-1, keepdims=True)
    s_hi = jnp.sum(jnp.where(lo, 0.0, t), axis=-1, keepdims=True)
    return jnp.where(lo, lax.rsqrt(s_lo / group + RMS_EPS), lax.rsqrt(s_hi / group + RMS_EPS))


def _proj_kernel(x_ref, g_ref, w_ref, o_ref, h_ref):
    @pl.when(pl.program_id(1) == 0)
    def _():
        x = x_ref[...]
        r = lax.rsqrt(jnp.mean(x * x, axis=-1, keepdims=True) + RMS_EPS)
        h_ref[...] = ((x * r) * g_ref[...]).astype(BF16)

    o_ref[...] = jnp.dot(h_ref[...], w_ref[...].astype(BF16), preferred_element_type=F32)


def _norm_proj(x2d, gain, w_stack, layer):
    m, d = x2d.shape
    n = w_stack.shape[2]
    return pl.pallas_call(
        _proj_kernel,
        grid=(m // TM_PROJ, n // TN_PROJ),
        in_specs=[
            pl.BlockSpec((TM_PROJ, d), lambda i, j: (i, 0)),
            pl.BlockSpec((1, d), lambda i, j: (0, 0)),
            pl.BlockSpec((None, d, TN_PROJ), lambda i, j: (layer, 0, j)),
        ],
        out_specs=pl.BlockSpec((TM_PROJ, TN_PROJ), lambda i, j: (i, j)),
        out_shape=jax.ShapeDtypeStruct((m, n), F32),
        scratch_shapes=[pltpu.VMEM((TM_PROJ, d), BF16)],
        compiler_params=_cparams(("parallel", "arbitrary")),
        name="norm_proj",
    )(x2d, gain.reshape(1, d), w_stack)


def _mem_kv_kernel(mem_ref, g_ref, w_ref, gk_ref, k_ref, v_ref):
    x = mem_ref[0]
    r = lax.rsqrt(jnp.mean(x * x, axis=-1, keepdims=True) + RMS_EPS)
    mn = ((x * r) * g_ref[...]).astype(BF16)
    mkv = jnp.dot(mn, w_ref[...].astype(BF16), preferred_element_type=F32)
    for h in range(N_MEM_HEADS):
        kh = mkv[:, h * HEAD_DIM:(h + 1) * HEAD_DIM]
        k_ref[0, :, h * HEAD_DIM:(h + 1) * HEAD_DIM] = ((kh * _group_rms_scale(kh, HEAD_DIM)) * gk_ref[...]).astype(BF16)
    v_ref[0] = mkv[:, MEM_WIDTH:].astype(BF16)


def _mem_kv(mem, gain, w_stack, layer, gk):
    b = mem.shape[0]
    shp = jax.ShapeDtypeStruct((b, MEM_LEN, MEM_WIDTH), BF16)
    return pl.pallas_call(
        _mem_kv_kernel,
        grid=(b,),
        in_specs=[
            pl.BlockSpec((1, MEM_LEN, D_MODEL), lambda i: (i, 0, 0)),
            pl.BlockSpec((1, D_MODEL), lambda i: (0, 0)),
            pl.BlockSpec((None, D_MODEL, 2 * MEM_WIDTH), lambda i: (layer, 0, 0)),
            pl.BlockSpec((1, HEAD_DIM), lambda i: (0, 0)),
        ],
        out_specs=[pl.BlockSpec((1, MEM_LEN, MEM_WIDTH), lambda i: (i, 0, 0))] * 2,
        out_shape=[shp, shp],
        compiler_params=_cparams(("parallel",)),
        name="mem_kv",
    )(mem, gain.reshape(1, D_MODEL), w_stack, gk.reshape(1, HEAD_DIM))


def _prep_a_kernel(negm_ref, q_ref, k_ref, v_ref, qm_ref, gq_ref, gk_ref, gm_ref, qx_ref, kx_ref, vtx_ref, qmn_ref):
    q_scale = DIFF_QK_DIM ** -0.5 * LOG2E
    m_scale = HEAD_DIM ** -0.5 * LOG2E
    lane = lax.broadcasted_iota(I32, (TS_PREP, HEAD_DIM), 1)
    neg_shift = negm_ref[0]

    def extend(x, extra):
        return jnp.where(lane < DIFF_QK_DIM, x, jnp.where(lane == ONE_COL, extra, 0.0))

    ones_rows = jnp.where(lax.broadcasted_iota(I32, (V_ROWS - HEAD_DIM, TS_PREP), 0) == 0, 1.0, 0.0).astype(BF16)
    for h in range(N_HEADS):
        sl = slice(h * HEAD_DIM, (h + 1) * HEAD_DIM)
        q = q_ref[0, :, sl]
        qn = ((q * _group_rms_scale(q, DIFF_QK_DIM)) * gq_ref[...]) * q_scale
        k = k_ref[0, :, sl]
        kn = (k * _group_rms_scale(k, DIFF_QK_DIM)) * gk_ref[...]
        for c in range(2):
            qc = qn if c == 0 else pltpu.roll(qn, DIFF_QK_DIM, axis=1)
            kc = kn if c == 0 else pltpu.roll(kn, DIFF_QK_DIM, axis=1)
            qx_ref[0, 2 * h + c] = extend(qc, neg_shift).T.astype(BF16)
            kx_ref[0, 2 * h + c] = extend(kc, 1.0).astype(BF16)
        vt = v_ref[0, :, sl].T.astype(BF16)
        vtx_ref[0, h, 0] = jnp.concatenate([vt, ones_rows], axis=0)
    for h in range(N_MEM_HEADS):
        sl = slice(h * HEAD_DIM, (h + 1) * HEAD_DIM)
        qm = qm_ref[0, :, sl]
        qmn_ref[0, :, sl] = (((qm * _group_rms_scale(qm, HEAD_DIM)) * gm_ref[...]) * m_scale).astype(BF16)


def _prep_a(proj, gq, gk, gm, neg_shift):
    assert TS_PREP == TK_A
    b, s, _ = proj.shape
    gq2 = jnp.concatenate([gq, gq]).reshape(1, HEAD_DIM)
    gk2 = jnp.concatenate([gk, gk]).reshape(1, HEAD_DIM)
    wblk = MIX_WIDTH // MEM_WIDTH
    return pl.pallas_call(
        _prep_a_kernel,
        grid=(b, s // TS_PREP),
        in_specs=[
            _smem(),
            pl.BlockSpec((1, TS_PREP, MIX_WIDTH), lambda i, j: (i, j, 0)),
            pl.BlockSpec((1, TS_PREP, MIX_WIDTH), lambda i, j: (i, j, 1)),
            pl.BlockSpec((1, TS_PREP, MIX_WIDTH), lambda i, j: (i, j, 2)),
            pl.BlockSpec((1, TS_PREP, MEM_WIDTH), lambda i, j: (i, j, 3 * wblk)),
            pl.BlockSpec((1, HEAD_DIM), lambda i, j: (0, 0)),
            pl.BlockSpec((1, HEAD_DIM), lambda i, j: (0, 0)),
            pl.BlockSpec((1, HEAD_DIM), lambda i, j: (0, 0)),
        ],
        out_specs=[
            pl.BlockSpec((1, 2 * N_HEADS, HEAD_DIM, TS_PREP), lambda i, j: (i, 0, 0, j)),
            pl.BlockSpec((1, 2 * N_HEADS, TS_PREP, HEAD_DIM), lambda i, j: (i, 0, j, 0)),
            pl.BlockSpec((1, N_HEADS, 1, V_ROWS, TK_A), lambda i, j: (i, 0, j, 0, 0)),
            pl.BlockSpec((1, TS_PREP, MEM_WIDTH), lambda i, j: (i, j, 0)),
        ],
        out_shape=[
            jax.ShapeDtypeStruct((b, 2 * N_HEADS, HEAD_DIM, s), BF16),
            jax.ShapeDtypeStruct((b, 2 * N_HEADS, s, HEAD_DIM), BF16),
            jax.ShapeDtypeStruct((b, N_HEADS, s // TK_A, V_ROWS, TK_A), BF16),
            jax.ShapeDtypeStruct((b, s, MEM_WIDTH), BF16),
        ],
        compiler_params=_cparams(("parallel", "parallel")),
        name="prep_a",
    )(neg_shift.reshape(1), proj, proj, proj, proj, gq2, gk2, gm.reshape(1, HEAD_DIM))


def _diff_attn_kernel(qlo_ref, qhi_ref, klo_ref, khi_ref,
                      far_ref, lq1_ref, lk1_ref, lq2_ref, lk2_ref,
                      qx_ref, kx_ref, vtx_ref, posq_ref, posk_ref, tbl_ref, gsub_ref,
                      *rest, lam_init, bounded, n_cast):
    w_f32_refs, o_ref, w_bf16_refs = rest[:n_cast], rest[n_cast], rest[n_cast + 1:2 * n_cast + 1]
    acc_ref, m_scratch = rest[2 * n_cast + 1], rest[2 * n_cast + 2:]
    b, iq, j = pl.program_id(0), pl.program_id(1), pl.program_id(2)
    pos_far = klo_ref[b, j] - qhi_ref[b, iq] >= FAR_DIST
    neg_far = khi_ref[b, j] - qlo_ref[b, iq] <= -FAR_DIST
    far = jnp.logical_or(pos_far, neg_far)

    @pl.when(j == 0)
    def _():
        acc_ref[...] = jnp.zeros(acc_ref.shape, F32)
        if not bounded:
            m_scratch[0][...] = jnp.full(m_scratch[0].shape, NEG_BIG, F32)

    def logits(hc):
        return jnp.dot(kx_ref[0, hc], qx_ref[0, hc], preferred_element_type=F32)

    def accumulate(hc, s, far_const):
        h = hc // 2
        if bounded:
            pv = jnp.dot(vtx_ref[0, h, 0], jnp.exp2(s).astype(BF16), preferred_element_type=F32)
            acc_ref[hc] += pv if far_const is None else far_const * pv
        else:
            m_ref = m_scratch[0]
            off = 0.0 if far_const is None else far_const
            m_old = m_ref[hc]
            m_new = jnp.maximum(m_old, jnp.max(s, axis=0, keepdims=True) + off)
            p = jnp.exp2(s - (m_new - off)).astype(BF16)
            acc_ref[hc] = (jnp.exp2(m_old - m_new) * acc_ref[hc]
                           + jnp.dot(vtx_ref[0, h, 0], p, preferred_element_type=F32))
            m_ref[hc] = m_new

    def all_heads(bias_of_head, far_const_of_head):
        for src, dst in zip(w_f32_refs, w_bf16_refs):
            dst[...] = src[...].astype(BF16)
        s_next = logits(0)
        bias_next = None if bias_of_head is None else bias_of_head(0)
        for hc in range(2 * N_HEADS):
            h, c = divmod(hc, 2)
            s = s_next
            if hc + 1 < 2 * N_HEADS:
                s_next = logits(hc + 1)
            if bias_of_head is not None:
                if c == 0:
                    bias = bias_next
                    if h + 1 < N_HEADS:
                        bias_next = bias_of_head(h + 1)
                s = s + bias
            accumulate(hc, s, None if far_const_of_head is None else far_const_of_head(h))

    @pl.when(far)
    def _():
        all_heads(None, lambda h: jnp.where(pos_far, far_ref[1, h], far_ref[0, h]))

    @pl.when(jnp.logical_not(far))
    def _():
        bucket = []

        def bias_of_head(h):
            if not bucket:
                bucket.append(_t5_bucket(posk_ref[0] - posq_ref[0]))
            return _table_lookup(tbl_ref[h], bucket[0])

        all_heads(bias_of_head, None)

    @pl.when(j == pl.num_programs(2) - 1)
    def _():
        lam = (jnp.exp(jnp.sum(lq1_ref[...] * lk1_ref[...], axis=-1, keepdims=True))
               - jnp.exp(jnp.sum(lq2_ref[...] * lk2_ref[...], axis=-1, keepdims=True)) + lam_init)
        for h in range(N_HEADS):
            a0, a1 = acc_ref[2 * h], acc_ref[2 * h + 1]
            o_t = (a0[:HEAD_DIM] / a0[HEAD_DIM:HEAD_DIM + 1]
                   - lam * (a1[:HEAD_DIM] / a1[HEAD_DIM:HEAD_DIM + 1]))
            r = lax.rsqrt(jnp.mean(o_t * o_t, axis=0, keepdims=True) + RMS_EPS)
            o_t = ((o_t * r) * gsub_ref[...]) * (1.0 - lam_init)
            o_ref[0, :, h * HEAD_DIM:(h + 1) * HEAD_DIM] = o_t.T.astype(BF16)


def _diff_attention(qx, kx, vtx, positions, tbl_log2, far_consts, lam_vecs, gsub, lam_init, bounded, cast_srcs):
    b, _, s, _ = kx.shape
    nq, nk = s // TQ_A, s // TK_A
    pq = positions.reshape(b, nq, TQ_A)
    pk = positions.reshape(b, nk, TK_A)
    ranges = (pq.min(-1), pq.max(-1), pk.min(-1), pk.max(-1))
    lam_spec = pl.BlockSpec((1, DIFF_QK_DIM), lambda i, q, k, *_: (0, 0))
    scratch = [pltpu.VMEM((2 * N_HEADS, V_ROWS, TQ_A), F32)]
    if not bounded:
        scratch.append(pltpu.VMEM((2 * N_HEADS, 1, TQ_A), F32))
    n_steps = b * nq * nk
    assert all(w.shape[0] % (16 * n_steps) == 0 for w in cast_srcs)
    cast_specs = [pl.BlockSpec((w.shape[0] // n_steps, w.shape[1]), lambda i, q, k, *_: ((i * nq + q) * nk + k, 0))
                  for w in cast_srcs]
    grid_spec = pltpu.PrefetchScalarGridSpec(
        num_scalar_prefetch=4,
        grid=(b, nq, nk),
        in_specs=[
            _smem(), lam_spec, lam_spec, lam_spec, lam_spec,
            pl.BlockSpec((1, 2 * N_HEADS, HEAD_DIM, TQ_A), lambda i, q, k, *_: (i, 0, 0, q)),
            pl.BlockSpec((1, 2 * N_HEADS, TK_A, HEAD_DIM), lambda i, q, k, *_: (i, 0, k, 0)),
            pl.BlockSpec((1, N_HEADS, 1, V_ROWS, TK_A), lambda i, q, k, *_: (i, 0, k, 0, 0)),
            pl.BlockSpec((1, 1, TQ_A), lambda i, q, k, *_: (i, 0, q)),
            pl.BlockSpec((1, TK_A, 1), lambda i, q, k, *_: (i, k, 0)),
            pl.BlockSpec((N_HEADS, 1, LANES), lambda i, q, k, *_: (0, 0, 0)),
            pl.BlockSpec((HEAD_DIM, 1), lambda i, q, k, *_: (0, 0)),
            *cast_specs,
        ],
        out_specs=[pl.BlockSpec((1, TQ_A, MIX_WIDTH), lambda i, q, k, *_: (i, q, 0)), *cast_specs],
        scratch_shapes=scratch,
    )
    outs = pl.pallas_call(
        functools.partial(_diff_attn_kernel, lam_init=lam_init, bounded=bounded, n_cast=len(cast_srcs)),
        grid_spec=grid_spec,
        out_shape=[jax.ShapeDtypeStruct((b, s, MIX_WIDTH), BF16),
                   *[jax.ShapeDtypeStruct(w.shape, BF16) for w in cast_srcs]],
        compiler_params=_cparams(("arbitrary", "arbitrary", "arbitrary")),
        name="diff_attn_bounded" if bounded else "diff_attn_running_max",
    )(*ranges, far_consts, *[v.reshape(1, DIFF_QK_DIM) for v in lam_vecs],
      qx, kx, vtx, positions.reshape(b, 1, s), positions.reshape(b, s, 1), tbl_log2, gsub.reshape(HEAD_DIM, 1),
      *cast_srcs)
    return outs[0], tuple(outs[1:])


def _prep_b_kernel(q_ref, k_ref, v_ref, qm_ref, gq_ref, gk_ref, gm_ref, qt_ref, kn_ref, vt_ref, qmn_ref):
    scale = HEAD_DIM ** -0.5 * LOG2E
    n_blk = TS_PREP // WINDOW
    ones_rows = jnp.where(lax.broadcasted_iota(I32, (V_ROWS - HEAD_DIM, TS_PREP), 0) == 0, 1.0, 0.0).astype(BF16)
    for h in range(N_HEADS):
        g, hg = divmod(h, GQA_GROUP)
        q = q_ref[0, :, h * HEAD_DIM:(h + 1) * HEAD_DIM]
        qt = (((q * _group_rms_scale(q, HEAD_DIM)) * gq_ref[...]) * scale).T.astype(BF16)
        for n in range(n_blk):
            qt_ref[0, n, g, :, hg * WINDOW:(hg + 1) * WINDOW] = qt[:, n * WINDOW:(n + 1) * WINDOW]
    for g in range(N_KV_HEADS):
        sl = slice(g * HEAD_DIM, (g + 1) * HEAD_DIM)
        k = k_ref[0, :, sl]
        kn_ref[0, g] = ((k * _group_rms_scale(k, HEAD_DIM)) * gk_ref[...]).astype(BF16)
        vt = jnp.concatenate([v_ref[0, :, sl].T.astype(BF16), ones_rows], axis=0)
        for n in range(n_blk):
            vt_ref[0, g, n] = vt[:, n * WINDOW:(n + 1) * WINDOW]
        qm = qm_ref[0, :, sl]
        qmn_ref[0, :, sl] = (((qm * _group_rms_scale(qm, HEAD_DIM)) * gm_ref[...]) * scale).astype(BF16)


def _prep_b(proj, gq, gk, gm):
    b, s, _ = proj.shape
    kblk = MIX_WIDTH // KV_WIDTH
    n_blk = TS_PREP // WINDOW
    g = lambda v: v.reshape(1, HEAD_DIM)
    return pl.pallas_call(
        _prep_b_kernel,
        grid=(b, s // TS_PREP),
        in_specs=[
            pl.BlockSpec((1, TS_PREP, MIX_WIDTH), lambda i, j: (i, j, 0)),
            pl.BlockSpec((1, TS_PREP, KV_WIDTH), lambda i, j: (i, j, kblk)),
            pl.BlockSpec((1, TS_PREP, KV_WIDTH), lambda i, j: (i, j, kblk + 1)),
            pl.BlockSpec((1, TS_PREP, MEM_WIDTH), lambda i, j: (i, j, kblk + 2)),
            pl.BlockSpec((1, HEAD_DIM), lambda i, j: (0, 0)),
            pl.BlockSpec((1, HEAD_DIM), lambda i, j: (0, 0)),
            pl.BlockSpec((1, HEAD_DIM), lambda i, j: (0, 0)),
        ],
        out_specs=[
            pl.BlockSpec((1, n_blk, N_KV_HEADS, HEAD_DIM, GQA_GROUP * WINDOW), lambda i, j: (i, j, 0, 0, 0)),
            pl.BlockSpec((1, N_KV_HEADS, TS_PREP, HEAD_DIM), lambda i, j: (i, 0, j, 0)),
            pl.BlockSpec((1, N_KV_HEADS, n_blk, V_ROWS, WINDOW), lambda i, j: (i, 0, j, 0, 0)),
            pl.BlockSpec((1, TS_PREP, MEM_WIDTH), lambda i, j: (i, j, 0)),
        ],
        out_shape=[
            jax.ShapeDtypeStruct((b, s // WINDOW, N_KV_HEADS, HEAD_DIM, GQA_GROUP * WINDOW), BF16),
            jax.ShapeDtypeStruct((b, N_KV_HEADS, s, HEAD_DIM), BF16),
            jax.ShapeDtypeStruct((b, N_KV_HEADS, s // WINDOW, V_ROWS, WINDOW), BF16),
            jax.ShapeDtypeStruct((b, s, MEM_WIDTH), BF16),
        ],
        compiler_params=_cparams(("parallel", "parallel")),
        name="prep_b",
    )(proj, proj, proj, proj, g(gq), g(gk), g(gm))


def _win_attn_kernel(sink_ref, qt_ref, k_ref, vt_ref, posq_ref, posk_ref, tbl_ref, o_ref):
    n_blocks = posq_ref.shape[1]
    n_sub = o_ref.shape[1] // WINDOW
    kw = 3 * WINDOW
    lane3 = lax.broadcasted_iota(I32, (1, GQA_GROUP * WINDOW), 1)

    def window(nl):
        n = pl.program_id(1) * n_sub + nl
        nb = jnp.clip(n - 1, 0, n_blocks - 3)
        return n, nb, pl.multiple_of(nb * WINDOW, WINDOW)

    def logits(nl, g):
        start = window(nl)[2]
        return jnp.dot(k_ref[0, g, pl.ds(start, kw), :], qt_ref[0, nl, g], preferred_element_type=F32)

    def masked_bucket(nl):
        n, _, start = window(nl)
        rel = posk_ref[0, pl.ds(start, kw), :] - posq_ref[0, pl.ds(n, 1), :]
        ki = start + lax.broadcasted_iota(I32, (kw, WINDOW), 0)
        qi = n * WINDOW + lax.broadcasted_iota(I32, (kw, WINDOW), 1)
        return jnp.where(jnp.abs(ki - qi) <= WINDOW, _t5_bucket(rel), N_BUCKETS)

    def finish(nl, g, s, bucket):
        nb = window(nl)[1]
        heads = range(g * GQA_GROUP, (g + 1) * GQA_GROUP)
        z = s + jnp.concatenate([_table_lookup(tbl_ref[h], bucket) for h in heads], axis=1)
        sink = jnp.where(lane3 < WINDOW, sink_ref[heads[0]],
                         jnp.where(lane3 < 2 * WINDOW, sink_ref[heads[1]], sink_ref[heads[2]]))
        m = jnp.maximum(jnp.max(z, axis=0, keepdims=True), sink)
        p = jnp.exp2(z - m).astype(BF16)
        vt = jnp.concatenate([vt_ref[0, g, nb + t] for t in range(3)], axis=1)
        acc = jnp.dot(vt, p, preferred_element_type=F32)
        o_t = acc[:HEAD_DIM] / (acc[HEAD_DIM:HEAD_DIM + 1] + jnp.exp2(sink - m))
        for hg, h in enumerate(heads):
            o_ref[0, nl * WINDOW:(nl + 1) * WINDOW, h * HEAD_DIM:(h + 1) * HEAD_DIM] = (
                o_t[:, hg * WINDOW:(hg + 1) * WINDOW].T.astype(BF16))

    chains = [(nl, g) for nl in range(n_sub) for g in range(N_KV_HEADS)]
    s_next = logits(*chains[0])
    bucket = None
    for idx, (nl, g) in enumerate(chains):
        s = s_next
        if idx + 1 < len(chains):
            s_next = logits(*chains[idx + 1])
        if g == 0:
            bucket = masked_bucket(nl)
        finish(nl, g, s, bucket)


def _win_attention(qt, kn, vt, positions, tbl_log2, sink_log2):
    b, _, s, _ = kn.shape
    n_blocks = s // WINDOW
    n_sub = TQ_B // WINDOW
    return pl.pallas_call(
        _win_attn_kernel,
        grid=(b, s // TQ_B),
        in_specs=[
            _smem(),
            pl.BlockSpec((1, n_sub, N_KV_HEADS, HEAD_DIM, GQA_GROUP * WINDOW), lambda i, j: (i, j, 0, 0, 0)),
            pl.BlockSpec((1, N_KV_HEADS, s, HEAD_DIM), lambda i, j: (i, 0, 0, 0)),
            pl.BlockSpec((1, N_KV_HEADS, n_blocks, V_ROWS, WINDOW), lambda i, j: (i, 0, 0, 0, 0)),
            pl.BlockSpec((1, n_blocks, WINDOW), lambda i, j: (i, 0, 0)),
            pl.BlockSpec((1, s, 1), lambda i, j: (i, 0, 0)),
            pl.BlockSpec((N_HEADS, 1, LANES), lambda i, j: (0, 0, 0)),
        ],
        out_specs=pl.BlockSpec((1, TQ_B, MIX_WIDTH), lambda i, j: (i, j, 0)),
        out_shape=jax.ShapeDtypeStruct((b, s, MIX_WIDTH), BF16),
        compiler_params=_cparams(("parallel", "parallel")),
        name="win_attn",
    )(sink_log2, qt, kn, vt, positions.reshape(b, n_blocks, WINDOW), positions.reshape(b, s, 1), tbl_log2)


def _mem_attn_kernel(q_ref, k_ref, v_ref, o_ref):
    for h in range(N_MEM_HEADS):
        sl = slice(h * HEAD_DIM, (h + 1) * HEAD_DIM)
        s = lax.dot_general(q_ref[0, :, sl], k_ref[0, :, sl], (((1,), (1,)), ((), ())), preferred_element_type=F32)
        e = jnp.exp2(s - jnp.max(s, axis=-1, keepdims=True))
        p = (e / jnp.sum(e, axis=-1, keepdims=True)).astype(BF16)
        o_ref[0, :, sl] = jnp.dot(p, v_ref[0, :, sl], preferred_element_type=F32).astype(BF16)


def _mem_attention(qmn, k_m, v_m):
    b, s, _ = qmn.shape
    return pl.pallas_call(
        _mem_attn_kernel,
        grid=(b, s // TQ_MEM),
        in_specs=[
            pl.BlockSpec((1, TQ_MEM, MEM_WIDTH), lambda i, j: (i, j, 0)),
            pl.BlockSpec((1, MEM_LEN, MEM_WIDTH), lambda i, j: (i, 0, 0)),
            pl.BlockSpec((1, MEM_LEN, MEM_WIDTH), lambda i, j: (i, 0, 0)),
        ],
        out_specs=pl.BlockSpec((1, TQ_MEM, MEM_WIDTH), lambda i, j: (i, j, 0)),
        out_shape=jax.ShapeDtypeStruct((b, s, MEM_WIDTH), BF16),
        compiler_params=_cparams(("parallel", "parallel")),
        name="mem_attn",
    )(qmn, k_m, v_m)


def _out_proj_kernel(x_ref, o_ref, om_ref, wo_ref, wm_ref, y_ref):
    y_ref[...] = (x_ref[...]
                  + jnp.dot(o_ref[...], wo_ref[...].astype(BF16), preferred_element_type=F32)
                  + jnp.dot(om_ref[...], wm_ref[...].astype(BF16), preferred_element_type=F32))


def _out_proj(x2d, o2d, om2d, w_stack, layer):
    m, d = x2d.shape
    return pl.pallas_call(
        _out_proj_kernel,
        grid=(m // TM_OUT, d // TN_OUT),
        in_specs=[
            pl.BlockSpec((TM_OUT, TN_OUT), lambda i, j: (i, j)),
            pl.BlockSpec((TM_OUT, MIX_WIDTH), lambda i, j: (i, 0)),
            pl.BlockSpec((TM_OUT, MEM_WIDTH), lambda i, j: (i, 0)),
            pl.BlockSpec((None, MIX_WIDTH, TN_OUT), lambda i, j: (layer, 0, j)),
            pl.BlockSpec((None, MEM_WIDTH, TN_OUT), lambda i, j: (layer, MIX_WIDTH // MEM_WIDTH, j)),
        ],
        out_specs=pl.BlockSpec((TM_OUT, TN_OUT), lambda i, j: (i, j)),
        out_shape=jax.ShapeDtypeStruct((m, d), F32),
        compiler_params=_cparams(("parallel", "parallel")),
        name="out_proj",
    )(x2d, o2d, om2d, w_stack, w_stack)


def _mlp_kernel(x_ref, g_ref, wu_ref, wd_ref, y_ref, h_ref):
    f = pl.program_id(1)

    @pl.when(f == 0)
    def _():
        x = x_ref[...]
        r = lax.rsqrt(jnp.mean(x * x, axis=-1, keepdims=True) + RMS_EPS)
        h_ref[...] = ((x * r) * g_ref[...]).astype(BF16)
        y_ref[...] = x

    u = jnp.maximum(jnp.dot(h_ref[...], wu_ref[...].astype(BF16), preferred_element_type=F32), 0.0)
    y_ref[...] += jnp.dot((u * u).astype(BF16), wd_ref[...].astype(BF16), preferred_element_type=F32)


def _mlp(x2d, gain, wu_stack, wd_stack, layer):
    m, d = x2d.shape
    return pl.pallas_call(
        _mlp_kernel,
        grid=(m // TM_MLP, D_FF // TF_MLP),
        in_specs=[
            pl.BlockSpec((TM_MLP, d), lambda i, f: (i, 0)),
            pl.BlockSpec((1, d), lambda i, f: (0, 0)),
            pl.BlockSpec((None, d, TF_MLP), lambda i, f: (layer, 0, f)),
            pl.BlockSpec((None, TF_MLP, d), lambda i, f: (layer, f, 0)),
        ],
        out_specs=pl.BlockSpec((TM_MLP, d), lambda i, f: (i, 0)),
        out_shape=jax.ShapeDtypeStruct((m, d), F32),
        scratch_shapes=[pltpu.VMEM((TM_MLP, d), BF16)],
        compiler_params=_cparams(("parallel", "arbitrary")),
        name="mlp",
    )(x2d, gain.reshape(1, d), wu_stack, wd_stack)


def _bias_tables(rel_bias):
    t = (rel_bias.astype(F32) * LOG2E).T
    tbl = jnp.zeros((N_HEADS, 1, LANES), F32).at[:, 0, :N_BUCKETS].set(t).at[:, 0, N_BUCKETS].set(NEG_BIG)
    far = jnp.stack([t[:, HALF_BUCKETS - 1], t[:, N_BUCKETS - 1]])
    return tbl, far


def _diff_logit_shift(gq, gk, rel_bias):
    qk = 1.01 * DIFF_QK_DIM * jnp.max(jnp.abs(gq)) * jnp.max(jnp.abs(gk)) * (DIFF_QK_DIM ** -0.5 * LOG2E)
    return jnp.ceil(qk + jnp.max(jnp.abs(rel_bias)) * LOG2E).astype(F32)


def kernel(x, mem, positions, rel_bias, norm_attn, norm_mem, norm_mlp, w_in_a, a_q_norm, a_k_norm, a_lambda_q1, a_lambda_k1, a_lambda_q2, a_lambda_k2, a_subln, w_in_b, b_q_norm, b_k_norm, b_sink, w_mem_kv, m_q_norm, m_k_norm, w_out, w_up, w_down):
    b, s, d = x.shape
    depth = norm_attn.shape[0]
    tbl_log2, far_log2 = _bias_tables(rel_bias)
    x2d = x.reshape(b * s, d)
    later_w = (w_out, w_up, w_down, w_in_b)
    for i in range(depth):
        j = i // 2
        k_m, v_m = _mem_kv(mem, norm_mem[i], w_mem_kv, i, m_k_norm[i])
        if i % 2 == 0:
            proj = _norm_proj(x2d, norm_attn[i], w_in_a, j).reshape(b, s, -1)
            shift = _diff_logit_shift(a_q_norm[j], a_k_norm[j], rel_bias)
            bounded = shift <= SHIFT_LIMIT
            qx, kx, vtx, qmn = _prep_a(proj, a_q_norm[j], a_k_norm[j], m_q_norm[i], jnp.where(bounded, -shift, 0.0))
            lam_init = 0.8 - 0.6 * math.exp(-0.3 * i)
            cast_srcs = tuple(w.reshape(-1, w.shape[-1]) for w in later_w) if i == 0 else ()
            attn = functools.partial(
                _diff_attention, qx, kx, vtx, positions, tbl_log2,
                lam_vecs=(a_lambda_q1[j], a_lambda_k1[j], a_lambda_q2[j], a_lambda_k2[j]), gsub=a_subln[j],
                lam_init=lam_init, cast_srcs=cast_srcs)
            o, casts = lax.cond(bounded,
                                lambda: attn(far_consts=jnp.exp2(far_log2), bounded=True),
                                lambda: attn(far_consts=far_log2, bounded=False))
            if i == 0:
                later_w = tuple(c.reshape(w.shape) for c, w in zip(casts, later_w))
                w_out, w_up, w_down, w_in_b = later_w
        else:
            proj = _norm_proj(x2d, norm_attn[i], w_in_b, j).reshape(b, s, -1)
            qt, kn, vt, qmn = _prep_b(proj, b_q_norm[j], b_k_norm[j], m_q_norm[i])
            o = _win_attention(qt, kn, vt, positions, tbl_log2, b_sink[j].astype(F32) * LOG2E)
        o_m = _mem_attention(qmn, k_m, v_m)
        x2d = _out_proj(x2d, o.reshape(b * s, MIX_WIDTH), o_m.reshape(b * s, MEM_WIDTH), w_out, i)
        x2d = _mlp(x2d, norm_mlp[i], w_up, w_down, i)
    return x2d.reshape(b, s, d)
```

```python
import functools
import math

import jax
import jax.numpy as jnp
from jax import lax
from jax.experimental import pallas as pl
from jax.experimental.pallas import tpu as pltpu

F32 = jnp.float32
BF16 = jnp.bfloat16
I32 = jnp.int32

D_MODEL = 2048
N_HEADS = 12
HEAD_DIM = 128
DIFF_QK_DIM = 64
N_KV_HEADS = 4
GQA_GROUP = 3
MIX_WIDTH = N_HEADS * HEAD_DIM
KV_WIDTH = N_KV_HEADS * HEAD_DIM
WINDOW = 128
N_MEM_HEADS = 4
MEM_WIDTH = N_MEM_HEADS * HEAD_DIM
MEM_LEN = 256
D_FF = 4 * D_MODEL
N_BUCKETS = 32
MAX_DISTANCE = 128
RMS_EPS = 1e-6
NEG_BIG = -1e30
LOG2E = math.log2(math.e)

HALF_BUCKETS = N_BUCKETS // 2
MAX_EXACT = HALF_BUCKETS // 2
FAR_DIST = 91

ONE_COL = DIFF_QK_DIM
V_ROWS = HEAD_DIM + 16
SHIFT_LIMIT = 50.0

LANES = 128
VMEM_LIMIT = 56 * 1024 * 1024

TM_PROJ, TN_PROJ = 1024, 1024
TS_PREP = 512
TQ_A, TK_A = 512, 512
TQ_B = 512
TQ_MEM = 512
TM_OUT, TN_OUT = 512, 2048
TM_MLP, TF_MLP = 1024, 512


def _cparams(sem):
    return pltpu.CompilerParams(dimension_semantics=sem, vmem_limit_bytes=VMEM_LIMIT)


def _smem():
    return pl.BlockSpec(memory_space=pltpu.SMEM)


def _t5_bucket(rel):
    side = jnp.where(rel > 0, HALF_BUCKETS, 0)
    n = jnp.abs(rel)
    n_f = jnp.maximum(n, 1).astype(F32)
    large = MAX_EXACT + (jnp.log(n_f / MAX_EXACT) / math.log(MAX_DISTANCE / MAX_EXACT)
                         * (HALF_BUCKETS - MAX_EXACT)).astype(I32)
    large = jnp.minimum(large, HALF_BUCKETS - 1)
    return side + jnp.where(n < MAX_EXACT, n, large)


def _table_lookup(tbl_row, bucket):
    rows, cols = bucket.shape
    tb = jnp.broadcast_to(tbl_row, (rows, LANES))
    parts = [jnp.take_along_axis(tb, bucket[:, c:c + LANES], axis=1) for c in range(0, cols, LANES)]
    return parts[0] if len(parts) == 1 else jnp.concatenate(parts, axis=1)


def _group_rms_scale(x, group):
    t = x * x
    if group == LANES:
        return lax.rsqrt(jnp.mean(t, axis=-1, keepdims=True) + RMS_EPS)
    lane = lax.broadcasted_iota(I32, x.shape, 1)
    lo = lane < group
    s_lo = jnp.sum(jnp.where(lo, t, 0.0), axis=-1, keepdims=True)
    s_hi = jnp.sum(jnp.where(lo, 0.0, t), axis=-1, keepdims=True)
    return jnp.where(lo, lax.rsqrt(s_lo / group + RMS_EPS), lax.rsqrt(s_hi / group + RMS_EPS))


def _proj_kernel(x_ref, g_ref, w_ref, o_ref, h_ref):
    @pl.when(pl.program_id(1) == 0)
    def _():
        x = x_ref[...]
        r = lax.rsqrt(jnp.mean(x * x, axis=-1, keepdims=True) + RMS_EPS)
        h_ref[...] = ((x * r) * g_ref[...]).astype(BF16)

    o_ref[...] = jnp.dot(h_ref[...], w_ref[...].astype(BF16), preferred_element_type=F32)


def _norm_proj(x2d, gain, w_stack, layer):
    m, d = x2d.shape
    n = w_stack.shape[2]
    return pl.pallas_call(
        _proj_kernel,
        grid=(m // TM_PROJ, n // TN_PROJ),
        in_specs=[
            pl.BlockSpec((TM_PROJ, d), lambda i, j: (i, 0)),
            pl.BlockSpec((1, d), lambda i, j: (0, 0)),
            pl.BlockSpec((None, d, TN_PROJ), lambda i, j: (layer, 0, j)),
        ],
        out_specs=pl.BlockSpec((TM_PROJ, TN_PROJ), lambda i, j: (i, j)),
        out_shape=jax.ShapeDtypeStruct((m, n), F32),
        scratch_shapes=[pltpu.VMEM((TM_PROJ, d), BF16)],
        compiler_params=_cparams(("parallel", "arbitrary")),
        name="norm_proj",
    )(x2d, gain.reshape(1, d), w_stack)


def _mem_kv_kernel(mem_ref, g_ref, w_ref, gk_ref, k_ref, v_ref):
    x = mem_ref[0]
    r = lax.rsqrt(jnp.mean(x * x, axis=-1, keepdims=True) + RMS_EPS)
    mn = ((x * r) * g_ref[...]).astype(BF16)
    mkv = jnp.dot(mn, w_ref[...].astype(BF16), preferred_element_type=F32)
    for h in range(N_MEM_HEADS):
        kh = mkv[:, h * HEAD_DIM:(h + 1) * HEAD_DIM]
        k_ref[0, :, h * HEAD_DIM:(h + 1) * HEAD_DIM] = ((kh * _group_rms_scale(kh, HEAD_DIM)) * gk_ref[...]).astype(BF16)
    v_ref[0] = mkv[:, MEM_WIDTH:].astype(BF16)


def _mem_kv(mem, gain, w_stack, layer, gk):
    b = mem.shape[0]
    shp = jax.ShapeDtypeStruct((b, MEM_LEN, MEM_WIDTH), BF16)
    return pl.pallas_call(
        _mem_kv_kernel,
        grid=(b,),
        in_specs=[
            pl.BlockSpec((1, MEM_LEN, D_MODEL), lambda i: (i, 0, 0)),
            pl.BlockSpec((1, D_MODEL), lambda i: (0, 0)),
            pl.BlockSpec((None, D_MODEL, 2 * MEM_WIDTH), lambda i: (layer, 0, 0)),
            pl.BlockSpec((1, HEAD_DIM), lambda i: (0, 0)),
        ],
        out_specs=[pl.BlockSpec((1, MEM_LEN, MEM_WIDTH), lambda i: (i, 0, 0))] * 2,
        out_shape=[shp, shp],
        compiler_params=_cparams(("parallel",)),
        name="mem_kv",
    )(mem, gain.reshape(1, D_MODEL), w_stack, gk.reshape(1, HEAD_DIM))


def _prep_a_kernel(negm_ref, q_ref, k_ref, v_ref, qm_ref, gq_ref, gk_ref, gm_ref, qx_ref, kx_ref, vtx_ref, qmn_ref):
    q_scale = DIFF_QK_DIM ** -0.5 * LOG2E
    m_scale = HEAD_DIM ** -0.5 * LOG2E
    lane = lax.broadcasted_iota(I32, (TS_PREP, HEAD_DIM), 1)
    neg_shift = negm_ref[0]

    def extend(x, extra):
        return jnp.where(lane < DIFF_QK_DIM, x, jnp.where(lane == ONE_COL, extra, 0.0))

    ones_rows = jnp.where(lax.broadcasted_iota(I32, (V_ROWS - HEAD_DIM, TS_PREP), 0) == 0, 1.0, 0.0).astype(BF16)
    for h in range(N_HEADS):
        sl = slice(h * HEAD_DIM, (h + 1) * HEAD_DIM)
        q = q_ref[0, :, sl]
        qn = ((q * _group_rms_scale(q, DIFF_QK_DIM)) * gq_ref[...]) * q_scale
        k = k_ref[0, :, sl]
        kn = (k * _group_rms_scale(k, DIFF_QK_DIM)) * gk_ref[...]
        for c in range(2):
            qc = qn if c == 0 else pltpu.roll(qn, DIFF_QK_DIM, axis=1)
            kc = kn if c == 0 else pltpu.roll(kn, DIFF_QK_DIM, axis=1)
            qx_ref[0, 2 * h + c] = extend(qc, neg_shift).T.astype(BF16)
            kx_ref[0, 2 * h + c] = extend(kc, 1.0).astype(BF16)
        vt = v_ref[0, :, sl].T.astype(BF16)
        vtx_ref[0, h, 0] = jnp.concatenate([vt, ones_rows], axis=0)
    for h in range(N_MEM_HEADS):
        sl = slice(h * HEAD_DIM, (h + 1) * HEAD_DIM)
        qm = qm_ref[0, :, sl]
        qmn_ref[0, :, sl] = (((qm * _group_rms_scale(qm, HEAD_DIM)) * gm_ref[...]) * m_scale).astype(BF16)


def _prep_a(proj, gq, gk, gm, neg_shift):
    assert TS_PREP == TK_A
    b, s, _ = proj.shape
    gq2 = jnp.concatenate([gq, gq]).reshape(1, HEAD_DIM)
    gk2 = jnp.concatenate([gk, gk]).reshape(1, HEAD_DIM)
    wblk = MIX_WIDTH // MEM_WIDTH
    return pl.pallas_call(
        _prep_a_kernel,
        grid=(b, s // TS_PREP),
        in_specs=[
            _smem(),
            pl.BlockSpec((1, TS_PREP, MIX_WIDTH), lambda i, j: (i, j, 0)),
            pl.BlockSpec((1, TS_PREP, MIX_WIDTH), lambda i, j: (i, j, 1)),
            pl.BlockSpec((1, TS_PREP, MIX_WIDTH), lambda i, j: (i, j, 2)),
            pl.BlockSpec((1, TS_PREP, MEM_WIDTH), lambda i, j: (i, j, 3 * wblk)),
            pl.BlockSpec((1, HEAD_DIM), lambda i, j: (0, 0)),
            pl.BlockSpec((1, HEAD_DIM), lambda i, j: (0, 0)),
            pl.BlockSpec((1, HEAD_DIM), lambda i, j: (0, 0)),
        ],
        out_specs=[
            pl.BlockSpec((1, 2 * N_HEADS, HEAD_DIM, TS_PREP), lambda i, j: (i, 0, 0, j)),
            pl.BlockSpec((1, 2 * N_HEADS, TS_PREP, HEAD_DIM), lambda i, j: (i, 0, j, 0)),
            pl.BlockSpec((1, N_HEADS, 1, V_ROWS, TK_A), lambda i, j: (i, 0, j, 0, 0)),
            pl.BlockSpec((1, TS_PREP, MEM_WIDTH), lambda i, j: (i, j, 0)),
        ],
        out_shape=[
            jax.ShapeDtypeStruct((b, 2 * N_HEADS, HEAD_DIM, s), BF16),
            jax.ShapeDtypeStruct((b, 2 * N_HEADS, s, HEAD_DIM), BF16),
            jax.ShapeDtypeStruct((b, N_HEADS, s // TK_A, V_ROWS, TK_A), BF16),
            jax.ShapeDtypeStruct((b, s, MEM_WIDTH), BF16),
        ],
        compiler_params=_cparams(("parallel", "parallel")),
        name="prep_a",
    )(neg_shift.reshape(1), proj, proj, proj, proj, gq2, gk2, gm.reshape(1, HEAD_DIM))


def _diff_attn_kernel(qlo_ref, qhi_ref, klo_ref, khi_ref,
                      far_ref, lq1_ref, lk1_ref, lq2_ref, lk2_ref,
                      qx_ref, kx_ref, vtx_ref, posq_ref, posk_ref, tbl_ref, gsub_ref,
                      *rest, lam_init, bounded, n_cast):
    w_f32_refs, o_ref, w_bf16_refs = rest[:n_cast], rest[n_cast], rest[n_cast + 1:2 * n_cast + 1]
    acc_ref, m_scratch = rest[2 * n_cast + 1], rest[2 * n_cast + 2:]
    b, iq, j = pl.program_id(0), pl.program_id(1), pl.program_id(2)
    pos_far = klo_ref[b, j] - qhi_ref[b, iq] >= FAR_DIST
    neg_far = khi_ref[b, j] - qlo_ref[b, iq] <= -FAR_DIST
    far = jnp.logical_or(pos_far, neg_far)

    @pl.when(j == 0)
    def _():
        acc_ref[...] = jnp.zeros(acc_ref.shape, F32)
        if not bounded:
            m_scratch[0][...] = jnp.full(m_scratch[0].shape, NEG_BIG, F32)

    def logits(hc):
        return jnp.dot(kx_ref[0, hc], qx_ref[0, hc], preferred_element_type=F32)

    def accumulate(hc, s, far_const):
        h = hc // 2
        if bounded:
            pv = jnp.dot(vtx_ref[0, h, 0], jnp.exp2(s).astype(BF16), preferred_element_type=F32)
            acc_ref[hc] += pv if far_const is None else far_const * pv
        else:
            m_ref = m_scratch[0]
            off = 0.0 if far_const is None else far_const
            m_old = m_ref[hc]
            m_new = jnp.maximum(m_old, jnp.max(s, axis=0, keepdims=True) + off)
            p = jnp.exp2(s - (m_new - off)).astype(BF16)
            acc_ref[hc] = (jnp.exp2(m_old - m_new) * acc_ref[hc]
                           + jnp.dot(vtx_ref[0, h, 0], p, preferred_element_type=F32))
            m_ref[hc] = m_new

    def all_heads(bias_of_head, far_const_of_head):
        for src, dst in zip(w_f32_refs, w_bf16_refs):
            dst[...] = src[...].astype(BF16)
        s_next = logits(0)
        bias_next = None if bias_of_head is None else bias_of_head(0)
        for hc in range(2 * N_HEADS):
            h, c = divmod(hc, 2)
            s = s_next
            if hc + 1 < 2 * N_HEADS:
                s_next = logits(hc + 1)
            if bias_of_head is not None:
                if c == 0:
                    bias = bias_next
                    if h + 1 < N_HEADS:
                        bias_next = bias_of_head(h + 1)
                s = s + bias
            accumulate(hc, s, None if far_const_of_head is None else far_const_of_head(h))

    @pl.when(far)
    def _():
        all_heads(None, lambda h: jnp.where(pos_far, far_ref[1, h], far_ref[0, h]))

    @pl.when(jnp.logical_not(far))
    def _():
        bucket = []

        def bias_of_head(h):
            if not bucket:
                bucket.append(_t5_bucket(posk_ref[0] - posq_ref[0]))
            return _table_lookup(tbl_ref[h], bucket[0])

        all_heads(bias_of_head, None)

    @pl.when(j == pl.num_programs(2) - 1)
    def _():
        lam = (jnp.exp(jnp.sum(lq1_ref[...] * lk1_ref[...], axis=-1, keepdims=True))
               - jnp.exp(jnp.sum(lq2_ref[...] * lk2_ref[...], axis=-1, keepdims=True)) + lam_init)
        for h in range(N_HEADS):
            a0, a1 = acc_ref[2 * h], acc_ref[2 * h + 1]
            o_t = (a0[:HEAD_DIM] / a0[HEAD_DIM:HEAD_DIM + 1]
                   - lam * (a1[:HEAD_DIM] / a1[HEAD_DIM:HEAD_DIM + 1]))
            r = lax.rsqrt(jnp.mean(o_t * o_t, axis=0, keepdims=True) + RMS_EPS)
            o_t = ((o_t * r) * gsub_ref[...]) * (1.0 - lam_init)
            o_ref[0, :, h * HEAD_DIM:(h + 1) * HEAD_DIM] = o_t.T.astype(BF16)


def _diff_attention(qx, kx, vtx, positions, tbl_log2, far_consts, lam_vecs, gsub, lam_init, bounded, cast_srcs):
    b, _, s, _ = kx.shape
    nq, nk = s // TQ_A, s // TK_A
    pq = positions.reshape(b, nq, TQ_A)
    pk = positions.reshape(b, nk, TK_A)
    ranges = (pq.min(-1), pq.max(-1), pk.min(-1), pk.max(-1))
    lam_spec = pl.BlockSpec((1, DIFF_QK_DIM), lambda i, q, k, *_: (0, 0))
    scratch = [pltpu.VMEM((2 * N_HEADS, V_ROWS, TQ_A), F32)]
    if not bounded:
        scratch.append(pltpu.VMEM((2 * N_HEADS, 1, TQ_A), F32))
    n_steps = b * nq * nk
    assert all(w.shape[0] % (16 * n_steps) == 0 for w in cast_srcs)
    cast_specs = [pl.BlockSpec((w.shape[0] // n_steps, w.shape[1]), lambda i, q, k, *_: ((i * nq + q) * nk + k, 0))
                  for w in cast_srcs]
    grid_spec = pltpu.PrefetchScalarGridSpec(
        num_scalar_prefetch=4,
        grid=(b, nq, nk),
        in_specs=[
            _smem(), lam_spec, lam_spec, lam_spec, lam_spec,
            pl.BlockSpec((1, 2 * N_HEADS, HEAD_DIM, TQ_A), lambda i, q, k, *_: (i, 0, 0, q)),
            pl.BlockSpec((1, 2 * N_HEADS, TK_A, HEAD_DIM), lambda i, q, k, *_: (i, 0, k, 0)),
            pl.BlockSpec((1, N_HEADS, 1, V_ROWS, TK_A), lambda i, q, k, *_: (i, 0, k, 0, 0)),
            pl.BlockSpec((1, 1, TQ_A), lambda i, q, k, *_: (i, 0, q)),
            pl.BlockSpec((1, TK_A, 1), lambda i, q, k, *_: (i, k, 0)),
            pl.BlockSpec((N_HEADS, 1, LANES), lambda i, q, k, *_: (0, 0, 0)),
            pl.BlockSpec((HEAD_DIM, 1), lambda i, q, k, *_: (0, 0)),
            *cast_specs,
        ],
        out_specs=[pl.BlockSpec((1, TQ_A, MIX_WIDTH), lambda i, q, k, *_: (i, q, 0)), *cast_specs],
        scratch_shapes=scratch,
    )
    outs = pl.pallas_call(
        functools.partial(_diff_attn_kernel, lam_init=lam_init, bounded=bounded, n_cast=len(cast_srcs)),
        grid_spec=grid_spec,
        out_shape=[jax.ShapeDtypeStruct((b, s, MIX_WIDTH), BF16),
                   *[jax.ShapeDtypeStruct(w.shape, BF16) for w in cast_srcs]],
        compiler_params=_cparams(("arbitrary", "arbitrary", "arbitrary")),
        name="diff_attn_bounded" if bounded else "diff_attn_running_max",
    )(*ranges, far_consts, *[v.reshape(1, DIFF_QK_DIM) for v in lam_vecs],
      qx, kx, vtx, positions.reshape(b, 1, s), positions.reshape(b, s, 1), tbl_log2, gsub.reshape(HEAD_DIM, 1),
      *cast_srcs)
    return outs[0], tuple(outs[1:])


def _prep_b_kernel(q_ref, k_ref, v_ref, qm_ref, gq_ref, gk_ref, gm_ref, qt_ref, kn_ref, vt_ref, qmn_ref):
    scale = HEAD_DIM ** -0.5 * LOG2E
    n_blk = TS_PREP // WINDOW
    ones_rows = jnp.where(lax.broadcasted_iota(I32, (V_ROWS - HEAD_DIM, TS_PREP), 0) == 0, 1.0, 0.0).astype(BF16)
    for h in range(N_HEADS):
        g, hg = divmod(h, GQA_GROUP)
        q = q_ref[0, :, h * HEAD_DIM:(h + 1) * HEAD_DIM]
        qt = (((q * _group_rms_scale(q, HEAD_DIM)) * gq_ref[...]) * scale).T.astype(BF16)
        for n in range(n_blk):
            qt_ref[0, n, g, :, hg * WINDOW:(hg + 1) * WINDOW] = qt[:, n * WINDOW:(n + 1) * WINDOW]
    for g in range(N_KV_HEADS):
        sl = slice(g * HEAD_DIM, (g + 1) * HEAD_DIM)
        k = k_ref[0, :, sl]
        kn_ref[0, g] = ((k * _group_rms_scale(k, HEAD_DIM)) * gk_ref[...]).astype(BF16)
        vt = jnp.concatenate([v_ref[0, :, sl].T.astype(BF16), ones_rows], axis=0)
        for n in range(n_blk):
            vt_ref[0, g, n] = vt[:, n * WINDOW:(n + 1) * WINDOW]
        qm = qm_ref[0, :, sl]
        qmn_ref[0, :, sl] = (((qm * _group_rms_scale(qm, HEAD_DIM)) * gm_ref[...]) * scale).astype(BF16)


def _prep_b(proj, gq, gk, gm):
    b, s, _ = proj.shape
    kblk = MIX_WIDTH // KV_WIDTH
    n_blk = TS_PREP // WINDOW
    g = lambda v: v.reshape(1, HEAD_DIM)
    return pl.pallas_call(
        _prep_b_kernel,
        grid=(b, s // TS_PREP),
        in_specs=[
            pl.BlockSpec((1, TS_PREP, MIX_WIDTH), lambda i, j: (i, j, 0)),
            pl.BlockSpec((1, TS_PREP, KV_WIDTH), lambda i, j: (i, j, kblk)),
            pl.BlockSpec((1, TS_PREP, KV_WIDTH), lambda i, j: (i, j, kblk + 1)),
            pl.BlockSpec((1, TS_PREP, MEM_WIDTH), lambda i, j: (i, j, kblk + 2)),
            pl.BlockSpec((1, HEAD_DIM), lambda i, j: (0, 0)),
            pl.BlockSpec((1, HEAD_DIM), lambda i, j: (0, 0)),
            pl.BlockSpec((1, HEAD_DIM), lambda i, j: (0, 0)),
        ],
        out_specs=[
            pl.BlockSpec((1, n_blk, N_KV_HEADS, HEAD_DIM, GQA_GROUP * WINDOW), lambda i, j: (i, j, 0, 0, 0)),
            pl.BlockSpec((1, N_KV_HEADS, TS_PREP, HEAD_DIM), lambda i, j: (i, 0, j, 0)),
            pl.BlockSpec((1, N_KV_HEADS, n_blk, V_ROWS, WINDOW), lambda i, j: (i, 0, j, 0, 0)),
            pl.BlockSpec((1, TS_PREP, MEM_WIDTH), lambda i, j: (i, j, 0)),
        ],
        out_shape=[
            jax.ShapeDtypeStruct((b, s // WINDOW, N_KV_HEADS, HEAD_DIM, GQA_GROUP * WINDOW), BF16),
            jax.ShapeDtypeStruct((b, N_KV_HEADS, s, HEAD_DIM), BF16),
            jax.ShapeDtypeStruct((b, N_KV_HEADS, s // WINDOW, V_ROWS, WINDOW), BF16),
            jax.ShapeDtypeStruct((b, s, MEM_WIDTH), BF16),
        ],
        compiler_params=_cparams(("parallel", "parallel")),
        name="prep_b",
    )(proj, proj, proj, proj, g(gq), g(gk), g(gm))


def _win_attn_kernel(sink_ref, qt_ref, k_ref, vt_ref, posq_ref, posk_ref, tbl_ref, o_ref):
    n_blocks = posq_ref.shape[1]
    n_sub = o_ref.shape[1] // WINDOW
    kw = 3 * WINDOW
    lane3 = lax.broadcasted_iota(I32, (1, GQA_GROUP * WINDOW), 1)

    def window(nl):
        n = pl.program_id(1) * n_sub + nl
        nb = jnp.clip(n - 1, 0, n_blocks - 3)
        return n, nb, pl.multiple_of(nb * WINDOW, WINDOW)

    def logits(nl, g):
        start = window(nl)[2]
        return jnp.dot(k_ref[0, g, pl.ds(start, kw), :], qt_ref[0, nl, g], preferred_element_type=F32)

    def masked_bucket(nl):
        n, _, start = window(nl)
        rel = posk_ref[0, pl.ds(start, kw), :] - posq_ref[0, pl.ds(n, 1), :]
        ki = start + lax.broadcasted_iota(I32, (kw, WINDOW), 0)
        qi = n * WINDOW + lax.broadcasted_iota(I32, (kw, WINDOW), 1)
        return jnp.where(jnp.abs(ki - qi) <= WINDOW, _t5_bucket(rel), N_BUCKETS)

    def finish(nl, g, s, bucket):
        nb = window(nl)[1]
        heads = range(g * GQA_GROUP, (g + 1) * GQA_GROUP)
        z = s + jnp.concatenate([_table_lookup(tbl_ref[h], bucket) for h in heads], axis=1)
        sink = jnp.where(lane3 < WINDOW, sink_ref[heads[0]],
                         jnp.where(lane3 < 2 * WINDOW, sink_ref[heads[1]], sink_ref[heads[2]]))
        m = jnp.maximum(jnp.max(z, axis=0, keepdims=True), sink)
        p = jnp.exp2(z - m).astype(BF16)
        vt = jnp.concatenate([vt_ref[0, g, nb + t] for t in range(3)], axis=1)
        acc = jnp.dot(vt, p, preferred_element_type=F32)
        o_t = acc[:HEAD_DIM] / (acc[HEAD_DIM:HEAD_DIM + 1] + jnp.exp2(sink - m))
        for hg, h in enumerate(heads):
            o_ref[0, nl * WINDOW:(nl + 1) * WINDOW, h * HEAD_DIM:(h + 1) * HEAD_DIM] = (
                o_t[:, hg * WINDOW:(hg + 1) * WINDOW].T.astype(BF16))

    chains = [(nl, g) for nl in range(n_sub) for g in range(N_KV_HEADS)]
    s_next = logits(*chains[0])
    bucket = None
    for idx, (nl, g) in enumerate(chains):
        s = s_next
        if idx + 1 < len(chains):
            s_next = logits(*chains[idx + 1])
        if g == 0:
            bucket = masked_bucket(nl)
        finish(nl, g, s, bucket)


def _win_attention(qt, kn, vt, positions, tbl_log2, sink_log2):
    b, _, s, _ = kn.shape
    n_blocks = s // WINDOW
    n_sub = TQ_B // WINDOW
    return pl.pallas_call(
        _win_attn_kernel,
        grid=(b, s // TQ_B),
        in_specs=[
            _smem(),
            pl.BlockSpec((1, n_sub, N_KV_HEADS, HEAD_DIM, GQA_GROUP * WINDOW), lambda i, j: (i, j, 0, 0, 0)),
            pl.BlockSpec((1, N_KV_HEADS, s, HEAD_DIM), lambda i, j: (i, 0, 0, 0)),
            pl.BlockSpec((1, N_KV_HEADS, n_blocks, V_ROWS, WINDOW), lambda i, j: (i, 0, 0, 0, 0)),
            pl.BlockSpec((1, n_blocks, WINDOW), lambda i, j: (i, 0, 0)),
            pl.BlockSpec((1, s, 1), lambda i, j: (i, 0, 0)),
            pl.BlockSpec((N_HEADS, 1, LANES), lambda i, j: (0, 0, 0)),
        ],
        out_specs=pl.BlockSpec((1, TQ_B, MIX_WIDTH), lambda i, j: (i, j, 0)),
        out_shape=jax.ShapeDtypeStruct((b, s, MIX_WIDTH), BF16),
        compiler_params=_cparams(("parallel", "parallel")),
        name="win_attn",
    )(sink_log2, qt, kn, vt, positions.reshape(b, n_blocks, WINDOW), positions.reshape(b, s, 1), tbl_log2)


def _mem_attn_kernel(q_ref, k_ref, v_ref, o_ref):
    for h in range(N_MEM_HEADS):
        sl = slice(h * HEAD_DIM, (h + 1) * HEAD_DIM)
        s = lax.dot_general(q_ref[0, :, sl], k_ref[0, :, sl], (((1,), (1,)), ((), ())), preferred_element_type=F32)
        e = jnp.exp2(s - jnp.max(s, axis=-1, keepdims=True))
        p = (e / jnp.sum(e, axis=-1, keepdims=True)).astype(BF16)
        o_ref[0, :, sl] = jnp.dot(p, v_ref[0, :, sl], preferred_element_type=F32).astype(BF16)


def _mem_attention(qmn, k_m, v_m):
    b, s, _ = qmn.shape
    return pl.pallas_call(
        _mem_attn_kernel,
        grid=(b, s // TQ_MEM),
        in_specs=[
            pl.BlockSpec((1, TQ_MEM, MEM_WIDTH), lambda i, j: (i, j, 0)),
            pl.BlockSpec((1, MEM_LEN, MEM_WIDTH), lambda i, j: (i, 0, 0)),
            pl.BlockSpec((1, MEM_LEN, MEM_WIDTH), lambda i, j: (i, 0, 0)),
        ],
        out_specs=pl.BlockSpec((1, TQ_MEM, MEM_WIDTH), lambda i, j: (i, j, 0)),
        out_shape=jax.ShapeDtypeStruct((b, s, MEM_WIDTH), BF16),
        compiler_params=_cparams(("parallel", "parallel")),
        name="mem_attn",
    )(qmn, k_m, v_m)


def _out_proj_kernel(x_ref, o_ref, om_ref, wo_ref, wm_ref, y_ref):
    y_ref[...] = (x_ref[...]
                  + jnp.dot(o_ref[...], wo_ref[...].astype(BF16), preferred_element_type=F32)
                  + jnp.dot(om_ref[...], wm_ref[...].astype(BF16), preferred_element_type=F32))


def _out_proj(x2d, o2d, om2d, w_stack, layer):
    m, d = x2d.shape
    return pl.pallas_call(
        _out_proj_kernel,
        grid=(m // TM_OUT, d // TN_OUT),
        in_specs=[
            pl.BlockSpec((TM_OUT, TN_OUT), lambda i, j: (i, j)),
            pl.BlockSpec((TM_OUT, MIX_WIDTH), lambda i, j: (i, 0)),
            pl.BlockSpec((TM_OUT, MEM_WIDTH), lambda i, j: (i, 0)),
            pl.BlockSpec((None, MIX_WIDTH, TN_OUT), lambda i, j: (layer, 0, j)),
            pl.BlockSpec((None, MEM_WIDTH, TN_OUT), lambda i, j: (layer, MIX_WIDTH // MEM_WIDTH, j)),
        ],
        out_specs=pl.BlockSpec((TM_OUT, TN_OUT), lambda i, j: (i, j)),
        out_shape=jax.ShapeDtypeStruct((m, d), F32),
        compiler_params=_cparams(("parallel", "parallel")),
        name="out_proj",
    )(x2d, o2d, om2d, w_stack, w_stack)


def _mlp_kernel(x_ref, g_ref, wu_ref, wd_ref, y_ref, h_ref):
    f = pl.program_id(1)

    @pl.when(f == 0)
    def _():
        x = x_ref[...]
        r = lax.rsqrt(jnp.mean(x * x, axis=-1, keepdims=True) + RMS_EPS)
        h_ref[...] = ((x * r) * g_ref[...]).astype(BF16)
        y_ref[...] = x

    u = jnp.maximum(jnp.dot(h_ref[...], wu_ref[...].astype(BF16), preferred_element_type=F32), 0.0)
    y_ref[...] += jnp.dot((u * u).astype(BF16), wd_ref[...].astype(BF16), preferred_element_type=F32)


def _mlp(x2d, gain, wu_stack, wd_stack, layer):
    m, d = x2d.shape
    return pl.pallas_call(
        _mlp_kernel,
        grid=(m // TM_MLP, D_FF // TF_MLP),
        in_specs=[
            pl.BlockSpec((TM_MLP, d), lambda i, f: (i, 0)),
            pl.BlockSpec((1, d), lambda i, f: (0, 0)),
            pl.BlockSpec((None, d, TF_MLP), lambda i, f: (layer, 0, f)),
            pl.BlockSpec((None, TF_MLP, d), lambda i, f: (layer, f, 0)),
        ],
        out_specs=pl.BlockSpec((TM_MLP, d), lambda i, f: (i, 0)),
        out_shape=jax.ShapeDtypeStruct((m, d), F32),
        scratch_shapes=[pltpu.VMEM((TM_MLP, d), BF16)],
        compiler_params=_cparams(("parallel", "arbitrary")),
        name="mlp",
    )(x2d, gain.reshape(1, d), wu_stack, wd_stack)


def _bias_tables(rel_bias):
    t = (rel_bias.astype(F32) * LOG2E).T
    tbl = jnp.zeros((N_HEADS, 1, LANES), F32).at[:, 0, :N_BUCKETS].set(t).at[:, 0, N_BUCKETS].set(NEG_BIG)
    far = jnp.stack([t[:, HALF_BUCKETS - 1], t[:, N_BUCKETS - 1]])
    return tbl, far


def _diff_logit_shift(gq, gk, rel_bias):
    qk = 1.01 * DIFF_QK_DIM * jnp.max(jnp.abs(gq)) * jnp.max(jnp.abs(gk)) * (DIFF_QK_DIM ** -0.5 * LOG2E)
    return jnp.ceil(qk + jnp.max(jnp.abs(rel_bias)) * LOG2E).astype(F32)


def kernel(x, mem, positions, rel_bias, norm_attn, norm_mem, norm_mlp, w_in_a, a_q_norm, a_k_norm, a_lambda_q1, a_lambda_k1, a_lambda_q2, a_lambda_k2, a_subln, w_in_b, b_q_norm, b_k_norm, b_sink, w_mem_kv, m_q_norm, m_k_norm, w_out, w_up, w_down):
    b, s, d = x.shape
    depth = norm_attn.shape[0]
    tbl_log2, far_log2 = _bias_tables(rel_bias)
    x2d = x.reshape(b * s, d)
    later_w = (w_out, w_up, w_down, w_in_b)
    for i in range(depth):
        j = i // 2
        k_m, v_m = _mem_kv(mem, norm_mem[i], w_mem_kv, i, m_k_norm[i])
        if i % 2 == 0:
            proj = _norm_proj(x2d, norm_attn[i], w_in_a, j).reshape(b, s, -1)
            shift = _diff_logit_shift(a_q_norm[j], a_k_norm[j], rel_bias)
            bounded = shift <= SHIFT_LIMIT
            qx, kx, vtx, qmn = _prep_a(proj, a_q_norm[j], a_k_norm[j], m_q_norm[i], jnp.where(bounded, -shift, 0.0))
            lam_init = 0.8 - 0.6 * math.exp(-0.3 * i)
            cast_srcs = tuple(w.reshape(-1, w.shape[-1]) for w in later_w) if i == 0 else ()
            attn = functools.partial(
                _diff_attention, qx, kx, vtx, positions, tbl_log2,
                lam_vecs=(a_lambda_q1[j], a_lambda_k1[j], a_lambda_q2[j], a_lambda_k2[j]), gsub=a_subln[j],
                lam_init=lam_init, cast_srcs=cast_srcs)
            o, casts = lax.cond(bounded,
                                lambda: attn(far_consts=jnp.exp2(far_log2), bounded=True),
                                lambda: attn(far_consts=far_log2, bounded=False))
            if i == 0:
                later_w = tuple(c.reshape(w.shape) for c, w in zip(casts, later_w))
                w_out, w_up, w_down, w_in_b = later_w
        else:
            proj = _norm_proj(x2d, norm_attn[i], w_in_b, j).reshape(b, s, -1)
            qt, kn, vt, qmn = _prep_b(proj, b_q_norm[j], b_k_norm[j], m_q_norm[i])
            o = _win_attention(qt, kn, vt, positions, tbl_log2, b_sink[j].astype(F32) * LOG2E)
        o_m = _mem_attention(qmn, k_m, v_m)
        x2d = _out_proj(x2d, o.reshape(b * s, MIX_WIDTH), o_m.reshape(b * s, MEM_WIDTH), w_out, i)
        x2d = _mlp(x2d, norm_mlp[i], w_up, w_down, i)
    return x2d.reshape(b, s, d)
```

```python
import functools
import math

import jax
import jax.numpy as jnp
from jax import lax
from jax.experimental import pallas as pl
from jax.experimental.pallas import tpu as pltpu

F32 = jnp.float32
BF16 = jnp.bfloat16
I32 = jnp.int32

D_MODEL = 2048
N_HEADS = 12
HEAD_DIM = 128
DIFF_QK_DIM = 64
N_KV_HEADS = 4
GQA_GROUP = 3
MIX_WIDTH = N_HEADS * HEAD_DIM
KV_WIDTH = N_KV_HEADS * HEAD_DIM
WINDOW = 128
N_MEM_HEADS = 4
MEM_WIDTH = N_MEM_HEADS * HEAD_DIM
MEM_LEN = 256
D_FF = 4 * D_MODEL
N_BUCKETS = 32
MAX_DISTANCE = 128
RMS_EPS = 1e-6
NEG_BIG = -1e30
LOG2E = math.log2(math.e)

HALF_BUCKETS = N_BUCKETS // 2
MAX_EXACT = HALF_BUCKETS // 2
FAR_DIST = 91

ONE_COL = DIFF_QK_DIM
V_ROWS = HEAD_DIM + 16
SHIFT_LIMIT = 50.0

LANES = 128

DIAG_ROWS_MAX = 512
DIAG_OFF_MIN = 5
DIAG_CENTER = DIAG_OFF_MIN + DIAG_ROWS_MAX + FAR_DIST
DIAG_OFF_MAX = DIAG_CENTER + FAR_DIST + LANES - 1
DIAG_ROWS = -(-(DIAG_OFF_MAX + DIAG_ROWS_MAX) // 16) * 16

VMEM_LIMIT = 56 * 1024 * 1024

TM_PROJ, TN_PROJ = 1024, 1024
TS_PREP = 512
TQ_A, TK_A = 512, 512
TQ_B = 512
TQ_MEM = 512
TM_OUT, TN_OUT = 512, 2048
TM_MLP, TF_MLP = 1024, 512


def _cparams(sem):
    return pltpu.CompilerParams(dimension_semantics=sem, vmem_limit_bytes=VMEM_LIMIT)


def _smem():
    return pl.BlockSpec(memory_space=pltpu.SMEM)


def _t5_bucket(rel):
    side = jnp.where(rel > 0, HALF_BUCKETS, 0)
    n = jnp.abs(rel)
    n_f = jnp.maximum(n, 1).astype(F32)
    large = MAX_EXACT + (jnp.log(n_f / MAX_EXACT) / math.log(MAX_DISTANCE / MAX_EXACT)
                         * (HALF_BUCKETS - MAX_EXACT)).astype(I32)
    large = jnp.minimum(large, HALF_BUCKETS - 1)
    return side + jnp.where(n < MAX_EXACT, n, large)


def _table_lookup(tbl_row, bucket):
    rows, cols = bucket.shape
    tb = jnp.broadcast_to(tbl_row, (rows, LANES))
    parts = [jnp.take_along_axis(tb, bucket[:, c:c + LANES], axis=1) for c in range(0, cols, LANES)]
    return parts[0] if len(parts) == 1 else jnp.concatenate(parts, axis=1)


def _group_rms_scale(x, group):
    t = x * x
    if group == LANES:
        return lax.rsqrt(jnp.mean(t, axis=-1, keepdims=True) + RMS_EPS)
    lane = lax.broadcasted_iota(I32, x.shape, 1)
    lo = lane < group
    s_lo = jnp.sum(jnp.where(lo, t, 0.0), axis=-1, keepdims=True)
    s_hi = jnp.sum(jnp.where(lo, 0.0, t), axis=-1, keepdims=True)
    return jnp.where(lo, lax.rsqrt(s_lo / group + RMS_EPS), lax.rsqrt(s_hi / group + RMS_EPS))


def _proj_kernel(x_ref, g_ref, w_ref, o_ref, h_ref):
    @pl.when(pl.program_id(1) == 0)
    def _():
        x = x_ref[...]
        r = lax.rsqrt(jnp.mean(x * x, axis=-1, keepdims=True) + RMS_EPS)
        h_ref[...] = ((x * r) * g_ref[...]).astype(BF16)

    o_ref[...] = jnp.dot(h_ref[...], w_ref[...].astype(BF16), preferred_element_type=F32)


def _norm_proj(x2d, gain, w_stack, layer):
    m, d = x2d.shape
    n = w_stack.shape[2]
    return pl.pallas_call(
        _proj_kernel,
        grid=(m // TM_PROJ, n // TN_PROJ),
        in_specs=[
            pl.BlockSpec((TM_PROJ, d), lambda i, j: (i, 0)),
            pl.BlockSpec((1, d), lambda i, j: (0, 0)),
            pl.BlockSpec((None, d, TN_PROJ), lambda i, j: (layer, 0, j)),
        ],
        out_specs=pl.BlockSpec((TM_PROJ, TN_PROJ), lambda i, j: (i, j)),
        out_shape=jax.ShapeDtypeStruct((m, n), F32),
        scratch_shapes=[pltpu.VMEM((TM_PROJ, d), BF16)],
        compiler_params=_cparams(("parallel", "arbitrary")),
        name="norm_proj",
    )(x2d, gain.reshape(1, d), w_stack)


def _mem_kv_kernel(mem_ref, g_ref, w_ref, gk_ref, k_ref, v_ref):
    x = mem_ref[0]
    r = lax.rsqrt(jnp.mean(x * x, axis=-1, keepdims=True) + RMS_EPS)
    mn = ((x * r) * g_ref[...]).astype(BF16)
    mkv = jnp.dot(mn, w_ref[...].astype(BF16), preferred_element_type=F32)
    for h in range(N_MEM_HEADS):
        kh = mkv[:, h * HEAD_DIM:(h + 1) * HEAD_DIM]
        k_ref[0, :, h * HEAD_DIM:(h + 1) * HEAD_DIM] = ((kh * _group_rms_scale(kh, HEAD_DIM)) * gk_ref[...]).astype(BF16)
    v_ref[0] = mkv[:, MEM_WIDTH:].astype(BF16)


def _mem_kv(mem, gain, w_stack, layer, gk):
    b = mem.shape[0]
    shp = jax.ShapeDtypeStruct((b, MEM_LEN, MEM_WIDTH), BF16)
    return pl.pallas_call(
        _mem_kv_kernel,
        grid=(b,),
        in_specs=[
            pl.BlockSpec((1, MEM_LEN, D_MODEL), lambda i: (i, 0, 0)),
            pl.BlockSpec((1, D_MODEL), lambda i: (0, 0)),
            pl.BlockSpec((None, D_MODEL, 2 * MEM_WIDTH), lambda i: (layer, 0, 0)),
            pl.BlockSpec((1, HEAD_DIM), lambda i: (0, 0)),
        ],
        out_specs=[pl.BlockSpec((1, MEM_LEN, MEM_WIDTH), lambda i: (i, 0, 0))] * 2,
        out_shape=[shp, shp],
        compiler_params=_cparams(("parallel",)),
        name="mem_kv",
    )(mem, gain.reshape(1, D_MODEL), w_stack, gk.reshape(1, HEAD_DIM))


def _prep_a_kernel(negm_ref, q_ref, k_ref, v_ref, qm_ref, gq_ref, gk_ref, gm_ref, qx_ref, kx_ref, vtx_ref, qmn_ref):
    q_scale = DIFF_QK_DIM ** -0.5 * LOG2E
    m_scale = HEAD_DIM ** -0.5 * LOG2E
    lane = lax.broadcasted_iota(I32, (TS_PREP, HEAD_DIM), 1)
    neg_shift = negm_ref[0]

    def extend(x, extra):
        return jnp.where(lane < DIFF_QK_DIM, x, jnp.where(lane == ONE_COL, extra, 0.0))

    ones_rows = jnp.where(lax.broadcasted_iota(I32, (V_ROWS - HEAD_DIM, TS_PREP), 0) == 0, 1.0, 0.0).astype(BF16)
    for h in range(N_HEADS):
        sl = slice(h * HEAD_DIM, (h + 1) * HEAD_DIM)
        q = q_ref[0, :, sl]
        qn = ((q * _group_rms_scale(q, DIFF_QK_DIM)) * gq_ref[...]) * q_scale
        k = k_ref[0, :, sl]
        kn = (k * _group_rms_scale(k, DIFF_QK_DIM)) * gk_ref[...]
        for c in range(2):
            qc = qn if c == 0 else pltpu.roll(qn, DIFF_QK_DIM, axis=1)
            kc = kn if c == 0 else pltpu.roll(kn, DIFF_QK_DIM, axis=1)
            qx_ref[0, 2 * h + c] = extend(qc, neg_shift).T.astype(BF16)
            kx_ref[0, 2 * h + c] = extend(kc, 1.0).astype(BF16)
        vt = v_ref[0, :, sl].T.astype(BF16)
        vtx_ref[0, h, 0] = jnp.concatenate([vt, ones_rows], axis=0)
    for h in range(N_MEM_HEADS):
        sl = slice(h * HEAD_DIM, (h + 1) * HEAD_DIM)
        qm = qm_ref[0, :, sl]
        qmn_ref[0, :, sl] = (((qm * _group_rms_scale(qm, HEAD_DIM)) * gm_ref[...]) * m_scale).astype(BF16)


def _prep_a(proj, gq, gk, gm, neg_shift):
    assert TS_PREP == TK_A
    b, s, _ = proj.shape
    gq2 = jnp.concatenate([gq, gq]).reshape(1, HEAD_DIM)
    gk2 = jnp.concatenate([gk, gk]).reshape(1, HEAD_DIM)
    wblk = MIX_WIDTH // MEM_WIDTH
    return pl.pallas_call(
        _prep_a_kernel,
        grid=(b, s // TS_PREP),
        in_specs=[
            _smem(),
            pl.BlockSpec((1, TS_PREP, MIX_WIDTH), lambda i, j: (i, j, 0)),
            pl.BlockSpec((1, TS_PREP, MIX_WIDTH), lambda i, j: (i, j, 1)),
            pl.BlockSpec((1, TS_PREP, MIX_WIDTH), lambda i, j: (i, j, 2)),
            pl.BlockSpec((1, TS_PREP, MEM_WIDTH), lambda i, j: (i, j, 3 * wblk)),
            pl.BlockSpec((1, HEAD_DIM), lambda i, j: (0, 0)),
            pl.BlockSpec((1, HEAD_DIM), lambda i, j: (0, 0)),
            pl.BlockSpec((1, HEAD_DIM), lambda i, j: (0, 0)),
        ],
        out_specs=[
            pl.BlockSpec((1, 2 * N_HEADS, HEAD_DIM, TS_PREP), lambda i, j: (i, 0, 0, j)),
            pl.BlockSpec((1, 2 * N_HEADS, TS_PREP, HEAD_DIM), lambda i, j: (i, 0, j, 0)),
            pl.BlockSpec((1, N_HEADS, 1, V_ROWS, TK_A), lambda i, j: (i, 0, j, 0, 0)),
            pl.BlockSpec((1, TS_PREP, MEM_WIDTH), lambda i, j: (i, j, 0)),
        ],
        out_shape=[
            jax.ShapeDtypeStruct((b, 2 * N_HEADS, HEAD_DIM, s), BF16),
            jax.ShapeDtypeStruct((b, 2 * N_HEADS, s, HEAD_DIM), BF16),
            jax.ShapeDtypeStruct((b, N_HEADS, s // TK_A, V_ROWS, TK_A), BF16),
            jax.ShapeDtypeStruct((b, s, MEM_WIDTH), BF16),
        ],
        compiler_params=_cparams(("parallel", "parallel")),
        name="prep_a",
    )(neg_shift.reshape(1), proj, proj, proj, proj, gq2, gk2, gm.reshape(1, HEAD_DIM))


def _diff_attn_kernel(qlo_ref, qhi_ref, klo_ref, khi_ref, qrun_ref, krun_ref,
                      far_ref, lq1_ref, lk1_ref, lq2_ref, lk2_ref,
                      qx_ref, kx_ref, vtx_ref, posq_ref, posk_ref, tbl_ref, diag_ref, gsub_ref,
                      *rest, lam_init, bounded, n_cast):
    w_f32_refs, o_ref, w_bf16_refs = rest[:n_cast], rest[n_cast], rest[n_cast + 1:2 * n_cast + 1]
    acc_ref, m_scratch = rest[2 * n_cast + 1], rest[2 * n_cast + 2:]
    b, iq, j = pl.program_id(0), pl.program_id(1), pl.program_id(2)
    pos_far = klo_ref[b, j] - qhi_ref[b, iq] >= FAR_DIST
    neg_far = khi_ref[b, j] - qlo_ref[b, iq] <= -FAR_DIST
    far = jnp.logical_or(pos_far, neg_far)

    @pl.when(j == 0)
    def _():
        acc_ref[...] = jnp.zeros(acc_ref.shape, F32)
        if not bounded:
            m_scratch[0][...] = jnp.full(m_scratch[0].shape, NEG_BIG, F32)

    def logits(hc):
        return jnp.dot(kx_ref[0, hc], qx_ref[0, hc], preferred_element_type=F32)

    def accumulate(hc, s, far_const):
        h = hc // 2
        if bounded:
            pv = jnp.dot(vtx_ref[0, h, 0], jnp.exp2(s).astype(BF16), preferred_element_type=F32)
            acc_ref[hc] += pv if far_const is None else far_const * pv
        else:
            m_ref = m_scratch[0]
            off = 0.0 if far_const is None else far_const
            m_old = m_ref[hc]
            m_new = jnp.maximum(m_old, jnp.max(s, axis=0, keepdims=True) + off)
            p = jnp.exp2(s - (m_new - off)).astype(BF16)
            acc_ref[hc] = (jnp.exp2(m_old - m_new) * acc_ref[hc]
                           + jnp.dot(vtx_ref[0, h, 0], p, preferred_element_type=F32))
            m_ref[hc] = m_new

    def all_heads(bias_of_head, far_const_of_head):
        for src, dst in zip(w_f32_refs, w_bf16_refs):
            dst[...] = src[...].astype(BF16)
        s_next = logits(0)
        bias_next = None if bias_of_head is None else bias_of_head(0)
        for hc in range(2 * N_HEADS):
            h, c = divmod(hc, 2)
            s = s_next
            if hc + 1 < 2 * N_HEADS:
                s_next = logits(hc + 1)
            if bias_of_head is not None:
                if c == 0:
                    bias = bias_next
                    if h + 1 < N_HEADS:
                        bias_next = bias_of_head(h + 1)
                s = s + bias
            accumulate(hc, s, None if far_const_of_head is None else far_const_of_head(h))

    @pl.when(far)
    def _():
        all_heads(None, lambda h: jnp.where(pos_far, far_ref[1, h], far_ref[0, h]))

    runs = jnp.logical_and(qrun_ref[b, iq] == 1, krun_ref[b, j] == 1)

    @pl.when(jnp.logical_and(jnp.logical_not(far), runs))
    def _():
        d0 = klo_ref[b, j] - qlo_ref[b, iq]

        def bias_of_head(h):
            offs = [jnp.clip(d0 - c + DIAG_CENTER, DIAG_OFF_MIN, DIAG_OFF_MAX) for c in range(0, TQ_A, LANES)]
            return jnp.concatenate([diag_ref[h, pl.ds(off, TK_A), :] for off in offs], axis=1)

        all_heads(bias_of_head, None)

    @pl.when(jnp.logical_and(jnp.logical_not(far), jnp.logical_not(runs)))
    def _():
        bucket = []

        def bias_of_head(h):
            if not bucket:
                bucket.append(_t5_bucket(posk_ref[0] - posq_ref[0]))
            return _table_lookup(tbl_ref[h], bucket[0])

        all_heads(bias_of_head, None)

    @pl.when(j == pl.num_programs(2) - 1)
    def _():
        lam = (jnp.exp(jnp.sum(lq1_ref[...] * lk1_ref[...], axis=-1, keepdims=True))
               - jnp.exp(jnp.sum(lq2_ref[...] * lk2_ref[...], axis=-1, keepdims=True)) + lam_init)
        for h in range(N_HEADS):
            a0, a1 = acc_ref[2 * h], acc_ref[2 * h + 1]
            o_t = (a0[:HEAD_DIM] / a0[HEAD_DIM:HEAD_DIM + 1]
                   - lam * (a1[:HEAD_DIM] / a1[HEAD_DIM:HEAD_DIM + 1]))
            r = lax.rsqrt(jnp.mean(o_t * o_t, axis=0, keepdims=True) + RMS_EPS)
            o_t = ((o_t * r) * gsub_ref[...]) * (1.0 - lam_init)
            o_ref[0, :, h * HEAD_DIM:(h + 1) * HEAD_DIM] = o_t.T.astype(BF16)


def _diff_attention(qx, kx, vtx, positions, tbl_log2, diag, far_consts, lam_vecs, gsub, lam_init, bounded,
                    cast_srcs):
    assert TK_A <= DIAG_ROWS_MAX
    b, _, s, _ = kx.shape
    nq, nk = s // TQ_A, s // TK_A
    pq = positions.reshape(b, nq, TQ_A)
    pk = positions.reshape(b, nk, TK_A)
    ranges = (pq.min(-1), pq.max(-1), pk.min(-1), pk.max(-1),
              _consecutive_runs(positions, [(t * TQ_A, (t + 1) * TQ_A) for t in range(nq)]),
              _consecutive_runs(positions, [(t * TK_A, (t + 1) * TK_A) for t in range(nk)]))
    lam_spec = pl.BlockSpec((1, DIFF_QK_DIM), lambda i, q, k, *_: (0, 0))
    scratch = [pltpu.VMEM((2 * N_HEADS, V_ROWS, TQ_A), F32)]
    if not bounded:
        scratch.append(pltpu.VMEM((2 * N_HEADS, 1, TQ_A), F32))
    n_steps = b * nq * nk
    assert all(w.shape[0] % (16 * n_steps) == 0 for w in cast_srcs)
    cast_specs = [pl.BlockSpec((w.shape[0] // n_steps, w.shape[1]), lambda i, q, k, *_: ((i * nq + q) * nk + k, 0))
                  for w in cast_srcs]
    grid_spec = pltpu.PrefetchScalarGridSpec(
        num_scalar_prefetch=len(ranges),
        grid=(b, nq, nk),
        in_specs=[
            _smem(), lam_spec, lam_spec, lam_spec, lam_spec,
            pl.BlockSpec((1, 2 * N_HEADS, HEAD_DIM, TQ_A), lambda i, q, k, *_: (i, 0, 0, q)),
            pl.BlockSpec((1, 2 * N_HEADS, TK_A, HEAD_DIM), lambda i, q, k, *_: (i, 0, k, 0)),
            pl.BlockSpec((1, N_HEADS, 1, V_ROWS, TK_A), lambda i, q, k, *_: (i, 0, k, 0, 0)),
            pl.BlockSpec((1, 1, TQ_A), lambda i, q, k, *_: (i, 0, q)),
            pl.BlockSpec((1, TK_A, 1), lambda i, q, k, *_: (i, k, 0)),
            pl.BlockSpec((N_HEADS, 1, LANES), lambda i, q, k, *_: (0, 0, 0)),
            pl.BlockSpec((N_HEADS, DIAG_ROWS, LANES), lambda i, q, k, *_: (0, 0, 0), pipeline_mode=pl.Buffered(1)),
            pl.BlockSpec((HEAD_DIM, 1), lambda i, q, k, *_: (0, 0)),
            *cast_specs,
        ],
        out_specs=[pl.BlockSpec((1, TQ_A, MIX_WIDTH), lambda i, q, k, *_: (i, q, 0)), *cast_specs],
        scratch_shapes=scratch,
    )
    outs = pl.pallas_call(
        functools.partial(_diff_attn_kernel, lam_init=lam_init, bounded=bounded, n_cast=len(cast_srcs)),
        grid_spec=grid_spec,
        out_shape=[jax.ShapeDtypeStruct((b, s, MIX_WIDTH), BF16),
                   *[jax.ShapeDtypeStruct(w.shape, BF16) for w in cast_srcs]],
        compiler_params=_cparams(("arbitrary", "arbitrary", "arbitrary")),
        name="diff_attn_bounded" if bounded else "diff_attn_running_max",
    )(*ranges, far_consts, *[v.reshape(1, DIFF_QK_DIM) for v in lam_vecs],
      qx, kx, vtx, positions.reshape(b, 1, s), positions.reshape(b, s, 1), tbl_log2, diag,
      gsub.reshape(HEAD_DIM, 1), *cast_srcs)
    return outs[0], tuple(outs[1:])


def _prep_b_kernel(q_ref, k_ref, v_ref, qm_ref, gq_ref, gk_ref, gm_ref, qt_ref, kn_ref, vt_ref, qmn_ref):
    scale = HEAD_DIM ** -0.5 * LOG2E
    n_blk = TS_PREP // WINDOW
    ones_rows = jnp.where(lax.broadcasted_iota(I32, (V_ROWS - HEAD_DIM, TS_PREP), 0) == 0, 1.0, 0.0).astype(BF16)
    for h in range(N_HEADS):
        g, hg = divmod(h, GQA_GROUP)
        q = q_ref[0, :, h * HEAD_DIM:(h + 1) * HEAD_DIM]
        qt = (((q * _group_rms_scale(q, HEAD_DIM)) * gq_ref[...]) * scale).T.astype(BF16)
        for n in range(n_blk):
            qt_ref[0, n, g, :, hg * WINDOW:(hg + 1) * WINDOW] = qt[:, n * WINDOW:(n + 1) * WINDOW]
    for g in range(N_KV_HEADS):
        sl = slice(g * HEAD_DIM, (g + 1) * HEAD_DIM)
        k = k_ref[0, :, sl]
        kn_ref[0, g] = ((k * _group_rms_scale(k, HEAD_DIM)) * gk_ref[...]).astype(BF16)
        vt = jnp.concatenate([v_ref[0, :, sl].T.astype(BF16), ones_rows], axis=0)
        for n in range(n_blk):
            vt_ref[0, g, n] = vt[:, n * WINDOW:(n + 1) * WINDOW]
        qm = qm_ref[0, :, sl]
        qmn_ref[0, :, sl] = (((qm * _group_rms_scale(qm, HEAD_DIM)) * gm_ref[...]) * scale).astype(BF16)


def _prep_b(proj, gq, gk, gm):
    b, s, _ = proj.shape
    kblk = MIX_WIDTH // KV_WIDTH
    n_blk = TS_PREP // WINDOW
    g = lambda v: v.reshape(1, HEAD_DIM)
    return pl.pallas_call(
        _prep_b_kernel,
        grid=(b, s // TS_PREP),
        in_specs=[
            pl.BlockSpec((1, TS_PREP, MIX_WIDTH), lambda i, j: (i, j, 0)),
            pl.BlockSpec((1, TS_PREP, KV_WIDTH), lambda i, j: (i, j, kblk)),
            pl.BlockSpec((1, TS_PREP, KV_WIDTH), lambda i, j: (i, j, kblk + 1)),
            pl.BlockSpec((1, TS_PREP, MEM_WIDTH), lambda i, j: (i, j, kblk + 2)),
            pl.BlockSpec((1, HEAD_DIM), lambda i, j: (0, 0)),
            pl.BlockSpec((1, HEAD_DIM), lambda i, j: (0, 0)),
            pl.BlockSpec((1, HEAD_DIM), lambda i, j: (0, 0)),
        ],
        out_specs=[
            pl.BlockSpec((1, n_blk, N_KV_HEADS, HEAD_DIM, GQA_GROUP * WINDOW), lambda i, j: (i, j, 0, 0, 0)),
            pl.BlockSpec((1, N_KV_HEADS, TS_PREP, HEAD_DIM), lambda i, j: (i, 0, j, 0)),
            pl.BlockSpec((1, N_KV_HEADS, n_blk, V_ROWS, WINDOW), lambda i, j: (i, 0, j, 0, 0)),
            pl.BlockSpec((1, TS_PREP, MEM_WIDTH), lambda i, j: (i, j, 0)),
        ],
        out_shape=[
            jax.ShapeDtypeStruct((b, s // WINDOW, N_KV_HEADS, HEAD_DIM, GQA_GROUP * WINDOW), BF16),
            jax.ShapeDtypeStruct((b, N_KV_HEADS, s, HEAD_DIM), BF16),
            jax.ShapeDtypeStruct((b, N_KV_HEADS, s // WINDOW, V_ROWS, WINDOW), BF16),
            jax.ShapeDtypeStruct((b, s, MEM_WIDTH), BF16),
        ],
        compiler_params=_cparams(("parallel", "parallel")),
        name="prep_b",
    )(proj, proj, proj, proj, g(gq), g(gk), g(gm))


def _win_attn_kernel(sink_ref, run_ref, qt_ref, k_ref, vt_ref, posq_ref, posk_ref, tbl_ref, diag_ref, o_ref):
    n_blocks = posq_ref.shape[1]
    n_sub = o_ref.shape[1] // WINDOW
    kw = 3 * WINDOW
    lane3 = lax.broadcasted_iota(I32, (1, GQA_GROUP * WINDOW), 1)

    def window(nl):
        n = pl.program_id(1) * n_sub + nl
        nb = jnp.clip(n - 1, 0, n_blocks - 3)
        return n, nb, pl.multiple_of(nb * WINDOW, WINDOW)

    def logits(nl, g):
        start = window(nl)[2]
        return jnp.dot(k_ref[0, g, pl.ds(start, kw), :], qt_ref[0, nl, g], preferred_element_type=F32)

    def in_window(nl):
        n, _, start = window(nl)
        ki = start + lax.broadcasted_iota(I32, (kw, WINDOW), 0)
        qi = n * WINDOW + lax.broadcasted_iota(I32, (kw, WINDOW), 1)
        return jnp.abs(ki - qi) <= WINDOW

    def masked_bucket(nl):
        n, _, start = window(nl)
        rel = posk_ref[0, pl.ds(start, kw), :] - posq_ref[0, pl.ds(n, 1), :]
        return jnp.where(in_window(nl), _t5_bucket(rel), N_BUCKETS)

    def bias_from_bucket(bucket, nl, h):
        return _table_lookup(tbl_ref[h], bucket)

    def bias_from_diag(valid, nl, h):
        n, _, start = window(nl)
        off = pl.multiple_of(start - n * WINDOW + DIAG_CENTER, 16)
        return jnp.where(valid, diag_ref[h, pl.ds(off, kw), :], NEG_BIG)

    def finish(nl, g, s, block_state, head_bias):
        nb = window(nl)[1]
        heads = range(g * GQA_GROUP, (g + 1) * GQA_GROUP)
        z = s + jnp.concatenate([head_bias(block_state, nl, h) for h in heads], axis=1)
        sink = jnp.where(lane3 < WINDOW, sink_ref[heads[0]],
                         jnp.where(lane3 < 2 * WINDOW, sink_ref[heads[1]], sink_ref[heads[2]]))
        m = jnp.maximum(jnp.max(z, axis=0, keepdims=True), sink)
        p = jnp.exp2(z - m).astype(BF16)
        vt = jnp.concatenate([vt_ref[0, g, nb + t] for t in range(3)], axis=1)
        acc = jnp.dot(vt, p, preferred_element_type=F32)
        o_t = acc[:HEAD_DIM] / (acc[HEAD_DIM:HEAD_DIM + 1] + jnp.exp2(sink - m))
        for hg, h in enumerate(heads):
            o_ref[0, nl * WINDOW:(nl + 1) * WINDOW, h * HEAD_DIM:(h + 1) * HEAD_DIM] = (
                o_t[:, hg * WINDOW:(hg + 1) * WINDOW].T.astype(BF16))

    def all_chains(block_state_of, head_bias):
        chains = [(nl, g) for nl in range(n_sub) for g in range(N_KV_HEADS)]
        s_next = logits(*chains[0])
        block_state = None
        for idx, (nl, g) in enumerate(chains):
            s = s_next
            if idx + 1 < len(chains):
                s_next = logits(*chains[idx + 1])
            if g == 0:
                block_state = block_state_of(nl)
            finish(nl, g, s, block_state, head_bias)

    is_run = run_ref[pl.program_id(0), pl.program_id(1)] == 1

    @pl.when(is_run)
    def _():
        all_chains(in_window, bias_from_diag)

    @pl.when(jnp.logical_not(is_run))
    def _():
        all_chains(masked_bucket, bias_from_bucket)


def _win_attention(qt, kn, vt, positions, tbl_log2, diag, sink_log2):
    b, _, s, _ = kn.shape
    n_blocks = s // WINDOW
    n_sub = TQ_B // WINDOW
    runs = _consecutive_runs(positions, [(max(0, t * TQ_B - WINDOW), min(s, (t + 1) * TQ_B + WINDOW))
                                         for t in range(s // TQ_B)])
    return pl.pallas_call(
        _win_attn_kernel,
        grid=(b, s // TQ_B),
        in_specs=[
            _smem(), _smem(),
            pl.BlockSpec((1, n_sub, N_KV_HEADS, HEAD_DIM, GQA_GROUP * WINDOW), lambda i, j: (i, j, 0, 0, 0)),
            pl.BlockSpec((1, N_KV_HEADS, s, HEAD_DIM), lambda i, j: (i, 0, 0, 0)),
            pl.BlockSpec((1, N_KV_HEADS, n_blocks, V_ROWS, WINDOW), lambda i, j: (i, 0, 0, 0, 0)),
            pl.BlockSpec((1, n_blocks, WINDOW), lambda i, j: (i, 0, 0)),
            pl.BlockSpec((1, s, 1), lambda i, j: (i, 0, 0)),
            pl.BlockSpec((N_HEADS, 1, LANES), lambda i, j: (0, 0, 0)),
            pl.BlockSpec((N_HEADS, DIAG_ROWS, LANES), lambda i, j: (0, 0, 0), pipeline_mode=pl.Buffered(1)),
        ],
        out_specs=pl.BlockSpec((1, TQ_B, MIX_WIDTH), lambda i, j: (i, j, 0)),
        out_shape=jax.ShapeDtypeStruct((b, s, MIX_WIDTH), BF16),
        compiler_params=_cparams(("parallel", "parallel")),
        name="win_attn",
    )(sink_log2, runs, qt, kn, vt, positions.reshape(b, n_blocks, WINDOW), positions.reshape(b, s, 1), tbl_log2,
      diag)


def _mem_attn_kernel(q_ref, k_ref, v_ref, o_ref):
    for h in range(N_MEM_HEADS):
        sl = slice(h * HEAD_DIM, (h + 1) * HEAD_DIM)
        s = lax.dot_general(q_ref[0, :, sl], k_ref[0, :, sl], (((1,), (1,)), ((), ())), preferred_element_type=F32)
        e = jnp.exp2(s - jnp.max(s, axis=-1, keepdims=True))
        p = (e / jnp.sum(e, axis=-1, keepdims=True)).astype(BF16)
        o_ref[0, :, sl] = jnp.dot(p, v_ref[0, :, sl], preferred_element_type=F32).astype(BF16)


def _mem_attention(qmn, k_m, v_m):
    b, s, _ = qmn.shape
    return pl.pallas_call(
        _mem_attn_kernel,
        grid=(b, s // TQ_MEM),
        in_specs=[
            pl.BlockSpec((1, TQ_MEM, MEM_WIDTH), lambda i, j: (i, j, 0)),
            pl.BlockSpec((1, MEM_LEN, MEM_WIDTH), lambda i, j: (i, 0, 0)),
            pl.BlockSpec((1, MEM_LEN, MEM_WIDTH), lambda i, j: (i, 0, 0)),
        ],
        out_specs=pl.BlockSpec((1, TQ_MEM, MEM_WIDTH), lambda i, j: (i, j, 0)),
        out_shape=jax.ShapeDtypeStruct((b, s, MEM_WIDTH), BF16),
        compiler_params=_cparams(("parallel", "parallel")),
        name="mem_attn",
    )(qmn, k_m, v_m)


def _out_proj_kernel(x_ref, o_ref, om_ref, wo_ref, wm_ref, y_ref):
    y_ref[...] = (x_ref[...]
                  + jnp.dot(o_ref[...], wo_ref[...].astype(BF16), preferred_element_type=F32)
                  + jnp.dot(om_ref[...], wm_ref[...].astype(BF16), preferred_element_type=F32))


def _out_proj(x2d, o2d, om2d, w_stack, layer):
    m, d = x2d.shape
    return pl.pallas_call(
        _out_proj_kernel,
        grid=(m // TM_OUT, d // TN_OUT),
        in_specs=[
            pl.BlockSpec((TM_OUT, TN_OUT), lambda i, j: (i, j)),
            pl.BlockSpec((TM_OUT, MIX_WIDTH), lambda i, j: (i, 0)),
            pl.BlockSpec((TM_OUT, MEM_WIDTH), lambda i, j: (i, 0)),
            pl.BlockSpec((None, MIX_WIDTH, TN_OUT), lambda i, j: (layer, 0, j)),
            pl.BlockSpec((None, MEM_WIDTH, TN_OUT), lambda i, j: (layer, MIX_WIDTH // MEM_WIDTH, j)),
        ],
        out_specs=pl.BlockSpec((TM_OUT, TN_OUT), lambda i, j: (i, j)),
        out_shape=jax.ShapeDtypeStruct((m, d), F32),
        compiler_params=_cparams(("parallel", "parallel")),
        name="out_proj",
    )(x2d, o2d, om2d, w_stack, w_stack)


def _mlp_kernel(x_ref, g_ref, wu_ref, wd_ref, y_ref, h_ref):
    f = pl.program_id(1)

    @pl.when(f == 0)
    def _():
        x = x_ref[...]
        r = lax.rsqrt(jnp.mean(x * x, axis=-1, keepdims=True) + RMS_EPS)
        h_ref[...] = ((x * r) * g_ref[...]).astype(BF16)
        y_ref[...] = x

    u = jnp.maximum(jnp.dot(h_ref[...], wu_ref[...].astype(BF16), preferred_element_type=F32), 0.0)
    y_ref[...] += jnp.dot((u * u).astype(BF16), wd_ref[...].astype(BF16), preferred_element_type=F32)


def _mlp(x2d, gain, wu_stack, wd_stack, layer):
    m, d = x2d.shape
    return pl.pallas_call(
        _mlp_kernel,
        grid=(m // TM_MLP, D_FF // TF_MLP),
        in_specs=[
            pl.BlockSpec((TM_MLP, d), lambda i, f: (i, 0)),
            pl.BlockSpec((1, d), lambda i, f: (0, 0)),
            pl.BlockSpec((None, d, TF_MLP), lambda i, f: (layer, 0, f)),
            pl.BlockSpec((None, TF_MLP, d), lambda i, f: (layer, f, 0)),
        ],
        out_specs=pl.BlockSpec((TM_MLP, d), lambda i, f: (i, 0)),
        out_shape=jax.ShapeDtypeStruct((m, d), F32),
        scratch_shapes=[pltpu.VMEM((TM_MLP, d), BF16)],
        compiler_params=_cparams(("parallel", "arbitrary")),
        name="mlp",
    )(x2d, gain.reshape(1, d), wu_stack, wd_stack)


def _diag_table_kernel(tbl_ref, o_ref):
    r = lax.broadcasted_iota(I32, (DIAG_ROWS, LANES), 0)
    l = lax.broadcasted_iota(I32, (DIAG_ROWS, LANES), 1)
    o_ref[0] = _table_lookup(tbl_ref[0], _t5_bucket(r - l - DIAG_CENTER))


def _diag_bias_table(tbl_log2):
    return pl.pallas_call(
        _diag_table_kernel,
        grid=(N_HEADS,),
        in_specs=[pl.BlockSpec((1, 1, LANES), lambda h: (h, 0, 0))],
        out_specs=pl.BlockSpec((1, DIAG_ROWS, LANES), lambda h: (h, 0, 0)),
        out_shape=jax.ShapeDtypeStruct((N_HEADS, DIAG_ROWS, LANES), F32),
        compiler_params=_cparams(("parallel",)),
        name="diag_bias_table",
    )(tbl_log2)


def _consecutive_runs(positions, ranges):
    b = positions.shape[0]
    step_ok = (positions[:, 1:] - positions[:, :-1] == 1).astype(I32)
    c = jnp.concatenate([jnp.zeros((b, 1), I32), jnp.cumsum(step_ok, axis=1)], axis=1)
    return jnp.stack([(c[:, hi - 1] - c[:, lo] == hi - 1 - lo).astype(I32) for lo, hi in ranges], axis=1)


def _bias_tables(rel_bias):
    t = (rel_bias.astype(F32) * LOG2E).T
    tbl = jnp.zeros((N_HEADS, 1, LANES), F32).at[:, 0, :N_BUCKETS].set(t).at[:, 0, N_BUCKETS].set(NEG_BIG)
    far = jnp.stack([t[:, HALF_BUCKETS - 1], t[:, N_BUCKETS - 1]])
    return tbl, far


def _diff_logit_shift(gq, gk, rel_bias):
    qk = 1.01 * DIFF_QK_DIM * jnp.max(jnp.abs(gq)) * jnp.max(jnp.abs(gk)) * (DIFF_QK_DIM ** -0.5 * LOG2E)
    return jnp.ceil(qk + jnp.max(jnp.abs(rel_bias)) * LOG2E).astype(F32)


def kernel(x, mem, positions, rel_bias, norm_attn, norm_mem, norm_mlp, w_in_a, a_q_norm, a_k_norm, a_lambda_q1, a_lambda_k1, a_lambda_q2, a_lambda_k2, a_subln, w_in_b, b_q_norm, b_k_norm, b_sink, w_mem_kv, m_q_norm, m_k_norm, w_out, w_up, w_down):
    b, s, d = x.shape
    depth = norm_attn.shape[0]
    tbl_log2, far_log2 = _bias_tables(rel_bias)
    diag = _diag_bias_table(tbl_log2)
    x2d = x.reshape(b * s, d)
    later_w = (w_out, w_up, w_down, w_in_b)
    for i in range(depth):
        j = i // 2
        k_m, v_m = _mem_kv(mem, norm_mem[i], w_mem_kv, i, m_k_norm[i])
        if i % 2 == 0:
            proj = _norm_proj(x2d, norm_attn[i], w_in_a, j).reshape(b, s, -1)
            shift = _diff_logit_shift(a_q_norm[j], a_k_norm[j], rel_bias)
            bounded = shift <= SHIFT_LIMIT
            qx, kx, vtx, qmn = _prep_a(proj, a_q_norm[j], a_k_norm[j], m_q_norm[i], jnp.where(bounded, -shift, 0.0))
            lam_init = 0.8 - 0.6 * math.exp(-0.3 * i)
            cast_srcs = tuple(w.reshape(-1, w.shape[-1]) for w in later_w) if i == 0 else ()
            attn = functools.partial(
                _diff_attention, qx, kx, vtx, positions, tbl_log2, diag,
                lam_vecs=(a_lambda_q1[j], a_lambda_k1[j], a_lambda_q2[j], a_lambda_k2[j]), gsub=a_subln[j],
                lam_init=lam_init, cast_srcs=cast_srcs)
            o, casts = lax.cond(bounded,
                                lambda: attn(far_consts=jnp.exp2(far_log2), bounded=True),
                                lambda: attn(far_consts=far_log2, bounded=False))
            if i == 0:
                later_w = tuple(c.reshape(w.shape) for c, w in zip(casts, later_w))
                w_out, w_up, w_down, w_in_b = later_w
        else:
            proj = _norm_proj(x2d, norm_attn[i], w_in_b, j).reshape(b, s, -1)
            qt, kn, vt, qmn = _prep_b(proj, b_q_norm[j], b_k_norm[j], m_q_norm[i])
            o = _win_attention(qt, kn, vt, positions, tbl_log2, diag, b_sink[j].astype(F32) * LOG2E)
        o_m = _mem_attention(qmn, k_m, v_m)
        x2d = _out_proj(x2d, o.reshape(b * s, MIX_WIDTH), o_m.reshape(b * s, MEM_WIDTH), w_out, i)
        x2d = _mlp(x2d, norm_mlp[i], w_up, w_down, i)
    return x2d.reshape(b, s, d)
```

```python
import functools
import math

import jax
import jax.numpy as jnp
from jax import lax
from jax.experimental import pallas as pl
from jax.experimental.pallas import tpu as pltpu

F32 = jnp.float32
BF16 = jnp.bfloat16
I32 = jnp.int32

D_MODEL = 2048
N_HEADS = 12
HEAD_DIM = 128
DIFF_QK_DIM = 64
N_KV_HEADS = 4
GQA_GROUP = 3
MIX_WIDTH = N_HEADS * HEAD_DIM
KV_WIDTH = N_KV_HEADS * HEAD_DIM
WINDOW = 128
N_MEM_HEADS = 4
MEM_WIDTH = N_MEM_HEADS * HEAD_DIM
MEM_LEN = 256
D_FF = 4 * D_MODEL
N_BUCKETS = 32
MAX_DISTANCE = 128
RMS_EPS = 1e-6
NEG_BIG = -1e30
LOG2E = math.log2(math.e)

HALF_BUCKETS = N_BUCKETS // 2
MAX_EXACT = HALF_BUCKETS // 2
FAR_DIST = 91

ONE_COL = DIFF_QK_DIM
V_ROWS = HEAD_DIM + 16
SHIFT_LIMIT = 50.0

LANES = 128

DIAG_ROWS_MAX = 512
DIAG_OFF_MIN = 5
DIAG_CENTER = DIAG_OFF_MIN + DIAG_ROWS_MAX + FAR_DIST
DIAG_OFF_MAX = DIAG_CENTER + FAR_DIST + LANES - 1
DIAG_ROWS = -(-(DIAG_OFF_MAX + DIAG_ROWS_MAX) // 16) * 16

VMEM_LIMIT = 56 * 1024 * 1024

TM_PROJ, TN_PROJ = 1024, 1024
TS_PREP = 512
TQ_A, TK_A = 512, 512
TQ_B = 512
TQ_MEM = 512
TM_OUT, TN_OUT = 512, 2048
TM_MLP, TF_MLP = 1024, 512


def _cparams(sem):
    return pltpu.CompilerParams(dimension_semantics=sem, vmem_limit_bytes=VMEM_LIMIT)


def _smem():
    return pl.BlockSpec(memory_space=pltpu.SMEM)


def _t5_bucket(rel):
    side = jnp.where(rel > 0, HALF_BUCKETS, 0)
    n = jnp.abs(rel)
    n_f = jnp.maximum(n, 1).astype(F32)
    large = MAX_EXACT + (jnp.log(n_f / MAX_EXACT) / math.log(MAX_DISTANCE / MAX_EXACT)
                         * (HALF_BUCKETS - MAX_EXACT)).astype(I32)
    large = jnp.minimum(large, HALF_BUCKETS - 1)
    return side + jnp.where(n < MAX_EXACT, n, large)


def _table_lookup(tbl_row, bucket):
    rows, cols = bucket.shape
    tb = jnp.broadcast_to(tbl_row, (rows, LANES))
    parts = [jnp.take_along_axis(tb, bucket[:, c:c + LANES], axis=1) for c in range(0, cols, LANES)]
    return parts[0] if len(parts) == 1 else jnp.concatenate(parts, axis=1)


def _group_rms_scale(x, group):
    t = x * x
    if group == LANES:
        return lax.rsqrt(jnp.mean(t, axis=-1, keepdims=True) + RMS_EPS)
    lane = lax.broadcasted_iota(I32, x.shape, 1)
    lo = lane < group
    s_lo = jnp.sum(jnp.where(lo, t, 0.0), axis=-1, keepdims=True)
    s_hi = jnp.sum(jnp.where(lo, 0.0, t), axis=-1, keepdims=True)
    return jnp.where(lo, lax.rsqrt(s_lo / group + RMS_EPS), lax.rsqrt(s_hi / group + RMS_EPS))


def _proj_kernel(x_ref, g_ref, w_ref, o_ref, h_ref):
    @pl.when(pl.program_id(1) == 0)
    def _():
        x = x_ref[...]
        r = lax.rsqrt(jnp.mean(x * x, axis=-1, keepdims=True) + RMS_EPS)
        h_ref[...] = ((x * r) * g_ref[...]).astype(BF16)

    o_ref[...] = jnp.dot(h_ref[...], w_ref[...].astype(BF16), preferred_element_type=F32).astype(BF16)


def _norm_proj(x2d, gain, w_stack, layer):
    m, d = x2d.shape
    n = w_stack.shape[2]
    return pl.pallas_call(
        _proj_kernel,
        grid=(m // TM_PROJ, n // TN_PROJ),
        in_specs=[
            pl.BlockSpec((TM_PROJ, d), lambda i, j: (i, 0)),
            pl.BlockSpec((1, d), lambda i, j: (0, 0)),
            pl.BlockSpec((None, d, TN_PROJ), lambda i, j: (layer, 0, j)),
        ],
        out_specs=pl.BlockSpec((TM_PROJ, TN_PROJ), lambda i, j: (i, j)),
        out_shape=jax.ShapeDtypeStruct((m, n), BF16),
        scratch_shapes=[pltpu.VMEM((TM_PROJ, d), BF16)],
        compiler_params=_cparams(("parallel", "arbitrary")),
        name="norm_proj",
    )(x2d, gain.reshape(1, d), w_stack)


def _mem_kv_kernel(mem_ref, g_ref, w_ref, gk_ref, k_ref, v_ref):
    x = mem_ref[0]
    r = lax.rsqrt(jnp.mean(x * x, axis=-1, keepdims=True) + RMS_EPS)
    mn = ((x * r) * g_ref[...]).astype(BF16)
    mkv = jnp.dot(mn, w_ref[...].astype(BF16), preferred_element_type=F32)
    for h in range(N_MEM_HEADS):
        kh = mkv[:, h * HEAD_DIM:(h + 1) * HEAD_DIM]
        k_ref[0, :, h * HEAD_DIM:(h + 1) * HEAD_DIM] = ((kh * _group_rms_scale(kh, HEAD_DIM)) * gk_ref[...]).astype(BF16)
    v_ref[0] = mkv[:, MEM_WIDTH:].astype(BF16)


def _mem_kv(mem, gain, w_stack, layer, gk):
    b = mem.shape[0]
    shp = jax.ShapeDtypeStruct((b, MEM_LEN, MEM_WIDTH), BF16)
    return pl.pallas_call(
        _mem_kv_kernel,
        grid=(b,),
        in_specs=[
            pl.BlockSpec((1, MEM_LEN, D_MODEL), lambda i: (i, 0, 0)),
            pl.BlockSpec((1, D_MODEL), lambda i: (0, 0)),
            pl.BlockSpec((None, D_MODEL, 2 * MEM_WIDTH), lambda i: (layer, 0, 0)),
            pl.BlockSpec((1, HEAD_DIM), lambda i: (0, 0)),
        ],
        out_specs=[pl.BlockSpec((1, MEM_LEN, MEM_WIDTH), lambda i: (i, 0, 0))] * 2,
        out_shape=[shp, shp],
        compiler_params=_cparams(("parallel",)),
        name="mem_kv",
    )(mem, gain.reshape(1, D_MODEL), w_stack, gk.reshape(1, HEAD_DIM))


def _prep_a_kernel(negm_ref, q_ref, k_ref, v_ref, qm_ref, gq_ref, gk_ref, gm_ref, qx_ref, kx_ref, vtx_ref, qmn_ref):
    q_scale = DIFF_QK_DIM ** -0.5 * LOG2E
    m_scale = HEAD_DIM ** -0.5 * LOG2E
    lane = lax.broadcasted_iota(I32, (TS_PREP, HEAD_DIM), 1)
    row = lax.broadcasted_iota(I32, (HEAD_DIM, TS_PREP), 0)
    lo_lanes, lo_rows = lane < DIFF_QK_DIM, row < DIFF_QK_DIM
    neg_shift = negm_ref[0]
    ones_rows = jnp.where(lax.broadcasted_iota(I32, (V_ROWS - HEAD_DIM, TS_PREP), 0) == 0, 1.0, 0.0).astype(BF16)
    for h in range(N_HEADS):
        sl = slice(h * HEAD_DIM, (h + 1) * HEAD_DIM)
        qt = q_ref[0, :, sl].astype(F32).T
        t = qt * qt
        r_lo = lax.rsqrt(jnp.mean(t[:DIFF_QK_DIM], axis=0, keepdims=True) + RMS_EPS)
        r_hi = lax.rsqrt(jnp.mean(t[DIFF_QK_DIM:], axis=0, keepdims=True) + RMS_EPS)
        qn = ((qt * jnp.where(lo_rows, r_lo, r_hi)) * gq_ref[...]) * q_scale
        qx_ref[0, 2 * h] = jnp.where(lo_rows, qn, jnp.where(row == ONE_COL, neg_shift, 0.0)).astype(BF16)
        qx_ref[0, 2 * h + 1] = jnp.where(lo_rows, jnp.where(row == ONE_COL - 1, neg_shift, 0.0), qn).astype(BF16)
        k = k_ref[0, :, sl].astype(F32)
        kn = (k * _group_rms_scale(k, DIFF_QK_DIM)) * gk_ref[...]
        kx_ref[0, 2 * h] = jnp.where(lo_lanes, kn, jnp.where(lane == ONE_COL, 1.0, 0.0)).astype(BF16)
        kx_ref[0, 2 * h + 1] = jnp.where(lo_lanes, jnp.where(lane == ONE_COL - 1, 1.0, 0.0), kn).astype(BF16)
        vt = v_ref[0, :, sl].astype(F32).T.astype(BF16)
        vtx_ref[0, h, 0] = jnp.concatenate([vt, ones_rows], axis=0)
    for h in range(N_MEM_HEADS):
        sl = slice(h * HEAD_DIM, (h + 1) * HEAD_DIM)
        qm = qm_ref[0, :, sl].astype(F32)
        qmn_ref[0, :, sl] = (((qm * _group_rms_scale(qm, HEAD_DIM)) * gm_ref[...]) * m_scale).astype(BF16)


def _prep_a(proj, gq, gk, gm, neg_shift):
    assert TS_PREP == TK_A
    b, s, _ = proj.shape
    gq2 = jnp.concatenate([gq, gq]).reshape(HEAD_DIM, 1)
    gk2 = jnp.concatenate([gk, gk]).reshape(1, HEAD_DIM)
    wblk = MIX_WIDTH // MEM_WIDTH
    return pl.pallas_call(
        _prep_a_kernel,
        grid=(b, s // TS_PREP),
        in_specs=[
            _smem(),
            pl.BlockSpec((1, TS_PREP, MIX_WIDTH), lambda i, j: (i, j, 0)),
            pl.BlockSpec((1, TS_PREP, MIX_WIDTH), lambda i, j: (i, j, 1)),
            pl.BlockSpec((1, TS_PREP, MIX_WIDTH), lambda i, j: (i, j, 2)),
            pl.BlockSpec((1, TS_PREP, MEM_WIDTH), lambda i, j: (i, j, 3 * wblk)),
            pl.BlockSpec((HEAD_DIM, 1), lambda i, j: (0, 0)),
            pl.BlockSpec((1, HEAD_DIM), lambda i, j: (0, 0)),
            pl.BlockSpec((1, HEAD_DIM), lambda i, j: (0, 0)),
        ],
        out_specs=[
            pl.BlockSpec((1, 2 * N_HEADS, HEAD_DIM, TS_PREP), lambda i, j: (i, 0, 0, j)),
            pl.BlockSpec((1, 2 * N_HEADS, TS_PREP, HEAD_DIM), lambda i, j: (i, 0, j, 0)),
            pl.BlockSpec((1, N_HEADS, 1, V_ROWS, TK_A), lambda i, j: (i, 0, j, 0, 0)),
            pl.BlockSpec((1, TS_PREP, MEM_WIDTH), lambda i, j: (i, j, 0)),
        ],
        out_shape=[
            jax.ShapeDtypeStruct((b, 2 * N_HEADS, HEAD_DIM, s), BF16),
            jax.ShapeDtypeStruct((b, 2 * N_HEADS, s, HEAD_DIM), BF16),
            jax.ShapeDtypeStruct((b, N_HEADS, s // TK_A, V_ROWS, TK_A), BF16),
            jax.ShapeDtypeStruct((b, s, MEM_WIDTH), BF16),
        ],
        compiler_params=_cparams(("parallel", "parallel")),
        name="prep_a",
    )(neg_shift.reshape(1), proj, proj, proj, proj, gq2, gk2, gm.reshape(1, HEAD_DIM))


def _diff_attn_kernel(qlo_ref, qhi_ref, klo_ref, khi_ref, qrun_ref, krun_ref,
                      far_ref, lq1_ref, lk1_ref, lq2_ref, lk2_ref,
                      qx_ref, kx_ref, vtx_ref, posq_ref, posk_ref, tbl_ref, diag_ref, gsub_ref,
                      *rest, lam_init, bounded, n_cast):
    w_f32_refs, o_ref, w_bf16_refs = rest[:n_cast], rest[n_cast], rest[n_cast + 1:2 * n_cast + 1]
    acc_ref, m_scratch = rest[2 * n_cast + 1], rest[2 * n_cast + 2:]
    b, iq, j = pl.program_id(0), pl.program_id(1), pl.program_id(2)
    pos_far = klo_ref[b, j] - qhi_ref[b, iq] >= FAR_DIST
    neg_far = khi_ref[b, j] - qlo_ref[b, iq] <= -FAR_DIST
    far = jnp.logical_or(pos_far, neg_far)

    @pl.when(j == 0)
    def _():
        acc_ref[...] = jnp.zeros(acc_ref.shape, F32)
        if not bounded:
            m_scratch[0][...] = jnp.full(m_scratch[0].shape, NEG_BIG, F32)

    def logits(hc):
        return jnp.dot(kx_ref[0, hc], qx_ref[0, hc], preferred_element_type=F32)

    def accumulate(hc, s, far_const):
        h = hc // 2
        if bounded:
            pv = jnp.dot(vtx_ref[0, h, 0], jnp.exp2(s).astype(BF16), preferred_element_type=F32)
            acc_ref[hc] += pv if far_const is None else far_const * pv
        else:
            m_ref = m_scratch[0]
            off = 0.0 if far_const is None else far_const
            m_old = m_ref[hc]
            m_new = jnp.maximum(m_old, jnp.max(s, axis=0, keepdims=True) + off)
            p = jnp.exp2(s - (m_new - off)).astype(BF16)
            acc_ref[hc] = (jnp.exp2(m_old - m_new) * acc_ref[hc]
                           + jnp.dot(vtx_ref[0, h, 0], p, preferred_element_type=F32))
            m_ref[hc] = m_new

    def all_heads(bias_of_head, far_const_of_head):
        for src, dst in zip(w_f32_refs, w_bf16_refs):
            dst[...] = src[...].astype(BF16)
        s_next = logits(0)
        bias_next = None if bias_of_head is None else bias_of_head(0)
        for hc in range(2 * N_HEADS):
            h, c = divmod(hc, 2)
            s = s_next
            if hc + 1 < 2 * N_HEADS:
                s_next = logits(hc + 1)
            if bias_of_head is not None:
                if c == 0:
                    bias = bias_next
                    if h + 1 < N_HEADS:
                        bias_next = bias_of_head(h + 1)
                s = s + bias
            accumulate(hc, s, None if far_const_of_head is None else far_const_of_head(h))

    @pl.when(far)
    def _():
        all_heads(None, lambda h: jnp.where(pos_far, far_ref[1, h], far_ref[0, h]))

    runs = jnp.logical_and(qrun_ref[b, iq] == 1, krun_ref[b, j] == 1)

    @pl.when(jnp.logical_and(jnp.logical_not(far), runs))
    def _():
        d0 = klo_ref[b, j] - qlo_ref[b, iq]

        def bias_of_head(h):
            offs = [jnp.clip(d0 - c + DIAG_CENTER, DIAG_OFF_MIN, DIAG_OFF_MAX) for c in range(0, TQ_A, LANES)]
            return jnp.concatenate([diag_ref[h, pl.ds(off, TK_A), :] for off in offs], axis=1)

        all_heads(bias_of_head, None)

    @pl.when(jnp.logical_and(jnp.logical_not(far), jnp.logical_not(runs)))
    def _():
        bucket = []

        def bias_of_head(h):
            if not bucket:
                bucket.append(_t5_bucket(posk_ref[0] - posq_ref[0]))
            return _table_lookup(tbl_ref[h], bucket[0])

        all_heads(bias_of_head, None)

    @pl.when(j == pl.num_programs(2) - 1)
    def _():
        lam = (jnp.exp(jnp.sum(lq1_ref[...] * lk1_ref[...], axis=-1, keepdims=True))
               - jnp.exp(jnp.sum(lq2_ref[...] * lk2_ref[...], axis=-1, keepdims=True)) + lam_init)
        for h in range(N_HEADS):
            a0, a1 = acc_ref[2 * h], acc_ref[2 * h + 1]
            o_t = (a0[:HEAD_DIM] / a0[HEAD_DIM:HEAD_DIM + 1]
                   - lam * (a1[:HEAD_DIM] / a1[HEAD_DIM:HEAD_DIM + 1]))
            r = lax.rsqrt(jnp.mean(o_t * o_t, axis=0, keepdims=True) + RMS_EPS)
            o_t = ((o_t * r) * gsub_ref[...]) * (1.0 - lam_init)
            o_ref[0, :, h * HEAD_DIM:(h + 1) * HEAD_DIM] = o_t.T.astype(BF16)


def _diff_attention(qx, kx, vtx, positions, tbl_log2, diag, far_consts, lam_vecs, gsub, lam_init, bounded,
                    cast_srcs):
    assert TK_A <= DIAG_ROWS_MAX
    b, _, s, _ = kx.shape
    nq, nk = s // TQ_A, s // TK_A
    pq = positions.reshape(b, nq, TQ_A)
    pk = positions.reshape(b, nk, TK_A)
    ranges = (pq.min(-1), pq.max(-1), pk.min(-1), pk.max(-1),
              _consecutive_runs(positions, [(t * TQ_A, (t + 1) * TQ_A) for t in range(nq)]),
              _consecutive_runs(positions, [(t * TK_A, (t + 1) * TK_A) for t in range(nk)]))
    lam_spec = pl.BlockSpec((1, DIFF_QK_DIM), lambda i, q, k, *_: (0, 0))
    scratch = [pltpu.VMEM((2 * N_HEADS, V_ROWS, TQ_A), F32)]
    if not bounded:
        scratch.append(pltpu.VMEM((2 * N_HEADS, 1, TQ_A), F32))
    n_steps = b * nq * nk
    assert all(w.shape[0] % (16 * n_steps) == 0 for w in cast_srcs)
    cast_specs = [pl.BlockSpec((w.shape[0] // n_steps, w.shape[1]), lambda i, q, k, *_: ((i * nq + q) * nk + k, 0))
                  for w in cast_srcs]
    grid_spec = pltpu.PrefetchScalarGridSpec(
        num_scalar_prefetch=len(ranges),
        grid=(b, nq, nk),
        in_specs=[
            _smem(), lam_spec, lam_spec, lam_spec, lam_spec,
            pl.BlockSpec((1, 2 * N_HEADS, HEAD_DIM, TQ_A), lambda i, q, k, *_: (i, 0, 0, q)),
            pl.BlockSpec((1, 2 * N_HEADS, TK_A, HEAD_DIM), lambda i, q, k, *_: (i, 0, k, 0)),
            pl.BlockSpec((1, N_HEADS, 1, V_ROWS, TK_A), lambda i, q, k, *_: (i, 0, k, 0, 0)),
            pl.BlockSpec((1, 1, TQ_A), lambda i, q, k, *_: (i, 0, q)),
            pl.BlockSpec((1, TK_A, 1), lambda i, q, k, *_: (i, k, 0)),
            pl.BlockSpec((N_HEADS, 1, LANES), lambda i, q, k, *_: (0, 0, 0)),
            pl.BlockSpec((N_HEADS, DIAG_ROWS, LANES), lambda i, q, k, *_: (0, 0, 0), pipeline_mode=pl.Buffered(1)),
            pl.BlockSpec((HEAD_DIM, 1), lambda i, q, k, *_: (0, 0)),
            *cast_specs,
        ],
        out_specs=[pl.BlockSpec((1, TQ_A, MIX_WIDTH), lambda i, q, k, *_: (i, q, 0)), *cast_specs],
        scratch_shapes=scratch,
    )
    outs = pl.pallas_call(
        functools.partial(_diff_attn_kernel, lam_init=lam_init, bounded=bounded, n_cast=len(cast_srcs)),
        grid_spec=grid_spec,
        out_shape=[jax.ShapeDtypeStruct((b, s, MIX_WIDTH), BF16),
                   *[jax.ShapeDtypeStruct(w.shape, BF16) for w in cast_srcs]],
        compiler_params=_cparams(("arbitrary", "arbitrary", "arbitrary")),
        name="diff_attn_bounded" if bounded else "diff_attn_running_max",
    )(*ranges, far_consts, *[v.reshape(1, DIFF_QK_DIM) for v in lam_vecs],
      qx, kx, vtx, positions.reshape(b, 1, s), positions.reshape(b, s, 1), tbl_log2, diag,
      gsub.reshape(HEAD_DIM, 1), *cast_srcs)
    return outs[0], tuple(outs[1:])


def _prep_b_kernel(q_ref, k_ref, v_ref, qm_ref, gq_ref, gk_ref, gm_ref, qt_ref, kn_ref, vt_ref, qmn_ref):
    scale = HEAD_DIM ** -0.5 * LOG2E
    n_blk = TS_PREP // WINDOW
    ones_rows = jnp.where(lax.broadcasted_iota(I32, (V_ROWS - HEAD_DIM, TS_PREP), 0) == 0, 1.0, 0.0).astype(BF16)
    for h in range(N_HEADS):
        g, hg = divmod(h, GQA_GROUP)
        qt = q_ref[0, :, h * HEAD_DIM:(h + 1) * HEAD_DIM].astype(F32).T
        r = lax.rsqrt(jnp.mean(qt * qt, axis=0, keepdims=True) + RMS_EPS)
        qt = (((qt * r) * gq_ref[...]) * scale).astype(BF16)
        for n in range(n_blk):
            qt_ref[0, n, g, :, hg * WINDOW:(hg + 1) * WINDOW] = qt[:, n * WINDOW:(n + 1) * WINDOW]
    for g in range(N_KV_HEADS):
        sl = slice(g * HEAD_DIM, (g + 1) * HEAD_DIM)
        k = k_ref[0, :, sl].astype(F32)
        kn_ref[0, g] = ((k * _group_rms_scale(k, HEAD_DIM)) * gk_ref[...]).astype(BF16)
        vt = jnp.concatenate([v_ref[0, :, sl].astype(F32).T.astype(BF16), ones_rows], axis=0)
        for n in range(n_blk):
            vt_ref[0, g, n] = vt[:, n * WINDOW:(n + 1) * WINDOW]
        qm = qm_ref[0, :, sl].astype(F32)
        qmn_ref[0, :, sl] = (((qm * _group_rms_scale(qm, HEAD_DIM)) * gm_ref[...]) * scale).astype(BF16)


def _prep_b(proj, gq, gk, gm):
    b, s, _ = proj.shape
    kblk = MIX_WIDTH // KV_WIDTH
    n_blk = TS_PREP // WINDOW
    g = lambda v: v.reshape(1, HEAD_DIM)
    return pl.pallas_call(
        _prep_b_kernel,
        grid=(b, s // TS_PREP),
        in_specs=[
            pl.BlockSpec((1, TS_PREP, MIX_WIDTH), lambda i, j: (i, j, 0)),
            pl.BlockSpec((1, TS_PREP, KV_WIDTH), lambda i, j: (i, j, kblk)),
            pl.BlockSpec((1, TS_PREP, KV_WIDTH), lambda i, j: (i, j, kblk + 1)),
            pl.BlockSpec((1, TS_PREP, MEM_WIDTH), lambda i, j: (i, j, kblk + 2)),
            pl.BlockSpec((HEAD_DIM, 1), lambda i, j: (0, 0)),
            pl.BlockSpec((1, HEAD_DIM), lambda i, j: (0, 0)),
            pl.BlockSpec((1, HEAD_DIM), lambda i, j: (0, 0)),
        ],
        out_specs=[
            pl.BlockSpec((1, n_blk, N_KV_HEADS, HEAD_DIM, GQA_GROUP * WINDOW), lambda i, j: (i, j, 0, 0, 0)),
            pl.BlockSpec((1, N_KV_HEADS, TS_PREP, HEAD_DIM), lambda i, j: (i, 0, j, 0)),
            pl.BlockSpec((1, N_KV_HEADS, n_blk, V_ROWS, WINDOW), lambda i, j: (i, 0, j, 0, 0)),
            pl.BlockSpec((1, TS_PREP, MEM_WIDTH), lambda i, j: (i, j, 0)),
        ],
        out_shape=[
            jax.ShapeDtypeStruct((b, s // WINDOW, N_KV_HEADS, HEAD_DIM, GQA_GROUP * WINDOW), BF16),
            jax.ShapeDtypeStruct((b, N_KV_HEADS, s, HEAD_DIM), BF16),
            jax.ShapeDtypeStruct((b, N_KV_HEADS, s // WINDOW, V_ROWS, WINDOW), BF16),
            jax.ShapeDtypeStruct((b, s, MEM_WIDTH), BF16),
        ],
        compiler_params=_cparams(("parallel", "parallel")),
        name="prep_b",
    )(proj, proj, proj, proj, gq.reshape(HEAD_DIM, 1), g(gk), g(gm))


def _win_attn_kernel(sink_ref, run_ref, qt_ref, k_ref, vt_ref, posq_ref, posk_ref, tbl_ref, diag_ref, o_ref):
    n_blocks = posq_ref.shape[1]
    n_sub = o_ref.shape[1] // WINDOW
    kw = 3 * WINDOW
    lane3 = lax.broadcasted_iota(I32, (1, GQA_GROUP * WINDOW), 1)

    def window(nl):
        n = pl.program_id(1) * n_sub + nl
        nb = jnp.clip(n - 1, 0, n_blocks - 3)
        return n, nb, pl.multiple_of(nb * WINDOW, WINDOW)

    def logits(nl, g):
        start = window(nl)[2]
        return jnp.dot(k_ref[0, g, pl.ds(start, kw), :], qt_ref[0, nl, g], preferred_element_type=F32)

    def in_window(nl):
        n, _, start = window(nl)
        ki = start + lax.broadcasted_iota(I32, (kw, WINDOW), 0)
        qi = n * WINDOW + lax.broadcasted_iota(I32, (kw, WINDOW), 1)
        return jnp.abs(ki - qi) <= WINDOW

    def masked_bucket(nl):
        n, _, start = window(nl)
        rel = posk_ref[0, pl.ds(start, kw), :] - posq_ref[0, pl.ds(n, 1), :]
        return jnp.where(in_window(nl), _t5_bucket(rel), N_BUCKETS)

    def bias_from_bucket(bucket, nl, h):
        return _table_lookup(tbl_ref[h], bucket)

    def bias_from_diag(valid, nl, h):
        n, _, start = window(nl)
        off = pl.multiple_of(start - n * WINDOW + DIAG_CENTER, 16)
        return jnp.where(valid, diag_ref[h, pl.ds(off, kw), :], NEG_BIG)

    def finish(nl, g, s, block_state, head_bias):
        nb = window(nl)[1]
        heads = range(g * GQA_GROUP, (g + 1) * GQA_GROUP)
        z = s + jnp.concatenate([head_bias(block_state, nl, h) for h in heads], axis=1)
        sink = jnp.where(lane3 < WINDOW, sink_ref[heads[0]],
                         jnp.where(lane3 < 2 * WINDOW, sink_ref[heads[1]], sink_ref[heads[2]]))
        m = jnp.maximum(jnp.max(z, axis=0, keepdims=True), sink)
        p = jnp.exp2(z - m).astype(BF16)
        vt = jnp.concatenate([vt_ref[0, g, nb + t] for t in range(3)], axis=1)
        acc = jnp.dot(vt, p, preferred_element_type=F32)
        o_t = acc[:HEAD_DIM] / (acc[HEAD_DIM:HEAD_DIM + 1] + jnp.exp2(sink - m))
        for hg, h in enumerate(heads):
            o_ref[0, nl * WINDOW:(nl + 1) * WINDOW, h * HEAD_DIM:(h + 1) * HEAD_DIM] = (
                o_t[:, hg * WINDOW:(hg + 1) * WINDOW].T.astype(BF16))

    def all_chains(block_state_of, head_bias):
        chains = [(nl, g) for nl in range(n_sub) for g in range(N_KV_HEADS)]
        s_next = logits(*chains[0])
        block_state = None
        for idx, (nl, g) in enumerate(chains):
            s = s_next
            if idx + 1 < len(chains):
                s_next = logits(*chains[idx + 1])
            if g == 0:
                block_state = block_state_of(nl)
            finish(nl, g, s, block_state, head_bias)

    is_run = run_ref[pl.program_id(0), pl.program_id(1)] == 1

    @pl.when(is_run)
    def _():
        all_chains(in_window, bias_from_diag)

    @pl.when(jnp.logical_not(is_run))
    def _():
        all_chains(masked_bucket, bias_from_bucket)


def _win_attention(qt, kn, vt, positions, tbl_log2, diag, sink_log2):
    b, _, s, _ = kn.shape
    n_blocks = s // WINDOW
    n_sub = TQ_B // WINDOW
    runs = _consecutive_runs(positions, [(max(0, t * TQ_B - WINDOW), min(s, (t + 1) * TQ_B + WINDOW))
                                         for t in range(s // TQ_B)])
    return pl.pallas_call(
        _win_attn_kernel,
        grid=(b, s // TQ_B),
        in_specs=[
            _smem(), _smem(),
            pl.BlockSpec((1, n_sub, N_KV_HEADS, HEAD_DIM, GQA_GROUP * WINDOW), lambda i, j: (i, j, 0, 0, 0)),
            pl.BlockSpec((1, N_KV_HEADS, s, HEAD_DIM), lambda i, j: (i, 0, 0, 0)),
            pl.BlockSpec((1, N_KV_HEADS, n_blocks, V_ROWS, WINDOW), lambda i, j: (i, 0, 0, 0, 0)),
            pl.BlockSpec((1, n_blocks, WINDOW), lambda i, j: (i, 0, 0)),
            pl.BlockSpec((1, s, 1), lambda i, j: (i, 0, 0)),
            pl.BlockSpec((N_HEADS, 1, LANES), lambda i, j: (0, 0, 0)),
            pl.BlockSpec((N_HEADS, DIAG_ROWS, LANES), lambda i, j: (0, 0, 0), pipeline_mode=pl.Buffered(1)),
        ],
        out_specs=pl.BlockSpec((1, TQ_B, MIX_WIDTH), lambda i, j: (i, j, 0)),
        out_shape=jax.ShapeDtypeStruct((b, s, MIX_WIDTH), BF16),
        compiler_params=_cparams(("parallel", "parallel")),
        name="win_attn",
    )(sink_log2, runs, qt, kn, vt, positions.reshape(b, n_blocks, WINDOW), positions.reshape(b, s, 1), tbl_log2,
      diag)


def _mem_attn_kernel(q_ref, k_ref, v_ref, o_ref):
    for h in range(N_MEM_HEADS):
        sl = slice(h * HEAD_DIM, (h + 1) * HEAD_DIM)
        s = lax.dot_general(q_ref[0, :, sl], k_ref[0, :, sl], (((1,), (1,)), ((), ())), preferred_element_type=F32)
        e = jnp.exp2(s - jnp.max(s, axis=-1, keepdims=True))
        p = (e / jnp.sum(e, axis=-1, keepdims=True)).astype(BF16)
        o_ref[0, :, sl] = jnp.dot(p, v_ref[0, :, sl], preferred_element_type=F32).astype(BF16)


def _mem_attention(qmn, k_m, v_m):
    b, s, _ = qmn.shape
    return pl.pallas_call(
        _mem_attn_kernel,
        grid=(b, s // TQ_MEM),
        in_specs=[
            pl.BlockSpec((1, TQ_MEM, MEM_WIDTH), lambda i, j: (i, j, 0)),
            pl.BlockSpec((1, MEM_LEN, MEM_WIDTH), lambda i, j: (i, 0, 0)),
            pl.BlockSpec((1, MEM_LEN, MEM_WIDTH), lambda i, j: (i, 0, 0)),
        ],
        out_specs=pl.BlockSpec((1, TQ_MEM, MEM_WIDTH), lambda i, j: (i, j, 0)),
        out_shape=jax.ShapeDtypeStruct((b, s, MEM_WIDTH), BF16),
        compiler_params=_cparams(("parallel", "parallel")),
        name="mem_attn",
    )(qmn, k_m, v_m)


def _out_proj_kernel(x_ref, o_ref, om_ref, wo_ref, wm_ref, y_ref):
    y_ref[...] = (x_ref[...]
                  + jnp.dot(o_ref[...], wo_ref[...].astype(BF16), preferred_element_type=F32)
                  + jnp.dot(om_ref[...], wm_ref[...].astype(BF16), preferred_element_type=F32))


def _out_proj(x2d, o2d, om2d, w_stack, layer):
    m, d = x2d.shape
    return pl.pallas_call(
        _out_proj_kernel,
        grid=(m // TM_OUT, d // TN_OUT),
        in_specs=[
            pl.BlockSpec((TM_OUT, TN_OUT), lambda i, j: (i, j)),
            pl.BlockSpec((TM_OUT, MIX_WIDTH), lambda i, j: (i, 0)),
            pl.BlockSpec((TM_OUT, MEM_WIDTH), lambda i, j: (i, 0)),
            pl.BlockSpec((None, MIX_WIDTH, TN_OUT), lambda i, j: (layer, 0, j)),
            pl.BlockSpec((None, MEM_WIDTH, TN_OUT), lambda i, j: (layer, MIX_WIDTH // MEM_WIDTH, j)),
        ],
        out_specs=pl.BlockSpec((TM_OUT, TN_OUT), lambda i, j: (i, j)),
        out_shape=jax.ShapeDtypeStruct((m, d), F32),
        compiler_params=_cparams(("parallel", "parallel")),
        name="out_proj",
    )(x2d, o2d, om2d, w_stack, w_stack)


def _mlp_kernel(x_ref, g_ref, wu_ref, wd_ref, y_ref, h_ref):
    f = pl.program_id(1)

    @pl.when(f == 0)
    def _():
        x = x_ref[...]
        r = lax.rsqrt(jnp.mean(x * x, axis=-1, keepdims=True) + RMS_EPS)
        h_ref[...] = ((x * r) * g_ref[...]).astype(BF16)
        y_ref[...] = x

    u = jnp.maximum(jnp.dot(h_ref[...], wu_ref[...].astype(BF16), preferred_element_type=F32), 0.0)
    y_ref[...] += jnp.dot((u * u).astype(BF16), wd_ref[...].astype(BF16), preferred_element_type=F32)


def _mlp(x2d, gain, wu_stack, wd_stack, layer):
    m, d = x2d.shape
    return pl.pallas_call(
        _mlp_kernel,
        grid=(m // TM_MLP, D_FF // TF_MLP),
        in_specs=[
            pl.BlockSpec((TM_MLP, d), lambda i, f: (i, 0)),
            pl.BlockSpec((1, d), lambda i, f: (0, 0)),
            pl.BlockSpec((None, d, TF_MLP), lambda i, f: (layer, 0, f)),
            pl.BlockSpec((None, TF_MLP, d), lambda i, f: (layer, f, 0)),
        ],
        out_specs=pl.BlockSpec((TM_MLP, d), lambda i, f: (i, 0)),
        out_shape=jax.ShapeDtypeStruct((m, d), F32),
        scratch_shapes=[pltpu.VMEM((TM_MLP, d), BF16)],
        compiler_params=_cparams(("parallel", "arbitrary")),
        name="mlp",
    )(x2d, gain.reshape(1, d), wu_stack, wd_stack)


def _diag_table_kernel(tbl_ref, o_ref):
    r = lax.broadcasted_iota(I32, (DIAG_ROWS, LANES), 0)
    l = lax.broadcasted_iota(I32, (DIAG_ROWS, LANES), 1)
    o_ref[0] = _table_lookup(tbl_ref[0], _t5_bucket(r - l - DIAG_CENTER))


def _diag_bias_table(tbl_log2):
    return pl.pallas_call(
        _diag_table_kernel,
        grid=(N_HEADS,),
        in_specs=[pl.BlockSpec((1, 1, LANES), lambda h: (h, 0, 0))],
        out_specs=pl.BlockSpec((1, DIAG_ROWS, LANES), lambda h: (h, 0, 0)),
        out_shape=jax.ShapeDtypeStruct((N_HEADS, DIAG_ROWS, LANES), F32),
        compiler_params=_cparams(("parallel",)),
        name="diag_bias_table",
    )(tbl_log2)


def _consecutive_runs(positions, ranges):
    b = positions.shape[0]
    step_ok = (positions[:, 1:] - positions[:, :-1] == 1).astype(I32)
    c = jnp.concatenate([jnp.zeros((b, 1), I32), jnp.cumsum(step_ok, axis=1)], axis=1)
    return jnp.stack([(c[:, hi - 1] - c[:, lo] == hi - 1 - lo).astype(I32) for lo, hi in ranges], axis=1)


def _bias_tables(rel_bias):
    t = (rel_bias.astype(F32) * LOG2E).T
    tbl = jnp.zeros((N_HEADS, 1, LANES), F32).at[:, 0, :N_BUCKETS].set(t).at[:, 0, N_BUCKETS].set(NEG_BIG)
    far = jnp.stack([t[:, HALF_BUCKETS - 1], t[:, N_BUCKETS - 1]])
    return tbl, far


def _diff_logit_shift(gq, gk, rel_bias):
    qk = 1.01 * DIFF_QK_DIM * jnp.max(jnp.abs(gq)) * jnp.max(jnp.abs(gk)) * (DIFF_QK_DIM ** -0.5 * LOG2E)
    return jnp.ceil(qk + jnp.max(jnp.abs(rel_bias)) * LOG2E).astype(F32)


def kernel(x, mem, positions, rel_bias, norm_attn, norm_mem, norm_mlp, w_in_a, a_q_norm, a_k_norm, a_lambda_q1, a_lambda_k1, a_lambda_q2, a_lambda_k2, a_subln, w_in_b, b_q_norm, b_k_norm, b_sink, w_mem_kv, m_q_norm, m_k_norm, w_out, w_up, w_down):
    b, s, d = x.shape
    depth = norm_attn.shape[0]
    tbl_log2, far_log2 = _bias_tables(rel_bias)
    diag = _diag_bias_table(tbl_log2)
    x2d = x.reshape(b * s, d)
    later_w = (w_out, w_up, w_down, w_in_b)
    for i in range(depth):
        j = i // 2
        k_m, v_m = _mem_kv(mem, norm_mem[i], w_mem_kv, i, m_k_norm[i])
        if i % 2 == 0:
            proj = _norm_proj(x2d, norm_attn[i], w_in_a, j).reshape(b, s, -1)
            shift = _diff_logit_shift(a_q_norm[j], a_k_norm[j], rel_bias)
            bounded = shift <= SHIFT_LIMIT
            qx, kx, vtx, qmn = _prep_a(proj, a_q_norm[j], a_k_norm[j], m_q_norm[i], jnp.where(bounded, -shift, 0.0))
            lam_init = 0.8 - 0.6 * math.exp(-0.3 * i)
            cast_srcs = tuple(w.reshape(-1, w.shape[-1]) for w in later_w) if i == 0 else ()
            attn = functools.partial(
                _diff_attention, qx, kx, vtx, positions, tbl_log2, diag,
                lam_vecs=(a_lambda_q1[j], a_lambda_k1[j], a_lambda_q2[j], a_lambda_k2[j]), gsub=a_subln[j],
                lam_init=lam_init, cast_srcs=cast_srcs)
            o, casts = lax.cond(bounded,
                                lambda: attn(far_consts=jnp.exp2(far_log2), bounded=True),
                                lambda: attn(far_consts=far_log2, bounded=False))
            if i == 0:
                later_w = tuple(c.reshape(w.shape) for c, w in zip(casts, later_w))
                w_out, w_up, w_down, w_in_b = later_w
        else:
            proj = _norm_proj(x2d, norm_attn[i], w_in_b, j).reshape(b, s, -1)
            qt, kn, vt, qmn = _prep_b(proj, b_q_norm[j], b_k_norm[j], m_q_norm[i])
            o = _win_attention(qt, kn, vt, positions, tbl_log2, diag, b_sink[j].astype(F32) * LOG2E)
        o_m = _mem_attention(qmn, k_m, v_m)
        x2d = _out_proj(x2d, o.reshape(b * s, MIX_WIDTH), o_m.reshape(b * s, MEM_WIDTH), w_out, i)
        x2d = _mlp(x2d, norm_mlp[i], w_up, w_down, i)
    return x2d.reshape(b, s, d)
```

```python
import functools
import math

import jax
import jax.numpy as jnp
from jax import lax
from jax.experimental import pallas as pl
from jax.experimental.pallas import tpu as pltpu

F32 = jnp.float32
BF16 = jnp.bfloat16
I32 = jnp.int32

D_MODEL = 2048
N_HEADS = 12
HEAD_DIM = 128
DIFF_QK_DIM = 64
N_KV_HEADS = 4
GQA_GROUP = 3
MIX_WIDTH = N_HEADS * HEAD_DIM
KV_WIDTH = N_KV_HEADS * HEAD_DIM
WINDOW = 128
N_MEM_HEADS = 4
MEM_WIDTH = N_MEM_HEADS * HEAD_DIM
MEM_LEN = 256
D_FF = 4 * D_MODEL
N_BUCKETS = 32
MAX_DISTANCE = 128
RMS_EPS = 1e-6
NEG_BIG = -1e30
LOG2E = math.log2(math.e)

HALF_BUCKETS = N_BUCKETS // 2
MAX_EXACT = HALF_BUCKETS // 2
FAR_DIST = 91

ONE_COL = DIFF_QK_DIM
V_ROWS = HEAD_DIM + 16
SHIFT_LIMIT = 50.0

LANES = 128

DIAG_ROWS_MAX = 512
DIAG_OFF_MIN = 5
DIAG_CENTER = DIAG_OFF_MIN + DIAG_ROWS_MAX + FAR_DIST
DIAG_OFF_MAX = DIAG_CENTER + FAR_DIST + LANES - 1
DIAG_ROWS = -(-(DIAG_OFF_MAX + DIAG_ROWS_MAX) // 16) * 16

VMEM_LIMIT = 56 * 1024 * 1024

TM_PROJ, TN_PROJ = 1024, 1024
TS_PREP = 512
TQ_A, TK_A = 512, 512
TQ_B = 512
TQ_MEM = 512
TM_OUT, TN_OUT = 512, 2048
TM_MLP, TF_MLP = 1024, 512


def _cparams(sem):
    return pltpu.CompilerParams(dimension_semantics=sem, vmem_limit_bytes=VMEM_LIMIT)


def _smem():
    return pl.BlockSpec(memory_space=pltpu.SMEM)


def _t5_bucket(rel):
    side = jnp.where(rel > 0, HALF_BUCKETS, 0)
    n = jnp.abs(rel)
    n_f = jnp.maximum(n, 1).astype(F32)
    large = MAX_EXACT + (jnp.log(n_f / MAX_EXACT) / math.log(MAX_DISTANCE / MAX_EXACT)
                         * (HALF_BUCKETS - MAX_EXACT)).astype(I32)
    large = jnp.minimum(large, HALF_BUCKETS - 1)
    return side + jnp.where(n < MAX_EXACT, n, large)


def _table_lookup(tbl_row, bucket):
    rows, cols = bucket.shape
    tb = jnp.broadcast_to(tbl_row, (rows, LANES))
    parts = [jnp.take_along_axis(tb, bucket[:, c:c + LANES], axis=1) for c in range(0, cols, LANES)]
    return parts[0] if len(parts) == 1 else jnp.concatenate(parts, axis=1)


def _group_rms_scale(x, group):
    t = x * x
    if group == LANES:
        return lax.rsqrt(jnp.mean(t, axis=-1, keepdims=True) + RMS_EPS)
    lane = lax.broadcasted_iota(I32, x.shape, 1)
    lo = lane < group
    s_lo = jnp.sum(jnp.where(lo, t, 0.0), axis=-1, keepdims=True)
    s_hi = jnp.sum(jnp.where(lo, 0.0, t), axis=-1, keepdims=True)
    return jnp.where(lo, lax.rsqrt(s_lo / group + RMS_EPS), lax.rsqrt(s_hi / group + RMS_EPS))


def _proj_kernel(x_ref, g_ref, w_ref, o_ref, h_ref):
    @pl.when(pl.program_id(1) == 0)
    def _():
        x = x_ref[...]
        r = lax.rsqrt(jnp.mean(x * x, axis=-1, keepdims=True) + RMS_EPS)
        h_ref[...] = ((x * r) * g_ref[...]).astype(BF16)

    o_ref[...] = jnp.dot(h_ref[...], w_ref[...].astype(BF16), preferred_element_type=F32).astype(BF16)


def _norm_proj(x2d, gain, w_stack, layer):
    m, d = x2d.shape
    n = w_stack.shape[2]
    return pl.pallas_call(
        _proj_kernel,
        grid=(m // TM_PROJ, n // TN_PROJ),
        in_specs=[
            pl.BlockSpec((TM_PROJ, d), lambda i, j: (i, 0)),
            pl.BlockSpec((1, d), lambda i, j: (0, 0)),
            pl.BlockSpec((None, d, TN_PROJ), lambda i, j: (layer, 0, j)),
        ],
        out_specs=pl.BlockSpec((TM_PROJ, TN_PROJ), lambda i, j: (i, j)),
        out_shape=jax.ShapeDtypeStruct((m, n), BF16),
        scratch_shapes=[pltpu.VMEM((TM_PROJ, d), BF16)],
        compiler_params=_cparams(("parallel", "arbitrary")),
        name="norm_proj",
    )(x2d, gain.reshape(1, d), w_stack)


def _mem_kv_kernel(mem_ref, g_ref, w_ref, gk_ref, k_ref, v_ref):
    x = mem_ref[0]
    r = lax.rsqrt(jnp.mean(x * x, axis=-1, keepdims=True) + RMS_EPS)
    mn = ((x * r) * g_ref[...]).astype(BF16)
    mkv = jnp.dot(mn, w_ref[...].astype(BF16), preferred_element_type=F32)
    for h in range(N_MEM_HEADS):
        kh = mkv[:, h * HEAD_DIM:(h + 1) * HEAD_DIM]
        k_ref[0, :, h * HEAD_DIM:(h + 1) * HEAD_DIM] = ((kh * _group_rms_scale(kh, HEAD_DIM)) * gk_ref[...]).astype(BF16)
    v_ref[0] = mkv[:, MEM_WIDTH:].astype(BF16)


def _mem_kv(mem, gain, w_stack, layer, gk):
    b = mem.shape[0]
    shp = jax.ShapeDtypeStruct((b, MEM_LEN, MEM_WIDTH), BF16)
    return pl.pallas_call(
        _mem_kv_kernel,
        grid=(b,),
        in_specs=[
            pl.BlockSpec((1, MEM_LEN, D_MODEL), lambda i: (i, 0, 0)),
            pl.BlockSpec((1, D_MODEL), lambda i: (0, 0)),
            pl.BlockSpec((None, D_MODEL, 2 * MEM_WIDTH), lambda i: (layer, 0, 0)),
            pl.BlockSpec((1, HEAD_DIM), lambda i: (0, 0)),
        ],
        out_specs=[pl.BlockSpec((1, MEM_LEN, MEM_WIDTH), lambda i: (i, 0, 0))] * 2,
        out_shape=[shp, shp],
        compiler_params=_cparams(("parallel",)),
        name="mem_kv",
    )(mem, gain.reshape(1, D_MODEL), w_stack, gk.reshape(1, HEAD_DIM))


def _prep_a_kernel(negm_ref, q_ref, k_ref, v_ref, qm_ref, gq_ref, gk_ref, gm_ref, qx_ref, kx_ref, vtx_ref, qmn_ref):
    q_scale = DIFF_QK_DIM ** -0.5 * LOG2E
    m_scale = HEAD_DIM ** -0.5 * LOG2E
    lane = lax.broadcasted_iota(I32, (TS_PREP, HEAD_DIM), 1)
    row = lax.broadcasted_iota(I32, (HEAD_DIM, TS_PREP), 0)
    lo_lanes, lo_rows = lane < DIFF_QK_DIM, row < DIFF_QK_DIM
    neg_shift = negm_ref[0]
    ones_rows = jnp.where(lax.broadcasted_iota(I32, (V_ROWS - HEAD_DIM, TS_PREP), 0) == 0, 1.0, 0.0).astype(BF16)
    for h in range(N_HEADS):
        sl = slice(h * HEAD_DIM, (h + 1) * HEAD_DIM)
        qt = q_ref[0, :, sl].astype(F32).T
        t = qt * qt
        r_lo = lax.rsqrt(jnp.mean(t[:DIFF_QK_DIM], axis=0, keepdims=True) + RMS_EPS)
        r_hi = lax.rsqrt(jnp.mean(t[DIFF_QK_DIM:], axis=0, keepdims=True) + RMS_EPS)
        qn = ((qt * jnp.where(lo_rows, r_lo, r_hi)) * gq_ref[...]) * q_scale
        qx_ref[0, 2 * h] = jnp.where(lo_rows, qn, jnp.where(row == ONE_COL, neg_shift, 0.0)).astype(BF16)
        qx_ref[0, 2 * h + 1] = jnp.where(lo_rows, jnp.where(row == ONE_COL - 1, neg_shift, 0.0), qn).astype(BF16)
        k = k_ref[0, :, sl].astype(F32)
        kn = (k * _group_rms_scale(k, DIFF_QK_DIM)) * gk_ref[...]
        kx_ref[0, 2 * h] = jnp.where(lo_lanes, kn, jnp.where(lane == ONE_COL, 1.0, 0.0)).astype(BF16)
        kx_ref[0, 2 * h + 1] = jnp.where(lo_lanes, jnp.where(lane == ONE_COL - 1, 1.0, 0.0), kn).astype(BF16)
        vt = v_ref[0, :, sl].astype(F32).T.astype(BF16)
        vtx_ref[0, h, 0] = jnp.concatenate([vt, ones_rows], axis=0)
    for h in range(N_MEM_HEADS):
        sl = slice(h * HEAD_DIM, (h + 1) * HEAD_DIM)
        qm = qm_ref[0, :, sl].astype(F32)
        qmn_ref[0, :, sl] = (((qm * _group_rms_scale(qm, HEAD_DIM)) * gm_ref[...]) * m_scale).astype(BF16)


def _prep_a(proj, gq, gk, gm, neg_shift):
    assert TS_PREP == TK_A
    b, s, _ = proj.shape
    gq2 = jnp.concatenate([gq, gq]).reshape(HEAD_DIM, 1)
    gk2 = jnp.concatenate([gk, gk]).reshape(1, HEAD_DIM)
    wblk = MIX_WIDTH // MEM_WIDTH
    return pl.pallas_call(
        _prep_a_kernel,
        grid=(b, s // TS_PREP),
        in_specs=[
            _smem(),
            pl.BlockSpec((1, TS_PREP, MIX_WIDTH), lambda i, j: (i, j, 0)),
            pl.BlockSpec((1, TS_PREP, MIX_WIDTH), lambda i, j: (i, j, 1)),
            pl.BlockSpec((1, TS_PREP, MIX_WIDTH), lambda i, j: (i, j, 2)),
            pl.BlockSpec((1, TS_PREP, MEM_WIDTH), lambda i, j: (i, j, 3 * wblk)),
            pl.BlockSpec((HEAD_DIM, 1), lambda i, j: (0, 0)),
            pl.BlockSpec((1, HEAD_DIM), lambda i, j: (0, 0)),
            pl.BlockSpec((1, HEAD_DIM), lambda i, j: (0, 0)),
        ],
        out_specs=[
            pl.BlockSpec((1, 2 * N_HEADS, HEAD_DIM, TS_PREP), lambda i, j: (i, 0, 0, j)),
            pl.BlockSpec((1, 2 * N_HEADS, TS_PREP, HEAD_DIM), lambda i, j: (i, 0, j, 0)),
            pl.BlockSpec((1, N_HEADS, 1, V_ROWS, TK_A), lambda i, j: (i, 0, j, 0, 0)),
            pl.BlockSpec((1, TS_PREP, MEM_WIDTH), lambda i, j: (i, j, 0)),
        ],
        out_shape=[
            jax.ShapeDtypeStruct((b, 2 * N_HEADS, HEAD_DIM, s), BF16),
            jax.ShapeDtypeStruct((b, 2 * N_HEADS, s, HEAD_DIM), BF16),
            jax.ShapeDtypeStruct((b, N_HEADS, s // TK_A, V_ROWS, TK_A), BF16),
            jax.ShapeDtypeStruct((b, s, MEM_WIDTH), BF16),
        ],
        compiler_params=_cparams(("parallel", "parallel")),
        name="prep_a",
    )(neg_shift.reshape(1), proj, proj, proj, proj, gq2, gk2, gm.reshape(1, HEAD_DIM))


def _diff_attn_kernel(qlo_ref, qhi_ref, klo_ref, khi_ref, qrun_ref, krun_ref,
                      far_ref, lq1_ref, lk1_ref, lq2_ref, lk2_ref,
                      qx_ref, kx_ref, vtx_ref, posq_ref, posk_ref, tbl_ref, diag_ref, gsub_ref,
                      *rest, lam_init, bounded, n_cast):
    w_f32_refs, o_ref, w_bf16_refs = rest[:n_cast], rest[n_cast], rest[n_cast + 1:2 * n_cast + 1]
    acc_ref, m_scratch = rest[2 * n_cast + 1], rest[2 * n_cast + 2:]
    b, iq, j = pl.program_id(0), pl.program_id(1), pl.program_id(2)
    pos_far = klo_ref[b, j] - qhi_ref[b, iq] >= FAR_DIST
    neg_far = khi_ref[b, j] - qlo_ref[b, iq] <= -FAR_DIST
    far = jnp.logical_or(pos_far, neg_far)

    @pl.when(j == 0)
    def _():
        acc_ref[...] = jnp.zeros(acc_ref.shape, F32)
        if not bounded:
            m_scratch[0][...] = jnp.full(m_scratch[0].shape, NEG_BIG, F32)

    def logits(hc):
        return jnp.dot(kx_ref[0, hc], qx_ref[0, hc], preferred_element_type=F32)

    def accumulate(hc, s, far_const, bias):
        h = hc // 2
        if bounded:
            e = jnp.exp2(s)
            if bias is not None:
                e = e * bias
            pv = jnp.dot(vtx_ref[0, h, 0], e.astype(BF16), preferred_element_type=F32)
            acc_ref[hc] += pv if far_const is None else far_const * pv
        else:
            if bias is not None:
                s = s + bias
            m_ref = m_scratch[0]
            off = 0.0 if far_const is None else far_const
            m_old = m_ref[hc]
            m_new = jnp.maximum(m_old, jnp.max(s, axis=0, keepdims=True) + off)
            p = jnp.exp2(s - (m_new - off)).astype(BF16)
            acc_ref[hc] = (jnp.exp2(m_old - m_new) * acc_ref[hc]
                           + jnp.dot(vtx_ref[0, h, 0], p, preferred_element_type=F32))
            m_ref[hc] = m_new

    def all_heads(bias_of_head, far_const_of_head):
        for src, dst in zip(w_f32_refs, w_bf16_refs):
            dst[...] = src[...].astype(BF16)
        s_next = logits(0)
        bias = None
        bias_next = None if bias_of_head is None else bias_of_head(0)
        for hc in range(2 * N_HEADS):
            h, c = divmod(hc, 2)
            s = s_next
            if hc + 1 < 2 * N_HEADS:
                s_next = logits(hc + 1)
            if bias_of_head is not None and c == 0:
                bias = bias_next
                if h + 1 < N_HEADS:
                    bias_next = bias_of_head(h + 1)
            accumulate(hc, s, None if far_const_of_head is None else far_const_of_head(h), bias)

    @pl.when(far)
    def _():
        all_heads(None, lambda h: jnp.where(pos_far, far_ref[1, h], far_ref[0, h]))

    runs = jnp.logical_and(qrun_ref[b, iq] == 1, krun_ref[b, j] == 1)

    @pl.when(jnp.logical_and(jnp.logical_not(far), runs))
    def _():
        d0 = klo_ref[b, j] - qlo_ref[b, iq]

        def bias_of_head(h):
            offs = [jnp.clip(d0 - c + DIAG_CENTER, DIAG_OFF_MIN, DIAG_OFF_MAX) for c in range(0, TQ_A, LANES)]
            return jnp.concatenate([diag_ref[h, pl.ds(off, TK_A), :] for off in offs], axis=1)

        all_heads(bias_of_head, None)

    @pl.when(jnp.logical_and(jnp.logical_not(far), jnp.logical_not(runs)))
    def _():
        bucket = []

        def bias_of_head(h):
            if not bucket:
                bucket.append(_t5_bucket(posk_ref[0] - posq_ref[0]))
            return _table_lookup(tbl_ref[h], bucket[0])

        all_heads(bias_of_head, None)

    @pl.when(j == pl.num_programs(2) - 1)
    def _():
        lam = (jnp.exp(jnp.sum(lq1_ref[...] * lk1_ref[...], axis=-1, keepdims=True))
               - jnp.exp(jnp.sum(lq2_ref[...] * lk2_ref[...], axis=-1, keepdims=True)) + lam_init)
        for h in range(N_HEADS):
            a0, a1 = acc_ref[2 * h], acc_ref[2 * h + 1]
            o_t = (a0[:HEAD_DIM] / a0[HEAD_DIM:HEAD_DIM + 1]
                   - lam * (a1[:HEAD_DIM] / a1[HEAD_DIM:HEAD_DIM + 1]))
            r = lax.rsqrt(jnp.mean(o_t * o_t, axis=0, keepdims=True) + RMS_EPS)
            o_t = ((o_t * r) * gsub_ref[...]) * (1.0 - lam_init)
            o_ref[0, :, h * HEAD_DIM:(h + 1) * HEAD_DIM] = o_t.T.astype(BF16)


def _diff_attention(qx, kx, vtx, positions, tbl_log2, diag, far_consts, lam_vecs, gsub, lam_init, bounded,
                    cast_srcs):
    assert TK_A <= DIAG_ROWS_MAX
    b, _, s, _ = kx.shape
    nq, nk = s // TQ_A, s // TK_A
    pq = positions.reshape(b, nq, TQ_A)
    pk = positions.reshape(b, nk, TK_A)
    ranges = (pq.min(-1), pq.max(-1), pk.min(-1), pk.max(-1),
              _consecutive_runs(positions, [(t * TQ_A, (t + 1) * TQ_A) for t in range(nq)]),
              _consecutive_runs(positions, [(t * TK_A, (t + 1) * TK_A) for t in range(nk)]))
    lam_spec = pl.BlockSpec((1, DIFF_QK_DIM), lambda i, q, k, *_: (0, 0))
    scratch = [pltpu.VMEM((2 * N_HEADS, V_ROWS, TQ_A), F32)]
    if not bounded:
        scratch.append(pltpu.VMEM((2 * N_HEADS, 1, TQ_A), F32))
    n_steps = b * nq * nk
    assert all(w.shape[0] % (16 * n_steps) == 0 for w in cast_srcs)
    cast_specs = [pl.BlockSpec((w.shape[0] // n_steps, w.shape[1]), lambda i, q, k, *_: ((i * nq + q) * nk + k, 0))
                  for w in cast_srcs]
    grid_spec = pltpu.PrefetchScalarGridSpec(
        num_scalar_prefetch=len(ranges),
        grid=(b, nq, nk),
        in_specs=[
            _smem(), lam_spec, lam_spec, lam_spec, lam_spec,
            pl.BlockSpec((1, 2 * N_HEADS, HEAD_DIM, TQ_A), lambda i, q, k, *_: (i, 0, 0, q)),
            pl.BlockSpec((1, 2 * N_HEADS, TK_A, HEAD_DIM), lambda i, q, k, *_: (i, 0, k, 0)),
            pl.BlockSpec((1, N_HEADS, 1, V_ROWS, TK_A), lambda i, q, k, *_: (i, 0, k, 0, 0)),
            pl.BlockSpec((1, 1, TQ_A), lambda i, q, k, *_: (i, 0, q)),
            pl.BlockSpec((1, TK_A, 1), lambda i, q, k, *_: (i, k, 0)),
            pl.BlockSpec((N_HEADS, 1, LANES), lambda i, q, k, *_: (0, 0, 0)),
            pl.BlockSpec((N_HEADS, DIAG_ROWS, LANES), lambda i, q, k, *_: (0, 0, 0), pipeline_mode=pl.Buffered(1)),
            pl.BlockSpec((HEAD_DIM, 1), lambda i, q, k, *_: (0, 0)),
            *cast_specs,
        ],
        out_specs=[pl.BlockSpec((1, TQ_A, MIX_WIDTH), lambda i, q, k, *_: (i, q, 0)), *cast_specs],
        scratch_shapes=scratch,
    )
    outs = pl.pallas_call(
        functools.partial(_diff_attn_kernel, lam_init=lam_init, bounded=bounded, n_cast=len(cast_srcs)),
        grid_spec=grid_spec,
        out_shape=[jax.ShapeDtypeStruct((b, s, MIX_WIDTH), BF16),
                   *[jax.ShapeDtypeStruct(w.shape, BF16) for w in cast_srcs]],
        compiler_params=_cparams(("arbitrary", "arbitrary", "arbitrary")),
        name="diff_attn_bounded" if bounded else "diff_attn_running_max",
    )(*ranges, far_consts, *[v.reshape(1, DIFF_QK_DIM) for v in lam_vecs],
      qx, kx, vtx, positions.reshape(b, 1, s), positions.reshape(b, s, 1), tbl_log2, diag,
      gsub.reshape(HEAD_DIM, 1), *cast_srcs)
    return outs[0], tuple(outs[1:])


def _prep_b_kernel(q_ref, k_ref, v_ref, qm_ref, gq_ref, gk_ref, gm_ref, qt_ref, kn_ref, vt_ref, qmn_ref):
    scale = HEAD_DIM ** -0.5 * LOG2E
    n_blk = TS_PREP // WINDOW
    ones_rows = jnp.where(lax.broadcasted_iota(I32, (V_ROWS - HEAD_DIM, TS_PREP), 0) == 0, 1.0, 0.0).astype(BF16)
    for h in range(N_HEADS):
        g, hg = divmod(h, GQA_GROUP)
        qt = q_ref[0, :, h * HEAD_DIM:(h + 1) * HEAD_DIM].astype(F32).T
        r = lax.rsqrt(jnp.mean(qt * qt, axis=0, keepdims=True) + RMS_EPS)
        qt = (((qt * r) * gq_ref[...]) * scale).astype(BF16)
        for n in range(n_blk):
            qt_ref[0, n, g, :, hg * WINDOW:(hg + 1) * WINDOW] = qt[:, n * WINDOW:(n + 1) * WINDOW]
    for g in range(N_KV_HEADS):
        sl = slice(g * HEAD_DIM, (g + 1) * HEAD_DIM)
        k = k_ref[0, :, sl].astype(F32)
        kn_ref[0, g] = ((k * _group_rms_scale(k, HEAD_DIM)) * gk_ref[...]).astype(BF16)
        vt = jnp.concatenate([v_ref[0, :, sl].astype(F32).T.astype(BF16), ones_rows], axis=0)
        for n in range(n_blk):
            vt_ref[0, g, n] = vt[:, n * WINDOW:(n + 1) * WINDOW]
        qm = qm_ref[0, :, sl].astype(F32)
        qmn_ref[0, :, sl] = (((qm * _group_rms_scale(qm, HEAD_DIM)) * gm_ref[...]) * scale).astype(BF16)


def _prep_b(proj, gq, gk, gm):
    b, s, _ = proj.shape
    kblk = MIX_WIDTH // KV_WIDTH
    n_blk = TS_PREP // WINDOW
    g = lambda v: v.reshape(1, HEAD_DIM)
    return pl.pallas_call(
        _prep_b_kernel,
        grid=(b, s // TS_PREP),
        in_specs=[
            pl.BlockSpec((1, TS_PREP, MIX_WIDTH), lambda i, j: (i, j, 0)),
            pl.BlockSpec((1, TS_PREP, KV_WIDTH), lambda i, j: (i, j, kblk)),
            pl.BlockSpec((1, TS_PREP, KV_WIDTH), lambda i, j: (i, j, kblk + 1)),
            pl.BlockSpec((1, TS_PREP, MEM_WIDTH), lambda i, j: (i, j, kblk + 2)),
            pl.BlockSpec((HEAD_DIM, 1), lambda i, j: (0, 0)),
            pl.BlockSpec((1, HEAD_DIM), lambda i, j: (0, 0)),
            pl.BlockSpec((1, HEAD_DIM), lambda i, j: (0, 0)),
        ],
        out_specs=[
            pl.BlockSpec((1, n_blk, N_KV_HEADS, HEAD_DIM, GQA_GROUP * WINDOW), lambda i, j: (i, j, 0, 0, 0)),
            pl.BlockSpec((1, N_KV_HEADS, TS_PREP, HEAD_DIM), lambda i, j: (i, 0, j, 0)),
            pl.BlockSpec((1, N_KV_HEADS, n_blk, V_ROWS, WINDOW), lambda i, j: (i, 0, j, 0, 0)),
            pl.BlockSpec((1, TS_PREP, MEM_WIDTH), lambda i, j: (i, j, 0)),
        ],
        out_shape=[
            jax.ShapeDtypeStruct((b, s // WINDOW, N_KV_HEADS, HEAD_DIM, GQA_GROUP * WINDOW), BF16),
            jax.ShapeDtypeStruct((b, N_KV_HEADS, s, HEAD_DIM), BF16),
            jax.ShapeDtypeStruct((b, N_KV_HEADS, s // WINDOW, V_ROWS, WINDOW), BF16),
            jax.ShapeDtypeStruct((b, s, MEM_WIDTH), BF16),
        ],
        compiler_params=_cparams(("parallel", "parallel")),
        name="prep_b",
    )(proj, proj, proj, proj, gq.reshape(HEAD_DIM, 1), g(gk), g(gm))


def _win_attn_kernel(sink_ref, run_ref, qt_ref, k_ref, vt_ref, posq_ref, posk_ref, tbl_ref, diag_ref, o_ref):
    n_blocks = posq_ref.shape[1]
    n_sub = o_ref.shape[1] // WINDOW
    kw = 3 * WINDOW
    lane3 = lax.broadcasted_iota(I32, (1, GQA_GROUP * WINDOW), 1)

    def window(nl):
        n = pl.program_id(1) * n_sub + nl
        nb = jnp.clip(n - 1, 0, n_blocks - 3)
        return n, nb, pl.multiple_of(nb * WINDOW, WINDOW)

    def logits(nl, g):
        start = window(nl)[2]
        return jnp.dot(k_ref[0, g, pl.ds(start, kw), :], qt_ref[0, nl, g], preferred_element_type=F32)

    def in_window(nl):
        n, _, start = window(nl)
        ki = start + lax.broadcasted_iota(I32, (kw, WINDOW), 0)
        qi = n * WINDOW + lax.broadcasted_iota(I32, (kw, WINDOW), 1)
        return jnp.abs(ki - qi) <= WINDOW

    def masked_bucket(nl):
        n, _, start = window(nl)
        rel = posk_ref[0, pl.ds(start, kw), :] - posq_ref[0, pl.ds(n, 1), :]
        return jnp.where(in_window(nl), _t5_bucket(rel), N_BUCKETS)

    def bias_from_bucket(bucket, nl, h):
        return _table_lookup(tbl_ref[h], bucket)

    def bias_from_diag(valid, nl, h):
        n, _, start = window(nl)
        off = pl.multiple_of(start - n * WINDOW + DIAG_CENTER, 16)
        return jnp.where(valid, diag_ref[h, pl.ds(off, kw), :], NEG_BIG)

    def finish(nl, g, s, block_state, head_bias):
        nb = window(nl)[1]
        heads = range(g * GQA_GROUP, (g + 1) * GQA_GROUP)
        z = s + jnp.concatenate([head_bias(block_state, nl, h) for h in heads], axis=1)
        sink = jnp.where(lane3 < WINDOW, sink_ref[heads[0]],
                         jnp.where(lane3 < 2 * WINDOW, sink_ref[heads[1]], sink_ref[heads[2]]))
        m = jnp.maximum(jnp.max(z, axis=0, keepdims=True), sink)
        p = jnp.exp2(z - m).astype(BF16)
        vt = jnp.concatenate([vt_ref[0, g, nb + t] for t in range(3)], axis=1)
        acc = jnp.dot(vt, p, preferred_element_type=F32)
        o_t = acc[:HEAD_DIM] / (acc[HEAD_DIM:HEAD_DIM + 1] + jnp.exp2(sink - m))
        for hg, h in enumerate(heads):
            o_ref[0, nl * WINDOW:(nl + 1) * WINDOW, h * HEAD_DIM:(h + 1) * HEAD_DIM] = (
                o_t[:, hg * WINDOW:(hg + 1) * WINDOW].T.astype(BF16))

    def all_chains(block_state_of, head_bias):
        chains = [(nl, g) for nl in range(n_sub) for g in range(N_KV_HEADS)]
        s_next = logits(*chains[0])
        block_state = None
        for idx, (nl, g) in enumerate(chains):
            s = s_next
            if idx + 1 < len(chains):
                s_next = logits(*chains[idx + 1])
            if g == 0:
                block_state = block_state_of(nl)
            finish(nl, g, s, block_state, head_bias)

    is_run = run_ref[pl.program_id(0), pl.program_id(1)] == 1

    @pl.when(is_run)
    def _():
        all_chains(in_window, bias_from_diag)

    @pl.when(jnp.logical_not(is_run))
    def _():
        all_chains(masked_bucket, bias_from_bucket)


def _win_attention(qt, kn, vt, positions, tbl_log2, diag, sink_log2):
    b, _, s, _ = kn.shape
    n_blocks = s // WINDOW
    n_sub = TQ_B // WINDOW
    runs = _consecutive_runs(positions, [(max(0, t * TQ_B - WINDOW), min(s, (t + 1) * TQ_B + WINDOW))
                                         for t in range(s // TQ_B)])
    return pl.pallas_call(
        _win_attn_kernel,
        grid=(b, s // TQ_B),
        in_specs=[
            _smem(), _smem(),
            pl.BlockSpec((1, n_sub, N_KV_HEADS, HEAD_DIM, GQA_GROUP * WINDOW), lambda i, j: (i, j, 0, 0, 0)),
            pl.BlockSpec((1, N_KV_HEADS, s, HEAD_DIM), lambda i, j: (i, 0, 0, 0)),
            pl.BlockSpec((1, N_KV_HEADS, n_blocks, V_ROWS, WINDOW), lambda i, j: (i, 0, 0, 0, 0)),
            pl.BlockSpec((1, n_blocks, WINDOW), lambda i, j: (i, 0, 0)),
            pl.BlockSpec((1, s, 1), lambda i, j: (i, 0, 0)),
            pl.BlockSpec((N_HEADS, 1, LANES), lambda i, j: (0, 0, 0)),
            pl.BlockSpec((N_HEADS, DIAG_ROWS, LANES), lambda i, j: (0, 0, 0), pipeline_mode=pl.Buffered(1)),
        ],
        out_specs=pl.BlockSpec((1, TQ_B, MIX_WIDTH), lambda i, j: (i, j, 0)),
        out_shape=jax.ShapeDtypeStruct((b, s, MIX_WIDTH), BF16),
        compiler_params=_cparams(("parallel", "parallel")),
        name="win_attn",
    )(sink_log2, runs, qt, kn, vt, positions.reshape(b, n_blocks, WINDOW), positions.reshape(b, s, 1), tbl_log2,
      diag)


def _mem_attn_kernel(q_ref, k_ref, v_ref, o_ref):
    for h in range(N_MEM_HEADS):
        sl = slice(h * HEAD_DIM, (h + 1) * HEAD_DIM)
        s = lax.dot_general(q_ref[0, :, sl], k_ref[0, :, sl], (((1,), (1,)), ((), ())), preferred_element_type=F32)
        e = jnp.exp2(s - jnp.max(s, axis=-1, keepdims=True))
        p = (e / jnp.sum(e, axis=-1, keepdims=True)).astype(BF16)
        o_ref[0, :, sl] = jnp.dot(p, v_ref[0, :, sl], preferred_element_type=F32).astype(BF16)


def _mem_attention(qmn, k_m, v_m):
    b, s, _ = qmn.shape
    return pl.pallas_call(
        _mem_attn_kernel,
        grid=(b, s // TQ_MEM),
        in_specs=[
            pl.BlockSpec((1, TQ_MEM, MEM_WIDTH), lambda i, j: (i, j, 0)),
            pl.BlockSpec((1, MEM_LEN, MEM_WIDTH), lambda i, j: (i, 0, 0)),
            pl.BlockSpec((1, MEM_LEN, MEM_WIDTH), lambda i, j: (i, 0, 0)),
        ],
        out_specs=pl.BlockSpec((1, TQ_MEM, MEM_WIDTH), lambda i, j: (i, j, 0)),
        out_shape=jax.ShapeDtypeStruct((b, s, MEM_WIDTH), BF16),
        compiler_params=_cparams(("parallel", "parallel")),
        name="mem_attn",
    )(qmn, k_m, v_m)


def _out_proj_kernel(x_ref, o_ref, om_ref, wo_ref, wm_ref, y_ref):
    y_ref[...] = (x_ref[...]
                  + jnp.dot(o_ref[...], wo_ref[...].astype(BF16), preferred_element_type=F32)
                  + jnp.dot(om_ref[...], wm_ref[...].astype(BF16), preferred_element_type=F32))


def _out_proj(x2d, o2d, om2d, w_stack, layer):
    m, d = x2d.shape
    return pl.pallas_call(
        _out_proj_kernel,
        grid=(m // TM_OUT, d // TN_OUT),
        in_specs=[
            pl.BlockSpec((TM_OUT, TN_OUT), lambda i, j: (i, j)),
            pl.BlockSpec((TM_OUT, MIX_WIDTH), lambda i, j: (i, 0)),
            pl.BlockSpec((TM_OUT, MEM_WIDTH), lambda i, j: (i, 0)),
            pl.BlockSpec((None, MIX_WIDTH, TN_OUT), lambda i, j: (layer, 0, j)),
            pl.BlockSpec((None, MEM_WIDTH, TN_OUT), lambda i, j: (layer, MIX_WIDTH // MEM_WIDTH, j)),
        ],
        out_specs=pl.BlockSpec((TM_OUT, TN_OUT), lambda i, j: (i, j)),
        out_shape=jax.ShapeDtypeStruct((m, d), F32),
        compiler_params=_cparams(("parallel", "parallel")),
        name="out_proj",
    )(x2d, o2d, om2d, w_stack, w_stack)


def _mlp_kernel(x_ref, g_ref, wu_ref, wd_ref, y_ref, h_ref):
    f = pl.program_id(1)

    @pl.when(f == 0)
    def _():
        x = x_ref[...]
        r = lax.rsqrt(jnp.mean(x * x, axis=-1, keepdims=True) + RMS_EPS)
        h_ref[...] = ((x * r) * g_ref[...]).astype(BF16)
        y_ref[...] = x

    u = jnp.maximum(jnp.dot(h_ref[...], wu_ref[...].astype(BF16), preferred_element_type=F32), 0.0)
    y_ref[...] += jnp.dot((u * u).astype(BF16), wd_ref[...].astype(BF16), preferred_element_type=F32)


def _mlp(x2d, gain, wu_stack, wd_stack, layer):
    m, d = x2d.shape
    return pl.pallas_call(
        _mlp_kernel,
        grid=(m // TM_MLP, D_FF // TF_MLP),
        in_specs=[
            pl.BlockSpec((TM_MLP, d), lambda i, f: (i, 0)),
            pl.BlockSpec((1, d), lambda i, f: (0, 0)),
            pl.BlockSpec((None, d, TF_MLP), lambda i, f: (layer, 0, f)),
            pl.BlockSpec((None, TF_MLP, d), lambda i, f: (layer, f, 0)),
        ],
        out_specs=pl.BlockSpec((TM_MLP, d), lambda i, f: (i, 0)),
        out_shape=jax.ShapeDtypeStruct((m, d), F32),
        scratch_shapes=[pltpu.VMEM((TM_MLP, d), BF16)],
        compiler_params=_cparams(("parallel", "arbitrary")),
        name="mlp",
    )(x2d, gain.reshape(1, d), wu_stack, wd_stack)


def _diag_table_kernel(tbl_ref, o_ref):
    r = lax.broadcasted_iota(I32, (DIAG_ROWS, LANES), 0)
    l = lax.broadcasted_iota(I32, (DIAG_ROWS, LANES), 1)
    o_ref[0] = _table_lookup(tbl_ref[0], _t5_bucket(r - l - DIAG_CENTER))


def _diag_bias_table(tbl_log2):
    return pl.pallas_call(
        _diag_table_kernel,
        grid=(N_HEADS,),
        in_specs=[pl.BlockSpec((1, 1, LANES), lambda h: (h, 0, 0))],
        out_specs=pl.BlockSpec((1, DIAG_ROWS, LANES), lambda h: (h, 0, 0)),
        out_shape=jax.ShapeDtypeStruct((N_HEADS, DIAG_ROWS, LANES), F32),
        compiler_params=_cparams(("parallel",)),
        name="diag_bias_table",
    )(tbl_log2)


def _consecutive_runs(positions, ranges):
    b = positions.shape[0]
    step_ok = (positions[:, 1:] - positions[:, :-1] == 1).astype(I32)
    c = jnp.concatenate([jnp.zeros((b, 1), I32), jnp.cumsum(step_ok, axis=1)], axis=1)
    return jnp.stack([(c[:, hi - 1] - c[:, lo] == hi - 1 - lo).astype(I32) for lo, hi in ranges], axis=1)


def _bias_tables(rel_bias):
    t = (rel_bias.astype(F32) * LOG2E).T
    tbl = jnp.zeros((N_HEADS, 1, LANES), F32).at[:, 0, :N_BUCKETS].set(t).at[:, 0, N_BUCKETS].set(NEG_BIG)
    far = jnp.stack([t[:, HALF_BUCKETS - 1], t[:, N_BUCKETS - 1]])
    return tbl, far


def _diff_logit_shift(gq, gk, rel_bias):
    qk = 1.01 * DIFF_QK_DIM * jnp.max(jnp.abs(gq)) * jnp.max(jnp.abs(gk)) * (DIFF_QK_DIM ** -0.5 * LOG2E)
    return jnp.ceil(qk + jnp.max(jnp.abs(rel_bias)) * LOG2E).astype(F32)


def kernel(x, mem, positions, rel_bias, norm_attn, norm_mem, norm_mlp, w_in_a, a_q_norm, a_k_norm, a_lambda_q1, a_lambda_k1, a_lambda_q2, a_lambda_k2, a_subln, w_in_b, b_q_norm, b_k_norm, b_sink, w_mem_kv, m_q_norm, m_k_norm, w_out, w_up, w_down):
    b, s, d = x.shape
    depth = norm_attn.shape[0]
    tbl_log2, far_log2 = _bias_tables(rel_bias)
    diag = _diag_bias_table(tbl_log2)
    x2d = x.reshape(b * s, d)
    later_w = (w_out, w_up, w_down, w_in_b)
    for i in range(depth):
        j = i // 2
        k_m, v_m = _mem_kv(mem, norm_mem[i], w_mem_kv, i, m_k_norm[i])
        if i % 2 == 0:
            proj = _norm_proj(x2d, norm_attn[i], w_in_a, j).reshape(b, s, -1)
            shift = _diff_logit_shift(a_q_norm[j], a_k_norm[j], rel_bias)
            bounded = shift <= SHIFT_LIMIT
            qx, kx, vtx, qmn = _prep_a(proj, a_q_norm[j], a_k_norm[j], m_q_norm[i], jnp.where(bounded, -shift, 0.0))
            lam_init = 0.8 - 0.6 * math.exp(-0.3 * i)
            cast_srcs = tuple(w.reshape(-1, w.shape[-1]) for w in later_w) if i == 0 else ()
            attn = functools.partial(
                _diff_attention, qx, kx, vtx, positions,
                lam_vecs=(a_lambda_q1[j], a_lambda_k1[j], a_lambda_q2[j], a_lambda_k2[j]), gsub=a_subln[j],
                lam_init=lam_init, cast_srcs=cast_srcs)
            o, casts = lax.cond(
                bounded,
                lambda: attn(jnp.exp2(tbl_log2), jnp.exp2(diag), far_consts=jnp.exp2(far_log2), bounded=True),
                lambda: attn(tbl_log2, diag, far_consts=far_log2, bounded=False))
            if i == 0:
                later_w = tuple(c.reshape(w.shape) for c, w in zip(casts, later_w))
                w_out, w_up, w_down, w_in_b = later_w
        else:
            proj = _norm_proj(x2d, norm_attn[i], w_in_b, j).reshape(b, s, -1)
            qt, kn, vt, qmn = _prep_b(proj, b_q_norm[j], b_k_norm[j], m_q_norm[i])
            o = _win_attention(qt, kn, vt, positions, tbl_log2, diag, b_sink[j].astype(F32) * LOG2E)
        o_m = _mem_attention(qmn, k_m, v_m)
        x2d = _out_proj(x2d, o.reshape(b * s, MIX_WIDTH), o_m.reshape(b * s, MEM_WIDTH), w_out, i)
        x2d = _mlp(x2d, norm_mlp[i], w_up, w_down, i)
    return x2d.reshape(b, s, d)
```

```python
import functools
import math

import jax
import jax.numpy as jnp
from jax import lax
from jax.experimental import pallas as pl
from jax.experimental.pallas import tpu as pltpu

F32 = jnp.float32
BF16 = jnp.bfloat16
I32 = jnp.int32

D_MODEL = 2048
N_HEADS = 12
HEAD_DIM = 128
DIFF_QK_DIM = 64
N_KV_HEADS = 4
GQA_GROUP = 3
MIX_WIDTH = N_HEADS * HEAD_DIM
KV_WIDTH = N_KV_HEADS * HEAD_DIM
WINDOW = 128
N_MEM_HEADS = 4
MEM_WIDTH = N_MEM_HEADS * HEAD_DIM
MEM_LEN = 256
D_FF = 4 * D_MODEL
N_BUCKETS = 32
MAX_DISTANCE = 128
RMS_EPS = 1e-6
NEG_BIG = -1e30
LOG2E = math.log2(math.e)

HALF_BUCKETS = N_BUCKETS // 2
MAX_EXACT = HALF_BUCKETS // 2
FAR_DIST = 91

ONE_COL = DIFF_QK_DIM
V_ROWS = HEAD_DIM + 16
SHIFT_LIMIT = 50.0

LANES = 128

DIAG_ROWS_MAX = 512
DIAG_OFF_MIN = 5
DIAG_CENTER = DIAG_OFF_MIN + DIAG_ROWS_MAX + FAR_DIST
DIAG_OFF_MAX = DIAG_CENTER + FAR_DIST + LANES - 1
DIAG_ROWS = -(-(DIAG_OFF_MAX + DIAG_ROWS_MAX) // 16) * 16

VMEM_LIMIT = 56 * 1024 * 1024

TM_PROJ, TN_PROJ = 1024, 1024
TS_PREP = 512
TQ_A, TK_A = 512, 512
TQ_B = 512
TQ_MEM = 2048
TM_OUT, TN_OUT = 512, 2048
TM_MLP, TF_MLP = 1024, 512


def _cparams(sem):
    return pltpu.CompilerParams(dimension_semantics=sem, vmem_limit_bytes=VMEM_LIMIT)


def _smem():
    return pl.BlockSpec(memory_space=pltpu.SMEM)


def _t5_bucket(rel):
    side = jnp.where(rel > 0, HALF_BUCKETS, 0)
    n = jnp.abs(rel)
    n_f = jnp.maximum(n, 1).astype(F32)
    large = MAX_EXACT + (jnp.log(n_f / MAX_EXACT) / math.log(MAX_DISTANCE / MAX_EXACT)
                         * (HALF_BUCKETS - MAX_EXACT)).astype(I32)
    large = jnp.minimum(large, HALF_BUCKETS - 1)
    return side + jnp.where(n < MAX_EXACT, n, large)


def _table_lookup(tbl_row, bucket):
    rows, cols = bucket.shape
    tb = jnp.broadcast_to(tbl_row, (rows, LANES))
    parts = [jnp.take_along_axis(tb, bucket[:, c:c + LANES], axis=1) for c in range(0, cols, LANES)]
    return parts[0] if len(parts) == 1 else jnp.concatenate(parts, axis=1)


def _group_rms_scale(x, group):
    t = x * x
    if group == LANES:
        return lax.rsqrt(jnp.mean(t, axis=-1, keepdims=True) + RMS_EPS)
    lane = lax.broadcasted_iota(I32, x.shape, 1)
    lo = lane < group
    s_lo = jnp.sum(jnp.where(lo, t, 0.0), axis=-1, keepdims=True)
    s_hi = jnp.sum(jnp.where(lo, 0.0, t), axis=-1, keepdims=True)
    return jnp.where(lo, lax.rsqrt(s_lo / group + RMS_EPS), lax.rsqrt(s_hi / group + RMS_EPS))


def _proj_kernel(x_ref, g_ref, w_ref, o_ref, h_ref):
    @pl.when(pl.program_id(1) == 0)
    def _():
        x = x_ref[...]
        r = lax.rsqrt(jnp.mean(x * x, axis=-1, keepdims=True) + RMS_EPS)
        h_ref[...] = ((x * r) * g_ref[...]).astype(BF16)

    o_ref[...] = jnp.dot(h_ref[...], w_ref[...].astype(BF16), preferred_element_type=F32).astype(BF16)


def _norm_proj(x2d, gain, w_stack, layer):
    m, d = x2d.shape
    n = w_stack.shape[2]
    return pl.pallas_call(
        _proj_kernel,
        grid=(m // TM_PROJ, n // TN_PROJ),
        in_specs=[
            pl.BlockSpec((TM_PROJ, d), lambda i, j: (i, 0)),
            pl.BlockSpec((1, d), lambda i, j: (0, 0)),
            pl.BlockSpec((None, d, TN_PROJ), lambda i, j: (layer, 0, j)),
        ],
        out_specs=pl.BlockSpec((TM_PROJ, TN_PROJ), lambda i, j: (i, j)),
        out_shape=jax.ShapeDtypeStruct((m, n), BF16),
        scratch_shapes=[pltpu.VMEM((TM_PROJ, d), BF16)],
        compiler_params=_cparams(("parallel", "arbitrary")),
        name="norm_proj",
    )(x2d, gain.reshape(1, d), w_stack)


def _mem_kv_kernel(mem_ref, g_ref, w_ref, gk_ref, k_ref, v_ref):
    x = mem_ref[0]
    r = lax.rsqrt(jnp.mean(x * x, axis=-1, keepdims=True) + RMS_EPS)
    mn = ((x * r) * g_ref[...]).astype(BF16)
    mkv = jnp.dot(mn, w_ref[...].astype(BF16), preferred_element_type=F32)
    for h in range(N_MEM_HEADS):
        kh = mkv[:, h * HEAD_DIM:(h + 1) * HEAD_DIM]
        k_ref[0, :, h * HEAD_DIM:(h + 1) * HEAD_DIM] = ((kh * _group_rms_scale(kh, HEAD_DIM)) * gk_ref[...]).astype(BF16)
    v_ref[0] = mkv[:, MEM_WIDTH:].astype(BF16)


def _mem_kv(mem, gain, w_stack, layer, gk):
    b = mem.shape[0]
    shp = jax.ShapeDtypeStruct((b, MEM_LEN, MEM_WIDTH), BF16)
    return pl.pallas_call(
        _mem_kv_kernel,
        grid=(b,),
        in_specs=[
            pl.BlockSpec((1, MEM_LEN, D_MODEL), lambda i: (i, 0, 0)),
            pl.BlockSpec((1, D_MODEL), lambda i: (0, 0)),
            pl.BlockSpec((None, D_MODEL, 2 * MEM_WIDTH), lambda i: (layer, 0, 0)),
            pl.BlockSpec((1, HEAD_DIM), lambda i: (0, 0)),
        ],
        out_specs=[pl.BlockSpec((1, MEM_LEN, MEM_WIDTH), lambda i: (i, 0, 0))] * 2,
        out_shape=[shp, shp],
        compiler_params=_cparams(("parallel",)),
        name="mem_kv",
    )(mem, gain.reshape(1, D_MODEL), w_stack, gk.reshape(1, HEAD_DIM))


def _prep_a_kernel(negm_ref, q_ref, k_ref, v_ref, qm_ref, gq_ref, gk_ref, gm_ref, qx_ref, kx_ref, vtx_ref, qmn_ref):
    q_scale = DIFF_QK_DIM ** -0.5 * LOG2E
    m_scale = HEAD_DIM ** -0.5 * LOG2E
    lane = lax.broadcasted_iota(I32, (TS_PREP, HEAD_DIM), 1)
    row = lax.broadcasted_iota(I32, (HEAD_DIM, TS_PREP), 0)
    lo_lanes, lo_rows = lane < DIFF_QK_DIM, row < DIFF_QK_DIM
    neg_shift = negm_ref[0]
    ones_rows = jnp.where(lax.broadcasted_iota(I32, (V_ROWS - HEAD_DIM, TS_PREP), 0) == 0, 1.0, 0.0).astype(BF16)
    for h in range(N_HEADS):
        sl = slice(h * HEAD_DIM, (h + 1) * HEAD_DIM)
        qt = q_ref[0, :, sl].astype(F32).T
        t = qt * qt
        r_lo = lax.rsqrt(jnp.mean(t[:DIFF_QK_DIM], axis=0, keepdims=True) + RMS_EPS)
        r_hi = lax.rsqrt(jnp.mean(t[DIFF_QK_DIM:], axis=0, keepdims=True) + RMS_EPS)
        qn = ((qt * jnp.where(lo_rows, r_lo, r_hi)) * gq_ref[...]) * q_scale
        qx_ref[0, 2 * h] = jnp.where(lo_rows, qn, jnp.where(row == ONE_COL, neg_shift, 0.0)).astype(BF16)
        qx_ref[0, 2 * h + 1] = jnp.where(lo_rows, jnp.where(row == ONE_COL - 1, neg_shift, 0.0), qn).astype(BF16)
        k = k_ref[0, :, sl].astype(F32)
        kn = (k * _group_rms_scale(k, DIFF_QK_DIM)) * gk_ref[...]
        kx_ref[0, 2 * h] = jnp.where(lo_lanes, kn, jnp.where(lane == ONE_COL, 1.0, 0.0)).astype(BF16)
        kx_ref[0, 2 * h + 1] = jnp.where(lo_lanes, jnp.where(lane == ONE_COL - 1, 1.0, 0.0), kn).astype(BF16)
        vt = v_ref[0, :, sl].astype(F32).T.astype(BF16)
        vtx_ref[0, h, 0] = jnp.concatenate([vt, ones_rows], axis=0)
    for h in range(N_MEM_HEADS):
        sl = slice(h * HEAD_DIM, (h + 1) * HEAD_DIM)
        qm = qm_ref[0, :, sl].astype(F32)
        qmn_ref[0, :, sl] = (((qm * _group_rms_scale(qm, HEAD_DIM)) * gm_ref[...]) * m_scale).astype(BF16)


def _prep_a(proj, gq, gk, gm, neg_shift):
    assert TS_PREP == TK_A
    b, s, _ = proj.shape
    gq2 = jnp.concatenate([gq, gq]).reshape(HEAD_DIM, 1)
    gk2 = jnp.concatenate([gk, gk]).reshape(1, HEAD_DIM)
    wblk = MIX_WIDTH // MEM_WIDTH
    return pl.pallas_call(
        _prep_a_kernel,
        grid=(b, s // TS_PREP),
        in_specs=[
            _smem(),
            pl.BlockSpec((1, TS_PREP, MIX_WIDTH), lambda i, j: (i, j, 0)),
            pl.BlockSpec((1, TS_PREP, MIX_WIDTH), lambda i, j: (i, j, 1)),
            pl.BlockSpec((1, TS_PREP, MIX_WIDTH), lambda i, j: (i, j, 2)),
            pl.BlockSpec((1, TS_PREP, MEM_WIDTH), lambda i, j: (i, j, 3 * wblk)),
            pl.BlockSpec((HEAD_DIM, 1), lambda i, j: (0, 0)),
            pl.BlockSpec((1, HEAD_DIM), lambda i, j: (0, 0)),
            pl.BlockSpec((1, HEAD_DIM), lambda i, j: (0, 0)),
        ],
        out_specs=[
            pl.BlockSpec((1, 2 * N_HEADS, HEAD_DIM, TS_PREP), lambda i, j: (i, 0, 0, j)),
            pl.BlockSpec((1, 2 * N_HEADS, TS_PREP, HEAD_DIM), lambda i, j: (i, 0, j, 0)),
            pl.BlockSpec((1, N_HEADS, 1, V_ROWS, TK_A), lambda i, j: (i, 0, j, 0, 0)),
            pl.BlockSpec((1, TS_PREP, MEM_WIDTH), lambda i, j: (i, j, 0)),
        ],
        out_shape=[
            jax.ShapeDtypeStruct((b, 2 * N_HEADS, HEAD_DIM, s), BF16),
            jax.ShapeDtypeStruct((b, 2 * N_HEADS, s, HEAD_DIM), BF16),
            jax.ShapeDtypeStruct((b, N_HEADS, s // TK_A, V_ROWS, TK_A), BF16),
            jax.ShapeDtypeStruct((b, s, MEM_WIDTH), BF16),
        ],
        compiler_params=_cparams(("parallel", "parallel")),
        name="prep_a",
    )(neg_shift.reshape(1), proj, proj, proj, proj, gq2, gk2, gm.reshape(1, HEAD_DIM))


def _diff_attn_kernel(qlo_ref, qhi_ref, klo_ref, khi_ref, qrun_ref, krun_ref,
                      far_ref, lq1_ref, lk1_ref, lq2_ref, lk2_ref,
                      qx_ref, kx_ref, vtx_ref, posq_ref, posk_ref, tbl_ref, diag_ref, gsub_ref,
                      *rest, lam_init, bounded, n_cast):
    w_f32_refs, o_ref, w_bf16_refs = rest[:n_cast], rest[n_cast], rest[n_cast + 1:2 * n_cast + 1]
    acc_ref, m_scratch = rest[2 * n_cast + 1], rest[2 * n_cast + 2:]
    b, iq, j = pl.program_id(0), pl.program_id(1), pl.program_id(2)
    pos_far = klo_ref[b, j] - qhi_ref[b, iq] >= FAR_DIST
    neg_far = khi_ref[b, j] - qlo_ref[b, iq] <= -FAR_DIST
    far = jnp.logical_or(pos_far, neg_far)

    @pl.when(j == 0)
    def _():
        acc_ref[...] = jnp.zeros(acc_ref.shape, F32)
        if not bounded:
            m_scratch[0][...] = jnp.full(m_scratch[0].shape, NEG_BIG, F32)

    def logits(hc):
        return jnp.dot(kx_ref[0, hc], qx_ref[0, hc], preferred_element_type=F32)

    def accumulate(hc, s, far_const, bias):
        h = hc // 2
        if bounded:
            e = jnp.exp2(s)
            if bias is not None:
                e = e * bias
            pv = jnp.dot(vtx_ref[0, h, 0], e.astype(BF16), preferred_element_type=F32)
            acc_ref[hc] += pv if far_const is None else far_const * pv
        else:
            if bias is not None:
                s = s + bias
            m_ref = m_scratch[0]
            off = 0.0 if far_const is None else far_const
            m_old = m_ref[hc]
            m_new = jnp.maximum(m_old, jnp.max(s, axis=0, keepdims=True) + off)
            p = jnp.exp2(s - (m_new - off)).astype(BF16)
            acc_ref[hc] = (jnp.exp2(m_old - m_new) * acc_ref[hc]
                           + jnp.dot(vtx_ref[0, h, 0], p, preferred_element_type=F32))
            m_ref[hc] = m_new

    def all_heads(bias_of_head, far_const_of_head):
        for src, dst in zip(w_f32_refs, w_bf16_refs):
            dst[...] = src[...].astype(BF16)
        s_next = logits(0)
        bias = None
        bias_next = None if bias_of_head is None else bias_of_head(0)
        for hc in range(2 * N_HEADS):
            h, c = divmod(hc, 2)
            s = s_next
            if hc + 1 < 2 * N_HEADS:
                s_next = logits(hc + 1)
            if bias_of_head is not None and c == 0:
                bias = bias_next
                if h + 1 < N_HEADS:
                    bias_next = bias_of_head(h + 1)
            accumulate(hc, s, None if far_const_of_head is None else far_const_of_head(h), bias)

    @pl.when(far)
    def _():
        all_heads(None, lambda h: jnp.where(pos_far, far_ref[1, h], far_ref[0, h]))

    runs = jnp.logical_and(qrun_ref[b, iq] == 1, krun_ref[b, j] == 1)

    @pl.when(jnp.logical_and(jnp.logical_not(far), runs))
    def _():
        d0 = klo_ref[b, j] - qlo_ref[b, iq]

        def bias_of_head(h):
            offs = [jnp.clip(d0 - c + DIAG_CENTER, DIAG_OFF_MIN, DIAG_OFF_MAX) for c in range(0, TQ_A, LANES)]
            return jnp.concatenate([diag_ref[h, pl.ds(off, TK_A), :] for off in offs], axis=1)

        all_heads(bias_of_head, None)

    @pl.when(jnp.logical_and(jnp.logical_not(far), jnp.logical_not(runs)))
    def _():
        bucket = []

        def bias_of_head(h):
            if not bucket:
                bucket.append(_t5_bucket(posk_ref[0] - posq_ref[0]))
            return _table_lookup(tbl_ref[h], bucket[0])

        all_heads(bias_of_head, None)

    @pl.when(j == pl.num_programs(2) - 1)
    def _():
        lam = (jnp.exp(jnp.sum(lq1_ref[...] * lk1_ref[...], axis=-1, keepdims=True))
               - jnp.exp(jnp.sum(lq2_ref[...] * lk2_ref[...], axis=-1, keepdims=True)) + lam_init)
        for h in range(N_HEADS):
            a0, a1 = acc_ref[2 * h], acc_ref[2 * h + 1]
            o_t = (a0[:HEAD_DIM] / a0[HEAD_DIM:HEAD_DIM + 1]
                   - lam * (a1[:HEAD_DIM] / a1[HEAD_DIM:HEAD_DIM + 1]))
            r = lax.rsqrt(jnp.mean(o_t * o_t, axis=0, keepdims=True) + RMS_EPS)
            o_t = ((o_t * r) * gsub_ref[...]) * (1.0 - lam_init)
            o_ref[0, :, h * HEAD_DIM:(h + 1) * HEAD_DIM] = o_t.T.astype(BF16)


def _diff_attention(qx, kx, vtx, positions, tbl_log2, diag, far_consts, lam_vecs, gsub, lam_init, bounded,
                    cast_srcs):
    assert TK_A <= DIAG_ROWS_MAX
    b, _, s, _ = kx.shape
    nq, nk = s // TQ_A, s // TK_A
    pq = positions.reshape(b, nq, TQ_A)
    pk = positions.reshape(b, nk, TK_A)
    ranges = (pq.min(-1), pq.max(-1), pk.min(-1), pk.max(-1),
              _consecutive_runs(positions, [(t * TQ_A, (t + 1) * TQ_A) for t in range(nq)]),
              _consecutive_runs(positions, [(t * TK_A, (t + 1) * TK_A) for t in range(nk)]))
    lam_spec = pl.BlockSpec((1, DIFF_QK_DIM), lambda i, q, k, *_: (0, 0))
    scratch = [pltpu.VMEM((2 * N_HEADS, V_ROWS, TQ_A), F32)]
    if not bounded:
        scratch.append(pltpu.VMEM((2 * N_HEADS, 1, TQ_A), F32))
    n_steps = b * nq * nk
    assert all(w.shape[0] % (16 * n_steps) == 0 for w in cast_srcs)
    cast_specs = [pl.BlockSpec((w.shape[0] // n_steps, w.shape[1]), lambda i, q, k, *_: ((i * nq + q) * nk + k, 0))
                  for w in cast_srcs]
    grid_spec = pltpu.PrefetchScalarGridSpec(
        num_scalar_prefetch=len(ranges),
        grid=(b, nq, nk),
        in_specs=[
            _smem(), lam_spec, lam_spec, lam_spec, lam_spec,
            pl.BlockSpec((1, 2 * N_HEADS, HEAD_DIM, TQ_A), lambda i, q, k, *_: (i, 0, 0, q)),
            pl.BlockSpec((1, 2 * N_HEADS, TK_A, HEAD_DIM), lambda i, q, k, *_: (i, 0, k, 0)),
            pl.BlockSpec((1, N_HEADS, 1, V_ROWS, TK_A), lambda i, q, k, *_: (i, 0, k, 0, 0)),
            pl.BlockSpec((1, 1, TQ_A), lambda i, q, k, *_: (i, 0, q)),
            pl.BlockSpec((1, TK_A, 1), lambda i, q, k, *_: (i, k, 0)),
            pl.BlockSpec((N_HEADS, 1, LANES), lambda i, q, k, *_: (0, 0, 0)),
            pl.BlockSpec((N_HEADS, DIAG_ROWS, LANES), lambda i, q, k, *_: (0, 0, 0), pipeline_mode=pl.Buffered(1)),
            pl.BlockSpec((HEAD_DIM, 1), lambda i, q, k, *_: (0, 0)),
            *cast_specs,
        ],
        out_specs=[pl.BlockSpec((1, TQ_A, MIX_WIDTH), lambda i, q, k, *_: (i, q, 0)), *cast_specs],
        scratch_shapes=scratch,
    )
    outs = pl.pallas_call(
        functools.partial(_diff_attn_kernel, lam_init=lam_init, bounded=bounded, n_cast=len(cast_srcs)),
        grid_spec=grid_spec,
        out_shape=[jax.ShapeDtypeStruct((b, s, MIX_WIDTH), BF16),
                   *[jax.ShapeDtypeStruct(w.shape, BF16) for w in cast_srcs]],
        compiler_params=_cparams(("arbitrary", "arbitrary", "arbitrary")),
        name="diff_attn_bounded" if bounded else "diff_attn_running_max",
    )(*ranges, far_consts, *[v.reshape(1, DIFF_QK_DIM) for v in lam_vecs],
      qx, kx, vtx, positions.reshape(b, 1, s), positions.reshape(b, s, 1), tbl_log2, diag,
      gsub.reshape(HEAD_DIM, 1), *cast_srcs)
    return outs[0], tuple(outs[1:])


def _prep_b_kernel(q_ref, k_ref, v_ref, qm_ref, gq_ref, gk_ref, gm_ref, qt_ref, kn_ref, vt_ref, qmn_ref):
    scale = HEAD_DIM ** -0.5 * LOG2E
    n_blk = TS_PREP // WINDOW
    ones_rows = jnp.where(lax.broadcasted_iota(I32, (V_ROWS - HEAD_DIM, TS_PREP), 0) == 0, 1.0, 0.0).astype(BF16)
    for h in range(N_HEADS):
        g, hg = divmod(h, GQA_GROUP)
        qt = q_ref[0, :, h * HEAD_DIM:(h + 1) * HEAD_DIM].astype(F32).T
        r = lax.rsqrt(jnp.mean(qt * qt, axis=0, keepdims=True) + RMS_EPS)
        qt = (((qt * r) * gq_ref[...]) * scale).astype(BF16)
        for n in range(n_blk):
            qt_ref[0, n, g, :, hg * WINDOW:(hg + 1) * WINDOW] = qt[:, n * WINDOW:(n + 1) * WINDOW]
    for g in range(N_KV_HEADS):
        sl = slice(g * HEAD_DIM, (g + 1) * HEAD_DIM)
        k = k_ref[0, :, sl].astype(F32)
        kn_ref[0, g] = ((k * _group_rms_scale(k, HEAD_DIM)) * gk_ref[...]).astype(BF16)
        vt = jnp.concatenate([v_ref[0, :, sl].astype(F32).T.astype(BF16), ones_rows], axis=0)
        for n in range(n_blk):
            vt_ref[0, g, n] = vt[:, n * WINDOW:(n + 1) * WINDOW]
        qm = qm_ref[0, :, sl].astype(F32)
        qmn_ref[0, :, sl] = (((qm * _group_rms_scale(qm, HEAD_DIM)) * gm_ref[...]) * scale).astype(BF16)


def _prep_b(proj, gq, gk, gm):
    b, s, _ = proj.shape
    kblk = MIX_WIDTH // KV_WIDTH
    n_blk = TS_PREP // WINDOW
    g = lambda v: v.reshape(1, HEAD_DIM)
    return pl.pallas_call(
        _prep_b_kernel,
        grid=(b, s // TS_PREP),
        in_specs=[
            pl.BlockSpec((1, TS_PREP, MIX_WIDTH), lambda i, j: (i, j, 0)),
            pl.BlockSpec((1, TS_PREP, KV_WIDTH), lambda i, j: (i, j, kblk)),
            pl.BlockSpec((1, TS_PREP, KV_WIDTH), lambda i, j: (i, j, kblk + 1)),
            pl.BlockSpec((1, TS_PREP, MEM_WIDTH), lambda i, j: (i, j, kblk + 2)),
            pl.BlockSpec((HEAD_DIM, 1), lambda i, j: (0, 0)),
            pl.BlockSpec((1, HEAD_DIM), lambda i, j: (0, 0)),
            pl.BlockSpec((1, HEAD_DIM), lambda i, j: (0, 0)),
        ],
        out_specs=[
            pl.BlockSpec((1, n_blk, N_KV_HEADS, HEAD_DIM, GQA_GROUP * WINDOW), lambda i, j: (i, j, 0, 0, 0)),
            pl.BlockSpec((1, N_KV_HEADS, TS_PREP, HEAD_DIM), lambda i, j: (i, 0, j, 0)),
            pl.BlockSpec((1, N_KV_HEADS, n_blk, V_ROWS, WINDOW), lambda i, j: (i, 0, j, 0, 0)),
            pl.BlockSpec((1, TS_PREP, MEM_WIDTH), lambda i, j: (i, j, 0)),
        ],
        out_shape=[
            jax.ShapeDtypeStruct((b, s // WINDOW, N_KV_HEADS, HEAD_DIM, GQA_GROUP * WINDOW), BF16),
            jax.ShapeDtypeStruct((b, N_KV_HEADS, s, HEAD_DIM), BF16),
            jax.ShapeDtypeStruct((b, N_KV_HEADS, s // WINDOW, V_ROWS, WINDOW), BF16),
            jax.ShapeDtypeStruct((b, s, MEM_WIDTH), BF16),
        ],
        compiler_params=_cparams(("parallel", "parallel")),
        name="prep_b",
    )(proj, proj, proj, proj, gq.reshape(HEAD_DIM, 1), g(gk), g(gm))


def _win_attn_kernel(sink_ref, run_ref, qt_ref, k_ref, vt_ref, posq_ref, posk_ref, tbl_ref, diag_ref, o_ref):
    n_blocks = posq_ref.shape[1]
    n_sub = o_ref.shape[1] // WINDOW
    kw = 3 * WINDOW
    lane3 = lax.broadcasted_iota(I32, (1, GQA_GROUP * WINDOW), 1)

    def window(nl):
        n = pl.program_id(1) * n_sub + nl
        nb = jnp.clip(n - 1, 0, n_blocks - 3)
        return n, nb, pl.multiple_of(nb * WINDOW, WINDOW)

    def logits(nl, g):
        start = window(nl)[2]
        return jnp.dot(k_ref[0, g, pl.ds(start, kw), :], qt_ref[0, nl, g], preferred_element_type=F32)

    def in_window(nl):
        n, _, start = window(nl)
        ki = start + lax.broadcasted_iota(I32, (kw, WINDOW), 0)
        qi = n * WINDOW + lax.broadcasted_iota(I32, (kw, WINDOW), 1)
        return jnp.abs(ki - qi) <= WINDOW

    def masked_bucket(nl):
        n, _, start = window(nl)
        rel = posk_ref[0, pl.ds(start, kw), :] - posq_ref[0, pl.ds(n, 1), :]
        return jnp.where(in_window(nl), _t5_bucket(rel), N_BUCKETS)

    def bias_from_bucket(bucket, nl, h):
        return _table_lookup(tbl_ref[h], bucket)

    def bias_from_diag(valid, nl, h):
        n, _, start = window(nl)
        off = pl.multiple_of(start - n * WINDOW + DIAG_CENTER, 16)
        return jnp.where(valid, diag_ref[h, pl.ds(off, kw), :], NEG_BIG)

    def finish(nl, g, s, block_state, head_bias):
        nb = window(nl)[1]
        heads = range(g * GQA_GROUP, (g + 1) * GQA_GROUP)
        z = s + jnp.concatenate([head_bias(block_state, nl, h) for h in heads], axis=1)
        sink = jnp.where(lane3 < WINDOW, sink_ref[heads[0]],
                         jnp.where(lane3 < 2 * WINDOW, sink_ref[heads[1]], sink_ref[heads[2]]))
        m = jnp.maximum(jnp.max(z, axis=0, keepdims=True), sink)
        p = jnp.exp2(z - m).astype(BF16)
        vt = jnp.concatenate([vt_ref[0, g, nb + t] for t in range(3)], axis=1)
        acc = jnp.dot(vt, p, preferred_element_type=F32)
        o_t = acc[:HEAD_DIM] / (acc[HEAD_DIM:HEAD_DIM + 1] + jnp.exp2(sink - m))
        for hg, h in enumerate(heads):
            o_ref[0, nl * WINDOW:(nl + 1) * WINDOW, h * HEAD_DIM:(h + 1) * HEAD_DIM] = (
                o_t[:, hg * WINDOW:(hg + 1) * WINDOW].T.astype(BF16))

    def all_chains(block_state_of, head_bias):
        chains = [(nl, g) for nl in range(n_sub) for g in range(N_KV_HEADS)]
        s_next = logits(*chains[0])
        block_state = None
        for idx, (nl, g) in enumerate(chains):
            s = s_next
            if idx + 1 < len(chains):
                s_next = logits(*chains[idx + 1])
            if g == 0:
                block_state = block_state_of(nl)
            finish(nl, g, s, block_state, head_bias)

    is_run = run_ref[pl.program_id(0), pl.program_id(1)] == 1

    @pl.when(is_run)
    def _():
        all_chains(in_window, bias_from_diag)

    @pl.when(jnp.logical_not(is_run))
    def _():
        all_chains(masked_bucket, bias_from_bucket)


def _win_attention(qt, kn, vt, positions, tbl_log2, diag, sink_log2):
    b, _, s, _ = kn.shape
    n_blocks = s // WINDOW
    n_sub = TQ_B // WINDOW
    runs = _consecutive_runs(positions, [(max(0, t * TQ_B - WINDOW), min(s, (t + 1) * TQ_B + WINDOW))
                                         for t in range(s // TQ_B)])
    return pl.pallas_call(
        _win_attn_kernel,
        grid=(b, s // TQ_B),
        in_specs=[
            _smem(), _smem(),
            pl.BlockSpec((1, n_sub, N_KV_HEADS, HEAD_DIM, GQA_GROUP * WINDOW), lambda i, j: (i, j, 0, 0, 0)),
            pl.BlockSpec((1, N_KV_HEADS, s, HEAD_DIM), lambda i, j: (i, 0, 0, 0)),
            pl.BlockSpec((1, N_KV_HEADS, n_blocks, V_ROWS, WINDOW), lambda i, j: (i, 0, 0, 0, 0)),
            pl.BlockSpec((1, n_blocks, WINDOW), lambda i, j: (i, 0, 0)),
            pl.BlockSpec((1, s, 1), lambda i, j: (i, 0, 0)),
            pl.BlockSpec((N_HEADS, 1, LANES), lambda i, j: (0, 0, 0)),
            pl.BlockSpec((N_HEADS, DIAG_ROWS, LANES), lambda i, j: (0, 0, 0), pipeline_mode=pl.Buffered(1)),
        ],
        out_specs=pl.BlockSpec((1, TQ_B, MIX_WIDTH), lambda i, j: (i, j, 0)),
        out_shape=jax.ShapeDtypeStruct((b, s, MIX_WIDTH), BF16),
        compiler_params=_cparams(("parallel", "parallel")),
        name="win_attn",
    )(sink_log2, runs, qt, kn, vt, positions.reshape(b, n_blocks, WINDOW), positions.reshape(b, s, 1), tbl_log2,
      diag)


def _mem_attn_kernel(q_ref, k_ref, v_ref, o_ref):
    for h in range(N_MEM_HEADS):
        sl = slice(h * HEAD_DIM, (h + 1) * HEAD_DIM)
        s = lax.dot_general(q_ref[0, :, sl], k_ref[0, :, sl], (((1,), (1,)), ((), ())), preferred_element_type=F32)
        e = jnp.exp2(s - jnp.max(s, axis=-1, keepdims=True))
        p = (e / jnp.sum(e, axis=-1, keepdims=True)).astype(BF16)
        o_ref[0, :, sl] = jnp.dot(p, v_ref[0, :, sl], preferred_element_type=F32).astype(BF16)


def _mem_attention(qmn, k_m, v_m):
    b, s, _ = qmn.shape
    return pl.pallas_call(
        _mem_attn_kernel,
        grid=(b, s // TQ_MEM),
        in_specs=[
            pl.BlockSpec((1, TQ_MEM, MEM_WIDTH), lambda i, j: (i, j, 0)),
            pl.BlockSpec((1, MEM_LEN, MEM_WIDTH), lambda i, j: (i, 0, 0)),
            pl.BlockSpec((1, MEM_LEN, MEM_WIDTH), lambda i, j: (i, 0, 0)),
        ],
        out_specs=pl.BlockSpec((1, TQ_MEM, MEM_WIDTH), lambda i, j: (i, j, 0)),
        out_shape=jax.ShapeDtypeStruct((b, s, MEM_WIDTH), BF16),
        compiler_params=_cparams(("parallel", "parallel")),
        name="mem_attn",
    )(qmn, k_m, v_m)


def _out_proj_kernel(x_ref, o_ref, om_ref, wo_ref, wm_ref, y_ref):
    y_ref[...] = (x_ref[...]
                  + jnp.dot(o_ref[...], wo_ref[...].astype(BF16), preferred_element_type=F32)
                  + jnp.dot(om_ref[...], wm_ref[...].astype(BF16), preferred_element_type=F32))


def _out_proj(x2d, o2d, om2d, w_stack, layer):
    m, d = x2d.shape
    return pl.pallas_call(
        _out_proj_kernel,
        grid=(m // TM_OUT, d // TN_OUT),
        in_specs=[
            pl.BlockSpec((TM_OUT, TN_OUT), lambda i, j: (i, j)),
            pl.BlockSpec((TM_OUT, MIX_WIDTH), lambda i, j: (i, 0)),
            pl.BlockSpec((TM_OUT, MEM_WIDTH), lambda i, j: (i, 0)),
            pl.BlockSpec((None, MIX_WIDTH, TN_OUT), lambda i, j: (layer, 0, j)),
            pl.BlockSpec((None, MEM_WIDTH, TN_OUT), lambda i, j: (layer, MIX_WIDTH // MEM_WIDTH, j)),
        ],
        out_specs=pl.BlockSpec((TM_OUT, TN_OUT), lambda i, j: (i, j)),
        out_shape=jax.ShapeDtypeStruct((m, d), F32),
        compiler_params=_cparams(("parallel", "parallel")),
        name="out_proj",
    )(x2d, o2d, om2d, w_stack, w_stack)


def _mlp_kernel(x_ref, g_ref, wu_ref, wd_ref, y_ref, h_ref):
    f = pl.program_id(1)

    @pl.when(f == 0)
    def _():
        x = x_ref[...]
        r = lax.rsqrt(jnp.mean(x * x, axis=-1, keepdims=True) + RMS_EPS)
        h_ref[...] = ((x * r) * g_ref[...]).astype(BF16)
        y_ref[...] = x

    u = jnp.maximum(jnp.dot(h_ref[...], wu_ref[...].astype(BF16), preferred_element_type=F32), 0.0)
    y_ref[...] += jnp.dot((u * u).astype(BF16), wd_ref[...].astype(BF16), preferred_element_type=F32)


def _mlp(x2d, gain, wu_stack, wd_stack, layer):
    m, d = x2d.shape
    return pl.pallas_call(
        _mlp_kernel,
        grid=(m // TM_MLP, D_FF // TF_MLP),
        in_specs=[
            pl.BlockSpec((TM_MLP, d), lambda i, f: (i, 0)),
            pl.BlockSpec((1, d), lambda i, f: (0, 0)),
            pl.BlockSpec((None, d, TF_MLP), lambda i, f: (layer, 0, f)),
            pl.BlockSpec((None, TF_MLP, d), lambda i, f: (layer, f, 0)),
        ],
        out_specs=pl.BlockSpec((TM_MLP, d), lambda i, f: (i, 0)),
        out_shape=jax.ShapeDtypeStruct((m, d), F32),
        scratch_shapes=[pltpu.VMEM((TM_MLP, d), BF16)],
        compiler_params=_cparams(("parallel", "arbitrary")),
        name="mlp",
    )(x2d, gain.reshape(1, d), wu_stack, wd_stack)


def _diag_table_kernel(tbl_ref, o_ref, m_ref):
    r = lax.broadcasted_iota(I32, (DIAG_ROWS, LANES), 0)
    l = lax.broadcasted_iota(I32, (DIAG_ROWS, LANES), 1)
    bias = _table_lookup(tbl_ref[0], _t5_bucket(r - l - DIAG_CENTER))
    o_ref[0] = bias
    m_ref[0] = jnp.exp2(bias)


def _diag_bias_table(tbl_log2):
    spec = pl.BlockSpec((1, DIAG_ROWS, LANES), lambda h: (h, 0, 0))
    shape = jax.ShapeDtypeStruct((N_HEADS, DIAG_ROWS, LANES), F32)
    return pl.pallas_call(
        _diag_table_kernel,
        grid=(N_HEADS,),
        in_specs=[pl.BlockSpec((1, 1, LANES), lambda h: (h, 0, 0))],
        out_specs=[spec, spec],
        out_shape=[shape, shape],
        compiler_params=_cparams(("parallel",)),
        name="diag_bias_table",
    )(tbl_log2)


def _consecutive_runs(positions, ranges):
    b = positions.shape[0]
    step_ok = (positions[:, 1:] - positions[:, :-1] == 1).astype(I32)
    c = jnp.concatenate([jnp.zeros((b, 1), I32), jnp.cumsum(step_ok, axis=1)], axis=1)
    return jnp.stack([(c[:, hi - 1] - c[:, lo] == hi - 1 - lo).astype(I32) for lo, hi in ranges], axis=1)


def _bias_tables(rel_bias):
    t = (rel_bias.astype(F32) * LOG2E).T
    tbl = jnp.zeros((N_HEADS, 1, LANES), F32).at[:, 0, :N_BUCKETS].set(t).at[:, 0, N_BUCKETS].set(NEG_BIG)
    far = jnp.stack([t[:, HALF_BUCKETS - 1], t[:, N_BUCKETS - 1]])
    return tbl, far


def _diff_logit_shift(gq, gk, rel_bias):
    qk = 1.01 * DIFF_QK_DIM * jnp.max(jnp.abs(gq)) * jnp.max(jnp.abs(gk)) * (DIFF_QK_DIM ** -0.5 * LOG2E)
    return jnp.ceil(qk + jnp.max(jnp.abs(rel_bias)) * LOG2E).astype(F32)


def kernel(x, mem, positions, rel_bias, norm_attn, norm_mem, norm_mlp, w_in_a, a_q_norm, a_k_norm, a_lambda_q1, a_lambda_k1, a_lambda_q2, a_lambda_k2, a_subln, w_in_b, b_q_norm, b_k_norm, b_sink, w_mem_kv, m_q_norm, m_k_norm, w_out, w_up, w_down):
    b, s, d = x.shape
    depth = norm_attn.shape[0]
    tbl_log2, far_log2 = _bias_tables(rel_bias)
    diag, diag_mult = _diag_bias_table(tbl_log2)
    x2d = x.reshape(b * s, d)
    later_w = (w_out, w_up, w_down, w_in_b)
    for i in range(depth):
        j = i // 2
        k_m, v_m = _mem_kv(mem, norm_mem[i], w_mem_kv, i, m_k_norm[i])
        if i % 2 == 0:
            proj = _norm_proj(x2d, norm_attn[i], w_in_a, j).reshape(b, s, -1)
            shift = _diff_logit_shift(a_q_norm[j], a_k_norm[j], rel_bias)
            bounded = shift <= SHIFT_LIMIT
            qx, kx, vtx, qmn = _prep_a(proj, a_q_norm[j], a_k_norm[j], m_q_norm[i], jnp.where(bounded, -shift, 0.0))
            lam_init = 0.8 - 0.6 * math.exp(-0.3 * i)
            cast_srcs = tuple(w.reshape(-1, w.shape[-1]) for w in later_w) if i == 0 else ()
            attn = functools.partial(
                _diff_attention, qx, kx, vtx, positions,
                lam_vecs=(a_lambda_q1[j], a_lambda_k1[j], a_lambda_q2[j], a_lambda_k2[j]), gsub=a_subln[j],
                lam_init=lam_init, cast_srcs=cast_srcs)
            o, casts = lax.cond(
                bounded,
                lambda: attn(jnp.exp2(tbl_log2), diag_mult, far_consts=jnp.exp2(far_log2), bounded=True),
                lambda: attn(tbl_log2, diag, far_consts=far_log2, bounded=False))
            if i == 0:
                later_w = tuple(c.reshape(w.shape) for c, w in zip(casts, later_w))
                w_out, w_up, w_down, w_in_b = later_w
        else:
            proj = _norm_proj(x2d, norm_attn[i], w_in_b, j).reshape(b, s, -1)
            qt, kn, vt, qmn = _prep_b(proj, b_q_norm[j], b_k_norm[j], m_q_norm[i])
            o = _win_attention(qt, kn, vt, positions, tbl_log2, diag, b_sink[j].astype(F32) * LOG2E)
        o_m = _mem_attention(qmn, k_m, v_m)
        x2d = _out_proj(x2d, o.reshape(b * s, MIX_WIDTH), o_m.reshape(b * s, MEM_WIDTH), w_out, i)
        x2d = _mlp(x2d, norm_mlp[i], w_up, w_down, i)
    return x2d.reshape(b, s, d)
```

```python
import functools
import math

import jax
import jax.numpy as jnp
from jax import lax
from jax.experimental import pallas as pl
from jax.experimental.pallas import tpu as pltpu

F32 = jnp.float32
BF16 = jnp.bfloat16
I32 = jnp.int32

D_MODEL = 2048
N_HEADS = 12
HEAD_DIM = 128
DIFF_QK_DIM = 64
N_KV_HEADS = 4
GQA_GROUP = 3
MIX_WIDTH = N_HEADS * HEAD_DIM
KV_WIDTH = N_KV_HEADS * HEAD_DIM
WINDOW = 128
N_MEM_HEADS = 4
MEM_WIDTH = N_MEM_HEADS * HEAD_DIM
MEM_LEN = 256
D_FF = 4 * D_MODEL
N_BUCKETS = 32
MAX_DISTANCE = 128
RMS_EPS = 1e-6
NEG_BIG = -1e30
LOG2E = math.log2(math.e)

HALF_BUCKETS = N_BUCKETS // 2
MAX_EXACT = HALF_BUCKETS // 2
FAR_DIST = 91

ONE_COL = DIFF_QK_DIM
V_ROWS = HEAD_DIM + 16
SHIFT_LIMIT = 50.0

LANES = 128

DIAG_ROWS_MAX = 512
DIAG_OFF_MIN = 5
DIAG_CENTER = DIAG_OFF_MIN + DIAG_ROWS_MAX + FAR_DIST
DIAG_OFF_MAX = DIAG_CENTER + FAR_DIST + LANES - 1
DIAG_ROWS = -(-(DIAG_OFF_MAX + DIAG_ROWS_MAX) // 16) * 16

VMEM_LIMIT = 56 * 1024 * 1024

TM_PROJ, TN_PROJ = 1024, 1024
TS_PREP = 512
TQ_A, TK_A = 512, 512
TQ_B = 512
TQ_MEM = 2048
TM_OUT, TN_OUT = 512, 2048
TM_MLP, TF_MLP = 1024, 512


def _cparams(sem):
    return pltpu.CompilerParams(dimension_semantics=sem, vmem_limit_bytes=VMEM_LIMIT)


def _smem():
    return pl.BlockSpec(memory_space=pltpu.SMEM)


def _t5_bucket(rel):
    side = jnp.where(rel > 0, HALF_BUCKETS, 0)
    n = jnp.abs(rel)
    n_f = jnp.maximum(n, 1).astype(F32)
    large = MAX_EXACT + (jnp.log(n_f / MAX_EXACT) / math.log(MAX_DISTANCE / MAX_EXACT)
                         * (HALF_BUCKETS - MAX_EXACT)).astype(I32)
    large = jnp.minimum(large, HALF_BUCKETS - 1)
    return side + jnp.where(n < MAX_EXACT, n, large)


def _table_lookup(tbl_row, bucket):
    rows, cols = bucket.shape
    tb = jnp.broadcast_to(tbl_row, (rows, LANES))
    parts = [jnp.take_along_axis(tb, bucket[:, c:c + LANES], axis=1) for c in range(0, cols, LANES)]
    return parts[0] if len(parts) == 1 else jnp.concatenate(parts, axis=1)


def _group_rms_scale(x, group):
    t = x * x
    if group == LANES:
        return lax.rsqrt(jnp.mean(t, axis=-1, keepdims=True) + RMS_EPS)
    lane = lax.broadcasted_iota(I32, x.shape, 1)
    lo = lane < group
    s_lo = jnp.sum(jnp.where(lo, t, 0.0), axis=-1, keepdims=True)
    s_hi = jnp.sum(jnp.where(lo, 0.0, t), axis=-1, keepdims=True)
    return jnp.where(lo, lax.rsqrt(s_lo / group + RMS_EPS), lax.rsqrt(s_hi / group + RMS_EPS))


def _proj_kernel(x_ref, g_ref, w_ref, o_ref, h_ref):
    @pl.when(pl.program_id(1) == 0)
    def _():
        x = x_ref[...]
        r = lax.rsqrt(jnp.mean(x * x, axis=-1, keepdims=True) + RMS_EPS)
        h_ref[...] = ((x * r) * g_ref[...]).astype(BF16)

    o_ref[...] = jnp.dot(h_ref[...], w_ref[...].astype(BF16), preferred_element_type=F32).astype(BF16)


def _norm_proj(x2d, gain, w_stack, layer):
    m, d = x2d.shape
    n = w_stack.shape[2]
    return pl.pallas_call(
        _proj_kernel,
        grid=(m // TM_PROJ, n // TN_PROJ),
        in_specs=[
            pl.BlockSpec((TM_PROJ, d), lambda i, j: (i, 0)),
            pl.BlockSpec((1, d), lambda i, j: (0, 0)),
            pl.BlockSpec((None, d, TN_PROJ), lambda i, j: (layer, 0, j)),
        ],
        out_specs=pl.BlockSpec((TM_PROJ, TN_PROJ), lambda i, j: (i, j)),
        out_shape=jax.ShapeDtypeStruct((m, n), BF16),
        scratch_shapes=[pltpu.VMEM((TM_PROJ, d), BF16)],
        compiler_params=_cparams(("parallel", "arbitrary")),
        name="norm_proj",
    )(x2d, gain.reshape(1, d), w_stack)


def _mem_kv_kernel(mem_ref, g_ref, w_ref, gk_ref, k_ref, v_ref):
    x = mem_ref[0]
    r = lax.rsqrt(jnp.mean(x * x, axis=-1, keepdims=True) + RMS_EPS)
    mn = ((x * r) * g_ref[...]).astype(BF16)
    mkv = jnp.dot(mn, w_ref[...].astype(BF16), preferred_element_type=F32)
    for h in range(N_MEM_HEADS):
        kh = mkv[:, h * HEAD_DIM:(h + 1) * HEAD_DIM]
        k_ref[0, :, h * HEAD_DIM:(h + 1) * HEAD_DIM] = ((kh * _group_rms_scale(kh, HEAD_DIM)) * gk_ref[...]).astype(BF16)
    v_ref[0] = mkv[:, MEM_WIDTH:].astype(BF16)


def _mem_kv(mem, gain, w_stack, layer, gk):
    b = mem.shape[0]
    shp = jax.ShapeDtypeStruct((b, MEM_LEN, MEM_WIDTH), BF16)
    return pl.pallas_call(
        _mem_kv_kernel,
        grid=(b,),
        in_specs=[
            pl.BlockSpec((1, MEM_LEN, D_MODEL), lambda i: (i, 0, 0)),
            pl.BlockSpec((1, D_MODEL), lambda i: (0, 0)),
            pl.BlockSpec((None, D_MODEL, 2 * MEM_WIDTH), lambda i: (layer, 0, 0)),
            pl.BlockSpec((1, HEAD_DIM), lambda i: (0, 0)),
        ],
        out_specs=[pl.BlockSpec((1, MEM_LEN, MEM_WIDTH), lambda i: (i, 0, 0))] * 2,
        out_shape=[shp, shp],
        compiler_params=_cparams(("parallel",)),
        name="mem_kv",
    )(mem, gain.reshape(1, D_MODEL), w_stack, gk.reshape(1, HEAD_DIM))


def _prep_a_kernel(negm_ref, q_ref, k_ref, v_ref, qm_ref, gq_ref, gk_ref, gm_ref, qx_ref, kx_ref, vtx_ref, qmn_ref):
    q_scale = DIFF_QK_DIM ** -0.5 * LOG2E
    m_scale = HEAD_DIM ** -0.5 * LOG2E
    lane = lax.broadcasted_iota(I32, (TS_PREP, HEAD_DIM), 1)
    row = lax.broadcasted_iota(I32, (HEAD_DIM, TS_PREP), 0)
    lo_lanes, lo_rows = lane < DIFF_QK_DIM, row < DIFF_QK_DIM
    neg_shift = negm_ref[0]
    ones_rows = jnp.where(lax.broadcasted_iota(I32, (V_ROWS - HEAD_DIM, TS_PREP), 0) == 0, 1.0, 0.0).astype(BF16)
    for h in range(N_HEADS):
        sl = slice(h * HEAD_DIM, (h + 1) * HEAD_DIM)
        qt = q_ref[0, :, sl].astype(F32).T
        t = qt * qt
        r_lo = lax.rsqrt(jnp.mean(t[:DIFF_QK_DIM], axis=0, keepdims=True) + RMS_EPS)
        r_hi = lax.rsqrt(jnp.mean(t[DIFF_QK_DIM:], axis=0, keepdims=True) + RMS_EPS)
        qn = ((qt * jnp.where(lo_rows, r_lo, r_hi)) * gq_ref[...]) * q_scale
        qx_ref[0, 2 * h] = jnp.where(lo_rows, qn, jnp.where(row == ONE_COL, neg_shift, 0.0)).astype(BF16)
        qx_ref[0, 2 * h + 1] = jnp.where(lo_rows, jnp.where(row == ONE_COL - 1, neg_shift, 0.0), qn).astype(BF16)
        k = k_ref[0, :, sl].astype(F32)
        kn = (k * _group_rms_scale(k, DIFF_QK_DIM)) * gk_ref[...]
        kx_ref[0, 2 * h] = jnp.where(lo_lanes, kn, jnp.where(lane == ONE_COL, 1.0, 0.0)).astype(BF16)
        kx_ref[0, 2 * h + 1] = jnp.where(lo_lanes, jnp.where(lane == ONE_COL - 1, 1.0, 0.0), kn).astype(BF16)
        vt = v_ref[0, :, sl].astype(F32).T.astype(BF16)
        vtx_ref[0, h, 0] = jnp.concatenate([vt, ones_rows], axis=0)
    for h in range(N_MEM_HEADS):
        sl = slice(h * HEAD_DIM, (h + 1) * HEAD_DIM)
        qm = qm_ref[0, :, sl].astype(F32)
        qmn_ref[0, :, sl] = (((qm * _group_rms_scale(qm, HEAD_DIM)) * gm_ref[...]) * m_scale).astype(BF16)


def _prep_a(proj, gq, gk, gm, neg_shift):
    assert TS_PREP == TK_A
    b, s, _ = proj.shape
    gq2 = jnp.concatenate([gq, gq]).reshape(HEAD_DIM, 1)
    gk2 = jnp.concatenate([gk, gk]).reshape(1, HEAD_DIM)
    wblk = MIX_WIDTH // MEM_WIDTH
    return pl.pallas_call(
        _prep_a_kernel,
        grid=(b, s // TS_PREP),
        in_specs=[
            _smem(),
            pl.BlockSpec((1, TS_PREP, MIX_WIDTH), lambda i, j: (i, j, 0)),
            pl.BlockSpec((1, TS_PREP, MIX_WIDTH), lambda i, j: (i, j, 1)),
            pl.BlockSpec((1, TS_PREP, MIX_WIDTH), lambda i, j: (i, j, 2)),
            pl.BlockSpec((1, TS_PREP, MEM_WIDTH), lambda i, j: (i, j, 3 * wblk)),
            pl.BlockSpec((HEAD_DIM, 1), lambda i, j: (0, 0)),
            pl.BlockSpec((1, HEAD_DIM), lambda i, j: (0, 0)),
            pl.BlockSpec((1, HEAD_DIM), lambda i, j: (0, 0)),
        ],
        out_specs=[
            pl.BlockSpec((1, 2 * N_HEADS, HEAD_DIM, TS_PREP), lambda i, j: (i, 0, 0, j)),
            pl.BlockSpec((1, 2 * N_HEADS, TS_PREP, HEAD_DIM), lambda i, j: (i, 0, j, 0)),
            pl.BlockSpec((1, N_HEADS, 1, V_ROWS, TK_A), lambda i, j: (i, 0, j, 0, 0)),
            pl.BlockSpec((1, TS_PREP, MEM_WIDTH), lambda i, j: (i, j, 0)),
        ],
        out_shape=[
            jax.ShapeDtypeStruct((b, 2 * N_HEADS, HEAD_DIM, s), BF16),
            jax.ShapeDtypeStruct((b, 2 * N_HEADS, s, HEAD_DIM), BF16),
            jax.ShapeDtypeStruct((b, N_HEADS, s // TK_A, V_ROWS, TK_A), BF16),
            jax.ShapeDtypeStruct((b, s, MEM_WIDTH), BF16),
        ],
        compiler_params=_cparams(("parallel", "parallel")),
        name="prep_a",
    )(neg_shift.reshape(1), proj, proj, proj, proj, gq2, gk2, gm.reshape(1, HEAD_DIM))


def _diff_attn_kernel(qlo_ref, qhi_ref, klo_ref, khi_ref, qrun_ref, krun_ref,
                      far_ref, lq1_ref, lk1_ref, lq2_ref, lk2_ref,
                      qx_ref, kx_ref, vtx_ref, posq_ref, posk_ref, tbl_ref, diag_ref, gsub_ref,
                      *rest, lam_init, bounded, n_cast):
    w_f32_refs, o_ref, w_bf16_refs = rest[:n_cast], rest[n_cast], rest[n_cast + 1:2 * n_cast + 1]
    acc_ref, m_scratch = rest[2 * n_cast + 1], rest[2 * n_cast + 2:]
    b, iq, j = pl.program_id(0), pl.program_id(1), pl.program_id(2)
    pos_far = klo_ref[b, j] - qhi_ref[b, iq] >= FAR_DIST
    neg_far = khi_ref[b, j] - qlo_ref[b, iq] <= -FAR_DIST
    far = jnp.logical_or(pos_far, neg_far)

    @pl.when(j == 0)
    def _():
        acc_ref[...] = jnp.zeros(acc_ref.shape, F32)
        if not bounded:
            m_scratch[0][...] = jnp.full(m_scratch[0].shape, NEG_BIG, F32)

    def logits(hc):
        return jnp.dot(kx_ref[0, hc], qx_ref[0, hc], preferred_element_type=F32)

    def accumulate(hc, s, far_const, bias):
        h = hc // 2
        if bounded:
            e = jnp.exp2(s)
            if bias is not None:
                e = e * bias
            pv = jnp.dot(vtx_ref[0, h, 0], e.astype(BF16), preferred_element_type=F32)
            acc_ref[hc] += pv if far_const is None else far_const * pv
        else:
            if bias is not None:
                s = s + bias
            m_ref = m_scratch[0]
            off = 0.0 if far_const is None else far_const
            m_old = m_ref[hc]
            m_new = jnp.maximum(m_old, jnp.max(s, axis=0, keepdims=True) + off)
            p = jnp.exp2(s - (m_new - off)).astype(BF16)
            acc_ref[hc] = (jnp.exp2(m_old - m_new) * acc_ref[hc]
                           + jnp.dot(vtx_ref[0, h, 0], p, preferred_element_type=F32))
            m_ref[hc] = m_new

    def all_heads(bias_of_head, far_const_of_head):
        for src, dst in zip(w_f32_refs, w_bf16_refs):
            dst[...] = src[...].astype(BF16)
        s_next = logits(0)
        bias = None
        bias_next = None if bias_of_head is None else bias_of_head(0)
        for hc in range(2 * N_HEADS):
            h, c = divmod(hc, 2)
            s = s_next
            if hc + 1 < 2 * N_HEADS:
                s_next = logits(hc + 1)
            if bias_of_head is not None and c == 0:
                bias = bias_next
                if h + 1 < N_HEADS:
                    bias_next = bias_of_head(h + 1)
            accumulate(hc, s, None if far_const_of_head is None else far_const_of_head(h), bias)

    @pl.when(far)
    def _():
        all_heads(None, lambda h: jnp.where(pos_far, far_ref[1, h], far_ref[0, h]))

    runs = jnp.logical_and(qrun_ref[b, iq] == 1, krun_ref[b, j] == 1)

    @pl.when(jnp.logical_and(jnp.logical_not(far), runs))
    def _():
        d0 = klo_ref[b, j] - qlo_ref[b, iq]

        def bias_of_head(h):
            offs = [jnp.clip(d0 - c + DIAG_CENTER, DIAG_OFF_MIN, DIAG_OFF_MAX) for c in range(0, TQ_A, LANES)]
            return jnp.concatenate([diag_ref[h, pl.ds(off, TK_A), :] for off in offs], axis=1)

        all_heads(bias_of_head, None)

    @pl.when(jnp.logical_and(jnp.logical_not(far), jnp.logical_not(runs)))
    def _():
        bucket = []

        def bias_of_head(h):
            if not bucket:
                bucket.append(_t5_bucket(posk_ref[0] - posq_ref[0]))
            return _table_lookup(tbl_ref[h], bucket[0])

        all_heads(bias_of_head, None)

    @pl.when(j == pl.num_programs(2) - 1)
    def _():
        lam = (jnp.exp(jnp.sum(lq1_ref[...] * lk1_ref[...], axis=-1, keepdims=True))
               - jnp.exp(jnp.sum(lq2_ref[...] * lk2_ref[...], axis=-1, keepdims=True)) + lam_init)
        for h in range(N_HEADS):
            a0, a1 = acc_ref[2 * h], acc_ref[2 * h + 1]
            o_t = (a0[:HEAD_DIM] / a0[HEAD_DIM:HEAD_DIM + 1]
                   - lam * (a1[:HEAD_DIM] / a1[HEAD_DIM:HEAD_DIM + 1]))
            r = lax.rsqrt(jnp.mean(o_t * o_t, axis=0, keepdims=True) + RMS_EPS)
            o_t = ((o_t * r) * gsub_ref[...]) * (1.0 - lam_init)
            o_ref[0, :, h * HEAD_DIM:(h + 1) * HEAD_DIM] = o_t.T.astype(BF16)


def _diff_attention(qx, kx, vtx, positions, tbl_log2, diag, far_consts, lam_vecs, gsub, lam_init, bounded,
                    cast_srcs):
    assert TK_A <= DIAG_ROWS_MAX
    b, _, s, _ = kx.shape
    nq, nk = s // TQ_A, s // TK_A
    pq = positions.reshape(b, nq, TQ_A)
    pk = positions.reshape(b, nk, TK_A)
    ranges = (pq.min(-1), pq.max(-1), pk.min(-1), pk.max(-1),
              _consecutive_runs(positions, [(t * TQ_A, (t + 1) * TQ_A) for t in range(nq)]),
              _consecutive_runs(positions, [(t * TK_A, (t + 1) * TK_A) for t in range(nk)]))
    lam_spec = pl.BlockSpec((1, DIFF_QK_DIM), lambda i, q, k, *_: (0, 0))
    scratch = [pltpu.VMEM((2 * N_HEADS, V_ROWS, TQ_A), F32)]
    if not bounded:
        scratch.append(pltpu.VMEM((2 * N_HEADS, 1, TQ_A), F32))
    n_steps = b * nq * nk
    assert all(w.shape[0] % (16 * n_steps) == 0 for w in cast_srcs)
    cast_specs = [pl.BlockSpec((w.shape[0] // n_steps, w.shape[1]), lambda i, q, k, *_: ((i * nq + q) * nk + k, 0))
                  for w in cast_srcs]
    grid_spec = pltpu.PrefetchScalarGridSpec(
        num_scalar_prefetch=len(ranges),
        grid=(b, nq, nk),
        in_specs=[
            _smem(), lam_spec, lam_spec, lam_spec, lam_spec,
            pl.BlockSpec((1, 2 * N_HEADS, HEAD_DIM, TQ_A), lambda i, q, k, *_: (i, 0, 0, q)),
            pl.BlockSpec((1, 2 * N_HEADS, TK_A, HEAD_DIM), lambda i, q, k, *_: (i, 0, k, 0)),
            pl.BlockSpec((1, N_HEADS, 1, V_ROWS, TK_A), lambda i, q, k, *_: (i, 0, k, 0, 0)),
            pl.BlockSpec((1, 1, TQ_A), lambda i, q, k, *_: (i, 0, q)),
            pl.BlockSpec((1, TK_A, 1), lambda i, q, k, *_: (i, k, 0)),
            pl.BlockSpec((N_HEADS, 1, LANES), lambda i, q, k, *_: (0, 0, 0)),
            pl.BlockSpec((N_HEADS, DIAG_ROWS, LANES), lambda i, q, k, *_: (0, 0, 0), pipeline_mode=pl.Buffered(1)),
            pl.BlockSpec((HEAD_DIM, 1), lambda i, q, k, *_: (0, 0)),
            *cast_specs,
        ],
        out_specs=[pl.BlockSpec((1, TQ_A, MIX_WIDTH), lambda i, q, k, *_: (i, q, 0)), *cast_specs],
        scratch_shapes=scratch,
    )
    outs = pl.pallas_call(
        functools.partial(_diff_attn_kernel, lam_init=lam_init, bounded=bounded, n_cast=len(cast_srcs)),
        grid_spec=grid_spec,
        out_shape=[jax.ShapeDtypeStruct((b, s, MIX_WIDTH), BF16),
                   *[jax.ShapeDtypeStruct(w.shape, BF16) for w in cast_srcs]],
        compiler_params=_cparams(("arbitrary", "arbitrary", "arbitrary")),
        name="diff_attn_bounded" if bounded else "diff_attn_running_max",
    )(*ranges, far_consts, *[v.reshape(1, DIFF_QK_DIM) for v in lam_vecs],
      qx, kx, vtx, positions.reshape(b, 1, s), positions.reshape(b, s, 1), tbl_log2, diag,
      gsub.reshape(HEAD_DIM, 1), *cast_srcs)
    return outs[0], tuple(outs[1:])


def _prep_b_kernel(q_ref, k_ref, v_ref, qm_ref, gq_ref, gk_ref, gm_ref, qt_ref, kn_ref, vt_ref, qmn_ref):
    scale = HEAD_DIM ** -0.5 * LOG2E
    n_blk = TS_PREP // WINDOW
    ones_rows = jnp.where(lax.broadcasted_iota(I32, (V_ROWS - HEAD_DIM, TS_PREP), 0) == 0, 1.0, 0.0).astype(BF16)
    for h in range(N_HEADS):
        g, hg = divmod(h, GQA_GROUP)
        qt = q_ref[0, :, h * HEAD_DIM:(h + 1) * HEAD_DIM].astype(F32).T
        r = lax.rsqrt(jnp.mean(qt * qt, axis=0, keepdims=True) + RMS_EPS)
        qt = (((qt * r) * gq_ref[...]) * scale).astype(BF16)
        for n in range(n_blk):
            qt_ref[0, n, g, :, hg * WINDOW:(hg + 1) * WINDOW] = qt[:, n * WINDOW:(n + 1) * WINDOW]
    for g in range(N_KV_HEADS):
        sl = slice(g * HEAD_DIM, (g + 1) * HEAD_DIM)
        k = k_ref[0, :, sl].astype(F32)
        kn_ref[0, g] = ((k * _group_rms_scale(k, HEAD_DIM)) * gk_ref[...]).astype(BF16)
        vt = jnp.concatenate([v_ref[0, :, sl].astype(F32).T.astype(BF16), ones_rows], axis=0)
        for n in range(n_blk):
            vt_ref[0, g, n] = vt[:, n * WINDOW:(n + 1) * WINDOW]
        qm = qm_ref[0, :, sl].astype(F32)
        qmn_ref[0, :, sl] = (((qm * _group_rms_scale(qm, HEAD_DIM)) * gm_ref[...]) * scale).astype(BF16)


def _prep_b(proj, gq, gk, gm):
    b, s, _ = proj.shape
    kblk = MIX_WIDTH // KV_WIDTH
    n_blk = TS_PREP // WINDOW
    g = lambda v: v.reshape(1, HEAD_DIM)
    return pl.pallas_call(
        _prep_b_kernel,
        grid=(b, s // TS_PREP),
        in_specs=[
            pl.BlockSpec((1, TS_PREP, MIX_WIDTH), lambda i, j: (i, j, 0)),
            pl.BlockSpec((1, TS_PREP, KV_WIDTH), lambda i, j: (i, j, kblk)),
            pl.BlockSpec((1, TS_PREP, KV_WIDTH), lambda i, j: (i, j, kblk + 1)),
            pl.BlockSpec((1, TS_PREP, MEM_WIDTH), lambda i, j: (i, j, kblk + 2)),
            pl.BlockSpec((HEAD_DIM, 1), lambda i, j: (0, 0)),
            pl.BlockSpec((1, HEAD_DIM), lambda i, j: (0, 0)),
            pl.BlockSpec((1, HEAD_DIM), lambda i, j: (0, 0)),
        ],
        out_specs=[
            pl.BlockSpec((1, n_blk, N_KV_HEADS, HEAD_DIM, GQA_GROUP * WINDOW), lambda i, j: (i, j, 0, 0, 0)),
            pl.BlockSpec((1, N_KV_HEADS, TS_PREP, HEAD_DIM), lambda i, j: (i, 0, j, 0)),
            pl.BlockSpec((1, N_KV_HEADS, n_blk, V_ROWS, WINDOW), lambda i, j: (i, 0, j, 0, 0)),
            pl.BlockSpec((1, TS_PREP, MEM_WIDTH), lambda i, j: (i, j, 0)),
        ],
        out_shape=[
            jax.ShapeDtypeStruct((b, s // WINDOW, N_KV_HEADS, HEAD_DIM, GQA_GROUP * WINDOW), BF16),
            jax.ShapeDtypeStruct((b, N_KV_HEADS, s, HEAD_DIM), BF16),
            jax.ShapeDtypeStruct((b, N_KV_HEADS, s // WINDOW, V_ROWS, WINDOW), BF16),
            jax.ShapeDtypeStruct((b, s, MEM_WIDTH), BF16),
        ],
        compiler_params=_cparams(("parallel", "parallel")),
        name="prep_b",
    )(proj, proj, proj, proj, gq.reshape(HEAD_DIM, 1), g(gk), g(gm))


def _win_attn_kernel(sink_ref, run_ref, qt_ref, k_ref, vt_ref, posq_ref, posk_ref, tbl_ref, diag_ref, o_ref, *,
                     bounded):
    masked = 0.0 if bounded else NEG_BIG
    n_blocks = posq_ref.shape[1]
    n_sub = o_ref.shape[1] // WINDOW
    kw = 3 * WINDOW
    lane3 = lax.broadcasted_iota(I32, (1, GQA_GROUP * WINDOW), 1)

    def window(nl):
        n = pl.program_id(1) * n_sub + nl
        nb = jnp.clip(n - 1, 0, n_blocks - 3)
        return n, nb, pl.multiple_of(nb * WINDOW, WINDOW)

    def logits(nl, g):
        start = window(nl)[2]
        return jnp.dot(k_ref[0, g, pl.ds(start, kw), :], qt_ref[0, nl, g], preferred_element_type=F32)

    def in_window(nl):
        n, _, start = window(nl)
        ki = start + lax.broadcasted_iota(I32, (kw, WINDOW), 0)
        qi = n * WINDOW + lax.broadcasted_iota(I32, (kw, WINDOW), 1)
        return jnp.abs(ki - qi) <= WINDOW

    def masked_bucket(nl):
        n, _, start = window(nl)
        rel = posk_ref[0, pl.ds(start, kw), :] - posq_ref[0, pl.ds(n, 1), :]
        return jnp.where(in_window(nl), _t5_bucket(rel), N_BUCKETS)

    def bias_from_bucket(bucket, nl, h):
        return _table_lookup(tbl_ref[h], bucket)

    def bias_from_diag(valid, nl, h):
        n, _, start = window(nl)
        off = pl.multiple_of(start - n * WINDOW + DIAG_CENTER, 16)
        return jnp.where(valid, diag_ref[h, pl.ds(off, kw), :], masked)

    def finish(nl, g, s, block_state, head_bias):
        nb = window(nl)[1]
        heads = range(g * GQA_GROUP, (g + 1) * GQA_GROUP)
        bias = jnp.concatenate([head_bias(block_state, nl, h) for h in heads], axis=1)
        sink = jnp.where(lane3 < WINDOW, sink_ref[heads[0]],
                         jnp.where(lane3 < 2 * WINDOW, sink_ref[heads[1]], sink_ref[heads[2]]))
        if bounded:
            p = (jnp.exp2(s) * bias).astype(BF16)
            sink_term = sink
        else:
            z = s + bias
            m = jnp.maximum(jnp.max(z, axis=0, keepdims=True), sink)
            p = jnp.exp2(z - m).astype(BF16)
            sink_term = jnp.exp2(sink - m)
        vt = jnp.concatenate([vt_ref[0, g, nb + t] for t in range(3)], axis=1)
        acc = jnp.dot(vt, p, preferred_element_type=F32)
        o_t = acc[:HEAD_DIM] / (acc[HEAD_DIM:HEAD_DIM + 1] + sink_term)
        for hg, h in enumerate(heads):
            o_ref[0, nl * WINDOW:(nl + 1) * WINDOW, h * HEAD_DIM:(h + 1) * HEAD_DIM] = (
                o_t[:, hg * WINDOW:(hg + 1) * WINDOW].T.astype(BF16))

    def all_chains(block_state_of, head_bias):
        chains = [(nl, g) for nl in range(n_sub) for g in range(N_KV_HEADS)]
        s_next = logits(*chains[0])
        block_state = None
        for idx, (nl, g) in enumerate(chains):
            s = s_next
            if idx + 1 < len(chains):
                s_next = logits(*chains[idx + 1])
            if g == 0:
                block_state = block_state_of(nl)
            finish(nl, g, s, block_state, head_bias)

    is_run = run_ref[pl.program_id(0), pl.program_id(1)] == 1

    @pl.when(is_run)
    def _():
        all_chains(in_window, bias_from_diag)

    @pl.when(jnp.logical_not(is_run))
    def _():
        all_chains(masked_bucket, bias_from_bucket)


def _win_attention(qt, kn, vt, positions, tbl, diag, sink, bounded):
    b, _, s, _ = kn.shape
    n_blocks = s // WINDOW
    n_sub = TQ_B // WINDOW
    runs = _consecutive_runs(positions, [(max(0, t * TQ_B - WINDOW), min(s, (t + 1) * TQ_B + WINDOW))
                                         for t in range(s // TQ_B)])
    return pl.pallas_call(
        functools.partial(_win_attn_kernel, bounded=bounded),
        grid=(b, s // TQ_B),
        in_specs=[
            _smem(), _smem(),
            pl.BlockSpec((1, n_sub, N_KV_HEADS, HEAD_DIM, GQA_GROUP * WINDOW), lambda i, j: (i, j, 0, 0, 0)),
            pl.BlockSpec((1, N_KV_HEADS, s, HEAD_DIM), lambda i, j: (i, 0, 0, 0)),
            pl.BlockSpec((1, N_KV_HEADS, n_blocks, V_ROWS, WINDOW), lambda i, j: (i, 0, 0, 0, 0)),
            pl.BlockSpec((1, n_blocks, WINDOW), lambda i, j: (i, 0, 0)),
            pl.BlockSpec((1, s, 1), lambda i, j: (i, 0, 0)),
            pl.BlockSpec((N_HEADS, 1, LANES), lambda i, j: (0, 0, 0)),
            pl.BlockSpec((N_HEADS, DIAG_ROWS, LANES), lambda i, j: (0, 0, 0), pipeline_mode=pl.Buffered(1)),
        ],
        out_specs=pl.BlockSpec((1, TQ_B, MIX_WIDTH), lambda i, j: (i, j, 0)),
        out_shape=jax.ShapeDtypeStruct((b, s, MIX_WIDTH), BF16),
        compiler_params=_cparams(("parallel", "parallel")),
        name="win_attn_bounded" if bounded else "win_attn_max_shifted",
    )(sink, runs, qt, kn, vt, positions.reshape(b, n_blocks, WINDOW), positions.reshape(b, s, 1), tbl, diag)


def _mem_attn_kernel(q_ref, k_ref, v_ref, o_ref):
    for h in range(N_MEM_HEADS):
        sl = slice(h * HEAD_DIM, (h + 1) * HEAD_DIM)
        s = lax.dot_general(q_ref[0, :, sl], k_ref[0, :, sl], (((1,), (1,)), ((), ())), preferred_element_type=F32)
        e = jnp.exp2(s - jnp.max(s, axis=-1, keepdims=True))
        p = (e / jnp.sum(e, axis=-1, keepdims=True)).astype(BF16)
        o_ref[0, :, sl] = jnp.dot(p, v_ref[0, :, sl], preferred_element_type=F32).astype(BF16)


def _mem_attention(qmn, k_m, v_m):
    b, s, _ = qmn.shape
    return pl.pallas_call(
        _mem_attn_kernel,
        grid=(b, s // TQ_MEM),
        in_specs=[
            pl.BlockSpec((1, TQ_MEM, MEM_WIDTH), lambda i, j: (i, j, 0)),
            pl.BlockSpec((1, MEM_LEN, MEM_WIDTH), lambda i, j: (i, 0, 0)),
            pl.BlockSpec((1, MEM_LEN, MEM_WIDTH), lambda i, j: (i, 0, 0)),
        ],
        out_specs=pl.BlockSpec((1, TQ_MEM, MEM_WIDTH), lambda i, j: (i, j, 0)),
        out_shape=jax.ShapeDtypeStruct((b, s, MEM_WIDTH), BF16),
        compiler_params=_cparams(("parallel", "parallel")),
        name="mem_attn",
    )(qmn, k_m, v_m)


def _out_proj_kernel(x_ref, o_ref, om_ref, wo_ref, wm_ref, y_ref):
    y_ref[...] = (x_ref[...]
                  + jnp.dot(o_ref[...], wo_ref[...].astype(BF16), preferred_element_type=F32)
                  + jnp.dot(om_ref[...], wm_ref[...].astype(BF16), preferred_element_type=F32))


def _out_proj(x2d, o2d, om2d, w_stack, layer):
    m, d = x2d.shape
    return pl.pallas_call(
        _out_proj_kernel,
        grid=(m // TM_OUT, d // TN_OUT),
        in_specs=[
            pl.BlockSpec((TM_OUT, TN_OUT), lambda i, j: (i, j)),
            pl.BlockSpec((TM_OUT, MIX_WIDTH), lambda i, j: (i, 0)),
            pl.BlockSpec((TM_OUT, MEM_WIDTH), lambda i, j: (i, 0)),
            pl.BlockSpec((None, MIX_WIDTH, TN_OUT), lambda i, j: (layer, 0, j)),
            pl.BlockSpec((None, MEM_WIDTH, TN_OUT), lambda i, j: (layer, MIX_WIDTH // MEM_WIDTH, j)),
        ],
        out_specs=pl.BlockSpec((TM_OUT, TN_OUT), lambda i, j: (i, j)),
        out_shape=jax.ShapeDtypeStruct((m, d), F32),
        compiler_params=_cparams(("parallel", "parallel")),
        name="out_proj",
    )(x2d, o2d, om2d, w_stack, w_stack)


def _mlp_kernel(x_ref, g_ref, wu_ref, wd_ref, y_ref, h_ref):
    f = pl.program_id(1)

    @pl.when(f == 0)
    def _():
        x = x_ref[...]
        r = lax.rsqrt(jnp.mean(x * x, axis=-1, keepdims=True) + RMS_EPS)
        h_ref[...] = ((x * r) * g_ref[...]).astype(BF16)
        y_ref[...] = x

    u = jnp.maximum(jnp.dot(h_ref[...], wu_ref[...].astype(BF16), preferred_element_type=F32), 0.0)
    y_ref[...] += jnp.dot((u * u).astype(BF16), wd_ref[...].astype(BF16), preferred_element_type=F32)


def _mlp(x2d, gain, wu_stack, wd_stack, layer):
    m, d = x2d.shape
    return pl.pallas_call(
        _mlp_kernel,
        grid=(m // TM_MLP, D_FF // TF_MLP),
        in_specs=[
            pl.BlockSpec((TM_MLP, d), lambda i, f: (i, 0)),
            pl.BlockSpec((1, d), lambda i, f: (0, 0)),
            pl.BlockSpec((None, d, TF_MLP), lambda i, f: (layer, 0, f)),
            pl.BlockSpec((None, TF_MLP, d), lambda i, f: (layer, f, 0)),
        ],
        out_specs=pl.BlockSpec((TM_MLP, d), lambda i, f: (i, 0)),
        out_shape=jax.ShapeDtypeStruct((m, d), F32),
        scratch_shapes=[pltpu.VMEM((TM_MLP, d), BF16)],
        compiler_params=_cparams(("parallel", "arbitrary")),
        name="mlp",
    )(x2d, gain.reshape(1, d), wu_stack, wd_stack)


def _diag_table_kernel(tbl_ref, o_ref, m_ref):
    r = lax.broadcasted_iota(I32, (DIAG_ROWS, LANES), 0)
    l = lax.broadcasted_iota(I32, (DIAG_ROWS, LANES), 1)
    bias = _table_lookup(tbl_ref[0], _t5_bucket(r - l - DIAG_CENTER))
    o_ref[0] = bias
    m_ref[0] = jnp.exp2(bias)


def _diag_bias_table(tbl_log2):
    spec = pl.BlockSpec((1, DIAG_ROWS, LANES), lambda h: (h, 0, 0))
    shape = jax.ShapeDtypeStruct((N_HEADS, DIAG_ROWS, LANES), F32)
    return pl.pallas_call(
        _diag_table_kernel,
        grid=(N_HEADS,),
        in_specs=[pl.BlockSpec((1, 1, LANES), lambda h: (h, 0, 0))],
        out_specs=[spec, spec],
        out_shape=[shape, shape],
        compiler_params=_cparams(("parallel",)),
        name="diag_bias_table",
    )(tbl_log2)


def _consecutive_runs(positions, ranges):
    b = positions.shape[0]
    step_ok = (positions[:, 1:] - positions[:, :-1] == 1).astype(I32)
    c = jnp.concatenate([jnp.zeros((b, 1), I32), jnp.cumsum(step_ok, axis=1)], axis=1)
    return jnp.stack([(c[:, hi - 1] - c[:, lo] == hi - 1 - lo).astype(I32) for lo, hi in ranges], axis=1)


def _bias_tables(rel_bias):
    t = (rel_bias.astype(F32) * LOG2E).T
    tbl = jnp.zeros((N_HEADS, 1, LANES), F32).at[:, 0, :N_BUCKETS].set(t).at[:, 0, N_BUCKETS].set(NEG_BIG)
    far = jnp.stack([t[:, HALF_BUCKETS - 1], t[:, N_BUCKETS - 1]])
    return tbl, far


def _logit_bound(gq, gk, rel_bias, dim):
    qk = 1.01 * dim * jnp.max(jnp.abs(gq)) * jnp.max(jnp.abs(gk)) * (dim ** -0.5 * LOG2E)
    return jnp.ceil(qk + jnp.max(jnp.abs(rel_bias)) * LOG2E).astype(F32)


def kernel(x, mem, positions, rel_bias, norm_attn, norm_mem, norm_mlp, w_in_a, a_q_norm, a_k_norm, a_lambda_q1, a_lambda_k1, a_lambda_q2, a_lambda_k2, a_subln, w_in_b, b_q_norm, b_k_norm, b_sink, w_mem_kv, m_q_norm, m_k_norm, w_out, w_up, w_down):
    b, s, d = x.shape
    depth = norm_attn.shape[0]
    tbl_log2, far_log2 = _bias_tables(rel_bias)
    diag, diag_mult = _diag_bias_table(tbl_log2)
    x2d = x.reshape(b * s, d)
    later_w = (w_out, w_up, w_down, w_in_b)
    for i in range(depth):
        j = i // 2
        k_m, v_m = _mem_kv(mem, norm_mem[i], w_mem_kv, i, m_k_norm[i])
        if i % 2 == 0:
            proj = _norm_proj(x2d, norm_attn[i], w_in_a, j).reshape(b, s, -1)
            shift = _logit_bound(a_q_norm[j], a_k_norm[j], rel_bias, DIFF_QK_DIM)
            bounded = shift <= SHIFT_LIMIT
            qx, kx, vtx, qmn = _prep_a(proj, a_q_norm[j], a_k_norm[j], m_q_norm[i], jnp.where(bounded, -shift, 0.0))
            lam_init = 0.8 - 0.6 * math.exp(-0.3 * i)
            cast_srcs = tuple(w.reshape(-1, w.shape[-1]) for w in later_w) if i == 0 else ()
            attn = functools.partial(
                _diff_attention, qx, kx, vtx, positions,
                lam_vecs=(a_lambda_q1[j], a_lambda_k1[j], a_lambda_q2[j], a_lambda_k2[j]), gsub=a_subln[j],
                lam_init=lam_init, cast_srcs=cast_srcs)
            o, casts = lax.cond(
                bounded,
                lambda: attn(jnp.exp2(tbl_log2), diag_mult, far_consts=jnp.exp2(far_log2), bounded=True),
                lambda: attn(tbl_log2, diag, far_consts=far_log2, bounded=False))
            if i == 0:
                later_w = tuple(c.reshape(w.shape) for c, w in zip(casts, later_w))
                w_out, w_up, w_down, w_in_b = later_w
        else:
            proj = _norm_proj(x2d, norm_attn[i], w_in_b, j).reshape(b, s, -1)
            qt, kn, vt, qmn = _prep_b(proj, b_q_norm[j], b_k_norm[j], m_q_norm[i])
            sink_log2 = b_sink[j].astype(F32) * LOG2E
            small = jnp.logical_and(_logit_bound(b_q_norm[j], b_k_norm[j], rel_bias, HEAD_DIM) <= SHIFT_LIMIT,
                                    jnp.max(jnp.abs(sink_log2)) <= SHIFT_LIMIT)
            win = functools.partial(_win_attention, qt, kn, vt, positions)
            o = lax.cond(small,
                         lambda: win(jnp.exp2(tbl_log2), diag_mult, jnp.exp2(sink_log2), bounded=True),
                         lambda: win(tbl_log2, diag, sink_log2, bounded=False))
        o_m = _mem_attention(qmn, k_m, v_m)
        x2d = _out_proj(x2d, o.reshape(b * s, MIX_WIDTH), o_m.reshape(b * s, MEM_WIDTH), w_out, i)
        x2d = _mlp(x2d, norm_mlp[i], w_up, w_down, i)
    return x2d.reshape(b, s, d)
```

```python
import functools
import math

import jax
import jax.numpy as jnp
from jax import lax
from jax.experimental import pallas as pl
from jax.experimental.pallas import tpu as pltpu

F32 = jnp.float32
BF16 = jnp.bfloat16
I32 = jnp.int32

D_MODEL = 2048
N_HEADS = 12
HEAD_DIM = 128
DIFF_QK_DIM = 64
N_KV_HEADS = 4
GQA_GROUP = 3
MIX_WIDTH = N_HEADS * HEAD_DIM
KV_WIDTH = N_KV_HEADS * HEAD_DIM
WINDOW = 128
N_MEM_HEADS = 4
MEM_WIDTH = N_MEM_HEADS * HEAD_DIM
MEM_LEN = 256
D_FF = 4 * D_MODEL
N_BUCKETS = 32
MAX_DISTANCE = 128
RMS_EPS = 1e-6
NEG_BIG = -1e30
LOG2E = math.log2(math.e)

HALF_BUCKETS = N_BUCKETS // 2
MAX_EXACT = HALF_BUCKETS // 2
FAR_DIST = 91

ONE_COL = DIFF_QK_DIM
V_ROWS = HEAD_DIM + 16
SHIFT_LIMIT = 50.0

LANES = 128

DIAG_ROWS_MAX = 512
DIAG_OFF_MIN = 5
DIAG_CENTER = DIAG_OFF_MIN + DIAG_ROWS_MAX + FAR_DIST
DIAG_OFF_MAX = DIAG_CENTER + FAR_DIST + LANES - 1
DIAG_ROWS = -(-(DIAG_OFF_MAX + DIAG_ROWS_MAX) // 16) * 16

VMEM_LIMIT = 56 * 1024 * 1024

TM_PROJ, TN_PROJ = 1024, 1024
TS_PREP = 512
TQ_A, TK_A = 512, 512
TQ_B = 512
TQ_MEM = 2048
TM_OUT, TN_OUT = 512, 2048
TM_MLP, TF_MLP = 1024, 512


def _cparams(sem):
    return pltpu.CompilerParams(dimension_semantics=sem, vmem_limit_bytes=VMEM_LIMIT)


def _smem():
    return pl.BlockSpec(memory_space=pltpu.SMEM)


def _t5_bucket(rel):
    side = jnp.where(rel > 0, HALF_BUCKETS, 0)
    n = jnp.abs(rel)
    n_f = jnp.maximum(n, 1).astype(F32)
    large = MAX_EXACT + (jnp.log(n_f / MAX_EXACT) / math.log(MAX_DISTANCE / MAX_EXACT)
                         * (HALF_BUCKETS - MAX_EXACT)).astype(I32)
    large = jnp.minimum(large, HALF_BUCKETS - 1)
    return side + jnp.where(n < MAX_EXACT, n, large)


def _table_lookup(tbl_row, bucket):
    rows, cols = bucket.shape
    tb = jnp.broadcast_to(tbl_row, (rows, LANES))
    parts = [jnp.take_along_axis(tb, bucket[:, c:c + LANES], axis=1) for c in range(0, cols, LANES)]
    return parts[0] if len(parts) == 1 else jnp.concatenate(parts, axis=1)


def _group_rms_scale(x, group):
    t = x * x
    if group == LANES:
        return lax.rsqrt(jnp.mean(t, axis=-1, keepdims=True) + RMS_EPS)
    lane = lax.broadcasted_iota(I32, x.shape, 1)
    lo = lane < group
    s_lo = jnp.sum(jnp.where(lo, t, 0.0), axis=-1, keepdims=True)
    s_hi = jnp.sum(jnp.where(lo, 0.0, t), axis=-1, keepdims=True)
    return jnp.where(lo, lax.rsqrt(s_lo / group + RMS_EPS), lax.rsqrt(s_hi / group + RMS_EPS))


def _proj_kernel(x_ref, g_ref, w_ref, o_ref, h_ref):
    @pl.when(pl.program_id(1) == 0)
    def _():
        x = x_ref[...]
        r = lax.rsqrt(jnp.mean(x * x, axis=-1, keepdims=True) + RMS_EPS)
        h_ref[...] = ((x * r) * g_ref[...]).astype(BF16)

    o_ref[...] = jnp.dot(h_ref[...], w_ref[...].astype(BF16), preferred_element_type=F32).astype(BF16)


def _norm_proj(x2d, gain, w_stack, layer):
    m, d = x2d.shape
    n = w_stack.shape[2]
    return pl.pallas_call(
        _proj_kernel,
        grid=(m // TM_PROJ, n // TN_PROJ),
        in_specs=[
            pl.BlockSpec((TM_PROJ, d), lambda i, j: (i, 0)),
            pl.BlockSpec((1, d), lambda i, j: (0, 0)),
            pl.BlockSpec((None, d, TN_PROJ), lambda i, j: (layer, 0, j)),
        ],
        out_specs=pl.BlockSpec((TM_PROJ, TN_PROJ), lambda i, j: (i, j)),
        out_shape=jax.ShapeDtypeStruct((m, n), BF16),
        scratch_shapes=[pltpu.VMEM((TM_PROJ, d), BF16)],
        compiler_params=_cparams(("parallel", "arbitrary")),
        name="norm_proj",
    )(x2d, gain.reshape(1, d), w_stack)


def _mem_kv_kernel(mem_ref, g_ref, w_ref, gk_ref, k_ref, v_ref):
    x = mem_ref[0]
    r = lax.rsqrt(jnp.mean(x * x, axis=-1, keepdims=True) + RMS_EPS)
    mn = ((x * r) * g_ref[...]).astype(BF16)
    mkv = jnp.dot(mn, w_ref[...].astype(BF16), preferred_element_type=F32)
    for h in range(N_MEM_HEADS):
        kh = mkv[:, h * HEAD_DIM:(h + 1) * HEAD_DIM]
        k_ref[0, :, h * HEAD_DIM:(h + 1) * HEAD_DIM] = ((kh * _group_rms_scale(kh, HEAD_DIM)) * gk_ref[...]).astype(BF16)
    v_ref[0] = mkv[:, MEM_WIDTH:].astype(BF16)


def _mem_kv(mem, gain, w_stack, layer, gk):
    b = mem.shape[0]
    shp = jax.ShapeDtypeStruct((b, MEM_LEN, MEM_WIDTH), BF16)
    return pl.pallas_call(
        _mem_kv_kernel,
        grid=(b,),
        in_specs=[
            pl.BlockSpec((1, MEM_LEN, D_MODEL), lambda i: (i, 0, 0)),
            pl.BlockSpec((1, D_MODEL), lambda i: (0, 0)),
            pl.BlockSpec((None, D_MODEL, 2 * MEM_WIDTH), lambda i: (layer, 0, 0)),
            pl.BlockSpec((1, HEAD_DIM), lambda i: (0, 0)),
        ],
        out_specs=[pl.BlockSpec((1, MEM_LEN, MEM_WIDTH), lambda i: (i, 0, 0))] * 2,
        out_shape=[shp, shp],
        compiler_params=_cparams(("parallel",)),
        name="mem_kv",
    )(mem, gain.reshape(1, D_MODEL), w_stack, gk.reshape(1, HEAD_DIM))


def _prep_a_kernel(negm_ref, q_ref, k_ref, v_ref, qm_ref, gq_ref, gk_ref, gm_ref, qx_ref, kx_ref, vtx_ref, qmn_ref):
    q_scale = DIFF_QK_DIM ** -0.5 * LOG2E
    m_scale = HEAD_DIM ** -0.5 * LOG2E
    lane = lax.broadcasted_iota(I32, (TS_PREP, HEAD_DIM), 1)
    row = lax.broadcasted_iota(I32, (HEAD_DIM, TS_PREP), 0)
    lo_lanes, lo_rows = lane < DIFF_QK_DIM, row < DIFF_QK_DIM
    neg_shift = negm_ref[0]
    ones_rows = jnp.where(lax.broadcasted_iota(I32, (V_ROWS - HEAD_DIM, TS_PREP), 0) == 0, 1.0, 0.0).astype(BF16)
    q_fill = (jnp.where(row == ONE_COL, neg_shift, 0.0), jnp.where(row == ONE_COL - 1, neg_shift, 0.0))
    k_fill = (jnp.where(lane == ONE_COL, 1.0, 0.0), jnp.where(lane == ONE_COL - 1, 1.0, 0.0))
    q_gain = gq_ref[...] * q_scale
    for h in range(N_HEADS):
        sl = slice(h * HEAD_DIM, (h + 1) * HEAD_DIM)
        qt = q_ref[0, :, sl].astype(F32).T
        t = qt * qt
        r_lo = lax.rsqrt(jnp.mean(t[:DIFF_QK_DIM], axis=0, keepdims=True) + RMS_EPS)
        r_hi = lax.rsqrt(jnp.mean(t[DIFF_QK_DIM:], axis=0, keepdims=True) + RMS_EPS)
        qn = (qt * jnp.where(lo_rows, r_lo, r_hi)) * q_gain
        qx_ref[0, 2 * h] = jnp.where(lo_rows, qn, q_fill[0]).astype(BF16)
        qx_ref[0, 2 * h + 1] = jnp.where(lo_rows, q_fill[1], qn).astype(BF16)
        k = k_ref[0, :, sl].astype(F32)
        kn = (k * _group_rms_scale(k, DIFF_QK_DIM)) * gk_ref[...]
        kx_ref[0, 2 * h] = jnp.where(lo_lanes, kn, k_fill[0]).astype(BF16)
        kx_ref[0, 2 * h + 1] = jnp.where(lo_lanes, k_fill[1], kn).astype(BF16)
        vt = v_ref[0, :, sl].astype(F32).T.astype(BF16)
        vtx_ref[0, h, 0] = jnp.concatenate([vt, ones_rows], axis=0)
    for h in range(N_MEM_HEADS):
        sl = slice(h * HEAD_DIM, (h + 1) * HEAD_DIM)
        qm = qm_ref[0, :, sl].astype(F32)
        qmn_ref[0, :, sl] = ((qm * _group_rms_scale(qm, HEAD_DIM)) * (gm_ref[...] * m_scale)).astype(BF16)


def _prep_a(proj, gq, gk, gm, neg_shift):
    assert TS_PREP == TK_A
    b, s, _ = proj.shape
    gq2 = jnp.concatenate([gq, gq]).reshape(HEAD_DIM, 1)
    gk2 = jnp.concatenate([gk, gk]).reshape(1, HEAD_DIM)
    wblk = MIX_WIDTH // MEM_WIDTH
    return pl.pallas_call(
        _prep_a_kernel,
        grid=(b, s // TS_PREP),
        in_specs=[
            _smem(),
            pl.BlockSpec((1, TS_PREP, MIX_WIDTH), lambda i, j: (i, j, 0)),
            pl.BlockSpec((1, TS_PREP, MIX_WIDTH), lambda i, j: (i, j, 1)),
            pl.BlockSpec((1, TS_PREP, MIX_WIDTH), lambda i, j: (i, j, 2)),
            pl.BlockSpec((1, TS_PREP, MEM_WIDTH), lambda i, j: (i, j, 3 * wblk)),
            pl.BlockSpec((HEAD_DIM, 1), lambda i, j: (0, 0)),
            pl.BlockSpec((1, HEAD_DIM), lambda i, j: (0, 0)),
            pl.BlockSpec((1, HEAD_DIM), lambda i, j: (0, 0)),
        ],
        out_specs=[
            pl.BlockSpec((1, 2 * N_HEADS, HEAD_DIM, TS_PREP), lambda i, j: (i, 0, 0, j)),
            pl.BlockSpec((1, 2 * N_HEADS, TS_PREP, HEAD_DIM), lambda i, j: (i, 0, j, 0)),
            pl.BlockSpec((1, N_HEADS, 1, V_ROWS, TK_A), lambda i, j: (i, 0, j, 0, 0)),
            pl.BlockSpec((1, TS_PREP, MEM_WIDTH), lambda i, j: (i, j, 0)),
        ],
        out_shape=[
            jax.ShapeDtypeStruct((b, 2 * N_HEADS, HEAD_DIM, s), BF16),
            jax.ShapeDtypeStruct((b, 2 * N_HEADS, s, HEAD_DIM), BF16),
            jax.ShapeDtypeStruct((b, N_HEADS, s // TK_A, V_ROWS, TK_A), BF16),
            jax.ShapeDtypeStruct((b, s, MEM_WIDTH), BF16),
        ],
        compiler_params=_cparams(("parallel", "parallel")),
        name="prep_a",
    )(neg_shift.reshape(1), proj, proj, proj, proj, gq2, gk2, gm.reshape(1, HEAD_DIM))


def _diff_attn_kernel(qlo_ref, qhi_ref, klo_ref, khi_ref, qrun_ref, krun_ref,
                      far_ref, lq1_ref, lk1_ref, lq2_ref, lk2_ref,
                      qx_ref, kx_ref, vtx_ref, posq_ref, posk_ref, tbl_ref, diag_ref, gsub_ref,
                      *rest, lam_init, bounded, n_cast):
    w_f32_refs, o_ref, w_bf16_refs = rest[:n_cast], rest[n_cast], rest[n_cast + 1:2 * n_cast + 1]
    acc_ref, m_scratch = rest[2 * n_cast + 1], rest[2 * n_cast + 2:]
    b, iq, j = pl.program_id(0), pl.program_id(1), pl.program_id(2)
    pos_far = klo_ref[b, j] - qhi_ref[b, iq] >= FAR_DIST
    neg_far = khi_ref[b, j] - qlo_ref[b, iq] <= -FAR_DIST
    far = jnp.logical_or(pos_far, neg_far)

    @pl.when(j == 0)
    def _():
        acc_ref[...] = jnp.zeros(acc_ref.shape, F32)
        if not bounded:
            m_scratch[0][...] = jnp.full(m_scratch[0].shape, NEG_BIG, F32)

    def logits(hc):
        return jnp.dot(kx_ref[0, hc], qx_ref[0, hc], preferred_element_type=F32)

    def accumulate(hc, s, far_const, bias):
        h = hc // 2
        if bounded:
            e = jnp.exp2(s)
            if bias is not None:
                e = e * bias
            pv = jnp.dot(vtx_ref[0, h, 0], e.astype(BF16), preferred_element_type=F32)
            acc_ref[hc] += pv if far_const is None else far_const * pv
        else:
            if bias is not None:
                s = s + bias
            m_ref = m_scratch[0]
            off = 0.0 if far_const is None else far_const
            m_old = m_ref[hc]
            m_new = jnp.maximum(m_old, jnp.max(s, axis=0, keepdims=True) + off)
            p = jnp.exp2(s - (m_new - off)).astype(BF16)
            acc_ref[hc] = (jnp.exp2(m_old - m_new) * acc_ref[hc]
                           + jnp.dot(vtx_ref[0, h, 0], p, preferred_element_type=F32))
            m_ref[hc] = m_new

    def all_heads(bias_of_head, far_const_of_head):
        for src, dst in zip(w_f32_refs, w_bf16_refs):
            dst[...] = src[...].astype(BF16)
        s_next = logits(0)
        bias = None
        bias_next = None if bias_of_head is None else bias_of_head(0)
        for hc in range(2 * N_HEADS):
            h, c = divmod(hc, 2)
            s = s_next
            if hc + 1 < 2 * N_HEADS:
                s_next = logits(hc + 1)
            if bias_of_head is not None and c == 0:
                bias = bias_next
                if h + 1 < N_HEADS:
                    bias_next = bias_of_head(h + 1)
            accumulate(hc, s, None if far_const_of_head is None else far_const_of_head(h), bias)

    @pl.when(far)
    def _():
        all_heads(None, lambda h: jnp.where(pos_far, far_ref[1, h], far_ref[0, h]))

    runs = jnp.logical_and(qrun_ref[b, iq] == 1, krun_ref[b, j] == 1)

    @pl.when(jnp.logical_and(jnp.logical_not(far), runs))
    def _():
        d0 = klo_ref[b, j] - qlo_ref[b, iq]

        def bias_of_head(h):
            offs = [jnp.clip(d0 - c + DIAG_CENTER, DIAG_OFF_MIN, DIAG_OFF_MAX) for c in range(0, TQ_A, LANES)]
            return jnp.concatenate([diag_ref[h, pl.ds(off, TK_A), :] for off in offs], axis=1)

        all_heads(bias_of_head, None)

    @pl.when(jnp.logical_and(jnp.logical_not(far), jnp.logical_not(runs)))
    def _():
        bucket = []

        def bias_of_head(h):
            if not bucket:
                bucket.append(_t5_bucket(posk_ref[0] - posq_ref[0]))
            return _table_lookup(tbl_ref[h], bucket[0])

        all_heads(bias_of_head, None)

    @pl.when(j == pl.num_programs(2) - 1)
    def _():
        lam = (jnp.exp(jnp.sum(lq1_ref[...] * lk1_ref[...], axis=-1, keepdims=True))
               - jnp.exp(jnp.sum(lq2_ref[...] * lk2_ref[...], axis=-1, keepdims=True)) + lam_init)
        for h in range(N_HEADS):
            a0, a1 = acc_ref[2 * h], acc_ref[2 * h + 1]
            o_t = (a0[:HEAD_DIM] / a0[HEAD_DIM:HEAD_DIM + 1]
                   - lam * (a1[:HEAD_DIM] / a1[HEAD_DIM:HEAD_DIM + 1]))
            r = lax.rsqrt(jnp.mean(o_t * o_t, axis=0, keepdims=True) + RMS_EPS)
            o_t = ((o_t * r) * gsub_ref[...]) * (1.0 - lam_init)
            o_ref[0, :, h * HEAD_DIM:(h + 1) * HEAD_DIM] = o_t.T.astype(BF16)


def _diff_attention(qx, kx, vtx, positions, tbl, diag2, far_consts, lam_vecs, gsub, lam_init, bounded, cast_srcs):
    assert TK_A <= DIAG_ROWS_MAX
    b, _, s, _ = kx.shape
    nq, nk = s // TQ_A, s // TK_A
    pq = positions.reshape(b, nq, TQ_A)
    pk = positions.reshape(b, nk, TK_A)
    ranges = (pq.min(-1), pq.max(-1), pk.min(-1), pk.max(-1),
              _consecutive_runs(positions, [(t * TQ_A, (t + 1) * TQ_A) for t in range(nq)]),
              _consecutive_runs(positions, [(t * TK_A, (t + 1) * TK_A) for t in range(nk)]))
    lam_spec = pl.BlockSpec((1, DIFF_QK_DIM), lambda i, q, k, *_: (0, 0))
    scratch = [pltpu.VMEM((2 * N_HEADS, V_ROWS, TQ_A), F32)]
    if not bounded:
        scratch.append(pltpu.VMEM((2 * N_HEADS, 1, TQ_A), F32))
    n_steps = b * nq * nk
    assert all(w.shape[0] % (16 * n_steps) == 0 for w in cast_srcs)
    cast_specs = [pl.BlockSpec((w.shape[0] // n_steps, w.shape[1]), lambda i, q, k, *_: ((i * nq + q) * nk + k, 0))
                  for w in cast_srcs]
    grid_spec = pltpu.PrefetchScalarGridSpec(
        num_scalar_prefetch=len(ranges),
        grid=(b, nq, nk),
        in_specs=[
            _smem(), lam_spec, lam_spec, lam_spec, lam_spec,
            pl.BlockSpec((1, 2 * N_HEADS, HEAD_DIM, TQ_A), lambda i, q, k, *_: (i, 0, 0, q)),
            pl.BlockSpec((1, 2 * N_HEADS, TK_A, HEAD_DIM), lambda i, q, k, *_: (i, 0, k, 0)),
            pl.BlockSpec((1, N_HEADS, 1, V_ROWS, TK_A), lambda i, q, k, *_: (i, 0, k, 0, 0)),
            pl.BlockSpec((1, 1, TQ_A), lambda i, q, k, *_: (i, 0, q)),
            pl.BlockSpec((1, TK_A, 1), lambda i, q, k, *_: (i, k, 0)),
            pl.BlockSpec((N_HEADS, 1, LANES), lambda i, q, k, *_: (0, 0, 0)),
            pl.BlockSpec((None, N_HEADS, DIAG_ROWS, LANES), lambda i, q, k, *_: (int(bounded), 0, 0, 0),
                         pipeline_mode=pl.Buffered(1)),
            pl.BlockSpec((HEAD_DIM, 1), lambda i, q, k, *_: (0, 0)),
            *cast_specs,
        ],
        out_specs=[pl.BlockSpec((1, TQ_A, MIX_WIDTH), lambda i, q, k, *_: (i, q, 0)), *cast_specs],
        scratch_shapes=scratch,
    )
    outs = pl.pallas_call(
        functools.partial(_diff_attn_kernel, lam_init=lam_init, bounded=bounded, n_cast=len(cast_srcs)),
        grid_spec=grid_spec,
        out_shape=[jax.ShapeDtypeStruct((b, s, MIX_WIDTH), BF16),
                   *[jax.ShapeDtypeStruct(w.shape, BF16) for w in cast_srcs]],
        compiler_params=_cparams(("arbitrary", "arbitrary", "arbitrary")),
        name="diff_attn_bounded" if bounded else "diff_attn_running_max",
    )(*ranges, far_consts, *[v.reshape(1, DIFF_QK_DIM) for v in lam_vecs],
      qx, kx, vtx, positions.reshape(b, 1, s), positions.reshape(b, s, 1), tbl, diag2,
      gsub.reshape(HEAD_DIM, 1), *cast_srcs)
    return outs[0], tuple(outs[1:])


def _prep_b_kernel(q_ref, k_ref, v_ref, qm_ref, gq_ref, gk_ref, gm_ref, qt_ref, kn_ref, vt_ref, qmn_ref):
    scale = HEAD_DIM ** -0.5 * LOG2E
    n_blk = TS_PREP // WINDOW
    ones_rows = jnp.where(lax.broadcasted_iota(I32, (V_ROWS - HEAD_DIM, TS_PREP), 0) == 0, 1.0, 0.0).astype(BF16)
    for h in range(N_HEADS):
        g, hg = divmod(h, GQA_GROUP)
        qt = q_ref[0, :, h * HEAD_DIM:(h + 1) * HEAD_DIM].astype(F32).T
        r = lax.rsqrt(jnp.mean(qt * qt, axis=0, keepdims=True) + RMS_EPS)
        qt = ((qt * r) * (gq_ref[...] * scale)).astype(BF16)
        for n in range(n_blk):
            qt_ref[0, n, g, :, hg * WINDOW:(hg + 1) * WINDOW] = qt[:, n * WINDOW:(n + 1) * WINDOW]
    for g in range(N_KV_HEADS):
        sl = slice(g * HEAD_DIM, (g + 1) * HEAD_DIM)
        k = k_ref[0, :, sl].astype(F32)
        kn_ref[0, g] = ((k * _group_rms_scale(k, HEAD_DIM)) * gk_ref[...]).astype(BF16)
        vt = jnp.concatenate([v_ref[0, :, sl].astype(F32).T.astype(BF16), ones_rows], axis=0)
        for n in range(n_blk):
            vt_ref[0, g, n] = vt[:, n * WINDOW:(n + 1) * WINDOW]
        qm = qm_ref[0, :, sl].astype(F32)
        qmn_ref[0, :, sl] = ((qm * _group_rms_scale(qm, HEAD_DIM)) * (gm_ref[...] * scale)).astype(BF16)


def _prep_b(proj, gq, gk, gm):
    b, s, _ = proj.shape
    kblk = MIX_WIDTH // KV_WIDTH
    n_blk = TS_PREP // WINDOW
    g = lambda v: v.reshape(1, HEAD_DIM)
    return pl.pallas_call(
        _prep_b_kernel,
        grid=(b, s // TS_PREP),
        in_specs=[
            pl.BlockSpec((1, TS_PREP, MIX_WIDTH), lambda i, j: (i, j, 0)),
            pl.BlockSpec((1, TS_PREP, KV_WIDTH), lambda i, j: (i, j, kblk)),
            pl.BlockSpec((1, TS_PREP, KV_WIDTH), lambda i, j: (i, j, kblk + 1)),
            pl.BlockSpec((1, TS_PREP, MEM_WIDTH), lambda i, j: (i, j, kblk + 2)),
            pl.BlockSpec((HEAD_DIM, 1), lambda i, j: (0, 0)),
            pl.BlockSpec((1, HEAD_DIM), lambda i, j: (0, 0)),
            pl.BlockSpec((1, HEAD_DIM), lambda i, j: (0, 0)),
        ],
        out_specs=[
            pl.BlockSpec((1, n_blk, N_KV_HEADS, HEAD_DIM, GQA_GROUP * WINDOW), lambda i, j: (i, j, 0, 0, 0)),
            pl.BlockSpec((1, N_KV_HEADS, TS_PREP, HEAD_DIM), lambda i, j: (i, 0, j, 0)),
            pl.BlockSpec((1, N_KV_HEADS, n_blk, V_ROWS, WINDOW), lambda i, j: (i, 0, j, 0, 0)),
            pl.BlockSpec((1, TS_PREP, MEM_WIDTH), lambda i, j: (i, j, 0)),
        ],
        out_shape=[
            jax.ShapeDtypeStruct((b, s // WINDOW, N_KV_HEADS, HEAD_DIM, GQA_GROUP * WINDOW), BF16),
            jax.ShapeDtypeStruct((b, N_KV_HEADS, s, HEAD_DIM), BF16),
            jax.ShapeDtypeStruct((b, N_KV_HEADS, s // WINDOW, V_ROWS, WINDOW), BF16),
            jax.ShapeDtypeStruct((b, s, MEM_WIDTH), BF16),
        ],
        compiler_params=_cparams(("parallel", "parallel")),
        name="prep_b",
    )(proj, proj, proj, proj, gq.reshape(HEAD_DIM, 1), g(gk), g(gm))


def _win_attn_kernel(small_ref, sink_ref, run_ref, qt_ref, k_ref, vt_ref, posq_ref, posk_ref, tbl_ref, diag_ref,
                     o_ref):
    n_blocks = posq_ref.shape[1]
    n_sub = o_ref.shape[1] // WINDOW
    kw = 3 * WINDOW
    lane3 = lax.broadcasted_iota(I32, (1, GQA_GROUP * WINDOW), 1)

    def window(nl):
        n = pl.program_id(1) * n_sub + nl
        nb = jnp.clip(n - 1, 0, n_blocks - 3)
        return n, nb, pl.multiple_of(nb * WINDOW, WINDOW)

    def logits(nl, g):
        start = window(nl)[2]
        return jnp.dot(k_ref[0, g, pl.ds(start, kw), :], qt_ref[0, nl, g], preferred_element_type=F32)

    def in_window(nl):
        n, _, start = window(nl)
        ki = start + lax.broadcasted_iota(I32, (kw, WINDOW), 0)
        qi = n * WINDOW + lax.broadcasted_iota(I32, (kw, WINDOW), 1)
        return jnp.abs(ki - qi) <= WINDOW

    def masked_bucket(nl):
        n, _, start = window(nl)
        rel = posk_ref[0, pl.ds(start, kw), :] - posq_ref[0, pl.ds(n, 1), :]
        return jnp.where(in_window(nl), _t5_bucket(rel), N_BUCKETS)

    def bias_from_bucket(bucket, nl, h, bounded):
        return _table_lookup(tbl_ref[int(bounded), h], bucket)

    def bias_from_diag(valid, nl, h, bounded):
        n, _, start = window(nl)
        off = pl.multiple_of(start - n * WINDOW + DIAG_CENTER, 16)
        return jnp.where(valid, diag_ref[int(bounded), h, pl.ds(off, kw), :], 0.0 if bounded else NEG_BIG)

    def finish(nl, g, s, block_state, head_bias, bounded):
        nb = window(nl)[1]
        heads = range(g * GQA_GROUP, (g + 1) * GQA_GROUP)
        bias = jnp.concatenate([head_bias(block_state, nl, h, bounded) for h in heads], axis=1)
        sinks = [sink_ref[int(bounded), h] for h in heads]
        sink = jnp.where(lane3 < WINDOW, sinks[0], jnp.where(lane3 < 2 * WINDOW, sinks[1], sinks[2]))
        if bounded:
            p = (jnp.exp2(s) * bias).astype(BF16)
            sink_term = sink
        else:
            z = s + bias
            m = jnp.maximum(jnp.max(z, axis=0, keepdims=True), sink)
            p = jnp.exp2(z - m).astype(BF16)
            sink_term = jnp.exp2(sink - m)
        vt = jnp.concatenate([vt_ref[0, g, nb + t] for t in range(3)], axis=1)
        acc = jnp.dot(vt, p, preferred_element_type=F32)
        o_t = acc[:HEAD_DIM] / (acc[HEAD_DIM:HEAD_DIM + 1] + sink_term)
        for hg, h in enumerate(heads):
            o_ref[0, nl * WINDOW:(nl + 1) * WINDOW, h * HEAD_DIM:(h + 1) * HEAD_DIM] = (
                o_t[:, hg * WINDOW:(hg + 1) * WINDOW].T.astype(BF16))

    def all_chains(block_state_of, head_bias, bounded):
        chains = [(nl, g) for nl in range(n_sub) for g in range(N_KV_HEADS)]
        s_next = logits(*chains[0])
        block_state = None
        for idx, (nl, g) in enumerate(chains):
            s = s_next
            if idx + 1 < len(chains):
                s_next = logits(*chains[idx + 1])
            if g == 0:
                block_state = block_state_of(nl)
            finish(nl, g, s, block_state, head_bias, bounded)

    is_run = run_ref[pl.program_id(0), pl.program_id(1)] == 1
    is_small = small_ref[0] == 1
    for bounded in (True, False):
        mode = is_small if bounded else jnp.logical_not(is_small)
        pl.when(jnp.logical_and(mode, is_run))(
            functools.partial(all_chains, in_window, bias_from_diag, bounded))
        pl.when(jnp.logical_and(mode, jnp.logical_not(is_run)))(
            functools.partial(all_chains, masked_bucket, bias_from_bucket, bounded))


def _win_attention(qt, kn, vt, positions, tbl2, diag2, sink2, small):
    b, _, s, _ = kn.shape
    n_blocks = s // WINDOW
    n_sub = TQ_B // WINDOW
    runs = _consecutive_runs(positions, [(max(0, t * TQ_B - WINDOW), min(s, (t + 1) * TQ_B + WINDOW))
                                         for t in range(s // TQ_B)])
    return pl.pallas_call(
        _win_attn_kernel,
        grid=(b, s // TQ_B),
        in_specs=[
            _smem(), _smem(), _smem(),
            pl.BlockSpec((1, n_sub, N_KV_HEADS, HEAD_DIM, GQA_GROUP * WINDOW), lambda i, j: (i, j, 0, 0, 0)),
            pl.BlockSpec((1, N_KV_HEADS, s, HEAD_DIM), lambda i, j: (i, 0, 0, 0)),
            pl.BlockSpec((1, N_KV_HEADS, n_blocks, V_ROWS, WINDOW), lambda i, j: (i, 0, 0, 0, 0)),
            pl.BlockSpec((1, n_blocks, WINDOW), lambda i, j: (i, 0, 0)),
            pl.BlockSpec((1, s, 1), lambda i, j: (i, 0, 0)),
            pl.BlockSpec((2, N_HEADS, 1, LANES), lambda i, j: (0, 0, 0, 0)),
            pl.BlockSpec((2, N_HEADS, DIAG_ROWS, LANES), lambda i, j: (0, 0, 0, 0), pipeline_mode=pl.Buffered(1)),
        ],
        out_specs=pl.BlockSpec((1, TQ_B, MIX_WIDTH), lambda i, j: (i, j, 0)),
        out_shape=jax.ShapeDtypeStruct((b, s, MIX_WIDTH), BF16),
        compiler_params=_cparams(("parallel", "parallel")),
        name="win_attn",
    )(small.astype(I32).reshape(1), sink2, runs, qt, kn, vt, positions.reshape(b, n_blocks, WINDOW),
      positions.reshape(b, s, 1), tbl2, diag2)


def _mem_attn_kernel(q_ref, k_ref, v_ref, o_ref):
    for h in range(N_MEM_HEADS):
        sl = slice(h * HEAD_DIM, (h + 1) * HEAD_DIM)
        s = lax.dot_general(q_ref[0, :, sl], k_ref[0, :, sl], (((1,), (1,)), ((), ())), preferred_element_type=F32)
        e = jnp.exp2(s - jnp.max(s, axis=-1, keepdims=True))
        p = (e / jnp.sum(e, axis=-1, keepdims=True)).astype(BF16)
        o_ref[0, :, sl] = jnp.dot(p, v_ref[0, :, sl], preferred_element_type=F32).astype(BF16)


def _mem_attention(qmn, k_m, v_m):
    b, s, _ = qmn.shape
    return pl.pallas_call(
        _mem_attn_kernel,
        grid=(b, s // TQ_MEM),
        in_specs=[
            pl.BlockSpec((1, TQ_MEM, MEM_WIDTH), lambda i, j: (i, j, 0)),
            pl.BlockSpec((1, MEM_LEN, MEM_WIDTH), lambda i, j: (i, 0, 0)),
            pl.BlockSpec((1, MEM_LEN, MEM_WIDTH), lambda i, j: (i, 0, 0)),
        ],
        out_specs=pl.BlockSpec((1, TQ_MEM, MEM_WIDTH), lambda i, j: (i, j, 0)),
        out_shape=jax.ShapeDtypeStruct((b, s, MEM_WIDTH), BF16),
        compiler_params=_cparams(("parallel", "parallel")),
        name="mem_attn",
    )(qmn, k_m, v_m)


def _out_proj_kernel(x_ref, o_ref, om_ref, wo_ref, wm_ref, y_ref):
    y_ref[...] = (x_ref[...]
                  + jnp.dot(o_ref[...], wo_ref[...].astype(BF16), preferred_element_type=F32)
                  + jnp.dot(om_ref[...], wm_ref[...].astype(BF16), preferred_element_type=F32))


def _out_proj(x2d, o2d, om2d, w_stack, layer):
    m, d = x2d.shape
    return pl.pallas_call(
        _out_proj_kernel,
        grid=(m // TM_OUT, d // TN_OUT),
        in_specs=[
            pl.BlockSpec((TM_OUT, TN_OUT), lambda i, j: (i, j)),
            pl.BlockSpec((TM_OUT, MIX_WIDTH), lambda i, j: (i, 0)),
            pl.BlockSpec((TM_OUT, MEM_WIDTH), lambda i, j: (i, 0)),
            pl.BlockSpec((None, MIX_WIDTH, TN_OUT), lambda i, j: (layer, 0, j)),
            pl.BlockSpec((None, MEM_WIDTH, TN_OUT), lambda i, j: (layer, MIX_WIDTH // MEM_WIDTH, j)),
        ],
        out_specs=pl.BlockSpec((TM_OUT, TN_OUT), lambda i, j: (i, j)),
        out_shape=jax.ShapeDtypeStruct((m, d), F32),
        compiler_params=_cparams(("parallel", "parallel")),
        name="out_proj",
    )(x2d, o2d, om2d, w_stack, w_stack)


def _mlp_kernel(x_ref, g_ref, wu_ref, wd_ref, y_ref, h_ref):
    f = pl.program_id(1)

    @pl.when(f == 0)
    def _():
        x = x_ref[...]
        r = lax.rsqrt(jnp.mean(x * x, axis=-1, keepdims=True) + RMS_EPS)
        h_ref[...] = ((x * r) * g_ref[...]).astype(BF16)
        y_ref[...] = x

    u = jnp.maximum(jnp.dot(h_ref[...], wu_ref[...].astype(BF16), preferred_element_type=F32), 0.0)
    y_ref[...] += jnp.dot((u * u).astype(BF16), wd_ref[...].astype(BF16), preferred_element_type=F32)


def _mlp(x2d, gain, wu_stack, wd_stack, layer):
    m, d = x2d.shape
    return pl.pallas_call(
        _mlp_kernel,
        grid=(m // TM_MLP, D_FF // TF_MLP),
        in_specs=[
            pl.BlockSpec((TM_MLP, d), lambda i, f: (i, 0)),
            pl.BlockSpec((1, d), lambda i, f: (0, 0)),
            pl.BlockSpec((None, d, TF_MLP), lambda i, f: (layer, 0, f)),
            pl.BlockSpec((None, TF_MLP, d), lambda i, f: (layer, f, 0)),
        ],
        out_specs=pl.BlockSpec((TM_MLP, d), lambda i, f: (i, 0)),
        out_shape=jax.ShapeDtypeStruct((m, d), F32),
        scratch_shapes=[pltpu.VMEM((TM_MLP, d), BF16)],
        compiler_params=_cparams(("parallel", "arbitrary")),
        name="mlp",
    )(x2d, gain.reshape(1, d), wu_stack, wd_stack)


def _diag_table_kernel(tbl_ref, o_ref):
    r = lax.broadcasted_iota(I32, (DIAG_ROWS, LANES), 0)
    l = lax.broadcasted_iota(I32, (DIAG_ROWS, LANES), 1)
    bias = _table_lookup(tbl_ref[0], _t5_bucket(r - l - DIAG_CENTER))
    o_ref[0, 0] = bias
    o_ref[1, 0] = jnp.exp2(bias)


def _diag_bias_table(tbl_log2):
    return pl.pallas_call(
        _diag_table_kernel,
        grid=(N_HEADS,),
        in_specs=[pl.BlockSpec((1, 1, LANES), lambda h: (h, 0, 0))],
        out_specs=pl.BlockSpec((2, 1, DIAG_ROWS, LANES), lambda h: (0, h, 0, 0)),
        out_shape=jax.ShapeDtypeStruct((2, N_HEADS, DIAG_ROWS, LANES), F32),
        compiler_params=_cparams(("parallel",)),
        name="diag_bias_table",
    )(tbl_log2)


def _consecutive_runs(positions, ranges):
    b = positions.shape[0]
    step_ok = (positions[:, 1:] - positions[:, :-1] == 1).astype(I32)
    c = jnp.concatenate([jnp.zeros((b, 1), I32), jnp.cumsum(step_ok, axis=1)], axis=1)
    return jnp.stack([(c[:, hi - 1] - c[:, lo] == hi - 1 - lo).astype(I32) for lo, hi in ranges], axis=1)


def _bias_tables(rel_bias):
    t = (rel_bias.astype(F32) * LOG2E).T
    tbl = jnp.zeros((N_HEADS, 1, LANES), F32).at[:, 0, :N_BUCKETS].set(t).at[:, 0, N_BUCKETS].set(NEG_BIG)
    far = jnp.stack([t[:, HALF_BUCKETS - 1], t[:, N_BUCKETS - 1]])
    return tbl, far


def _logit_bound(gq, gk, rel_bias, dim):
    qk = 1.01 * dim * jnp.max(jnp.abs(gq)) * jnp.max(jnp.abs(gk)) * (dim ** -0.5 * LOG2E)
    return jnp.ceil(qk + jnp.max(jnp.abs(rel_bias)) * LOG2E).astype(F32)


def kernel(x, mem, positions, rel_bias, norm_attn, norm_mem, norm_mlp, w_in_a, a_q_norm, a_k_norm, a_lambda_q1, a_lambda_k1, a_lambda_q2, a_lambda_k2, a_subln, w_in_b, b_q_norm, b_k_norm, b_sink, w_mem_kv, m_q_norm, m_k_norm, w_out, w_up, w_down):
    b, s, d = x.shape
    depth = norm_attn.shape[0]
    tbl_log2, far_log2 = _bias_tables(rel_bias)
    tbl_mult = jnp.exp2(tbl_log2)
    diag2 = _diag_bias_table(tbl_log2)
    x2d = x.reshape(b * s, d)
    later_w = (w_out, w_up, w_down, w_in_b)
    for i in range(depth):
        j = i // 2
        k_m, v_m = _mem_kv(mem, norm_mem[i], w_mem_kv, i, m_k_norm[i])
        if i % 2 == 0:
            proj = _norm_proj(x2d, norm_attn[i], w_in_a, j).reshape(b, s, -1)
            shift = _logit_bound(a_q_norm[j], a_k_norm[j], rel_bias, DIFF_QK_DIM)
            bounded = shift <= SHIFT_LIMIT
            qx, kx, vtx, qmn = _prep_a(proj, a_q_norm[j], a_k_norm[j], m_q_norm[i], jnp.where(bounded, -shift, 0.0))
            lam_init = 0.8 - 0.6 * math.exp(-0.3 * i)
            cast_srcs = tuple(w.reshape(-1, w.shape[-1]) for w in later_w) if i == 0 else ()
            attn = functools.partial(
                _diff_attention, qx, kx, vtx, positions,
                lam_vecs=(a_lambda_q1[j], a_lambda_k1[j], a_lambda_q2[j], a_lambda_k2[j]), gsub=a_subln[j],
                lam_init=lam_init, cast_srcs=cast_srcs)
            o, casts = lax.cond(
                bounded,
                lambda: attn(tbl_mult, diag2, far_consts=jnp.exp2(far_log2), bounded=True),
                lambda: attn(tbl_log2, diag2, far_consts=far_log2, bounded=False))
            if i == 0:
                later_w = tuple(c.reshape(w.shape) for c, w in zip(casts, later_w))
                w_out, w_up, w_down, w_in_b = later_w
        else:
            proj = _norm_proj(x2d, norm_attn[i], w_in_b, j).reshape(b, s, -1)
            qt, kn, vt, qmn = _prep_b(proj, b_q_norm[j], b_k_norm[j], m_q_norm[i])
            sink_log2 = b_sink[j].astype(F32) * LOG2E
            small = jnp.logical_and(_logit_bound(b_q_norm[j], b_k_norm[j], rel_bias, HEAD_DIM) <= SHIFT_LIMIT,
                                    jnp.max(jnp.abs(sink_log2)) <= SHIFT_LIMIT)
            o = _win_attention(qt, kn, vt, positions, jnp.stack([tbl_log2, tbl_mult]), diag2,
                               jnp.stack([sink_log2, jnp.exp2(sink_log2)]), small)
        o_m = _mem_attention(qmn, k_m, v_m)
        x2d = _out_proj(x2d, o.reshape(b * s, MIX_WIDTH), o_m.reshape(b * s, MEM_WIDTH), w_out, i)
        x2d = _mlp(x2d, norm_mlp[i], w_up, w_down, i)
    return x2d.reshape(b, s, d)
```

```python
import functools
import math

import jax
import jax.numpy as jnp
from jax import lax
from jax.experimental import pallas as pl
from jax.experimental.pallas import tpu as pltpu

F32 = jnp.float32
BF16 = jnp.bfloat16
I32 = jnp.int32

D_MODEL = 2048
N_HEADS = 12
HEAD_DIM = 128
DIFF_QK_DIM = 64
N_KV_HEADS = 4
GQA_GROUP = 3
MIX_WIDTH = N_HEADS * HEAD_DIM
KV_WIDTH = N_KV_HEADS * HEAD_DIM
WINDOW = 128
N_MEM_HEADS = 4
MEM_WIDTH = N_MEM_HEADS * HEAD_DIM
MEM_LEN = 256
D_FF = 4 * D_MODEL
N_BUCKETS = 32
MAX_DISTANCE = 128
RMS_EPS = 1e-6
NEG_BIG = -1e30
LOG2E = math.log2(math.e)

HALF_BUCKETS = N_BUCKETS // 2
MAX_EXACT = HALF_BUCKETS // 2
FAR_DIST = 91

ONE_COL = DIFF_QK_DIM
V_ROWS = HEAD_DIM + 16
SHIFT_LIMIT = 50.0

LANES = 128

DIAG_ROWS_MAX = 512
DIAG_OFF_MIN = 5
DIAG_CENTER = DIAG_OFF_MIN + DIAG_ROWS_MAX + FAR_DIST
DIAG_OFF_MAX = DIAG_CENTER + FAR_DIST + LANES - 1
DIAG_ROWS = -(-(DIAG_OFF_MAX + DIAG_ROWS_MAX) // 16) * 16

VMEM_LIMIT = 56 * 1024 * 1024

TM_PROJ, TN_PROJ = 1024, 1024
TS_PREP = 512
TQ_A, TK_A = 512, 512
TQ_B = 512
TQ_MEM = 2048
TM_OUT, TN_OUT = 512, 2048
TM_MLP, TF_MLP = 1024, 512


def _cparams(sem):
    return pltpu.CompilerParams(dimension_semantics=sem, vmem_limit_bytes=VMEM_LIMIT)


def _smem():
    return pl.BlockSpec(memory_space=pltpu.SMEM)


def _t5_bucket(rel):
    side = jnp.where(rel > 0, HALF_BUCKETS, 0)
    n = jnp.abs(rel)
    n_f = jnp.maximum(n, 1).astype(F32)
    large = MAX_EXACT + (jnp.log(n_f / MAX_EXACT) / math.log(MAX_DISTANCE / MAX_EXACT)
                         * (HALF_BUCKETS - MAX_EXACT)).astype(I32)
    large = jnp.minimum(large, HALF_BUCKETS - 1)
    return side + jnp.where(n < MAX_EXACT, n, large)


def _table_lookup(tbl_row, bucket):
    rows, cols = bucket.shape
    tb = jnp.broadcast_to(tbl_row, (rows, LANES))
    parts = [jnp.take_along_axis(tb, bucket[:, c:c + LANES], axis=1) for c in range(0, cols, LANES)]
    return parts[0] if len(parts) == 1 else jnp.concatenate(parts, axis=1)


def _group_rms_scale(x, group):
    t = x * x
    if group == LANES:
        return lax.rsqrt(jnp.mean(t, axis=-1, keepdims=True) + RMS_EPS)
    lane = lax.broadcasted_iota(I32, x.shape, 1)
    lo = lane < group
    s_lo = jnp.sum(jnp.where(lo, t, 0.0), axis=-1, keepdims=True)
    s_hi = jnp.sum(jnp.where(lo, 0.0, t), axis=-1, keepdims=True)
    return jnp.where(lo, lax.rsqrt(s_lo / group + RMS_EPS), lax.rsqrt(s_hi / group + RMS_EPS))


def _proj_kernel(x_ref, g_ref, w_ref, o_ref, h_ref):
    @pl.when(pl.program_id(1) == 0)
    def _():
        x = x_ref[...]
        r = lax.rsqrt(jnp.mean(x * x, axis=-1, keepdims=True) + RMS_EPS)
        h_ref[...] = ((x * r) * g_ref[...]).astype(BF16)

    o_ref[...] = jnp.dot(h_ref[...], w_ref[...].astype(BF16), preferred_element_type=F32).astype(BF16)


def _norm_proj(x2d, gain, w_stack, layer):
    m, d = x2d.shape
    n = w_stack.shape[2]
    return pl.pallas_call(
        _proj_kernel,
        grid=(m // TM_PROJ, n // TN_PROJ),
        in_specs=[
            pl.BlockSpec((TM_PROJ, d), lambda i, j: (i, 0)),
            pl.BlockSpec((1, d), lambda i, j: (0, 0)),
            pl.BlockSpec((None, d, TN_PROJ), lambda i, j: (layer, 0, j)),
        ],
        out_specs=pl.BlockSpec((TM_PROJ, TN_PROJ), lambda i, j: (i, j)),
        out_shape=jax.ShapeDtypeStruct((m, n), BF16),
        scratch_shapes=[pltpu.VMEM((TM_PROJ, d), BF16)],
        compiler_params=_cparams(("parallel", "arbitrary")),
        name="norm_proj",
    )(x2d, gain.reshape(1, d), w_stack)


def _mem_kv_kernel(mem_ref, g_ref, w_ref, gk_ref, k_ref, v_ref):
    x = mem_ref[0]
    r = lax.rsqrt(jnp.mean(x * x, axis=-1, keepdims=True) + RMS_EPS)
    mn = ((x * r) * g_ref[...]).astype(BF16)
    mkv = jnp.dot(mn, w_ref[...].astype(BF16), preferred_element_type=F32)
    for h in range(N_MEM_HEADS):
        kh = mkv[:, h * HEAD_DIM:(h + 1) * HEAD_DIM]
        k_ref[0, :, h * HEAD_DIM:(h + 1) * HEAD_DIM] = ((kh * _group_rms_scale(kh, HEAD_DIM)) * gk_ref[...]).astype(BF16)
    v_ref[0] = mkv[:, MEM_WIDTH:].astype(BF16)


def _mem_kv(mem, gain, w_stack, layer, gk):
    b = mem.shape[0]
    shp = jax.ShapeDtypeStruct((b, MEM_LEN, MEM_WIDTH), BF16)
    return pl.pallas_call(
        _mem_kv_kernel,
        grid=(b,),
        in_specs=[
            pl.BlockSpec((1, MEM_LEN, D_MODEL), lambda i: (i, 0, 0)),
            pl.BlockSpec((1, D_MODEL), lambda i: (0, 0)),
            pl.BlockSpec((None, D_MODEL, 2 * MEM_WIDTH), lambda i: (layer, 0, 0)),
            pl.BlockSpec((1, HEAD_DIM), lambda i: (0, 0)),
        ],
        out_specs=[pl.BlockSpec((1, MEM_LEN, MEM_WIDTH), lambda i: (i, 0, 0))] * 2,
        out_shape=[shp, shp],
        compiler_params=_cparams(("parallel",)),
        name="mem_kv",
    )(mem, gain.reshape(1, D_MODEL), w_stack, gk.reshape(1, HEAD_DIM))


def _prep_a_kernel(negm_ref, q_ref, k_ref, v_ref, qm_ref, gq_ref, gk_ref, gm_ref, qx_ref, kx_ref, vtx_ref, qmn_ref):
    q_scale = DIFF_QK_DIM ** -0.5 * LOG2E
    m_scale = HEAD_DIM ** -0.5 * LOG2E
    lane = lax.broadcasted_iota(I32, (TS_PREP, HEAD_DIM), 1)
    row = lax.broadcasted_iota(I32, (HEAD_DIM, TS_PREP), 0)
    lo_lanes, lo_rows = lane < DIFF_QK_DIM, row < DIFF_QK_DIM
    neg_shift = negm_ref[0]
    ones_rows = jnp.where(lax.broadcasted_iota(I32, (V_ROWS - HEAD_DIM, TS_PREP), 0) == 0, 1.0, 0.0).astype(BF16)
    q_fill = (jnp.where(row == ONE_COL, neg_shift, 0.0), jnp.where(row == ONE_COL - 1, neg_shift, 0.0))
    k_fill = (jnp.where(lane == ONE_COL, 1.0, 0.0), jnp.where(lane == ONE_COL - 1, 1.0, 0.0))
    q_gain = gq_ref[...] * q_scale
    for h in range(N_HEADS):
        sl = slice(h * HEAD_DIM, (h + 1) * HEAD_DIM)
        qt = q_ref[0, :, sl].astype(F32).T
        t = qt * qt
        r_lo = lax.rsqrt(jnp.mean(t[:DIFF_QK_DIM], axis=0, keepdims=True) + RMS_EPS)
        r_hi = lax.rsqrt(jnp.mean(t[DIFF_QK_DIM:], axis=0, keepdims=True) + RMS_EPS)
        qn = (qt * jnp.where(lo_rows, r_lo, r_hi)) * q_gain
        qx_ref[0, 2 * h] = jnp.where(lo_rows, qn, q_fill[0]).astype(BF16)
        qx_ref[0, 2 * h + 1] = jnp.where(lo_rows, q_fill[1], qn).astype(BF16)
        k = k_ref[0, :, sl].astype(F32)
        kn = (k * _group_rms_scale(k, DIFF_QK_DIM)) * gk_ref[...]
        kx_ref[0, 2 * h] = jnp.where(lo_lanes, kn, k_fill[0]).astype(BF16)
        kx_ref[0, 2 * h + 1] = jnp.where(lo_lanes, k_fill[1], kn).astype(BF16)
        vt = v_ref[0, :, sl].astype(F32).T.astype(BF16)
        vtx_ref[0, h, 0] = jnp.concatenate([vt, ones_rows], axis=0)
    for h in range(N_MEM_HEADS):
        sl = slice(h * HEAD_DIM, (h + 1) * HEAD_DIM)
        qm = qm_ref[0, :, sl].astype(F32)
        qmn_ref[0, :, sl] = ((qm * _group_rms_scale(qm, HEAD_DIM)) * (gm_ref[...] * m_scale)).astype(BF16)


def _prep_a(proj, gq, gk, gm, neg_shift):
    assert TS_PREP == TK_A
    b, s, _ = proj.shape
    gq2 = jnp.concatenate([gq, gq]).reshape(HEAD_DIM, 1)
    gk2 = jnp.concatenate([gk, gk]).reshape(1, HEAD_DIM)
    wblk = MIX_WIDTH // MEM_WIDTH
    return pl.pallas_call(
        _prep_a_kernel,
        grid=(b, s // TS_PREP),
        in_specs=[
            _smem(),
            pl.BlockSpec((1, TS_PREP, MIX_WIDTH), lambda i, j: (i, j, 0)),
            pl.BlockSpec((1, TS_PREP, MIX_WIDTH), lambda i, j: (i, j, 1)),
            pl.BlockSpec((1, TS_PREP, MIX_WIDTH), lambda i, j: (i, j, 2)),
            pl.BlockSpec((1, TS_PREP, MEM_WIDTH), lambda i, j: (i, j, 3 * wblk)),
            pl.BlockSpec((HEAD_DIM, 1), lambda i, j: (0, 0)),
            pl.BlockSpec((1, HEAD_DIM), lambda i, j: (0, 0)),
            pl.BlockSpec((1, HEAD_DIM), lambda i, j: (0, 0)),
        ],
        out_specs=[
            pl.BlockSpec((1, 2 * N_HEADS, HEAD_DIM, TS_PREP), lambda i, j: (i, 0, 0, j)),
            pl.BlockSpec((1, 2 * N_HEADS, TS_PREP, HEAD_DIM), lambda i, j: (i, 0, j, 0)),
            pl.BlockSpec((1, N_HEADS, 1, V_ROWS, TK_A), lambda i, j: (i, 0, j, 0, 0)),
            pl.BlockSpec((1, TS_PREP, MEM_WIDTH), lambda i, j: (i, j, 0)),
        ],
        out_shape=[
            jax.ShapeDtypeStruct((b, 2 * N_HEADS, HEAD_DIM, s), BF16),
            jax.ShapeDtypeStruct((b, 2 * N_HEADS, s, HEAD_DIM), BF16),
            jax.ShapeDtypeStruct((b, N_HEADS, s // TK_A, V_ROWS, TK_A), BF16),
            jax.ShapeDtypeStruct((b, s, MEM_WIDTH), BF16),
        ],
        compiler_params=_cparams(("parallel", "parallel")),
        name="prep_a",
    )(neg_shift.reshape(1), proj, proj, proj, proj, gq2, gk2, gm.reshape(1, HEAD_DIM))


def _diff_attn_kernel(qlo_ref, qhi_ref, klo_ref, khi_ref, qrun_ref, krun_ref,
                      mode_ref, far_ref, lq1_ref, lk1_ref, lq2_ref, lk2_ref,
                      qx_ref, kx_ref, vtx_ref, posq_ref, posk_ref, tbl_ref, diag_ref, gsub_ref,
                      *rest, lam_init, n_cast):
    w_f32_refs, o_ref, w_bf16_refs = rest[:n_cast], rest[n_cast], rest[n_cast + 1:2 * n_cast + 1]
    acc_ref, m_ref = rest[2 * n_cast + 1:]
    b, iq, j = pl.program_id(0), pl.program_id(1), pl.program_id(2)
    pos_far = klo_ref[b, j] - qhi_ref[b, iq] >= FAR_DIST
    neg_far = khi_ref[b, j] - qlo_ref[b, iq] <= -FAR_DIST
    far = jnp.logical_or(pos_far, neg_far)

    @pl.when(j == 0)
    def _():
        acc_ref[...] = jnp.zeros(acc_ref.shape, F32)
        m_ref[...] = jnp.full(m_ref.shape, NEG_BIG, F32)

    def logits(hc):
        return jnp.dot(kx_ref[0, hc], qx_ref[0, hc], preferred_element_type=F32)

    def accumulate(hc, s, far_const, bias, bounded):
        h = hc // 2
        if bounded:
            e = jnp.exp2(s)
            if bias is not None:
                e = e * bias
            pv = jnp.dot(vtx_ref[0, h, 0], e.astype(BF16), preferred_element_type=F32)
            acc_ref[hc] += pv if far_const is None else far_const * pv
        else:
            if bias is not None:
                s = s + bias
            off = 0.0 if far_const is None else far_const
            m_old = m_ref[hc]
            m_new = jnp.maximum(m_old, jnp.max(s, axis=0, keepdims=True) + off)
            p = jnp.exp2(s - (m_new - off)).astype(BF16)
            acc_ref[hc] = (jnp.exp2(m_old - m_new) * acc_ref[hc]
                           + jnp.dot(vtx_ref[0, h, 0], p, preferred_element_type=F32))
            m_ref[hc] = m_new

    def all_heads(bias_of_head, far_const_of_head, bounded):
        for src, dst in zip(w_f32_refs, w_bf16_refs):
            dst[...] = src[...].astype(BF16)
        s_next = logits(0)
        bias = None
        bias_next = None if bias_of_head is None else bias_of_head(0)
        for hc in range(2 * N_HEADS):
            h, c = divmod(hc, 2)
            s = s_next
            if hc + 1 < 2 * N_HEADS:
                s_next = logits(hc + 1)
            if bias_of_head is not None and c == 0:
                bias = bias_next
                if h + 1 < N_HEADS:
                    bias_next = bias_of_head(h + 1)
            accumulate(hc, s, None if far_const_of_head is None else far_const_of_head(h), bias, bounded)

    def far_tile(bounded):
        v = int(bounded)
        all_heads(None, lambda h: jnp.where(pos_far, far_ref[v, 1, h], far_ref[v, 0, h]), bounded)

    def run_tile(bounded):
        d0 = klo_ref[b, j] - qlo_ref[b, iq]

        def bias_of_head(h):
            offs = [jnp.clip(d0 - c + DIAG_CENTER, DIAG_OFF_MIN, DIAG_OFF_MAX) for c in range(0, TQ_A, LANES)]
            return jnp.concatenate([diag_ref[int(bounded), h, pl.ds(off, TK_A), :] for off in offs], axis=1)

        all_heads(bias_of_head, None, bounded)

    def generic_tile(bounded):
        bucket = []

        def bias_of_head(h):
            if not bucket:
                bucket.append(_t5_bucket(posk_ref[0] - posq_ref[0]))
            return _table_lookup(tbl_ref[int(bounded), h], bucket[0])

        all_heads(bias_of_head, None, bounded)

    runs = jnp.logical_and(qrun_ref[b, iq] == 1, krun_ref[b, j] == 1)
    near_run = jnp.logical_and(jnp.logical_not(far), runs)
    near_generic = jnp.logical_and(jnp.logical_not(far), jnp.logical_not(runs))
    is_bounded = mode_ref[0] == 1
    for bounded in (True, False):
        mode = is_bounded if bounded else jnp.logical_not(is_bounded)
        pl.when(jnp.logical_and(mode, far))(functools.partial(far_tile, bounded))
        pl.when(jnp.logical_and(mode, near_run))(functools.partial(run_tile, bounded))
        pl.when(jnp.logical_and(mode, near_generic))(functools.partial(generic_tile, bounded))

    @pl.when(j == pl.num_programs(2) - 1)
    def _():
        lam = (jnp.exp(jnp.sum(lq1_ref[...] * lk1_ref[...], axis=-1, keepdims=True))
               - jnp.exp(jnp.sum(lq2_ref[...] * lk2_ref[...], axis=-1, keepdims=True)) + lam_init)
        for h in range(N_HEADS):
            a0, a1 = acc_ref[2 * h], acc_ref[2 * h + 1]
            o_t = (a0[:HEAD_DIM] / a0[HEAD_DIM:HEAD_DIM + 1]
                   - lam * (a1[:HEAD_DIM] / a1[HEAD_DIM:HEAD_DIM + 1]))
            r = lax.rsqrt(jnp.mean(o_t * o_t, axis=0, keepdims=True) + RMS_EPS)
            o_t = ((o_t * r) * gsub_ref[...]) * (1.0 - lam_init)
            o_ref[0, :, h * HEAD_DIM:(h + 1) * HEAD_DIM] = o_t.T.astype(BF16)


def _diff_attention(qx, kx, vtx, positions, tbl2, diag2, far2, bounded, lam_vecs, gsub, lam_init, cast_srcs):
    assert TK_A <= DIAG_ROWS_MAX
    b, _, s, _ = kx.shape
    nq, nk = s // TQ_A, s // TK_A
    pq = positions.reshape(b, nq, TQ_A)
    pk = positions.reshape(b, nk, TK_A)
    ranges = (pq.min(-1), pq.max(-1), pk.min(-1), pk.max(-1),
              _consecutive_runs(positions, [(t * TQ_A, (t + 1) * TQ_A) for t in range(nq)]),
              _consecutive_runs(positions, [(t * TK_A, (t + 1) * TK_A) for t in range(nk)]))
    lam_spec = pl.BlockSpec((1, DIFF_QK_DIM), lambda i, q, k, *_: (0, 0))
    scratch = [pltpu.VMEM((2 * N_HEADS, V_ROWS, TQ_A), F32), pltpu.VMEM((2 * N_HEADS, 1, TQ_A), F32)]
    n_steps = b * nq * nk
    assert all(w.shape[0] % (16 * n_steps) == 0 for w in cast_srcs)
    cast_specs = [pl.BlockSpec((w.shape[0] // n_steps, w.shape[1]), lambda i, q, k, *_: ((i * nq + q) * nk + k, 0))
                  for w in cast_srcs]
    grid_spec = pltpu.PrefetchScalarGridSpec(
        num_scalar_prefetch=len(ranges),
        grid=(b, nq, nk),
        in_specs=[
            _smem(), _smem(), lam_spec, lam_spec, lam_spec, lam_spec,
            pl.BlockSpec((1, 2 * N_HEADS, HEAD_DIM, TQ_A), lambda i, q, k, *_: (i, 0, 0, q)),
            pl.BlockSpec((1, 2 * N_HEADS, TK_A, HEAD_DIM), lambda i, q, k, *_: (i, 0, k, 0)),
            pl.BlockSpec((1, N_HEADS, 1, V_ROWS, TK_A), lambda i, q, k, *_: (i, 0, k, 0, 0)),
            pl.BlockSpec((1, 1, TQ_A), lambda i, q, k, *_: (i, 0, q)),
            pl.BlockSpec((1, TK_A, 1), lambda i, q, k, *_: (i, k, 0)),
            pl.BlockSpec((2, N_HEADS, 1, LANES), lambda i, q, k, *_: (0, 0, 0, 0)),
            pl.BlockSpec((2, N_HEADS, DIAG_ROWS, LANES), lambda i, q, k, *_: (0, 0, 0, 0),
                         pipeline_mode=pl.Buffered(1)),
            pl.BlockSpec((HEAD_DIM, 1), lambda i, q, k, *_: (0, 0)),
            *cast_specs,
        ],
        out_specs=[pl.BlockSpec((1, TQ_A, MIX_WIDTH), lambda i, q, k, *_: (i, q, 0)), *cast_specs],
        scratch_shapes=scratch,
    )
    outs = pl.pallas_call(
        functools.partial(_diff_attn_kernel, lam_init=lam_init, n_cast=len(cast_srcs)),
        grid_spec=grid_spec,
        out_shape=[jax.ShapeDtypeStruct((b, s, MIX_WIDTH), BF16),
                   *[jax.ShapeDtypeStruct(w.shape, BF16) for w in cast_srcs]],
        compiler_params=_cparams(("arbitrary", "arbitrary", "arbitrary")),
        name="diff_attn",
    )(*ranges, bounded.astype(I32).reshape(1), far2, *[v.reshape(1, DIFF_QK_DIM) for v in lam_vecs],
      qx, kx, vtx, positions.reshape(b, 1, s), positions.reshape(b, s, 1), tbl2, diag2,
      gsub.reshape(HEAD_DIM, 1), *cast_srcs)
    return outs[0], tuple(outs[1:])


def _prep_b_kernel(q_ref, k_ref, v_ref, qm_ref, gq_ref, gk_ref, gm_ref, qt_ref, kn_ref, vt_ref, qmn_ref):
    scale = HEAD_DIM ** -0.5 * LOG2E
    n_blk = TS_PREP // WINDOW
    ones_rows = jnp.where(lax.broadcasted_iota(I32, (V_ROWS - HEAD_DIM, TS_PREP), 0) == 0, 1.0, 0.0).astype(BF16)
    for h in range(N_HEADS):
        g, hg = divmod(h, GQA_GROUP)
        qt = q_ref[0, :, h * HEAD_DIM:(h + 1) * HEAD_DIM].astype(F32).T
        r = lax.rsqrt(jnp.mean(qt * qt, axis=0, keepdims=True) + RMS_EPS)
        qt = ((qt * r) * (gq_ref[...] * scale)).astype(BF16)
        for n in range(n_blk):
            qt_ref[0, n, g, :, hg * WINDOW:(hg + 1) * WINDOW] = qt[:, n * WINDOW:(n + 1) * WINDOW]
    for g in range(N_KV_HEADS):
        sl = slice(g * HEAD_DIM, (g + 1) * HEAD_DIM)
        k = k_ref[0, :, sl].astype(F32)
        kn_ref[0, g] = ((k * _group_rms_scale(k, HEAD_DIM)) * gk_ref[...]).astype(BF16)
        vt = jnp.concatenate([v_ref[0, :, sl].astype(F32).T.astype(BF16), ones_rows], axis=0)
        for n in range(n_blk):
            vt_ref[0, g, n] = vt[:, n * WINDOW:(n + 1) * WINDOW]
        qm = qm_ref[0, :, sl].astype(F32)
        qmn_ref[0, :, sl] = ((qm * _group_rms_scale(qm, HEAD_DIM)) * (gm_ref[...] * scale)).astype(BF16)


def _prep_b(proj, gq, gk, gm):
    b, s, _ = proj.shape
    kblk = MIX_WIDTH // KV_WIDTH
    n_blk = TS_PREP // WINDOW
    g = lambda v: v.reshape(1, HEAD_DIM)
    return pl.pallas_call(
        _prep_b_kernel,
        grid=(b, s // TS_PREP),
        in_specs=[
            pl.BlockSpec((1, TS_PREP, MIX_WIDTH), lambda i, j: (i, j, 0)),
            pl.BlockSpec((1, TS_PREP, KV_WIDTH), lambda i, j: (i, j, kblk)),
            pl.BlockSpec((1, TS_PREP, KV_WIDTH), lambda i, j: (i, j, kblk + 1)),
            pl.BlockSpec((1, TS_PREP, MEM_WIDTH), lambda i, j: (i, j, kblk + 2)),
            pl.BlockSpec((HEAD_DIM, 1), lambda i, j: (0, 0)),
            pl.BlockSpec((1, HEAD_DIM), lambda i, j: (0, 0)),
            pl.BlockSpec((1, HEAD_DIM), lambda i, j: (0, 0)),
        ],
        out_specs=[
            pl.BlockSpec((1, n_blk, N_KV_HEADS, HEAD_DIM, GQA_GROUP * WINDOW), lambda i, j: (i, j, 0, 0, 0)),
            pl.BlockSpec((1, N_KV_HEADS, TS_PREP, HEAD_DIM), lambda i, j: (i, 0, j, 0)),
            pl.BlockSpec((1, N_KV_HEADS, n_blk, V_ROWS, WINDOW), lambda i, j: (i, 0, j, 0, 0)),
            pl.BlockSpec((1, TS_PREP, MEM_WIDTH), lambda i, j: (i, j, 0)),
        ],
        out_shape=[
            jax.ShapeDtypeStruct((b, s // WINDOW, N_KV_HEADS, HEAD_DIM, GQA_GROUP * WINDOW), BF16),
            jax.ShapeDtypeStruct((b, N_KV_HEADS, s, HEAD_DIM), BF16),
            jax.ShapeDtypeStruct((b, N_KV_HEADS, s // WINDOW, V_ROWS, WINDOW), BF16),
            jax.ShapeDtypeStruct((b, s, MEM_WIDTH), BF16),
        ],
        compiler_params=_cparams(("parallel", "parallel")),
        name="prep_b",
    )(proj, proj, proj, proj, gq.reshape(HEAD_DIM, 1), g(gk), g(gm))


def _win_attn_kernel(small_ref, sink_ref, run_ref, qt_ref, k_ref, vt_ref, posq_ref, posk_ref, tbl_ref, diag_ref,
                     o_ref):
    n_blocks = posq_ref.shape[1]
    n_sub = o_ref.shape[1] // WINDOW
    kw = 3 * WINDOW
    lane3 = lax.broadcasted_iota(I32, (1, GQA_GROUP * WINDOW), 1)

    def window(nl):
        n = pl.program_id(1) * n_sub + nl
        nb = jnp.clip(n - 1, 0, n_blocks - 3)
        return n, nb, pl.multiple_of(nb * WINDOW, WINDOW)

    def logits(nl, g):
        start = window(nl)[2]
        return jnp.dot(k_ref[0, g, pl.ds(start, kw), :], qt_ref[0, nl, g], preferred_element_type=F32)

    def in_window(nl):
        n, _, start = window(nl)
        ki = start + lax.broadcasted_iota(I32, (kw, WINDOW), 0)
        qi = n * WINDOW + lax.broadcasted_iota(I32, (kw, WINDOW), 1)
        return jnp.abs(ki - qi) <= WINDOW

    def masked_bucket(nl):
        n, _, start = window(nl)
        rel = posk_ref[0, pl.ds(start, kw), :] - posq_ref[0, pl.ds(n, 1), :]
        return jnp.where(in_window(nl), _t5_bucket(rel), N_BUCKETS)

    def bias_from_bucket(bucket, nl, h, bounded):
        return _table_lookup(tbl_ref[int(bounded), h], bucket)

    def bias_from_diag(valid, nl, h, bounded):
        n, _, start = window(nl)
        off = pl.multiple_of(start - n * WINDOW + DIAG_CENTER, 16)
        return jnp.where(valid, diag_ref[int(bounded), h, pl.ds(off, kw), :], 0.0 if bounded else NEG_BIG)

    def finish(nl, g, s, block_state, head_bias, bounded):
        nb = window(nl)[1]
        heads = range(g * GQA_GROUP, (g + 1) * GQA_GROUP)
        bias = jnp.concatenate([head_bias(block_state, nl, h, bounded) for h in heads], axis=1)
        sinks = [sink_ref[int(bounded), h] for h in heads]
        sink = jnp.where(lane3 < WINDOW, sinks[0], jnp.where(lane3 < 2 * WINDOW, sinks[1], sinks[2]))
        if bounded:
            p = (jnp.exp2(s) * bias).astype(BF16)
            sink_term = sink
        else:
            z = s + bias
            m = jnp.maximum(jnp.max(z, axis=0, keepdims=True), sink)
            p = jnp.exp2(z - m).astype(BF16)
            sink_term = jnp.exp2(sink - m)
        vt = jnp.concatenate([vt_ref[0, g, nb + t] for t in range(3)], axis=1)
        acc = jnp.dot(vt, p, preferred_element_type=F32)
        o_t = acc[:HEAD_DIM] / (acc[HEAD_DIM:HEAD_DIM + 1] + sink_term)
        for hg, h in enumerate(heads):
            o_ref[0, nl * WINDOW:(nl + 1) * WINDOW, h * HEAD_DIM:(h + 1) * HEAD_DIM] = (
                o_t[:, hg * WINDOW:(hg + 1) * WINDOW].T.astype(BF16))

    def all_chains(block_state_of, head_bias, bounded):
        chains = [(nl, g) for nl in range(n_sub) for g in range(N_KV_HEADS)]
        s_next = logits(*chains[0])
        block_state = None
        for idx, (nl, g) in enumerate(chains):
            s = s_next
            if idx + 1 < len(chains):
                s_next = logits(*chains[idx + 1])
            if g == 0:
                block_state = block_state_of(nl)
            finish(nl, g, s, block_state, head_bias, bounded)

    is_run = run_ref[pl.program_id(0), pl.program_id(1)] == 1
    is_small = small_ref[0] == 1
    for bounded in (True, False):
        mode = is_small if bounded else jnp.logical_not(is_small)
        pl.when(jnp.logical_and(mode, is_run))(
            functools.partial(all_chains, in_window, bias_from_diag, bounded))
        pl.when(jnp.logical_and(mode, jnp.logical_not(is_run)))(
            functools.partial(all_chains, masked_bucket, bias_from_bucket, bounded))


def _win_attention(qt, kn, vt, positions, tbl2, diag2, sink2, small):
    b, _, s, _ = kn.shape
    n_blocks = s // WINDOW
    n_sub = TQ_B // WINDOW
    runs = _consecutive_runs(positions, [(max(0, t * TQ_B - WINDOW), min(s, (t + 1) * TQ_B + WINDOW))
                                         for t in range(s // TQ_B)])
    return pl.pallas_call(
        _win_attn_kernel,
        grid=(b, s // TQ_B),
        in_specs=[
            _smem(), _smem(), _smem(),
            pl.BlockSpec((1, n_sub, N_KV_HEADS, HEAD_DIM, GQA_GROUP * WINDOW), lambda i, j: (i, j, 0, 0, 0)),
            pl.BlockSpec((1, N_KV_HEADS, s, HEAD_DIM), lambda i, j: (i, 0, 0, 0)),
            pl.BlockSpec((1, N_KV_HEADS, n_blocks, V_ROWS, WINDOW), lambda i, j: (i, 0, 0, 0, 0)),
            pl.BlockSpec((1, n_blocks, WINDOW), lambda i, j: (i, 0, 0)),
            pl.BlockSpec((1, s, 1), lambda i, j: (i, 0, 0)),
            pl.BlockSpec((2, N_HEADS, 1, LANES), lambda i, j: (0, 0, 0, 0)),
            pl.BlockSpec((2, N_HEADS, DIAG_ROWS, LANES), lambda i, j: (0, 0, 0, 0), pipeline_mode=pl.Buffered(1)),
        ],
        out_specs=pl.BlockSpec((1, TQ_B, MIX_WIDTH), lambda i, j: (i, j, 0)),
        out_shape=jax.ShapeDtypeStruct((b, s, MIX_WIDTH), BF16),
        compiler_params=_cparams(("parallel", "parallel")),
        name="win_attn",
    )(small.astype(I32).reshape(1), sink2, runs, qt, kn, vt, positions.reshape(b, n_blocks, WINDOW),
      positions.reshape(b, s, 1), tbl2, diag2)


def _mem_attn_kernel(q_ref, k_ref, v_ref, o_ref):
    for h in range(N_MEM_HEADS):
        sl = slice(h * HEAD_DIM, (h + 1) * HEAD_DIM)
        s = lax.dot_general(q_ref[0, :, sl], k_ref[0, :, sl], (((1,), (1,)), ((), ())), preferred_element_type=F32)
        e = jnp.exp2(s - jnp.max(s, axis=-1, keepdims=True))
        p = (e / jnp.sum(e, axis=-1, keepdims=True)).astype(BF16)
        o_ref[0, :, sl] = jnp.dot(p, v_ref[0, :, sl], preferred_element_type=F32).astype(BF16)


def _mem_attention(qmn, k_m, v_m):
    b, s, _ = qmn.shape
    return pl.pallas_call(
        _mem_attn_kernel,
        grid=(b, s // TQ_MEM),
        in_specs=[
            pl.BlockSpec((1, TQ_MEM, MEM_WIDTH), lambda i, j: (i, j, 0)),
            pl.BlockSpec((1, MEM_LEN, MEM_WIDTH), lambda i, j: (i, 0, 0)),
            pl.BlockSpec((1, MEM_LEN, MEM_WIDTH), lambda i, j: (i, 0, 0)),
        ],
        out_specs=pl.BlockSpec((1, TQ_MEM, MEM_WIDTH), lambda i, j: (i, j, 0)),
        out_shape=jax.ShapeDtypeStruct((b, s, MEM_WIDTH), BF16),
        compiler_params=_cparams(("parallel", "parallel")),
        name="mem_attn",
    )(qmn, k_m, v_m)


def _out_proj_kernel(x_ref, o_ref, om_ref, wo_ref, wm_ref, y_ref):
    y_ref[...] = (x_ref[...]
                  + jnp.dot(o_ref[...], wo_ref[...].astype(BF16), preferred_element_type=F32)
                  + jnp.dot(om_ref[...], wm_ref[...].astype(BF16), preferred_element_type=F32))


def _out_proj(x2d, o2d, om2d, w_stack, layer):
    m, d = x2d.shape
    return pl.pallas_call(
        _out_proj_kernel,
        grid=(m // TM_OUT, d // TN_OUT),
        in_specs=[
            pl.BlockSpec((TM_OUT, TN_OUT), lambda i, j: (i, j)),
            pl.BlockSpec((TM_OUT, MIX_WIDTH), lambda i, j: (i, 0)),
            pl.BlockSpec((TM_OUT, MEM_WIDTH), lambda i, j: (i, 0)),
            pl.BlockSpec((None, MIX_WIDTH, TN_OUT), lambda i, j: (layer, 0, j)),
            pl.BlockSpec((None, MEM_WIDTH, TN_OUT), lambda i, j: (layer, MIX_WIDTH // MEM_WIDTH, j)),
        ],
        out_specs=pl.BlockSpec((TM_OUT, TN_OUT), lambda i, j: (i, j)),
        out_shape=jax.ShapeDtypeStruct((m, d), F32),
        compiler_params=_cparams(("parallel", "parallel")),
        name="out_proj",
    )(x2d, o2d, om2d, w_stack, w_stack)


def _mlp_kernel(x_ref, g_ref, wu_ref, wd_ref, y_ref, h_ref):
    f = pl.program_id(1)

    @pl.when(f == 0)
    def _():
        x = x_ref[...]
        r = lax.rsqrt(jnp.mean(x * x, axis=-1, keepdims=True) + RMS_EPS)
        h_ref[...] = ((x * r) * g_ref[...]).astype(BF16)
        y_ref[...] = x

    u = jnp.maximum(jnp.dot(h_ref[...], wu_ref[...].astype(BF16), preferred_element_type=F32), 0.0)
    y_ref[...] += jnp.dot((u * u).astype(BF16), wd_ref[...].astype(BF16), preferred_element_type=F32)


def _mlp(x2d, gain, wu_stack, wd_stack, layer):
    m, d = x2d.shape
    return pl.pallas_call(
        _mlp_kernel,
        grid=(m // TM_MLP, D_FF // TF_MLP),
        in_specs=[
            pl.BlockSpec((TM_MLP, d), lambda i, f: (i, 0)),
            pl.BlockSpec((1, d), lambda i, f: (0, 0)),
            pl.BlockSpec((None, d, TF_MLP), lambda i, f: (layer, 0, f)),
            pl.BlockSpec((None, TF_MLP, d), lambda i, f: (layer, f, 0)),
        ],
        out_specs=pl.BlockSpec((TM_MLP, d), lambda i, f: (i, 0)),
        out_shape=jax.ShapeDtypeStruct((m, d), F32),
        scratch_shapes=[pltpu.VMEM((TM_MLP, d), BF16)],
        compiler_params=_cparams(("parallel", "arbitrary")),
        name="mlp",
    )(x2d, gain.reshape(1, d), wu_stack, wd_stack)


def _diag_table_kernel(tbl_ref, o_ref):
    r = lax.broadcasted_iota(I32, (DIAG_ROWS, LANES), 0)
    l = lax.broadcasted_iota(I32, (DIAG_ROWS, LANES), 1)
    bias = _table_lookup(tbl_ref[0], _t5_bucket(r - l - DIAG_CENTER))
    o_ref[0, 0] = bias
    o_ref[1, 0] = jnp.exp2(bias)


def _diag_bias_table(tbl_log2):
    return pl.pallas_call(
        _diag_table_kernel,
        grid=(N_HEADS,),
        in_specs=[pl.BlockSpec((1, 1, LANES), lambda h: (h, 0, 0))],
        out_specs=pl.BlockSpec((2, 1, DIAG_ROWS, LANES), lambda h: (0, h, 0, 0)),
        out_shape=jax.ShapeDtypeStruct((2, N_HEADS, DIAG_ROWS, LANES), F32),
        compiler_params=_cparams(("parallel",)),
        name="diag_bias_table",
    )(tbl_log2)


def _consecutive_runs(positions, ranges):
    b = positions.shape[0]
    step_ok = (positions[:, 1:] - positions[:, :-1] == 1).astype(I32)
    c = jnp.concatenate([jnp.zeros((b, 1), I32), jnp.cumsum(step_ok, axis=1)], axis=1)
    return jnp.stack([(c[:, hi - 1] - c[:, lo] == hi - 1 - lo).astype(I32) for lo, hi in ranges], axis=1)


def _bias_tables(rel_bias):
    t = (rel_bias.astype(F32) * LOG2E).T
    tbl = jnp.zeros((N_HEADS, 1, LANES), F32).at[:, 0, :N_BUCKETS].set(t).at[:, 0, N_BUCKETS].set(NEG_BIG)
    far = jnp.stack([t[:, HALF_BUCKETS - 1], t[:, N_BUCKETS - 1]])
    return tbl, far


def _logit_bound(gq, gk, rel_bias, dim):
    qk = 1.01 * dim * jnp.max(jnp.abs(gq)) * jnp.max(jnp.abs(gk)) * (dim ** -0.5 * LOG2E)
    return jnp.ceil(qk + jnp.max(jnp.abs(rel_bias)) * LOG2E).astype(F32)


def kernel(x, mem, positions, rel_bias, norm_attn, norm_mem, norm_mlp, w_in_a, a_q_norm, a_k_norm, a_lambda_q1, a_lambda_k1, a_lambda_q2, a_lambda_k2, a_subln, w_in_b, b_q_norm, b_k_norm, b_sink, w_mem_kv, m_q_norm, m_k_norm, w_out, w_up, w_down):
    b, s, d = x.shape
    depth = norm_attn.shape[0]
    tbl_log2, far_log2 = _bias_tables(rel_bias)
    tbl2 = jnp.stack([tbl_log2, jnp.exp2(tbl_log2)])
    far2 = jnp.stack([far_log2, jnp.exp2(far_log2)])
    diag2 = _diag_bias_table(tbl_log2)
    x2d = x.reshape(b * s, d)
    later_w = (w_out, w_up, w_down, w_in_b)
    for i in range(depth):
        j = i // 2
        k_m, v_m = _mem_kv(mem, norm_mem[i], w_mem_kv, i, m_k_norm[i])
        if i % 2 == 0:
            proj = _norm_proj(x2d, norm_attn[i], w_in_a, j).reshape(b, s, -1)
            shift = _logit_bound(a_q_norm[j], a_k_norm[j], rel_bias, DIFF_QK_DIM)
            bounded = shift <= SHIFT_LIMIT
            qx, kx, vtx, qmn = _prep_a(proj, a_q_norm[j], a_k_norm[j], m_q_norm[i], jnp.where(bounded, -shift, 0.0))
            lam_init = 0.8 - 0.6 * math.exp(-0.3 * i)
            cast_srcs = tuple(w.reshape(-1, w.shape[-1]) for w in later_w) if i == 0 else ()
            o, casts = _diff_attention(
                qx, kx, vtx, positions, tbl2, diag2, far2, bounded,
                lam_vecs=(a_lambda_q1[j], a_lambda_k1[j], a_lambda_q2[j], a_lambda_k2[j]), gsub=a_subln[j],
                lam_init=lam_init, cast_srcs=cast_srcs)
            if i == 0:
                later_w = tuple(c.reshape(w.shape) for c, w in zip(casts, later_w))
                w_out, w_up, w_down, w_in_b = later_w
        else:
            proj = _norm_proj(x2d, norm_attn[i], w_in_b, j).reshape(b, s, -1)
            qt, kn, vt, qmn = _prep_b(proj, b_q_norm[j], b_k_norm[j], m_q_norm[i])
            sink_log2 = b_sink[j].astype(F32) * LOG2E
            small = jnp.logical_and(_logit_bound(b_q_norm[j], b_k_norm[j], rel_bias, HEAD_DIM) <= SHIFT_LIMIT,
                                    jnp.max(jnp.abs(sink_log2)) <= SHIFT_LIMIT)
            o = _win_attention(qt, kn, vt, positions, tbl2, diag2,
                               jnp.stack([sink_log2, jnp.exp2(sink_log2)]), small)
        o_m = _mem_attention(qmn, k_m, v_m)
        x2d = _out_proj(x2d, o.reshape(b * s, MIX_WIDTH), o_m.reshape(b * s, MEM_WIDTH), w_out, i)
        x2d = _mlp(x2d, norm_mlp[i], w_up, w_down, i)
    return x2d.reshape(b, s, d)
```

```python
import functools
import math

import jax
import jax.numpy as jnp
from jax import lax
from jax.experimental import pallas as pl
from jax.experimental.pallas import tpu as pltpu

F32 = jnp.float32
BF16 = jnp.bfloat16
I32 = jnp.int32

D_MODEL = 2048
N_HEADS = 12
HEAD_DIM = 128
DIFF_QK_DIM = 64
N_KV_HEADS = 4
GQA_GROUP = 3
MIX_WIDTH = N_HEADS * HEAD_DIM
KV_WIDTH = N_KV_HEADS * HEAD_DIM
WINDOW = 128
N_MEM_HEADS = 4
MEM_WIDTH = N_MEM_HEADS * HEAD_DIM
MEM_LEN = 256
D_FF = 4 * D_MODEL
N_BUCKETS = 32
MAX_DISTANCE = 128
RMS_EPS = 1e-6
NEG_BIG = -1e30
LOG2E = math.log2(math.e)

HALF_BUCKETS = N_BUCKETS // 2
MAX_EXACT = HALF_BUCKETS // 2
FAR_DIST = 91

ONE_COL = DIFF_QK_DIM
V_ROWS = HEAD_DIM + 16
SHIFT_LIMIT = 50.0

LANES = 128

DIAG_ROWS_MAX = 512
DIAG_OFF_MIN = 5
DIAG_CENTER = DIAG_OFF_MIN + DIAG_ROWS_MAX + FAR_DIST
DIAG_OFF_MAX = DIAG_CENTER + FAR_DIST + LANES - 1
DIAG_ROWS = -(-(DIAG_OFF_MAX + DIAG_ROWS_MAX) // 16) * 16

VMEM_LIMIT = 56 * 1024 * 1024

TM_PROJ, TN_PROJ = 1024, 1024
TS_PREP = 512
TQ_A, TK_A = 512, 512
TQ_B = 512
TQ_MEM = 2048
TM_OUT, TN_OUT = 512, 2048
TM_MLP, TF_MLP = 1024, 512


def _cparams(sem):
    return pltpu.CompilerParams(dimension_semantics=sem, vmem_limit_bytes=VMEM_LIMIT)


def _smem():
    return pl.BlockSpec(memory_space=pltpu.SMEM)


def _t5_bucket(rel):
    side = jnp.where(rel > 0, HALF_BUCKETS, 0)
    n = jnp.abs(rel)
    n_f = jnp.maximum(n, 1).astype(F32)
    large = MAX_EXACT + (jnp.log(n_f / MAX_EXACT) / math.log(MAX_DISTANCE / MAX_EXACT)
                         * (HALF_BUCKETS - MAX_EXACT)).astype(I32)
    large = jnp.minimum(large, HALF_BUCKETS - 1)
    return side + jnp.where(n < MAX_EXACT, n, large)


def _table_lookup(tbl_row, bucket):
    rows, cols = bucket.shape
    tb = jnp.broadcast_to(tbl_row, (rows, LANES))
    parts = [jnp.take_along_axis(tb, bucket[:, c:c + LANES], axis=1) for c in range(0, cols, LANES)]
    return parts[0] if len(parts) == 1 else jnp.concatenate(parts, axis=1)


def _group_rms_scale(x, group):
    t = x * x
    if group == LANES:
        return lax.rsqrt(jnp.mean(t, axis=-1, keepdims=True) + RMS_EPS)
    lane = lax.broadcasted_iota(I32, x.shape, 1)
    lo = lane < group
    s_lo = jnp.sum(jnp.where(lo, t, 0.0), axis=-1, keepdims=True)
    s_hi = jnp.sum(jnp.where(lo, 0.0, t), axis=-1, keepdims=True)
    return jnp.where(lo, lax.rsqrt(s_lo / group + RMS_EPS), lax.rsqrt(s_hi / group + RMS_EPS))


def _proj_kernel(x_ref, g_ref, w_ref, o_ref, h_ref):
    def project(h):
        return jnp.dot(h, w_ref[...].astype(BF16), preferred_element_type=F32).astype(BF16)

    @pl.when(pl.program_id(1) == 0)
    def _():
        half = x_ref.shape[0] // 2
        for rows in (slice(0, half), slice(half, 2 * half)):
            x = x_ref[rows, :]
            r = lax.rsqrt(jnp.mean(x * x, axis=-1, keepdims=True) + RMS_EPS)
            h = ((x * r) * g_ref[...]).astype(BF16)
            h_ref[rows, :] = h
            o_ref[rows, :] = project(h)

    @pl.when(pl.program_id(1) > 0)
    def _():
        o_ref[...] = project(h_ref[...])


def _norm_proj(x2d, gain, w_stack, layer):
    m, d = x2d.shape
    n = w_stack.shape[2]
    return pl.pallas_call(
        _proj_kernel,
        grid=(m // TM_PROJ, n // TN_PROJ),
        in_specs=[
            pl.BlockSpec((TM_PROJ, d), lambda i, j: (i, 0)),
            pl.BlockSpec((1, d), lambda i, j: (0, 0)),
            pl.BlockSpec((None, d, TN_PROJ), lambda i, j: (layer, 0, j)),
        ],
        out_specs=pl.BlockSpec((TM_PROJ, TN_PROJ), lambda i, j: (i, j)),
        out_shape=jax.ShapeDtypeStruct((m, n), BF16),
        scratch_shapes=[pltpu.VMEM((TM_PROJ, d), BF16)],
        compiler_params=_cparams(("parallel", "arbitrary")),
        name="norm_proj",
    )(x2d, gain.reshape(1, d), w_stack)


def _mem_kv_kernel(mem_ref, g_ref, w_ref, gk_ref, k_ref, v_ref):
    x = mem_ref[0]
    r = lax.rsqrt(jnp.mean(x * x, axis=-1, keepdims=True) + RMS_EPS)
    mn = ((x * r) * g_ref[...]).astype(BF16)
    mkv = jnp.dot(mn, w_ref[...].astype(BF16), preferred_element_type=F32)
    for h in range(N_MEM_HEADS):
        kh = mkv[:, h * HEAD_DIM:(h + 1) * HEAD_DIM]
        k_ref[0, :, h * HEAD_DIM:(h + 1) * HEAD_DIM] = ((kh * _group_rms_scale(kh, HEAD_DIM)) * gk_ref[...]).astype(BF16)
    v_ref[0] = mkv[:, MEM_WIDTH:].astype(BF16)


def _mem_kv(mem, gain, w_stack, layer, gk):
    b = mem.shape[0]
    shp = jax.ShapeDtypeStruct((b, MEM_LEN, MEM_WIDTH), BF16)
    return pl.pallas_call(
        _mem_kv_kernel,
        grid=(b,),
        in_specs=[
            pl.BlockSpec((1, MEM_LEN, D_MODEL), lambda i: (i, 0, 0)),
            pl.BlockSpec((1, D_MODEL), lambda i: (0, 0)),
            pl.BlockSpec((None, D_MODEL, 2 * MEM_WIDTH), lambda i: (layer, 0, 0)),
            pl.BlockSpec((1, HEAD_DIM), lambda i: (0, 0)),
        ],
        out_specs=[pl.BlockSpec((1, MEM_LEN, MEM_WIDTH), lambda i: (i, 0, 0))] * 2,
        out_shape=[shp, shp],
        compiler_params=_cparams(("parallel",)),
        name="mem_kv",
    )(mem, gain.reshape(1, D_MODEL), w_stack, gk.reshape(1, HEAD_DIM))


def _prep_a_kernel(negm_ref, q_ref, k_ref, v_ref, qm_ref, gq_ref, gk_ref, gm_ref, qx_ref, kx_ref, vtx_ref, qmn_ref):
    q_scale = DIFF_QK_DIM ** -0.5 * LOG2E
    m_scale = HEAD_DIM ** -0.5 * LOG2E
    lane = lax.broadcasted_iota(I32, (TS_PREP, HEAD_DIM), 1)
    row = lax.broadcasted_iota(I32, (HEAD_DIM, TS_PREP), 0)
    lo_lanes, lo_rows = lane < DIFF_QK_DIM, row < DIFF_QK_DIM
    neg_shift = negm_ref[0]
    ones_rows = jnp.where(lax.broadcasted_iota(I32, (V_ROWS - HEAD_DIM, TS_PREP), 0) == 0, 1.0, 0.0).astype(BF16)
    q_fill = (jnp.where(row == ONE_COL, neg_shift, 0.0), jnp.where(row == ONE_COL - 1, neg_shift, 0.0))
    k_fill = (jnp.where(lane == ONE_COL, 1.0, 0.0), jnp.where(lane == ONE_COL - 1, 1.0, 0.0))
    q_gain = gq_ref[...] * q_scale
    for h in range(N_HEADS):
        sl = slice(h * HEAD_DIM, (h + 1) * HEAD_DIM)
        qt = q_ref[0, :, sl].astype(F32).T
        t = qt * qt
        r_lo = lax.rsqrt(jnp.mean(t[:DIFF_QK_DIM], axis=0, keepdims=True) + RMS_EPS)
        r_hi = lax.rsqrt(jnp.mean(t[DIFF_QK_DIM:], axis=0, keepdims=True) + RMS_EPS)
        qn = (qt * jnp.where(lo_rows, r_lo, r_hi)) * q_gain
        qx_ref[0, 2 * h] = jnp.where(lo_rows, qn, q_fill[0]).astype(BF16)
        qx_ref[0, 2 * h + 1] = jnp.where(lo_rows, q_fill[1], qn).astype(BF16)
        k = k_ref[0, :, sl].astype(F32)
        kn = (k * _group_rms_scale(k, DIFF_QK_DIM)) * gk_ref[...]
        kx_ref[0, 2 * h] = jnp.where(lo_lanes, kn, k_fill[0]).astype(BF16)
        kx_ref[0, 2 * h + 1] = jnp.where(lo_lanes, k_fill[1], kn).astype(BF16)
        vt = v_ref[0, :, sl].astype(F32).T.astype(BF16)
        vtx_ref[0, h, 0] = jnp.concatenate([vt, ones_rows], axis=0)
    for h in range(N_MEM_HEADS):
        sl = slice(h * HEAD_DIM, (h + 1) * HEAD_DIM)
        qm = qm_ref[0, :, sl].astype(F32)
        qmn_ref[0, :, sl] = ((qm * _group_rms_scale(qm, HEAD_DIM)) * (gm_ref[...] * m_scale)).astype(BF16)


def _prep_a(proj, gq, gk, gm, neg_shift):
    assert TS_PREP == TK_A
    b, s, _ = proj.shape
    gq2 = jnp.concatenate([gq, gq]).reshape(HEAD_DIM, 1)
    gk2 = jnp.concatenate([gk, gk]).reshape(1, HEAD_DIM)
    wblk = MIX_WIDTH // MEM_WIDTH
    return pl.pallas_call(
        _prep_a_kernel,
        grid=(b, s // TS_PREP),
        in_specs=[
            _smem(),
            pl.BlockSpec((1, TS_PREP, MIX_WIDTH), lambda i, j: (i, j, 0)),
            pl.BlockSpec((1, TS_PREP, MIX_WIDTH), lambda i, j: (i, j, 1)),
            pl.BlockSpec((1, TS_PREP, MIX_WIDTH), lambda i, j: (i, j, 2)),
            pl.BlockSpec((1, TS_PREP, MEM_WIDTH), lambda i, j: (i, j, 3 * wblk)),
            pl.BlockSpec((HEAD_DIM, 1), lambda i, j: (0, 0)),
            pl.BlockSpec((1, HEAD_DIM), lambda i, j: (0, 0)),
            pl.BlockSpec((1, HEAD_DIM), lambda i, j: (0, 0)),
        ],
        out_specs=[
            pl.BlockSpec((1, 2 * N_HEADS, HEAD_DIM, TS_PREP), lambda i, j: (i, 0, 0, j)),
            pl.BlockSpec((1, 2 * N_HEADS, TS_PREP, HEAD_DIM), lambda i, j: (i, 0, j, 0)),
            pl.BlockSpec((1, N_HEADS, 1, V_ROWS, TK_A), lambda i, j: (i, 0, j, 0, 0)),
            pl.BlockSpec((1, TS_PREP, MEM_WIDTH), lambda i, j: (i, j, 0)),
        ],
        out_shape=[
            jax.ShapeDtypeStruct((b, 2 * N_HEADS, HEAD_DIM, s), BF16),
            jax.ShapeDtypeStruct((b, 2 * N_HEADS, s, HEAD_DIM), BF16),
            jax.ShapeDtypeStruct((b, N_HEADS, s // TK_A, V_ROWS, TK_A), BF16),
            jax.ShapeDtypeStruct((b, s, MEM_WIDTH), BF16),
        ],
        compiler_params=_cparams(("parallel", "parallel")),
        name="prep_a",
    )(neg_shift.reshape(1), proj, proj, proj, proj, gq2, gk2, gm.reshape(1, HEAD_DIM))


def _diff_attn_kernel(qlo_ref, qhi_ref, klo_ref, khi_ref, qrun_ref, krun_ref,
                      far_ref, lq1_ref, lk1_ref, lq2_ref, lk2_ref,
                      qx_ref, kx_ref, vtx_ref, posq_ref, posk_ref, tbl_ref, diag_ref, gsub_ref,
                      *rest, lam_init, bounded, n_cast):
    w_f32_refs, o_ref, w_bf16_refs = rest[:n_cast], rest[n_cast], rest[n_cast + 1:2 * n_cast + 1]
    acc_ref, m_scratch = rest[2 * n_cast + 1], rest[2 * n_cast + 2:]
    b, iq, j = pl.program_id(0), pl.program_id(1), pl.program_id(2)
    pos_far = klo_ref[b, j] - qhi_ref[b, iq] >= FAR_DIST
    neg_far = khi_ref[b, j] - qlo_ref[b, iq] <= -FAR_DIST
    far = jnp.logical_or(pos_far, neg_far)

    @pl.when(j == 0)
    def _():
        acc_ref[...] = jnp.zeros(acc_ref.shape, F32)
        if not bounded:
            m_scratch[0][...] = jnp.full(m_scratch[0].shape, NEG_BIG, F32)

    def logits(hc):
        return jnp.dot(kx_ref[0, hc], qx_ref[0, hc], preferred_element_type=F32)

    def accumulate(hc, s, far_const, bias):
        h = hc // 2
        if bounded:
            e = jnp.exp2(s)
            if bias is not None:
                e = e * bias
            pv = jnp.dot(vtx_ref[0, h, 0], e.astype(BF16), preferred_element_type=F32)
            acc_ref[hc] += pv if far_const is None else far_const * pv
        else:
            if bias is not None:
                s = s + bias
            m_ref = m_scratch[0]
            off = 0.0 if far_const is None else far_const
            m_old = m_ref[hc]
            m_new = jnp.maximum(m_old, jnp.max(s, axis=0, keepdims=True) + off)
            p = jnp.exp2(s - (m_new - off)).astype(BF16)
            acc_ref[hc] = (jnp.exp2(m_old - m_new) * acc_ref[hc]
                           + jnp.dot(vtx_ref[0, h, 0], p, preferred_element_type=F32))
            m_ref[hc] = m_new

    def all_heads(bias_of_head, far_const_of_head):
        for src, dst in zip(w_f32_refs, w_bf16_refs):
            dst[...] = src[...].astype(BF16)
        s_next = logits(0)
        bias = None
        bias_next = None if bias_of_head is None else bias_of_head(0)
        for hc in range(2 * N_HEADS):
            h, c = divmod(hc, 2)
            s = s_next
            if hc + 1 < 2 * N_HEADS:
                s_next = logits(hc + 1)
            if bias_of_head is not None and c == 0:
                bias = bias_next
                if h + 1 < N_HEADS:
                    bias_next = bias_of_head(h + 1)
            accumulate(hc, s, None if far_const_of_head is None else far_const_of_head(h), bias)

    @pl.when(far)
    def _():
        all_heads(None, lambda h: jnp.where(pos_far, far_ref[1, h], far_ref[0, h]))

    runs = jnp.logical_and(qrun_ref[b, iq] == 1, krun_ref[b, j] == 1)

    @pl.when(jnp.logical_and(jnp.logical_not(far), runs))
    def _():
        d0 = klo_ref[b, j] - qlo_ref[b, iq]

        def bias_of_head(h):
            offs = [jnp.clip(d0 - c + DIAG_CENTER, DIAG_OFF_MIN, DIAG_OFF_MAX) for c in range(0, TQ_A, LANES)]
            return jnp.concatenate([diag_ref[h, pl.ds(off, TK_A), :] for off in offs], axis=1)

        all_heads(bias_of_head, None)

    @pl.when(jnp.logical_and(jnp.logical_not(far), jnp.logical_not(runs)))
    def _():
        bucket = []

        def bias_of_head(h):
            if not bucket:
                bucket.append(_t5_bucket(posk_ref[0] - posq_ref[0]))
            return _table_lookup(tbl_ref[h], bucket[0])

        all_heads(bias_of_head, None)

    @pl.when(j == pl.num_programs(2) - 1)
    def _():
        lam = (jnp.exp(jnp.sum(lq1_ref[...] * lk1_ref[...], axis=-1, keepdims=True))
               - jnp.exp(jnp.sum(lq2_ref[...] * lk2_ref[...], axis=-1, keepdims=True)) + lam_init)
        for h in range(N_HEADS):
            a0, a1 = acc_ref[2 * h], acc_ref[2 * h + 1]
            o_t = (a0[:HEAD_DIM] / a0[HEAD_DIM:HEAD_DIM + 1]
                   - lam * (a1[:HEAD_DIM] / a1[HEAD_DIM:HEAD_DIM + 1]))
            r = lax.rsqrt(jnp.mean(o_t * o_t, axis=0, keepdims=True) + RMS_EPS)
            o_t = ((o_t * r) * gsub_ref[...]) * (1.0 - lam_init)
            o_ref[0, :, h * HEAD_DIM:(h + 1) * HEAD_DIM] = o_t.T.astype(BF16)


def _diff_attention(qx, kx, vtx, positions, tbl, diag2, far_consts, lam_vecs, gsub, lam_init, bounded, cast_srcs):
    assert TK_A <= DIAG_ROWS_MAX
    b, _, s, _ = kx.shape
    nq, nk = s // TQ_A, s // TK_A
    pq = positions.reshape(b, nq, TQ_A)
    pk = positions.reshape(b, nk, TK_A)
    ranges = (pq.min(-1), pq.max(-1), pk.min(-1), pk.max(-1),
              _consecutive_runs(positions, [(t * TQ_A, (t + 1) * TQ_A) for t in range(nq)]),
              _consecutive_runs(positions, [(t * TK_A, (t + 1) * TK_A) for t in range(nk)]))
    lam_spec = pl.BlockSpec((1, DIFF_QK_DIM), lambda i, q, k, *_: (0, 0))
    scratch = [pltpu.VMEM((2 * N_HEADS, V_ROWS, TQ_A), F32)]
    if not bounded:
        scratch.append(pltpu.VMEM((2 * N_HEADS, 1, TQ_A), F32))
    n_steps = b * nq * nk
    assert all(w.shape[0] % (16 * n_steps) == 0 for w in cast_srcs)
    cast_specs = [pl.BlockSpec((w.shape[0] // n_steps, w.shape[1]), lambda i, q, k, *_: ((i * nq + q) * nk + k, 0))
                  for w in cast_srcs]
    grid_spec = pltpu.PrefetchScalarGridSpec(
        num_scalar_prefetch=len(ranges),
        grid=(b, nq, nk),
        in_specs=[
            _smem(), lam_spec, lam_spec, lam_spec, lam_spec,
            pl.BlockSpec((1, 2 * N_HEADS, HEAD_DIM, TQ_A), lambda i, q, k, *_: (i, 0, 0, q)),
            pl.BlockSpec((1, 2 * N_HEADS, TK_A, HEAD_DIM), lambda i, q, k, *_: (i, 0, k, 0)),
            pl.BlockSpec((1, N_HEADS, 1, V_ROWS, TK_A), lambda i, q, k, *_: (i, 0, k, 0, 0)),
            pl.BlockSpec((1, 1, TQ_A), lambda i, q, k, *_: (i, 0, q)),
            pl.BlockSpec((1, TK_A, 1), lambda i, q, k, *_: (i, k, 0)),
            pl.BlockSpec((N_HEADS, 1, LANES), lambda i, q, k, *_: (0, 0, 0)),
            pl.BlockSpec((None, N_HEADS, DIAG_ROWS, LANES), lambda i, q, k, *_: (int(bounded), 0, 0, 0),
                         pipeline_mode=pl.Buffered(1)),
            pl.BlockSpec((HEAD_DIM, 1), lambda i, q, k, *_: (0, 0)),
            *cast_specs,
        ],
        out_specs=[pl.BlockSpec((1, TQ_A, MIX_WIDTH), lambda i, q, k, *_: (i, q, 0)), *cast_specs],
        scratch_shapes=scratch,
    )
    outs = pl.pallas_call(
        functools.partial(_diff_attn_kernel, lam_init=lam_init, bounded=bounded, n_cast=len(cast_srcs)),
        grid_spec=grid_spec,
        out_shape=[jax.ShapeDtypeStruct((b, s, MIX_WIDTH), BF16),
                   *[jax.ShapeDtypeStruct(w.shape, BF16) for w in cast_srcs]],
        compiler_params=_cparams(("arbitrary", "arbitrary", "arbitrary")),
        name="diff_attn_bounded" if bounded else "diff_attn_running_max",
    )(*ranges, far_consts, *[v.reshape(1, DIFF_QK_DIM) for v in lam_vecs],
      qx, kx, vtx, positions.reshape(b, 1, s), positions.reshape(b, s, 1), tbl, diag2,
      gsub.reshape(HEAD_DIM, 1), *cast_srcs)
    return outs[0], tuple(outs[1:])


def _prep_b_kernel(q_ref, k_ref, v_ref, qm_ref, gq_ref, gk_ref, gm_ref, qt_ref, kn_ref, vt_ref, qmn_ref):
    scale = HEAD_DIM ** -0.5 * LOG2E
    n_blk = TS_PREP // WINDOW
    ones_rows = jnp.where(lax.broadcasted_iota(I32, (V_ROWS - HEAD_DIM, TS_PREP), 0) == 0, 1.0, 0.0).astype(BF16)
    for h in range(N_HEADS):
        g, hg = divmod(h, GQA_GROUP)
        qt = q_ref[0, :, h * HEAD_DIM:(h + 1) * HEAD_DIM].astype(F32).T
        r = lax.rsqrt(jnp.mean(qt * qt, axis=0, keepdims=True) + RMS_EPS)
        qt = ((qt * r) * (gq_ref[...] * scale)).astype(BF16)
        for n in range(n_blk):
            qt_ref[0, n, g, :, hg * WINDOW:(hg + 1) * WINDOW] = qt[:, n * WINDOW:(n + 1) * WINDOW]
    for g in range(N_KV_HEADS):
        sl = slice(g * HEAD_DIM, (g + 1) * HEAD_DIM)
        k = k_ref[0, :, sl].astype(F32)
        kn_ref[0, g] = ((k * _group_rms_scale(k, HEAD_DIM)) * gk_ref[...]).astype(BF16)
        vt = jnp.concatenate([v_ref[0, :, sl].astype(F32).T.astype(BF16), ones_rows], axis=0)
        for n in range(n_blk):
            vt_ref[0, g, n] = vt[:, n * WINDOW:(n + 1) * WINDOW]
        qm = qm_ref[0, :, sl].astype(F32)
        qmn_ref[0, :, sl] = ((qm * _group_rms_scale(qm, HEAD_DIM)) * (gm_ref[...] * scale)).astype(BF16)


def _prep_b(proj, gq, gk, gm):
    b, s, _ = proj.shape
    kblk = MIX_WIDTH // KV_WIDTH
    n_blk = TS_PREP // WINDOW
    g = lambda v: v.reshape(1, HEAD_DIM)
    return pl.pallas_call(
        _prep_b_kernel,
        grid=(b, s // TS_PREP),
        in_specs=[
            pl.BlockSpec((1, TS_PREP, MIX_WIDTH), lambda i, j: (i, j, 0)),
            pl.BlockSpec((1, TS_PREP, KV_WIDTH), lambda i, j: (i, j, kblk)),
            pl.BlockSpec((1, TS_PREP, KV_WIDTH), lambda i, j: (i, j, kblk + 1)),
            pl.BlockSpec((1, TS_PREP, MEM_WIDTH), lambda i, j: (i, j, kblk + 2)),
            pl.BlockSpec((HEAD_DIM, 1), lambda i, j: (0, 0)),
            pl.BlockSpec((1, HEAD_DIM), lambda i, j: (0, 0)),
            pl.BlockSpec((1, HEAD_DIM), lambda i, j: (0, 0)),
        ],
        out_specs=[
            pl.BlockSpec((1, n_blk, N_KV_HEADS, HEAD_DIM, GQA_GROUP * WINDOW), lambda i, j: (i, j, 0, 0, 0)),
            pl.BlockSpec((1, N_KV_HEADS, TS_PREP, HEAD_DIM), lambda i, j: (i, 0, j, 0)),
            pl.BlockSpec((1, N_KV_HEADS, n_blk, V_ROWS, WINDOW), lambda i, j: (i, 0, j, 0, 0)),
            pl.BlockSpec((1, TS_PREP, MEM_WIDTH), lambda i, j: (i, j, 0)),
        ],
        out_shape=[
            jax.ShapeDtypeStruct((b, s // WINDOW, N_KV_HEADS, HEAD_DIM, GQA_GROUP * WINDOW), BF16),
            jax.ShapeDtypeStruct((b, N_KV_HEADS, s, HEAD_DIM), BF16),
            jax.ShapeDtypeStruct((b, N_KV_HEADS, s // WINDOW, V_ROWS, WINDOW), BF16),
            jax.ShapeDtypeStruct((b, s, MEM_WIDTH), BF16),
        ],
        compiler_params=_cparams(("parallel", "parallel")),
        name="prep_b",
    )(proj, proj, proj, proj, gq.reshape(HEAD_DIM, 1), g(gk), g(gm))


def _win_attn_kernel(small_ref, sink_ref, run_ref, qt_ref, k_ref, vt_ref, posq_ref, posk_ref, tbl_ref, diag_ref,
                     o_ref):
    n_blocks = posq_ref.shape[1]
    n_sub = o_ref.shape[1] // WINDOW
    kw = 3 * WINDOW
    lane3 = lax.broadcasted_iota(I32, (1, GQA_GROUP * WINDOW), 1)

    def window(nl):
        n = pl.program_id(1) * n_sub + nl
        nb = jnp.clip(n - 1, 0, n_blocks - 3)
        return n, nb, pl.multiple_of(nb * WINDOW, WINDOW)

    def logits(nl, g):
        start = window(nl)[2]
        return jnp.dot(k_ref[0, g, pl.ds(start, kw), :], qt_ref[0, nl, g], preferred_element_type=F32)

    def in_window(nl):
        n, _, start = window(nl)
        ki = start + lax.broadcasted_iota(I32, (kw, WINDOW), 0)
        qi = n * WINDOW + lax.broadcasted_iota(I32, (kw, WINDOW), 1)
        return jnp.abs(ki - qi) <= WINDOW

    def masked_bucket(nl):
        n, _, start = window(nl)
        rel = posk_ref[0, pl.ds(start, kw), :] - posq_ref[0, pl.ds(n, 1), :]
        return jnp.where(in_window(nl), _t5_bucket(rel), N_BUCKETS)

    def bias_from_bucket(bucket, nl, h, bounded):
        return _table_lookup(tbl_ref[int(bounded), h], bucket)

    def bias_from_diag(valid, nl, h, bounded):
        n, _, start = window(nl)
        off = pl.multiple_of(start - n * WINDOW + DIAG_CENTER, 16)
        return jnp.where(valid, diag_ref[int(bounded), h, pl.ds(off, kw), :], 0.0 if bounded else NEG_BIG)

    def finish(nl, g, s, block_state, head_bias, bounded):
        nb = window(nl)[1]
        heads = range(g * GQA_GROUP, (g + 1) * GQA_GROUP)
        bias = jnp.concatenate([head_bias(block_state, nl, h, bounded) for h in heads], axis=1)
        sinks = [sink_ref[int(bounded), h] for h in heads]
        sink = jnp.where(lane3 < WINDOW, sinks[0], jnp.where(lane3 < 2 * WINDOW, sinks[1], sinks[2]))
        if bounded:
            p = (jnp.exp2(s) * bias).astype(BF16)
            sink_term = sink
        else:
            z = s + bias
            m = jnp.maximum(jnp.max(z, axis=0, keepdims=True), sink)
            p = jnp.exp2(z - m).astype(BF16)
            sink_term = jnp.exp2(sink - m)
        vt = jnp.concatenate([vt_ref[0, g, nb + t] for t in range(3)], axis=1)
        acc = jnp.dot(vt, p, preferred_element_type=F32)
        o_t = acc[:HEAD_DIM] / (acc[HEAD_DIM:HEAD_DIM + 1] + sink_term)
        for hg, h in enumerate(heads):
            o_ref[0, nl * WINDOW:(nl + 1) * WINDOW, h * HEAD_DIM:(h + 1) * HEAD_DIM] = (
                o_t[:, hg * WINDOW:(hg + 1) * WINDOW].T.astype(BF16))

    def all_chains(block_state_of, head_bias, bounded):
        chains = [(nl, g) for nl in range(n_sub) for g in range(N_KV_HEADS)]
        s_next = logits(*chains[0])
        block_state = None
        for idx, (nl, g) in enumerate(chains):
            s = s_next
            if idx + 1 < len(chains):
                s_next = logits(*chains[idx + 1])
            if g == 0:
                block_state = block_state_of(nl)
            finish(nl, g, s, block_state, head_bias, bounded)

    is_run = run_ref[pl.program_id(0), pl.program_id(1)] == 1
    is_small = small_ref[0] == 1
    for bounded in (True, False):
        mode = is_small if bounded else jnp.logical_not(is_small)
        pl.when(jnp.logical_and(mode, is_run))(
            functools.partial(all_chains, in_window, bias_from_diag, bounded))
        pl.when(jnp.logical_and(mode, jnp.logical_not(is_run)))(
            functools.partial(all_chains, masked_bucket, bias_from_bucket, bounded))


def _win_attention(qt, kn, vt, positions, tbl2, diag2, sink2, small):
    b, _, s, _ = kn.shape
    n_blocks = s // WINDOW
    n_sub = TQ_B // WINDOW
    runs = _consecutive_runs(positions, [(max(0, t * TQ_B - WINDOW), min(s, (t + 1) * TQ_B + WINDOW))
                                         for t in range(s // TQ_B)])
    return pl.pallas_call(
        _win_attn_kernel,
        grid=(b, s // TQ_B),
        in_specs=[
            _smem(), _smem(), _smem(),
            pl.BlockSpec((1, n_sub, N_KV_HEADS, HEAD_DIM, GQA_GROUP * WINDOW), lambda i, j: (i, j, 0, 0, 0)),
            pl.BlockSpec((1, N_KV_HEADS, s, HEAD_DIM), lambda i, j: (i, 0, 0, 0)),
            pl.BlockSpec((1, N_KV_HEADS, n_blocks, V_ROWS, WINDOW), lambda i, j: (i, 0, 0, 0, 0)),
            pl.BlockSpec((1, n_blocks, WINDOW), lambda i, j: (i, 0, 0)),
            pl.BlockSpec((1, s, 1), lambda i, j: (i, 0, 0)),
            pl.BlockSpec((2, N_HEADS, 1, LANES), lambda i, j: (0, 0, 0, 0)),
            pl.BlockSpec((2, N_HEADS, DIAG_ROWS, LANES), lambda i, j: (0, 0, 0, 0), pipeline_mode=pl.Buffered(1)),
        ],
        out_specs=pl.BlockSpec((1, TQ_B, MIX_WIDTH), lambda i, j: (i, j, 0)),
        out_shape=jax.ShapeDtypeStruct((b, s, MIX_WIDTH), BF16),
        compiler_params=_cparams(("parallel", "parallel")),
        name="win_attn",
    )(small.astype(I32).reshape(1), sink2, runs, qt, kn, vt, positions.reshape(b, n_blocks, WINDOW),
      positions.reshape(b, s, 1), tbl2, diag2)


def _mem_attn_kernel(q_ref, k_ref, v_ref, o_ref):
    for h in range(N_MEM_HEADS):
        sl = slice(h * HEAD_DIM, (h + 1) * HEAD_DIM)
        s = lax.dot_general(q_ref[0, :, sl], k_ref[0, :, sl], (((1,), (1,)), ((), ())), preferred_element_type=F32)
        e = jnp.exp2(s - jnp.max(s, axis=-1, keepdims=True))
        p = (e / jnp.sum(e, axis=-1, keepdims=True)).astype(BF16)
        o_ref[0, :, sl] = jnp.dot(p, v_ref[0, :, sl], preferred_element_type=F32).astype(BF16)


def _mem_attention(qmn, k_m, v_m):
    b, s, _ = qmn.shape
    return pl.pallas_call(
        _mem_attn_kernel,
        grid=(b, s // TQ_MEM),
        in_specs=[
            pl.BlockSpec((1, TQ_MEM, MEM_WIDTH), lambda i, j: (i, j, 0)),
            pl.BlockSpec((1, MEM_LEN, MEM_WIDTH), lambda i, j: (i, 0, 0)),
            pl.BlockSpec((1, MEM_LEN, MEM_WIDTH), lambda i, j: (i, 0, 0)),
        ],
        out_specs=pl.BlockSpec((1, TQ_MEM, MEM_WIDTH), lambda i, j: (i, j, 0)),
        out_shape=jax.ShapeDtypeStruct((b, s, MEM_WIDTH), BF16),
        compiler_params=_cparams(("parallel", "parallel")),
        name="mem_attn",
    )(qmn, k_m, v_m)


def _out_proj_kernel(x_ref, o_ref, om_ref, wo_ref, wm_ref, y_ref):
    y_ref[...] = (x_ref[...]
                  + jnp.dot(o_ref[...], wo_ref[...].astype(BF16), preferred_element_type=F32)
                  + jnp.dot(om_ref[...], wm_ref[...].astype(BF16), preferred_element_type=F32))


def _out_proj(x2d, o2d, om2d, w_stack, layer):
    m, d = x2d.shape
    return pl.pallas_call(
        _out_proj_kernel,
        grid=(m // TM_OUT, d // TN_OUT),
        in_specs=[
            pl.BlockSpec((TM_OUT, TN_OUT), lambda i, j: (i, j)),
            pl.BlockSpec((TM_OUT, MIX_WIDTH), lambda i, j: (i, 0)),
            pl.BlockSpec((TM_OUT, MEM_WIDTH), lambda i, j: (i, 0)),
            pl.BlockSpec((None, MIX_WIDTH, TN_OUT), lambda i, j: (layer, 0, j)),
            pl.BlockSpec((None, MEM_WIDTH, TN_OUT), lambda i, j: (layer, MIX_WIDTH // MEM_WIDTH, j)),
        ],
        out_specs=pl.BlockSpec((TM_OUT, TN_OUT), lambda i, j: (i, j)),
        out_shape=jax.ShapeDtypeStruct((m, d), F32),
        compiler_params=_cparams(("parallel", "parallel")),
        name="out_proj",
    )(x2d, o2d, om2d, w_stack, w_stack)


def _mlp_kernel(x_ref, g_ref, wu_ref, wd_ref, y_ref, h_ref):
    f = pl.program_id(1)

    def mlp_chunk(h):
        u = jnp.maximum(jnp.dot(h, wu_ref[...].astype(BF16), preferred_element_type=F32), 0.0)
        return jnp.dot((u * u).astype(BF16), wd_ref[...].astype(BF16), preferred_element_type=F32)

    @pl.when(f == 0)
    def _():
        half = x_ref.shape[0] // 2
        for rows in (slice(0, half), slice(half, 2 * half)):
            x = x_ref[rows, :]
            r = lax.rsqrt(jnp.mean(x * x, axis=-1, keepdims=True) + RMS_EPS)
            h = ((x * r) * g_ref[...]).astype(BF16)
            h_ref[rows, :] = h
            y_ref[rows, :] = x + mlp_chunk(h)

    @pl.when(f > 0)
    def _():
        y_ref[...] += mlp_chunk(h_ref[...])


def _mlp(x2d, gain, wu_stack, wd_stack, layer):
    m, d = x2d.shape
    return pl.pallas_call(
        _mlp_kernel,
        grid=(m // TM_MLP, D_FF // TF_MLP),
        in_specs=[
            pl.BlockSpec((TM_MLP, d), lambda i, f: (i, 0)),
            pl.BlockSpec((1, d), lambda i, f: (0, 0)),
            pl.BlockSpec((None, d, TF_MLP), lambda i, f: (layer, 0, f)),
            pl.BlockSpec((None, TF_MLP, d), lambda i, f: (layer, f, 0)),
        ],
        out_specs=pl.BlockSpec((TM_MLP, d), lambda i, f: (i, 0)),
        out_shape=jax.ShapeDtypeStruct((m, d), F32),
        scratch_shapes=[pltpu.VMEM((TM_MLP, d), BF16)],
        compiler_params=_cparams(("parallel", "arbitrary")),
        name="mlp",
    )(x2d, gain.reshape(1, d), wu_stack, wd_stack)


def _diag_table_kernel(tbl_ref, o_ref):
    r = lax.broadcasted_iota(I32, (DIAG_ROWS, LANES), 0)
    l = lax.broadcasted_iota(I32, (DIAG_ROWS, LANES), 1)
    bias = _table_lookup(tbl_ref[0], _t5_bucket(r - l - DIAG_CENTER))
    o_ref[0, 0] = bias
    o_ref[1, 0] = jnp.exp2(bias)


def _diag_bias_table(tbl_log2):
    return pl.pallas_call(
        _diag_table_kernel,
        grid=(N_HEADS,),
        in_specs=[pl.BlockSpec((1, 1, LANES), lambda h: (h, 0, 0))],
        out_specs=pl.BlockSpec((2, 1, DIAG_ROWS, LANES), lambda h: (0, h, 0, 0)),
        out_shape=jax.ShapeDtypeStruct((2, N_HEADS, DIAG_ROWS, LANES), F32),
        compiler_params=_cparams(("parallel",)),
        name="diag_bias_table",
    )(tbl_log2)


def _consecutive_runs(positions, ranges):
    b = positions.shape[0]
    step_ok = (positions[:, 1:] - positions[:, :-1] == 1).astype(I32)
    c = jnp.concatenate([jnp.zeros((b, 1), I32), jnp.cumsum(step_ok, axis=1)], axis=1)
    return jnp.stack([(c[:, hi - 1] - c[:, lo] == hi - 1 - lo).astype(I32) for lo, hi in ranges], axis=1)


def _bias_tables(rel_bias):
    t = (rel_bias.astype(F32) * LOG2E).T
    tbl = jnp.zeros((N_HEADS, 1, LANES), F32).at[:, 0, :N_BUCKETS].set(t).at[:, 0, N_BUCKETS].set(NEG_BIG)
    far = jnp.stack([t[:, HALF_BUCKETS - 1], t[:, N_BUCKETS - 1]])
    return tbl, far


def _logit_bound(gq, gk, rel_bias, dim):
    qk = 1.01 * dim * jnp.max(jnp.abs(gq)) * jnp.max(jnp.abs(gk)) * (dim ** -0.5 * LOG2E)
    return jnp.ceil(qk + jnp.max(jnp.abs(rel_bias)) * LOG2E).astype(F32)


def kernel(x, mem, positions, rel_bias, norm_attn, norm_mem, norm_mlp, w_in_a, a_q_norm, a_k_norm, a_lambda_q1, a_lambda_k1, a_lambda_q2, a_lambda_k2, a_subln, w_in_b, b_q_norm, b_k_norm, b_sink, w_mem_kv, m_q_norm, m_k_norm, w_out, w_up, w_down):
    b, s, d = x.shape
    depth = norm_attn.shape[0]
    tbl_log2, far_log2 = _bias_tables(rel_bias)
    tbl_mult = jnp.exp2(tbl_log2)
    diag2 = _diag_bias_table(tbl_log2)
    x2d = x.reshape(b * s, d)
    later_w = (w_out, w_up, w_down, w_in_b)
    for i in range(depth):
        j = i // 2
        k_m, v_m = _mem_kv(mem, norm_mem[i], w_mem_kv, i, m_k_norm[i])
        if i % 2 == 0:
            proj = _norm_proj(x2d, norm_attn[i], w_in_a, j).reshape(b, s, -1)
            shift = _logit_bound(a_q_norm[j], a_k_norm[j], rel_bias, DIFF_QK_DIM)
            bounded = shift <= SHIFT_LIMIT
            qx, kx, vtx, qmn = _prep_a(proj, a_q_norm[j], a_k_norm[j], m_q_norm[i], jnp.where(bounded, -shift, 0.0))
            lam_init = 0.8 - 0.6 * math.exp(-0.3 * i)
            cast_srcs = tuple(w.reshape(-1, w.shape[-1]) for w in later_w) if i == 0 else ()
            attn = functools.partial(
                _diff_attention, qx, kx, vtx, positions,
                lam_vecs=(a_lambda_q1[j], a_lambda_k1[j], a_lambda_q2[j], a_lambda_k2[j]), gsub=a_subln[j],
                lam_init=lam_init, cast_srcs=cast_srcs)
            o, casts = lax.cond(
                bounded,
                lambda: attn(tbl_mult, diag2, far_consts=jnp.exp2(far_log2), bounded=True),
                lambda: attn(tbl_log2, diag2, far_consts=far_log2, bounded=False))
            if i == 0:
                later_w = tuple(c.reshape(w.shape) for c, w in zip(casts, later_w))
                w_out, w_up, w_down, w_in_b = later_w
        else:
            proj = _norm_proj(x2d, norm_attn[i], w_in_b, j).reshape(b, s, -1)
            qt, kn, vt, qmn = _prep_b(proj, b_q_norm[j], b_k_norm[j], m_q_norm[i])
            sink_log2 = b_sink[j].astype(F32) * LOG2E
            small = jnp.logical_and(_logit_bound(b_q_norm[j], b_k_norm[j], rel_bias, HEAD_DIM) <= SHIFT_LIMIT,
                                    jnp.max(jnp.abs(sink_log2)) <= SHIFT_LIMIT)
            o = _win_attention(qt, kn, vt, positions, jnp.stack([tbl_log2, tbl_mult]), diag2,
                               jnp.stack([sink_log2, jnp.exp2(sink_log2)]), small)
        o_m = _mem_attention(qmn, k_m, v_m)
        x2d = _out_proj(x2d, o.reshape(b * s, MIX_WIDTH), o_m.reshape(b * s, MEM_WIDTH), w_out, i)
        x2d = _mlp(x2d, norm_mlp[i], w_up, w_down, i)
    return x2d.reshape(b, s, d)
```

```python
import functools
import math

import jax
import jax.numpy as jnp
from jax import lax
from jax.experimental import pallas as pl
from jax.experimental.pallas import tpu as pltpu

F32 = jnp.float32
BF16 = jnp.bfloat16
I32 = jnp.int32

D_MODEL = 2048
N_HEADS = 12
HEAD_DIM = 128
DIFF_QK_DIM = 64
N_KV_HEADS = 4
GQA_GROUP = 3
MIX_WIDTH = N_HEADS * HEAD_DIM
KV_WIDTH = N_KV_HEADS * HEAD_DIM
WINDOW = 128
N_MEM_HEADS = 4
MEM_WIDTH = N_MEM_HEADS * HEAD_DIM
MEM_LEN = 256
D_FF = 4 * D_MODEL
N_BUCKETS = 32
MAX_DISTANCE = 128
RMS_EPS = 1e-6
NEG_BIG = -1e30
LOG2E = math.log2(math.e)

HALF_BUCKETS = N_BUCKETS // 2
MAX_EXACT = HALF_BUCKETS // 2
FAR_DIST = 91

ONE_COL = DIFF_QK_DIM
V_ROWS = HEAD_DIM + 16
SHIFT_LIMIT = 50.0

LANES = 128

DIAG_ROWS_MAX = 512
DIAG_OFF_MIN = 5
DIAG_CENTER = DIAG_OFF_MIN + DIAG_ROWS_MAX + FAR_DIST
DIAG_OFF_MAX = DIAG_CENTER + FAR_DIST + LANES - 1
DIAG_ROWS = -(-(DIAG_OFF_MAX + DIAG_ROWS_MAX) // 16) * 16

VMEM_LIMIT = 56 * 1024 * 1024

TM_PROJ, TN_PROJ = 1024, 1024
TS_PREP = 512
TQ_A, TK_A = 512, 512
TQ_B = 512
TQ_MEM = 2048
TM_OUT, TN_OUT = 512, 2048
TM_MLP, TF_MLP = 1024, 512


def _cparams(sem):
    return pltpu.CompilerParams(dimension_semantics=sem, vmem_limit_bytes=VMEM_LIMIT)


def _smem():
    return pl.BlockSpec(memory_space=pltpu.SMEM)


def _t5_bucket(rel):
    side = jnp.where(rel > 0, HALF_BUCKETS, 0)
    n = jnp.abs(rel)
    n_f = jnp.maximum(n, 1).astype(F32)
    large = MAX_EXACT + (jnp.log(n_f / MAX_EXACT) / math.log(MAX_DISTANCE / MAX_EXACT)
                         * (HALF_BUCKETS - MAX_EXACT)).astype(I32)
    large = jnp.minimum(large, HALF_BUCKETS - 1)
    return side + jnp.where(n < MAX_EXACT, n, large)


def _table_lookup(tbl_row, bucket):
    rows, cols = bucket.shape
    tb = jnp.broadcast_to(tbl_row, (rows, LANES))
    parts = [jnp.take_along_axis(tb, bucket[:, c:c + LANES], axis=1) for c in range(0, cols, LANES)]
    return parts[0] if len(parts) == 1 else jnp.concatenate(parts, axis=1)


def _group_rms_scale(x, group):
    t = x * x
    if group == LANES:
        return lax.rsqrt(jnp.mean(t, axis=-1, keepdims=True) + RMS_EPS)
    lane = lax.broadcasted_iota(I32, x.shape, 1)
    lo = lane < group
    s_lo = jnp.sum(jnp.where(lo, t, 0.0), axis=-1, keepdims=True)
    s_hi = jnp.sum(jnp.where(lo, 0.0, t), axis=-1, keepdims=True)
    return jnp.where(lo, lax.rsqrt(s_lo / group + RMS_EPS), lax.rsqrt(s_hi / group + RMS_EPS))


def _proj_kernel(x_ref, g_ref, w_ref, o_ref, h_ref):
    def project(h):
        return jnp.dot(h, w_ref[...].astype(BF16), preferred_element_type=F32).astype(BF16)

    @pl.when(pl.program_id(1) == 0)
    def _():
        half = x_ref.shape[0] // 2
        for rows in (slice(0, half), slice(half, 2 * half)):
            x = x_ref[rows, :]
            r = lax.rsqrt(jnp.mean(x * x, axis=-1, keepdims=True) + RMS_EPS)
            h = ((x * r) * g_ref[...]).astype(BF16)
            h_ref[rows, :] = h
            o_ref[rows, :] = project(h)

    @pl.when(pl.program_id(1) > 0)
    def _():
        o_ref[...] = project(h_ref[...])


def _norm_proj(x2d, gain, w_stack, layer):
    m, d = x2d.shape
    n = w_stack.shape[2]
    return pl.pallas_call(
        _proj_kernel,
        grid=(m // TM_PROJ, n // TN_PROJ),
        in_specs=[
            pl.BlockSpec((TM_PROJ, d), lambda i, j: (i, 0)),
            pl.BlockSpec((1, d), lambda i, j: (0, 0)),
            pl.BlockSpec((None, d, TN_PROJ), lambda i, j: (layer, 0, j)),
        ],
        out_specs=pl.BlockSpec((TM_PROJ, TN_PROJ), lambda i, j: (i, j)),
        out_shape=jax.ShapeDtypeStruct((m, n), BF16),
        scratch_shapes=[pltpu.VMEM((TM_PROJ, d), BF16)],
        compiler_params=_cparams(("parallel", "arbitrary")),
        name="norm_proj",
    )(x2d, gain.reshape(1, d), w_stack)


def _mem_kv_kernel(mem_ref, g_ref, w_ref, gk_ref, k_ref, v_ref):
    x = mem_ref[0]
    r = lax.rsqrt(jnp.mean(x * x, axis=-1, keepdims=True) + RMS_EPS)
    mn = ((x * r) * g_ref[...]).astype(BF16)
    mkv = jnp.dot(mn, w_ref[...].astype(BF16), preferred_element_type=F32)
    for h in range(N_MEM_HEADS):
        kh = mkv[:, h * HEAD_DIM:(h + 1) * HEAD_DIM]
        k_ref[0, :, h * HEAD_DIM:(h + 1) * HEAD_DIM] = ((kh * _group_rms_scale(kh, HEAD_DIM)) * gk_ref[...]).astype(BF16)
    v_ref[0] = mkv[:, MEM_WIDTH:].astype(BF16)


def _mem_kv(mem, gain, w_stack, layer, gk):
    b = mem.shape[0]
    shp = jax.ShapeDtypeStruct((b, MEM_LEN, MEM_WIDTH), BF16)
    return pl.pallas_call(
        _mem_kv_kernel,
        grid=(b,),
        in_specs=[
            pl.BlockSpec((1, MEM_LEN, D_MODEL), lambda i: (i, 0, 0)),
            pl.BlockSpec((1, D_MODEL), lambda i: (0, 0)),
            pl.BlockSpec((None, D_MODEL, 2 * MEM_WIDTH), lambda i: (layer, 0, 0)),
            pl.BlockSpec((1, HEAD_DIM), lambda i: (0, 0)),
        ],
        out_specs=[pl.BlockSpec((1, MEM_LEN, MEM_WIDTH), lambda i: (i, 0, 0))] * 2,
        out_shape=[shp, shp],
        compiler_params=_cparams(("parallel",)),
        name="mem_kv",
    )(mem, gain.reshape(1, D_MODEL), w_stack, gk.reshape(1, HEAD_DIM))


def _prep_a_kernel(negm_ref, q_ref, k_ref, v_ref, qm_ref, gq_ref, gk_ref, gm_ref, qx_ref, kx_ref, vtx_ref, qmn_ref):
    q_scale = DIFF_QK_DIM ** -0.5 * LOG2E
    m_scale = HEAD_DIM ** -0.5 * LOG2E
    lane = lax.broadcasted_iota(I32, (TS_PREP, HEAD_DIM), 1)
    row = lax.broadcasted_iota(I32, (HEAD_DIM, TS_PREP), 0)
    lo_lanes, lo_rows = lane < DIFF_QK_DIM, row < DIFF_QK_DIM
    neg_shift = negm_ref[0]
    ones_rows = jnp.where(lax.broadcasted_iota(I32, (V_ROWS - HEAD_DIM, TS_PREP), 0) == 0, 1.0, 0.0).astype(BF16)
    q_fill = (jnp.where(row == ONE_COL, neg_shift, 0.0), jnp.where(row == ONE_COL - 1, neg_shift, 0.0))
    k_fill = (jnp.where(lane == ONE_COL, 1.0, 0.0), jnp.where(lane == ONE_COL - 1, 1.0, 0.0))
    q_gain = gq_ref[...] * q_scale
    for h in range(N_HEADS):
        sl = slice(h * HEAD_DIM, (h + 1) * HEAD_DIM)
        qt = q_ref[0, :, sl].T.astype(F32)
        t = qt * qt
        r_lo = lax.rsqrt(jnp.mean(t[:DIFF_QK_DIM], axis=0, keepdims=True) + RMS_EPS)
        r_hi = lax.rsqrt(jnp.mean(t[DIFF_QK_DIM:], axis=0, keepdims=True) + RMS_EPS)
        qn = (qt * jnp.where(lo_rows, r_lo, r_hi)) * q_gain
        qx_ref[0, 2 * h] = jnp.where(lo_rows, qn, q_fill[0]).astype(BF16)
        qx_ref[0, 2 * h + 1] = jnp.where(lo_rows, q_fill[1], qn).astype(BF16)
        k = k_ref[0, :, sl].astype(F32)
        kn = (k * _group_rms_scale(k, DIFF_QK_DIM)) * gk_ref[...]
        kx_ref[0, 2 * h] = jnp.where(lo_lanes, kn, k_fill[0]).astype(BF16)
        kx_ref[0, 2 * h + 1] = jnp.where(lo_lanes, k_fill[1], kn).astype(BF16)
        vt = v_ref[0, :, sl].T
        vtx_ref[0, h, 0] = jnp.concatenate([vt, ones_rows], axis=0)
    for h in range(N_MEM_HEADS):
        sl = slice(h * HEAD_DIM, (h + 1) * HEAD_DIM)
        qm = qm_ref[0, :, sl].astype(F32)
        qmn_ref[0, :, sl] = ((qm * _group_rms_scale(qm, HEAD_DIM)) * (gm_ref[...] * m_scale)).astype(BF16)


def _prep_a(proj, gq, gk, gm, neg_shift):
    assert TS_PREP == TK_A
    b, s, _ = proj.shape
    gq2 = jnp.concatenate([gq, gq]).reshape(HEAD_DIM, 1)
    gk2 = jnp.concatenate([gk, gk]).reshape(1, HEAD_DIM)
    wblk = MIX_WIDTH // MEM_WIDTH
    return pl.pallas_call(
        _prep_a_kernel,
        grid=(b, s // TS_PREP),
        in_specs=[
            _smem(),
            pl.BlockSpec((1, TS_PREP, MIX_WIDTH), lambda i, j: (i, j, 0)),
            pl.BlockSpec((1, TS_PREP, MIX_WIDTH), lambda i, j: (i, j, 1)),
            pl.BlockSpec((1, TS_PREP, MIX_WIDTH), lambda i, j: (i, j, 2)),
            pl.BlockSpec((1, TS_PREP, MEM_WIDTH), lambda i, j: (i, j, 3 * wblk)),
            pl.BlockSpec((HEAD_DIM, 1), lambda i, j: (0, 0)),
            pl.BlockSpec((1, HEAD_DIM), lambda i, j: (0, 0)),
            pl.BlockSpec((1, HEAD_DIM), lambda i, j: (0, 0)),
        ],
        out_specs=[
            pl.BlockSpec((1, 2 * N_HEADS, HEAD_DIM, TS_PREP), lambda i, j: (i, 0, 0, j)),
            pl.BlockSpec((1, 2 * N_HEADS, TS_PREP, HEAD_DIM), lambda i, j: (i, 0, j, 0)),
            pl.BlockSpec((1, N_HEADS, 1, V_ROWS, TK_A), lambda i, j: (i, 0, j, 0, 0)),
            pl.BlockSpec((1, TS_PREP, MEM_WIDTH), lambda i, j: (i, j, 0)),
        ],
        out_shape=[
            jax.ShapeDtypeStruct((b, 2 * N_HEADS, HEAD_DIM, s), BF16),
            jax.ShapeDtypeStruct((b, 2 * N_HEADS, s, HEAD_DIM), BF16),
            jax.ShapeDtypeStruct((b, N_HEADS, s // TK_A, V_ROWS, TK_A), BF16),
            jax.ShapeDtypeStruct((b, s, MEM_WIDTH), BF16),
        ],
        compiler_params=_cparams(("parallel", "parallel")),
        name="prep_a",
    )(neg_shift.reshape(1), proj, proj, proj, proj, gq2, gk2, gm.reshape(1, HEAD_DIM))


def _diff_attn_kernel(qlo_ref, qhi_ref, klo_ref, khi_ref, qrun_ref, krun_ref,
                      far_ref, lq1_ref, lk1_ref, lq2_ref, lk2_ref,
                      qx_ref, kx_ref, vtx_ref, posq_ref, posk_ref, tbl_ref, diag_ref, gsub_ref,
                      *rest, lam_init, bounded, n_cast):
    w_f32_refs, o_ref, w_bf16_refs = rest[:n_cast], rest[n_cast], rest[n_cast + 1:2 * n_cast + 1]
    acc_ref, m_scratch = rest[2 * n_cast + 1], rest[2 * n_cast + 2:]
    b, iq, j = pl.program_id(0), pl.program_id(1), pl.program_id(2)
    pos_far = klo_ref[b, j] - qhi_ref[b, iq] >= FAR_DIST
    neg_far = khi_ref[b, j] - qlo_ref[b, iq] <= -FAR_DIST
    far = jnp.logical_or(pos_far, neg_far)

    @pl.when(j == 0)
    def _():
        acc_ref[...] = jnp.zeros(acc_ref.shape, F32)
        if not bounded:
            m_scratch[0][...] = jnp.full(m_scratch[0].shape, NEG_BIG, F32)

    def logits(hc):
        return jnp.dot(kx_ref[0, hc], qx_ref[0, hc], preferred_element_type=F32)

    def accumulate(hc, s, far_const, bias):
        h = hc // 2
        if bounded:
            e = jnp.exp2(s)
            if bias is not None:
                e = e * bias
            pv = jnp.dot(vtx_ref[0, h, 0], e.astype(BF16), preferred_element_type=F32)
            acc_ref[hc] += pv if far_const is None else far_const * pv
        else:
            if bias is not None:
                s = s + bias
            m_ref = m_scratch[0]
            off = 0.0 if far_const is None else far_const
            m_old = m_ref[hc]
            m_new = jnp.maximum(m_old, jnp.max(s, axis=0, keepdims=True) + off)
            p = jnp.exp2(s - (m_new - off)).astype(BF16)
            acc_ref[hc] = (jnp.exp2(m_old - m_new) * acc_ref[hc]
                           + jnp.dot(vtx_ref[0, h, 0], p, preferred_element_type=F32))
            m_ref[hc] = m_new

    def all_heads(bias_of_head, far_const_of_head):
        for src, dst in zip(w_f32_refs, w_bf16_refs):
            dst[...] = src[...].astype(BF16)
        s_next = logits(0)
        bias = None
        bias_next = None if bias_of_head is None else bias_of_head(0)
        for hc in range(2 * N_HEADS):
            h, c = divmod(hc, 2)
            s = s_next
            if hc + 1 < 2 * N_HEADS:
                s_next = logits(hc + 1)
            if bias_of_head is not None and c == 0:
                bias = bias_next
                if h + 1 < N_HEADS:
                    bias_next = bias_of_head(h + 1)
            accumulate(hc, s, None if far_const_of_head is None else far_const_of_head(h), bias)

    @pl.when(far)
    def _():
        all_heads(None, lambda h: jnp.where(pos_far, far_ref[1, h], far_ref[0, h]))

    runs = jnp.logical_and(qrun_ref[b, iq] == 1, krun_ref[b, j] == 1)

    @pl.when(jnp.logical_and(jnp.logical_not(far), runs))
    def _():
        d0 = klo_ref[b, j] - qlo_ref[b, iq]

        def bias_of_head(h):
            offs = [jnp.clip(d0 - c + DIAG_CENTER, DIAG_OFF_MIN, DIAG_OFF_MAX) for c in range(0, TQ_A, LANES)]
            return jnp.concatenate([diag_ref[h, pl.ds(off, TK_A), :] for off in offs], axis=1)

        all_heads(bias_of_head, None)

    @pl.when(jnp.logical_and(jnp.logical_not(far), jnp.logical_not(runs)))
    def _():
        bucket = []

        def bias_of_head(h):
            if not bucket:
                bucket.append(_t5_bucket(posk_ref[0] - posq_ref[0]))
            return _table_lookup(tbl_ref[h], bucket[0])

        all_heads(bias_of_head, None)

    @pl.when(j == pl.num_programs(2) - 1)
    def _():
        lam = (jnp.exp(jnp.sum(lq1_ref[...] * lk1_ref[...], axis=-1, keepdims=True))
               - jnp.exp(jnp.sum(lq2_ref[...] * lk2_ref[...], axis=-1, keepdims=True)) + lam_init)
        for h in range(N_HEADS):
            a0, a1 = acc_ref[2 * h], acc_ref[2 * h + 1]
            o_t = (a0[:HEAD_DIM] / a0[HEAD_DIM:HEAD_DIM + 1]
                   - lam * (a1[:HEAD_DIM] / a1[HEAD_DIM:HEAD_DIM + 1]))
            r = lax.rsqrt(jnp.mean(o_t * o_t, axis=0, keepdims=True) + RMS_EPS)
            o_t = ((o_t * r) * gsub_ref[...]) * (1.0 - lam_init)
            o_ref[0, :, h * HEAD_DIM:(h + 1) * HEAD_DIM] = o_t.T.astype(BF16)


def _diff_attention(qx, kx, vtx, positions, tbl, diag2, far_consts, lam_vecs, gsub, lam_init, bounded, cast_srcs):
    assert TK_A <= DIAG_ROWS_MAX
    b, _, s, _ = kx.shape
    nq, nk = s // TQ_A, s // TK_A
    pq = positions.reshape(b, nq, TQ_A)
    pk = positions.reshape(b, nk, TK_A)
    ranges = (pq.min(-1), pq.max(-1), pk.min(-1), pk.max(-1),
              _consecutive_runs(positions, [(t * TQ_A, (t + 1) * TQ_A) for t in range(nq)]),
              _consecutive_runs(positions, [(t * TK_A, (t + 1) * TK_A) for t in range(nk)]))
    lam_spec = pl.BlockSpec((1, DIFF_QK_DIM), lambda i, q, k, *_: (0, 0))
    scratch = [pltpu.VMEM((2 * N_HEADS, V_ROWS, TQ_A), F32)]
    if not bounded:
        scratch.append(pltpu.VMEM((2 * N_HEADS, 1, TQ_A), F32))
    n_steps = b * nq * nk
    assert all(w.shape[0] % (16 * n_steps) == 0 for w in cast_srcs)
    cast_specs = [pl.BlockSpec((w.shape[0] // n_steps, w.shape[1]), lambda i, q, k, *_: ((i * nq + q) * nk + k, 0))
                  for w in cast_srcs]
    grid_spec = pltpu.PrefetchScalarGridSpec(
        num_scalar_prefetch=len(ranges),
        grid=(b, nq, nk),
        in_specs=[
            _smem(), lam_spec, lam_spec, lam_spec, lam_spec,
            pl.BlockSpec((1, 2 * N_HEADS, HEAD_DIM, TQ_A), lambda i, q, k, *_: (i, 0, 0, q)),
            pl.BlockSpec((1, 2 * N_HEADS, TK_A, HEAD_DIM), lambda i, q, k, *_: (i, 0, k, 0)),
            pl.BlockSpec((1, N_HEADS, 1, V_ROWS, TK_A), lambda i, q, k, *_: (i, 0, k, 0, 0)),
            pl.BlockSpec((1, 1, TQ_A), lambda i, q, k, *_: (i, 0, q)),
            pl.BlockSpec((1, TK_A, 1), lambda i, q, k, *_: (i, k, 0)),
            pl.BlockSpec((N_HEADS, 1, LANES), lambda i, q, k, *_: (0, 0, 0)),
            pl.BlockSpec((None, N_HEADS, DIAG_ROWS, LANES), lambda i, q, k, *_: (int(bounded), 0, 0, 0),
                         pipeline_mode=pl.Buffered(1)),
            pl.BlockSpec((HEAD_DIM, 1), lambda i, q, k, *_: (0, 0)),
            *cast_specs,
        ],
        out_specs=[pl.BlockSpec((1, TQ_A, MIX_WIDTH), lambda i, q, k, *_: (i, q, 0)), *cast_specs],
        scratch_shapes=scratch,
    )
    outs = pl.pallas_call(
        functools.partial(_diff_attn_kernel, lam_init=lam_init, bounded=bounded, n_cast=len(cast_srcs)),
        grid_spec=grid_spec,
        out_shape=[jax.ShapeDtypeStruct((b, s, MIX_WIDTH), BF16),
                   *[jax.ShapeDtypeStruct(w.shape, BF16) for w in cast_srcs]],
        compiler_params=_cparams(("arbitrary", "arbitrary", "arbitrary")),
        name="diff_attn_bounded" if bounded else "diff_attn_running_max",
    )(*ranges, far_consts, *[v.reshape(1, DIFF_QK_DIM) for v in lam_vecs],
      qx, kx, vtx, positions.reshape(b, 1, s), positions.reshape(b, s, 1), tbl, diag2,
      gsub.reshape(HEAD_DIM, 1), *cast_srcs)
    return outs[0], tuple(outs[1:])


def _prep_b_kernel(q_ref, k_ref, v_ref, qm_ref, gq_ref, gk_ref, gm_ref, qt_ref, kn_ref, vt_ref, qmn_ref):
    scale = HEAD_DIM ** -0.5 * LOG2E
    n_blk = TS_PREP // WINDOW
    ones_rows = jnp.where(lax.broadcasted_iota(I32, (V_ROWS - HEAD_DIM, TS_PREP), 0) == 0, 1.0, 0.0).astype(BF16)
    for h in range(N_HEADS):
        g, hg = divmod(h, GQA_GROUP)
        qt = q_ref[0, :, h * HEAD_DIM:(h + 1) * HEAD_DIM].T.astype(F32)
        r = lax.rsqrt(jnp.mean(qt * qt, axis=0, keepdims=True) + RMS_EPS)
        qt = ((qt * r) * (gq_ref[...] * scale)).astype(BF16)
        for n in range(n_blk):
            qt_ref[0, n, g, :, hg * WINDOW:(hg + 1) * WINDOW] = qt[:, n * WINDOW:(n + 1) * WINDOW]
    for g in range(N_KV_HEADS):
        sl = slice(g * HEAD_DIM, (g + 1) * HEAD_DIM)
        k = k_ref[0, :, sl].astype(F32)
        kn_ref[0, g] = ((k * _group_rms_scale(k, HEAD_DIM)) * gk_ref[...]).astype(BF16)
        vt = jnp.concatenate([v_ref[0, :, sl].T, ones_rows], axis=0)
        for n in range(n_blk):
            vt_ref[0, g, n] = vt[:, n * WINDOW:(n + 1) * WINDOW]
        qm = qm_ref[0, :, sl].astype(F32)
        qmn_ref[0, :, sl] = ((qm * _group_rms_scale(qm, HEAD_DIM)) * (gm_ref[...] * scale)).astype(BF16)


def _prep_b(proj, gq, gk, gm):
    b, s, _ = proj.shape
    kblk = MIX_WIDTH // KV_WIDTH
    n_blk = TS_PREP // WINDOW
    g = lambda v: v.reshape(1, HEAD_DIM)
    return pl.pallas_call(
        _prep_b_kernel,
        grid=(b, s // TS_PREP),
        in_specs=[
            pl.BlockSpec((1, TS_PREP, MIX_WIDTH), lambda i, j: (i, j, 0)),
            pl.BlockSpec((1, TS_PREP, KV_WIDTH), lambda i, j: (i, j, kblk)),
            pl.BlockSpec((1, TS_PREP, KV_WIDTH), lambda i, j: (i, j, kblk + 1)),
            pl.BlockSpec((1, TS_PREP, MEM_WIDTH), lambda i, j: (i, j, kblk + 2)),
            pl.BlockSpec((HEAD_DIM, 1), lambda i, j: (0, 0)),
            pl.BlockSpec((1, HEAD_DIM), lambda i, j: (0, 0)),
            pl.BlockSpec((1, HEAD_DIM), lambda i, j: (0, 0)),
        ],
        out_specs=[
            pl.BlockSpec((1, n_blk, N_KV_HEADS, HEAD_DIM, GQA_GROUP * WINDOW), lambda i, j: (i, j, 0, 0, 0)),
            pl.BlockSpec((1, N_KV_HEADS, TS_PREP, HEAD_DIM), lambda i, j: (i, 0, j, 0)),
            pl.BlockSpec((1, N_KV_HEADS, n_blk, V_ROWS, WINDOW), lambda i, j: (i, 0, j, 0, 0)),
            pl.BlockSpec((1, TS_PREP, MEM_WIDTH), lambda i, j: (i, j, 0)),
        ],
        out_shape=[
            jax.ShapeDtypeStruct((b, s // WINDOW, N_KV_HEADS, HEAD_DIM, GQA_GROUP * WINDOW), BF16),
            jax.ShapeDtypeStruct((b, N_KV_HEADS, s, HEAD_DIM), BF16),
            jax.ShapeDtypeStruct((b, N_KV_HEADS, s // WINDOW, V_ROWS, WINDOW), BF16),
            jax.ShapeDtypeStruct((b, s, MEM_WIDTH), BF16),
        ],
        compiler_params=_cparams(("parallel", "parallel")),
        name="prep_b",
    )(proj, proj, proj, proj, gq.reshape(HEAD_DIM, 1), g(gk), g(gm))


def _win_attn_kernel(small_ref, sink_ref, run_ref, qt_ref, k_ref, vt_ref, posq_ref, posk_ref, tbl_ref, diag_ref,
                     o_ref):
    n_blocks = posq_ref.shape[1]
    n_sub = o_ref.shape[1] // WINDOW
    kw = 3 * WINDOW
    lane3 = lax.broadcasted_iota(I32, (1, GQA_GROUP * WINDOW), 1)

    def window(nl):
        n = pl.program_id(1) * n_sub + nl
        nb = jnp.clip(n - 1, 0, n_blocks - 3)
        return n, nb, pl.multiple_of(nb * WINDOW, WINDOW)

    def logits(nl, g):
        start = window(nl)[2]
        return jnp.dot(k_ref[0, g, pl.ds(start, kw), :], qt_ref[0, nl, g], preferred_element_type=F32)

    def in_window(nl):
        n, _, start = window(nl)
        ki = start + lax.broadcasted_iota(I32, (kw, WINDOW), 0)
        qi = n * WINDOW + lax.broadcasted_iota(I32, (kw, WINDOW), 1)
        return jnp.abs(ki - qi) <= WINDOW

    def masked_bucket(nl):
        n, _, start = window(nl)
        rel = posk_ref[0, pl.ds(start, kw), :] - posq_ref[0, pl.ds(n, 1), :]
        return jnp.where(in_window(nl), _t5_bucket(rel), N_BUCKETS)

    def bias_from_bucket(bucket, nl, h, bounded):
        return _table_lookup(tbl_ref[int(bounded), h], bucket)

    def bias_from_diag(valid, nl, h, bounded):
        n, _, start = window(nl)
        off = pl.multiple_of(start - n * WINDOW + DIAG_CENTER, 16)
        return jnp.where(valid, diag_ref[int(bounded), h, pl.ds(off, kw), :], 0.0 if bounded else NEG_BIG)

    def finish(nl, g, s, block_state, head_bias, bounded):
        nb = window(nl)[1]
        heads = range(g * GQA_GROUP, (g + 1) * GQA_GROUP)
        bias = jnp.concatenate([head_bias(block_state, nl, h, bounded) for h in heads], axis=1)
        sinks = [sink_ref[int(bounded), h] for h in heads]
        sink = jnp.where(lane3 < WINDOW, sinks[0], jnp.where(lane3 < 2 * WINDOW, sinks[1], sinks[2]))
        if bounded:
            p = (jnp.exp2(s) * bias).astype(BF16)
            sink_term = sink
        else:
            z = s + bias
            m = jnp.maximum(jnp.max(z, axis=0, keepdims=True), sink)
            p = jnp.exp2(z - m).astype(BF16)
            sink_term = jnp.exp2(sink - m)
        vt = jnp.concatenate([vt_ref[0, g, nb + t] for t in range(3)], axis=1)
        acc = jnp.dot(vt, p, preferred_element_type=F32)
        o_t = acc[:HEAD_DIM] / (acc[HEAD_DIM:HEAD_DIM + 1] + sink_term)
        for hg, h in enumerate(heads):
            o_ref[0, nl * WINDOW:(nl + 1) * WINDOW, h * HEAD_DIM:(h + 1) * HEAD_DIM] = (
                o_t[:, hg * WINDOW:(hg + 1) * WINDOW].T.astype(BF16))

    def all_chains(block_state_of, head_bias, bounded):
        chains = [(nl, g) for nl in range(n_sub) for g in range(N_KV_HEADS)]
        s_next = logits(*chains[0])
        block_state = None
        for idx, (nl, g) in enumerate(chains):
            s = s_next
            if idx + 1 < len(chains):
                s_next = logits(*chains[idx + 1])
            if g == 0:
                block_state = block_state_of(nl)
            finish(nl, g, s, block_state, head_bias, bounded)

    is_run = run_ref[pl.program_id(0), pl.program_id(1)] == 1
    is_small = small_ref[0] == 1
    for bounded in (True, False):
        mode = is_small if bounded else jnp.logical_not(is_small)
        pl.when(jnp.logical_and(mode, is_run))(
            functools.partial(all_chains, in_window, bias_from_diag, bounded))
        pl.when(jnp.logical_and(mode, jnp.logical_not(is_run)))(
            functools.partial(all_chains, masked_bucket, bias_from_bucket, bounded))


def _win_attention(qt, kn, vt, positions, tbl2, diag2, sink2, small):
    b, _, s, _ = kn.shape
    n_blocks = s // WINDOW
    n_sub = TQ_B // WINDOW
    runs = _consecutive_runs(positions, [(max(0, t * TQ_B - WINDOW), min(s, (t + 1) * TQ_B + WINDOW))
                                         for t in range(s // TQ_B)])
    return pl.pallas_call(
        _win_attn_kernel,
        grid=(b, s // TQ_B),
        in_specs=[
            _smem(), _smem(), _smem(),
            pl.BlockSpec((1, n_sub, N_KV_HEADS, HEAD_DIM, GQA_GROUP * WINDOW), lambda i, j: (i, j, 0, 0, 0)),
            pl.BlockSpec((1, N_KV_HEADS, s, HEAD_DIM), lambda i, j: (i, 0, 0, 0)),
            pl.BlockSpec((1, N_KV_HEADS, n_blocks, V_ROWS, WINDOW), lambda i, j: (i, 0, 0, 0, 0)),
            pl.BlockSpec((1, n_blocks, WINDOW), lambda i, j: (i, 0, 0)),
            pl.BlockSpec((1, s, 1), lambda i, j: (i, 0, 0)),
            pl.BlockSpec((2, N_HEADS, 1, LANES), lambda i, j: (0, 0, 0, 0)),
            pl.BlockSpec((2, N_HEADS, DIAG_ROWS, LANES), lambda i, j: (0, 0, 0, 0), pipeline_mode=pl.Buffered(1)),
        ],
        out_specs=pl.BlockSpec((1, TQ_B, MIX_WIDTH), lambda i, j: (i, j, 0)),
        out_shape=jax.ShapeDtypeStruct((b, s, MIX_WIDTH), BF16),
        compiler_params=_cparams(("parallel", "parallel")),
        name="win_attn",
    )(small.astype(I32).reshape(1), sink2, runs, qt, kn, vt, positions.reshape(b, n_blocks, WINDOW),
      positions.reshape(b, s, 1), tbl2, diag2)


def _mem_attn_kernel(q_ref, k_ref, v_ref, o_ref):
    for h in range(N_MEM_HEADS):
        sl = slice(h * HEAD_DIM, (h + 1) * HEAD_DIM)
        s = lax.dot_general(q_ref[0, :, sl], k_ref[0, :, sl], (((1,), (1,)), ((), ())), preferred_element_type=F32)
        e = jnp.exp2(s - jnp.max(s, axis=-1, keepdims=True))
        p = (e / jnp.sum(e, axis=-1, keepdims=True)).astype(BF16)
        o_ref[0, :, sl] = jnp.dot(p, v_ref[0, :, sl], preferred_element_type=F32).astype(BF16)


def _mem_attention(qmn, k_m, v_m):
    b, s, _ = qmn.shape
    return pl.pallas_call(
        _mem_attn_kernel,
        grid=(b, s // TQ_MEM),
        in_specs=[
            pl.BlockSpec((1, TQ_MEM, MEM_WIDTH), lambda i, j: (i, j, 0)),
            pl.BlockSpec((1, MEM_LEN, MEM_WIDTH), lambda i, j: (i, 0, 0)),
            pl.BlockSpec((1, MEM_LEN, MEM_WIDTH), lambda i, j: (i, 0, 0)),
        ],
        out_specs=pl.BlockSpec((1, TQ_MEM, MEM_WIDTH), lambda i, j: (i, j, 0)),
        out_shape=jax.ShapeDtypeStruct((b, s, MEM_WIDTH), BF16),
        compiler_params=_cparams(("parallel", "parallel")),
        name="mem_attn",
    )(qmn, k_m, v_m)


def _out_proj_kernel(x_ref, o_ref, om_ref, wo_ref, wm_ref, y_ref):
    y_ref[...] = (x_ref[...]
                  + jnp.dot(o_ref[...], wo_ref[...].astype(BF16), preferred_element_type=F32)
                  + jnp.dot(om_ref[...], wm_ref[...].astype(BF16), preferred_element_type=F32))


def _out_proj(x2d, o2d, om2d, w_stack, layer):
    m, d = x2d.shape
    return pl.pallas_call(
        _out_proj_kernel,
        grid=(m // TM_OUT, d // TN_OUT),
        in_specs=[
            pl.BlockSpec((TM_OUT, TN_OUT), lambda i, j: (i, j)),
            pl.BlockSpec((TM_OUT, MIX_WIDTH), lambda i, j: (i, 0)),
            pl.BlockSpec((TM_OUT, MEM_WIDTH), lambda i, j: (i, 0)),
            pl.BlockSpec((None, MIX_WIDTH, TN_OUT), lambda i, j: (layer, 0, j)),
            pl.BlockSpec((None, MEM_WIDTH, TN_OUT), lambda i, j: (layer, MIX_WIDTH // MEM_WIDTH, j)),
        ],
        out_specs=pl.BlockSpec((TM_OUT, TN_OUT), lambda i, j: (i, j)),
        out_shape=jax.ShapeDtypeStruct((m, d), F32),
        compiler_params=_cparams(("parallel", "parallel")),
        name="out_proj",
    )(x2d, o2d, om2d, w_stack, w_stack)


def _mlp_kernel(x_ref, g_ref, wu_ref, wd_ref, y_ref, h_ref):
    f = pl.program_id(1)

    def mlp_chunk(h):
        u = jnp.maximum(jnp.dot(h, wu_ref[...].astype(BF16), preferred_element_type=F32), 0.0)
        return jnp.dot((u * u).astype(BF16), wd_ref[...].astype(BF16), preferred_element_type=F32)

    @pl.when(f == 0)
    def _():
        half = x_ref.shape[0] // 2
        for rows in (slice(0, half), slice(half, 2 * half)):
            x = x_ref[rows, :]
            r = lax.rsqrt(jnp.mean(x * x, axis=-1, keepdims=True) + RMS_EPS)
            h = ((x * r) * g_ref[...]).astype(BF16)
            h_ref[rows, :] = h
            y_ref[rows, :] = x + mlp_chunk(h)

    @pl.when(f > 0)
    def _():
        y_ref[...] += mlp_chunk(h_ref[...])


def _mlp(x2d, gain, wu_stack, wd_stack, layer):
    m, d = x2d.shape
    return pl.pallas_call(
        _mlp_kernel,
        grid=(m // TM_MLP, D_FF // TF_MLP),
        in_specs=[
            pl.BlockSpec((TM_MLP, d), lambda i, f: (i, 0)),
            pl.BlockSpec((1, d), lambda i, f: (0, 0)),
            pl.BlockSpec((None, d, TF_MLP), lambda i, f: (layer, 0, f)),
            pl.BlockSpec((None, TF_MLP, d), lambda i, f: (layer, f, 0)),
        ],
        out_specs=pl.BlockSpec((TM_MLP, d), lambda i, f: (i, 0)),
        out_shape=jax.ShapeDtypeStruct((m, d), F32),
        scratch_shapes=[pltpu.VMEM((TM_MLP, d), BF16)],
        compiler_params=_cparams(("parallel", "arbitrary")),
        name="mlp",
    )(x2d, gain.reshape(1, d), wu_stack, wd_stack)


def _diag_table_kernel(tbl_ref, o_ref):
    r = lax.broadcasted_iota(I32, (DIAG_ROWS, LANES), 0)
    l = lax.broadcasted_iota(I32, (DIAG_ROWS, LANES), 1)
    bias = _table_lookup(tbl_ref[0], _t5_bucket(r - l - DIAG_CENTER))
    o_ref[0, 0] = bias
    o_ref[1, 0] = jnp.exp2(bias)


def _diag_bias_table(tbl_log2):
    return pl.pallas_call(
        _diag_table_kernel,
        grid=(N_HEADS,),
        in_specs=[pl.BlockSpec((1, 1, LANES), lambda h: (h, 0, 0))],
        out_specs=pl.BlockSpec((2, 1, DIAG_ROWS, LANES), lambda h: (0, h, 0, 0)),
        out_shape=jax.ShapeDtypeStruct((2, N_HEADS, DIAG_ROWS, LANES), F32),
        compiler_params=_cparams(("parallel",)),
        name="diag_bias_table",
    )(tbl_log2)


def _consecutive_runs(positions, ranges):
    b = positions.shape[0]
    step_ok = (positions[:, 1:] - positions[:, :-1] == 1).astype(I32)
    c = jnp.concatenate([jnp.zeros((b, 1), I32), jnp.cumsum(step_ok, axis=1)], axis=1)
    return jnp.stack([(c[:, hi - 1] - c[:, lo] == hi - 1 - lo).astype(I32) for lo, hi in ranges], axis=1)


def _bias_tables(rel_bias):
    t = (rel_bias.astype(F32) * LOG2E).T
    tbl = jnp.zeros((N_HEADS, 1, LANES), F32).at[:, 0, :N_BUCKETS].set(t).at[:, 0, N_BUCKETS].set(NEG_BIG)
    far = jnp.stack([t[:, HALF_BUCKETS - 1], t[:, N_BUCKETS - 1]])
    return tbl, far


def _logit_bound(gq, gk, rel_bias, dim):
    qk = 1.01 * dim * jnp.max(jnp.abs(gq)) * jnp.max(jnp.abs(gk)) * (dim ** -0.5 * LOG2E)
    return jnp.ceil(qk + jnp.max(jnp.abs(rel_bias)) * LOG2E).astype(F32)


def kernel(x, mem, positions, rel_bias, norm_attn, norm_mem, norm_mlp, w_in_a, a_q_norm, a_k_norm, a_lambda_q1, a_lambda_k1, a_lambda_q2, a_lambda_k2, a_subln, w_in_b, b_q_norm, b_k_norm, b_sink, w_mem_kv, m_q_norm, m_k_norm, w_out, w_up, w_down):
    b, s, d = x.shape
    depth = norm_attn.shape[0]
    tbl_log2, far_log2 = _bias_tables(rel_bias)
    tbl_mult = jnp.exp2(tbl_log2)
    diag2 = _diag_bias_table(tbl_log2)
    x2d = x.reshape(b * s, d)
    later_w = (w_out, w_up, w_down, w_in_b)
    for i in range(depth):
        j = i // 2
        k_m, v_m = _mem_kv(mem, norm_mem[i], w_mem_kv, i, m_k_norm[i])
        if i % 2 == 0:
            proj = _norm_proj(x2d, norm_attn[i], w_in_a, j).reshape(b, s, -1)
            shift = _logit_bound(a_q_norm[j], a_k_norm[j], rel_bias, DIFF_QK_DIM)
            bounded = shift <= SHIFT_LIMIT
            qx, kx, vtx, qmn = _prep_a(proj, a_q_norm[j], a_k_norm[j], m_q_norm[i], jnp.where(bounded, -shift, 0.0))
            lam_init = 0.8 - 0.6 * math.exp(-0.3 * i)
            cast_srcs = tuple(w.reshape(-1, w.shape[-1]) for w in later_w) if i == 0 else ()
            attn = functools.partial(
                _diff_attention, qx, kx, vtx, positions,
                lam_vecs=(a_lambda_q1[j], a_lambda_k1[j], a_lambda_q2[j], a_lambda_k2[j]), gsub=a_subln[j],
                lam_init=lam_init, cast_srcs=cast_srcs)
            o, casts = lax.cond(
                bounded,
                lambda: attn(tbl_mult, diag2, far_consts=jnp.exp2(far_log2), bounded=True),
                lambda: attn(tbl_log2, diag2, far_consts=far_log2, bounded=False))
            if i == 0:
                later_w = tuple(c.reshape(w.shape) for c, w in zip(casts, later_w))
                w_out, w_up, w_down, w_in_b = later_w
        else:
            proj = _norm_proj(x2d, norm_attn[i], w_in_b, j).reshape(b, s, -1)
            qt, kn, vt, qmn = _prep_b(proj, b_q_norm[j], b_k_norm[j], m_q_norm[i])
            sink_log2 = b_sink[j].astype(F32) * LOG2E
            small = jnp.logical_and(_logit_bound(b_q_norm[j], b_k_norm[j], rel_bias, HEAD_DIM) <= SHIFT_LIMIT,
                                    jnp.max(jnp.abs(sink_log2)) <= SHIFT_LIMIT)
            o = _win_attention(qt, kn, vt, positions, jnp.stack([tbl_log2, tbl_mult]), diag2,
                               jnp.stack([sink_log2, jnp.exp2(sink_log2)]), small)
        o_m = _mem_attention(qmn, k_m, v_m)
        x2d = _out_proj(x2d, o.reshape(b * s, MIX_WIDTH), o_m.reshape(b * s, MEM_WIDTH), w_out, i)
        x2d = _mlp(x2d, norm_mlp[i], w_up, w_down, i)
    return x2d.reshape(b, s, d)
```

```python
import functools
import math

import jax
import jax.numpy as jnp
from jax import lax
from jax.experimental import pallas as pl
from jax.experimental.pallas import tpu as pltpu

F32 = jnp.float32
BF16 = jnp.bfloat16
I32 = jnp.int32

D_MODEL = 2048
N_HEADS = 12
HEAD_DIM = 128
DIFF_QK_DIM = 64
N_KV_HEADS = 4
GQA_GROUP = 3
MIX_WIDTH = N_HEADS * HEAD_DIM
KV_WIDTH = N_KV_HEADS * HEAD_DIM
WINDOW = 128
N_MEM_HEADS = 4
MEM_WIDTH = N_MEM_HEADS * HEAD_DIM
MEM_LEN = 256
D_FF = 4 * D_MODEL
N_BUCKETS = 32
MAX_DISTANCE = 128
RMS_EPS = 1e-6
NEG_BIG = -1e30
LOG2E = math.log2(math.e)

HALF_BUCKETS = N_BUCKETS // 2
MAX_EXACT = HALF_BUCKETS // 2
FAR_DIST = 91

ONE_COL = DIFF_QK_DIM
V_ROWS = HEAD_DIM + 16
SHIFT_LIMIT = 50.0

LANES = 128

DIAG_ROWS_MAX = 512
DIAG_OFF_MIN = 5
DIAG_CENTER = DIAG_OFF_MIN + DIAG_ROWS_MAX + FAR_DIST
DIAG_OFF_MAX = DIAG_CENTER + FAR_DIST + LANES - 1
DIAG_ROWS = -(-(DIAG_OFF_MAX + DIAG_ROWS_MAX) // 16) * 16

VMEM_LIMIT = 56 * 1024 * 1024

TM_PROJ, TN_PROJ = 1024, (1536, 1280, 1024)
TS_PREP = 512
TQ_A, TK_A = 512, 512
TQ_B = 512
TQ_MEM = 2048
TM_OUT, TN_OUT = 512, 2048
TM_MLP, TF_MLP = 1024, 512


def _cparams(sem):
    return pltpu.CompilerParams(dimension_semantics=sem, vmem_limit_bytes=VMEM_LIMIT)


def _smem():
    return pl.BlockSpec(memory_space=pltpu.SMEM)


def _t5_bucket(rel):
    side = jnp.where(rel > 0, HALF_BUCKETS, 0)
    n = jnp.abs(rel)
    n_f = jnp.maximum(n, 1).astype(F32)
    large = MAX_EXACT + (jnp.log(n_f / MAX_EXACT) / math.log(MAX_DISTANCE / MAX_EXACT)
                         * (HALF_BUCKETS - MAX_EXACT)).astype(I32)
    large = jnp.minimum(large, HALF_BUCKETS - 1)
    return side + jnp.where(n < MAX_EXACT, n, large)


def _table_lookup(tbl_row, bucket):
    rows, cols = bucket.shape
    tb = jnp.broadcast_to(tbl_row, (rows, LANES))
    parts = [jnp.take_along_axis(tb, bucket[:, c:c + LANES], axis=1) for c in range(0, cols, LANES)]
    return parts[0] if len(parts) == 1 else jnp.concatenate(parts, axis=1)


def _group_rms_scale(x, group):
    t = x * x
    if group == LANES:
        return lax.rsqrt(jnp.mean(t, axis=-1, keepdims=True) + RMS_EPS)
    lane = lax.broadcasted_iota(I32, x.shape, 1)
    lo = lane < group
    s_lo = jnp.sum(jnp.where(lo, t, 0.0), axis=-1, keepdims=True)
    s_hi = jnp.sum(jnp.where(lo, 0.0, t), axis=-1, keepdims=True)
    return jnp.where(lo, lax.rsqrt(s_lo / group + RMS_EPS), lax.rsqrt(s_hi / group + RMS_EPS))


def _proj_kernel(x_ref, g_ref, w_ref, o_ref, h_ref):
    def project(h):
        return jnp.dot(h, w_ref[...].astype(BF16), preferred_element_type=F32).astype(BF16)

    @pl.when(pl.program_id(1) == 0)
    def _():
        half = x_ref.shape[0] // 2
        for rows in (slice(0, half), slice(half, 2 * half)):
            x = x_ref[rows, :]
            r = lax.rsqrt(jnp.mean(x * x, axis=-1, keepdims=True) + RMS_EPS)
            h = ((x * r) * g_ref[...]).astype(BF16)
            h_ref[rows, :] = h
            o_ref[rows, :] = project(h)

    @pl.when(pl.program_id(1) > 0)
    def _():
        o_ref[...] = project(h_ref[...])


def _norm_proj(x2d, gain, w_stack, layer):
    m, d = x2d.shape
    n = w_stack.shape[2]
    tn = next(t for t in TN_PROJ if n % t == 0)
    return pl.pallas_call(
        _proj_kernel,
        grid=(m // TM_PROJ, n // tn),
        in_specs=[
            pl.BlockSpec((TM_PROJ, d), lambda i, j: (i, 0)),
            pl.BlockSpec((1, d), lambda i, j: (0, 0)),
            pl.BlockSpec((None, d, tn), lambda i, j: (layer, 0, j)),
        ],
        out_specs=pl.BlockSpec((TM_PROJ, tn), lambda i, j: (i, j)),
        out_shape=jax.ShapeDtypeStruct((m, n), BF16),
        scratch_shapes=[pltpu.VMEM((TM_PROJ, d), BF16)],
        compiler_params=_cparams(("parallel", "arbitrary")),
        name="norm_proj",
    )(x2d, gain.reshape(1, d), w_stack)


def _mem_kv_kernel(mem_ref, g_ref, w_ref, gk_ref, k_ref, v_ref):
    x = mem_ref[0]
    r = lax.rsqrt(jnp.mean(x * x, axis=-1, keepdims=True) + RMS_EPS)
    mn = ((x * r) * g_ref[...]).astype(BF16)
    mkv = jnp.dot(mn, w_ref[...].astype(BF16), preferred_element_type=F32)
    for h in range(N_MEM_HEADS):
        kh = mkv[:, h * HEAD_DIM:(h + 1) * HEAD_DIM]
        k_ref[0, :, h * HEAD_DIM:(h + 1) * HEAD_DIM] = ((kh * _group_rms_scale(kh, HEAD_DIM)) * gk_ref[...]).astype(BF16)
    v_ref[0] = mkv[:, MEM_WIDTH:].astype(BF16)


def _mem_kv(mem, gain, w_stack, layer, gk):
    b = mem.shape[0]
    shp = jax.ShapeDtypeStruct((b, MEM_LEN, MEM_WIDTH), BF16)
    return pl.pallas_call(
        _mem_kv_kernel,
        grid=(b,),
        in_specs=[
            pl.BlockSpec((1, MEM_LEN, D_MODEL), lambda i: (i, 0, 0)),
            pl.BlockSpec((1, D_MODEL), lambda i: (0, 0)),
            pl.BlockSpec((None, D_MODEL, 2 * MEM_WIDTH), lambda i: (layer, 0, 0)),
            pl.BlockSpec((1, HEAD_DIM), lambda i: (0, 0)),
        ],
        out_specs=[pl.BlockSpec((1, MEM_LEN, MEM_WIDTH), lambda i: (i, 0, 0))] * 2,
        out_shape=[shp, shp],
        compiler_params=_cparams(("parallel",)),
        name="mem_kv",
    )(mem, gain.reshape(1, D_MODEL), w_stack, gk.reshape(1, HEAD_DIM))


def _prep_a_kernel(negm_ref, q_ref, k_ref, v_ref, qm_ref, gq_ref, gk_ref, gm_ref, qx_ref, kx_ref, vtx_ref, qmn_ref):
    q_scale = DIFF_QK_DIM ** -0.5 * LOG2E
    m_scale = HEAD_DIM ** -0.5 * LOG2E
    lane = lax.broadcasted_iota(I32, (TS_PREP, HEAD_DIM), 1)
    row = lax.broadcasted_iota(I32, (HEAD_DIM, TS_PREP), 0)
    lo_lanes, lo_rows = lane < DIFF_QK_DIM, row < DIFF_QK_DIM
    neg_shift = negm_ref[0]
    ones_rows = jnp.where(lax.broadcasted_iota(I32, (V_ROWS - HEAD_DIM, TS_PREP), 0) == 0, 1.0, 0.0).astype(BF16)
    q_fill = (jnp.where(row == ONE_COL, neg_shift, 0.0), jnp.where(row == ONE_COL - 1, neg_shift, 0.0))
    k_fill = (jnp.where(lane == ONE_COL, 1.0, 0.0), jnp.where(lane == ONE_COL - 1, 1.0, 0.0))
    q_gain = gq_ref[...] * q_scale
    for h in range(N_HEADS):
        sl = slice(h * HEAD_DIM, (h + 1) * HEAD_DIM)
        qt = q_ref[0, :, sl].T.astype(F32)
        t = qt * qt
        r_lo = lax.rsqrt(jnp.mean(t[:DIFF_QK_DIM], axis=0, keepdims=True) + RMS_EPS)
        r_hi = lax.rsqrt(jnp.mean(t[DIFF_QK_DIM:], axis=0, keepdims=True) + RMS_EPS)
        qn = (qt * jnp.where(lo_rows, r_lo, r_hi)) * q_gain
        qx_ref[0, 2 * h] = jnp.where(lo_rows, qn, q_fill[0]).astype(BF16)
        qx_ref[0, 2 * h + 1] = jnp.where(lo_rows, q_fill[1], qn).astype(BF16)
        k = k_ref[0, :, sl].astype(F32)
        kn = (k * _group_rms_scale(k, DIFF_QK_DIM)) * gk_ref[...]
        kx_ref[0, 2 * h] = jnp.where(lo_lanes, kn, k_fill[0]).astype(BF16)
        kx_ref[0, 2 * h + 1] = jnp.where(lo_lanes, k_fill[1], kn).astype(BF16)
        vt = v_ref[0, :, sl].T
        vtx_ref[0, h, 0] = jnp.concatenate([vt, ones_rows], axis=0)
    for h in range(N_MEM_HEADS):
        sl = slice(h * HEAD_DIM, (h + 1) * HEAD_DIM)
        qm = qm_ref[0, :, sl].astype(F32)
        qmn_ref[0, :, sl] = ((qm * _group_rms_scale(qm, HEAD_DIM)) * (gm_ref[...] * m_scale)).astype(BF16)


def _prep_a(proj, gq, gk, gm, neg_shift):
    assert TS_PREP == TK_A
    b, s, _ = proj.shape
    gq2 = jnp.concatenate([gq, gq]).reshape(HEAD_DIM, 1)
    gk2 = jnp.concatenate([gk, gk]).reshape(1, HEAD_DIM)
    wblk = MIX_WIDTH // MEM_WIDTH
    return pl.pallas_call(
        _prep_a_kernel,
        grid=(b, s // TS_PREP),
        in_specs=[
            _smem(),
            pl.BlockSpec((1, TS_PREP, MIX_WIDTH), lambda i, j: (i, j, 0)),
            pl.BlockSpec((1, TS_PREP, MIX_WIDTH), lambda i, j: (i, j, 1)),
            pl.BlockSpec((1, TS_PREP, MIX_WIDTH), lambda i, j: (i, j, 2)),
            pl.BlockSpec((1, TS_PREP, MEM_WIDTH), lambda i, j: (i, j, 3 * wblk)),
            pl.BlockSpec((HEAD_DIM, 1), lambda i, j: (0, 0)),
            pl.BlockSpec((1, HEAD_DIM), lambda i, j: (0, 0)),
            pl.BlockSpec((1, HEAD_DIM), lambda i, j: (0, 0)),
        ],
        out_specs=[
            pl.BlockSpec((1, 2 * N_HEADS, HEAD_DIM, TS_PREP), lambda i, j: (i, 0, 0, j)),
            pl.BlockSpec((1, 2 * N_HEADS, TS_PREP, HEAD_DIM), lambda i, j: (i, 0, j, 0)),
            pl.BlockSpec((1, N_HEADS, 1, V_ROWS, TK_A), lambda i, j: (i, 0, j, 0, 0)),
            pl.BlockSpec((1, TS_PREP, MEM_WIDTH), lambda i, j: (i, j, 0)),
        ],
        out_shape=[
            jax.ShapeDtypeStruct((b, 2 * N_HEADS, HEAD_DIM, s), BF16),
            jax.ShapeDtypeStruct((b, 2 * N_HEADS, s, HEAD_DIM), BF16),
            jax.ShapeDtypeStruct((b, N_HEADS, s // TK_A, V_ROWS, TK_A), BF16),
            jax.ShapeDtypeStruct((b, s, MEM_WIDTH), BF16),
        ],
        compiler_params=_cparams(("parallel", "parallel")),
        name="prep_a",
    )(neg_shift.reshape(1), proj, proj, proj, proj, gq2, gk2, gm.reshape(1, HEAD_DIM))


def _diff_attn_kernel(qlo_ref, qhi_ref, klo_ref, khi_ref, qrun_ref, krun_ref,
                      far_ref, lq1_ref, lk1_ref, lq2_ref, lk2_ref,
                      qx_ref, kx_ref, vtx_ref, posq_ref, posk_ref, tbl_ref, diag_ref, gsub_ref,
                      *rest, lam_init, bounded, n_cast):
    w_f32_refs, o_ref, w_bf16_refs = rest[:n_cast], rest[n_cast], rest[n_cast + 1:2 * n_cast + 1]
    acc_ref, m_scratch = rest[2 * n_cast + 1], rest[2 * n_cast + 2:]
    b, iq, j = pl.program_id(0), pl.program_id(1), pl.program_id(2)
    pos_far = klo_ref[b, j] - qhi_ref[b, iq] >= FAR_DIST
    neg_far = khi_ref[b, j] - qlo_ref[b, iq] <= -FAR_DIST
    far = jnp.logical_or(pos_far, neg_far)

    @pl.when(j == 0)
    def _():
        acc_ref[...] = jnp.zeros(acc_ref.shape, F32)
        if not bounded:
            m_scratch[0][...] = jnp.full(m_scratch[0].shape, NEG_BIG, F32)

    def logits(hc):
        return jnp.dot(kx_ref[0, hc], qx_ref[0, hc], preferred_element_type=F32)

    def accumulate(hc, s, far_const, bias):
        h = hc // 2
        if bounded:
            e = jnp.exp2(s)
            if bias is not None:
                e = e * bias
            pv = jnp.dot(vtx_ref[0, h, 0], e.astype(BF16), preferred_element_type=F32)
            acc_ref[hc] += pv if far_const is None else far_const * pv
        else:
            if bias is not None:
                s = s + bias
            m_ref = m_scratch[0]
            off = 0.0 if far_const is None else far_const
            m_old = m_ref[hc]
            m_new = jnp.maximum(m_old, jnp.max(s, axis=0, keepdims=True) + off)
            p = jnp.exp2(s - (m_new - off)).astype(BF16)
            acc_ref[hc] = (jnp.exp2(m_old - m_new) * acc_ref[hc]
                           + jnp.dot(vtx_ref[0, h, 0], p, preferred_element_type=F32))
            m_ref[hc] = m_new

    def all_heads(bias_of_head, far_const_of_head):
        for src, dst in zip(w_f32_refs, w_bf16_refs):
            dst[...] = src[...].astype(BF16)
        s_next = logits(0)
        bias = None
        bias_next = None if bias_of_head is None else bias_of_head(0)
        for hc in range(2 * N_HEADS):
            h, c = divmod(hc, 2)
            s = s_next
            if hc + 1 < 2 * N_HEADS:
                s_next = logits(hc + 1)
            if bias_of_head is not None and c == 0:
                bias = bias_next
                if h + 1 < N_HEADS:
                    bias_next = bias_of_head(h + 1)
            accumulate(hc, s, None if far_const_of_head is None else far_const_of_head(h), bias)

    @pl.when(far)
    def _():
        all_heads(None, lambda h: jnp.where(pos_far, far_ref[1, h], far_ref[0, h]))

    runs = jnp.logical_and(qrun_ref[b, iq] == 1, krun_ref[b, j] == 1)

    @pl.when(jnp.logical_and(jnp.logical_not(far), runs))
    def _():
        d0 = klo_ref[b, j] - qlo_ref[b, iq]

        def bias_of_head(h):
            offs = [jnp.clip(d0 - c + DIAG_CENTER, DIAG_OFF_MIN, DIAG_OFF_MAX) for c in range(0, TQ_A, LANES)]
            return jnp.concatenate([diag_ref[h, pl.ds(off, TK_A), :] for off in offs], axis=1)

        all_heads(bias_of_head, None)

    @pl.when(jnp.logical_and(jnp.logical_not(far), jnp.logical_not(runs)))
    def _():
        bucket = []

        def bias_of_head(h):
            if not bucket:
                bucket.append(_t5_bucket(posk_ref[0] - posq_ref[0]))
            return _table_lookup(tbl_ref[h], bucket[0])

        all_heads(bias_of_head, None)

    @pl.when(j == pl.num_programs(2) - 1)
    def _():
        lam = (jnp.exp(jnp.sum(lq1_ref[...] * lk1_ref[...], axis=-1, keepdims=True))
               - jnp.exp(jnp.sum(lq2_ref[...] * lk2_ref[...], axis=-1, keepdims=True)) + lam_init)
        for h in range(N_HEADS):
            a0, a1 = acc_ref[2 * h], acc_ref[2 * h + 1]
            o_t = (a0[:HEAD_DIM] / a0[HEAD_DIM:HEAD_DIM + 1]
                   - lam * (a1[:HEAD_DIM] / a1[HEAD_DIM:HEAD_DIM + 1]))
            r = lax.rsqrt(jnp.mean(o_t * o_t, axis=0, keepdims=True) + RMS_EPS)
            o_t = ((o_t * r) * gsub_ref[...]) * (1.0 - lam_init)
            o_ref[0, :, h * HEAD_DIM:(h + 1) * HEAD_DIM] = o_t.T.astype(BF16)


def _diff_attention(qx, kx, vtx, positions, tbl, diag2, far_consts, lam_vecs, gsub, lam_init, bounded, cast_srcs):
    assert TK_A <= DIAG_ROWS_MAX
    b, _, s, _ = kx.shape
    nq, nk = s // TQ_A, s // TK_A
    pq = positions.reshape(b, nq, TQ_A)
    pk = positions.reshape(b, nk, TK_A)
    ranges = (pq.min(-1), pq.max(-1), pk.min(-1), pk.max(-1),
              _consecutive_runs(positions, [(t * TQ_A, (t + 1) * TQ_A) for t in range(nq)]),
              _consecutive_runs(positions, [(t * TK_A, (t + 1) * TK_A) for t in range(nk)]))
    lam_spec = pl.BlockSpec((1, DIFF_QK_DIM), lambda i, q, k, *_: (0, 0))
    scratch = [pltpu.VMEM((2 * N_HEADS, V_ROWS, TQ_A), F32)]
    if not bounded:
        scratch.append(pltpu.VMEM((2 * N_HEADS, 1, TQ_A), F32))
    n_steps = b * nq * nk
    assert all(w.shape[0] % (16 * n_steps) == 0 for w in cast_srcs)
    cast_specs = [pl.BlockSpec((w.shape[0] // n_steps, w.shape[1]), lambda i, q, k, *_: ((i * nq + q) * nk + k, 0))
                  for w in cast_srcs]
    grid_spec = pltpu.PrefetchScalarGridSpec(
        num_scalar_prefetch=len(ranges),
        grid=(b, nq, nk),
        in_specs=[
            _smem(), lam_spec, lam_spec, lam_spec, lam_spec,
            pl.BlockSpec((1, 2 * N_HEADS, HEAD_DIM, TQ_A), lambda i, q, k, *_: (i, 0, 0, q)),
            pl.BlockSpec((1, 2 * N_HEADS, TK_A, HEAD_DIM), lambda i, q, k, *_: (i, 0, k, 0)),
            pl.BlockSpec((1, N_HEADS, 1, V_ROWS, TK_A), lambda i, q, k, *_: (i, 0, k, 0, 0)),
            pl.BlockSpec((1, 1, TQ_A), lambda i, q, k, *_: (i, 0, q)),
            pl.BlockSpec((1, TK_A, 1), lambda i, q, k, *_: (i, k, 0)),
            pl.BlockSpec((N_HEADS, 1, LANES), lambda i, q, k, *_: (0, 0, 0)),
            pl.BlockSpec((None, N_HEADS, DIAG_ROWS, LANES), lambda i, q, k, *_: (int(bounded), 0, 0, 0),
                         pipeline_mode=pl.Buffered(1)),
            pl.BlockSpec((HEAD_DIM, 1), lambda i, q, k, *_: (0, 0)),
            *cast_specs,
        ],
        out_specs=[pl.BlockSpec((1, TQ_A, MIX_WIDTH), lambda i, q, k, *_: (i, q, 0)), *cast_specs],
        scratch_shapes=scratch,
    )
    outs = pl.pallas_call(
        functools.partial(_diff_attn_kernel, lam_init=lam_init, bounded=bounded, n_cast=len(cast_srcs)),
        grid_spec=grid_spec,
        out_shape=[jax.ShapeDtypeStruct((b, s, MIX_WIDTH), BF16),
                   *[jax.ShapeDtypeStruct(w.shape, BF16) for w in cast_srcs]],
        compiler_params=_cparams(("arbitrary", "arbitrary", "arbitrary")),
        name="diff_attn_bounded" if bounded else "diff_attn_running_max",
    )(*ranges, far_consts, *[v.reshape(1, DIFF_QK_DIM) for v in lam_vecs],
      qx, kx, vtx, positions.reshape(b, 1, s), positions.reshape(b, s, 1), tbl, diag2,
      gsub.reshape(HEAD_DIM, 1), *cast_srcs)
    return outs[0], tuple(outs[1:])


def _prep_b_kernel(q_ref, k_ref, v_ref, qm_ref, gq_ref, gk_ref, gm_ref, qt_ref, kn_ref, vt_ref, qmn_ref):
    scale = HEAD_DIM ** -0.5 * LOG2E
    n_blk = TS_PREP // WINDOW
    ones_rows = jnp.where(lax.broadcasted_iota(I32, (V_ROWS - HEAD_DIM, TS_PREP), 0) == 0, 1.0, 0.0).astype(BF16)
    for h in range(N_HEADS):
        g, hg = divmod(h, GQA_GROUP)
        qt = q_ref[0, :, h * HEAD_DIM:(h + 1) * HEAD_DIM].T.astype(F32)
        r = lax.rsqrt(jnp.mean(qt * qt, axis=0, keepdims=True) + RMS_EPS)
        qt = ((qt * r) * (gq_ref[...] * scale)).astype(BF16)
        for n in range(n_blk):
            qt_ref[0, n, g, :, hg * WINDOW:(hg + 1) * WINDOW] = qt[:, n * WINDOW:(n + 1) * WINDOW]
    for g in range(N_KV_HEADS):
        sl = slice(g * HEAD_DIM, (g + 1) * HEAD_DIM)
        k = k_ref[0, :, sl].astype(F32)
        kn_ref[0, g] = ((k * _group_rms_scale(k, HEAD_DIM)) * gk_ref[...]).astype(BF16)
        vt = jnp.concatenate([v_ref[0, :, sl].T, ones_rows], axis=0)
        for n in range(n_blk):
            vt_ref[0, g, n] = vt[:, n * WINDOW:(n + 1) * WINDOW]
        qm = qm_ref[0, :, sl].astype(F32)
        qmn_ref[0, :, sl] = ((qm * _group_rms_scale(qm, HEAD_DIM)) * (gm_ref[...] * scale)).astype(BF16)


def _prep_b(proj, gq, gk, gm):
    b, s, _ = proj.shape
    kblk = MIX_WIDTH // KV_WIDTH
    n_blk = TS_PREP // WINDOW
    g = lambda v: v.reshape(1, HEAD_DIM)
    return pl.pallas_call(
        _prep_b_kernel,
        grid=(b, s // TS_PREP),
        in_specs=[
            pl.BlockSpec((1, TS_PREP, MIX_WIDTH), lambda i, j: (i, j, 0)),
            pl.BlockSpec((1, TS_PREP, KV_WIDTH), lambda i, j: (i, j, kblk)),
            pl.BlockSpec((1, TS_PREP, KV_WIDTH), lambda i, j: (i, j, kblk + 1)),
            pl.BlockSpec((1, TS_PREP, MEM_WIDTH), lambda i, j: (i, j, kblk + 2)),
            pl.BlockSpec((HEAD_DIM, 1), lambda i, j: (0, 0)),
            pl.BlockSpec((1, HEAD_DIM), lambda i, j: (0, 0)),
            pl.BlockSpec((1, HEAD_DIM), lambda i, j: (0, 0)),
        ],
        out_specs=[
            pl.BlockSpec((1, n_blk, N_KV_HEADS, HEAD_DIM, GQA_GROUP * WINDOW), lambda i, j: (i, j, 0, 0, 0)),
            pl.BlockSpec((1, N_KV_HEADS, TS_PREP, HEAD_DIM), lambda i, j: (i, 0, j, 0)),
            pl.BlockSpec((1, N_KV_HEADS, n_blk, V_ROWS, WINDOW), lambda i, j: (i, 0, j, 0, 0)),
            pl.BlockSpec((1, TS_PREP, MEM_WIDTH), lambda i, j: (i, j, 0)),
        ],
        out_shape=[
            jax.ShapeDtypeStruct((b, s // WINDOW, N_KV_HEADS, HEAD_DIM, GQA_GROUP * WINDOW), BF16),
            jax.ShapeDtypeStruct((b, N_KV_HEADS, s, HEAD_DIM), BF16),
            jax.ShapeDtypeStruct((b, N_KV_HEADS, s // WINDOW, V_ROWS, WINDOW), BF16),
            jax.ShapeDtypeStruct((b, s, MEM_WIDTH), BF16),
        ],
        compiler_params=_cparams(("parallel", "parallel")),
        name="prep_b",
    )(proj, proj, proj, proj, gq.reshape(HEAD_DIM, 1), g(gk), g(gm))


def _win_attn_kernel(small_ref, sink_ref, run_ref, qt_ref, k_ref, vt_ref, posq_ref, posk_ref, tbl_ref, diag_ref,
                     o_ref):
    n_blocks = posq_ref.shape[1]
    n_sub = o_ref.shape[1] // WINDOW
    kw = 3 * WINDOW
    lane3 = lax.broadcasted_iota(I32, (1, GQA_GROUP * WINDOW), 1)

    def window(nl):
        n = pl.program_id(1) * n_sub + nl
        nb = jnp.clip(n - 1, 0, n_blocks - 3)
        return n, nb, pl.multiple_of(nb * WINDOW, WINDOW)

    def logits(nl, g):
        start = window(nl)[2]
        return jnp.dot(k_ref[0, g, pl.ds(start, kw), :], qt_ref[0, nl, g], preferred_element_type=F32)

    def in_window(nl):
        n, _, start = window(nl)
        ki = start + lax.broadcasted_iota(I32, (kw, WINDOW), 0)
        qi = n * WINDOW + lax.broadcasted_iota(I32, (kw, WINDOW), 1)
        return jnp.abs(ki - qi) <= WINDOW

    def masked_bucket(nl):
        n, _, start = window(nl)
        rel = posk_ref[0, pl.ds(start, kw), :] - posq_ref[0, pl.ds(n, 1), :]
        return jnp.where(in_window(nl), _t5_bucket(rel), N_BUCKETS)

    def bias_from_bucket(bucket, nl, h, bounded):
        return _table_lookup(tbl_ref[int(bounded), h], bucket)

    def bias_from_diag(valid, nl, h, bounded):
        n, _, start = window(nl)
        off = pl.multiple_of(start - n * WINDOW + DIAG_CENTER, 16)
        return jnp.where(valid, diag_ref[int(bounded), h, pl.ds(off, kw), :], 0.0 if bounded else NEG_BIG)

    def finish(nl, g, s, block_state, head_bias, bounded):
        nb = window(nl)[1]
        heads = range(g * GQA_GROUP, (g + 1) * GQA_GROUP)
        bias = jnp.concatenate([head_bias(block_state, nl, h, bounded) for h in heads], axis=1)
        sinks = [sink_ref[int(bounded), h] for h in heads]
        sink = jnp.where(lane3 < WINDOW, sinks[0], jnp.where(lane3 < 2 * WINDOW, sinks[1], sinks[2]))
        if bounded:
            p = (jnp.exp2(s) * bias).astype(BF16)
            sink_term = sink
        else:
            z = s + bias
            m = jnp.maximum(jnp.max(z, axis=0, keepdims=True), sink)
            p = jnp.exp2(z - m).astype(BF16)
            sink_term = jnp.exp2(sink - m)
        vt = jnp.concatenate([vt_ref[0, g, nb + t] for t in range(3)], axis=1)
        acc = jnp.dot(vt, p, preferred_element_type=F32)
        o_t = acc[:HEAD_DIM] / (acc[HEAD_DIM:HEAD_DIM + 1] + sink_term)
        for hg, h in enumerate(heads):
            o_ref[0, nl * WINDOW:(nl + 1) * WINDOW, h * HEAD_DIM:(h + 1) * HEAD_DIM] = (
                o_t[:, hg * WINDOW:(hg + 1) * WINDOW].T.astype(BF16))

    def all_chains(block_state_of, head_bias, bounded):
        chains = [(nl, g) for nl in range(n_sub) for g in range(N_KV_HEADS)]
        s_next = logits(*chains[0])
        block_state = None
        for idx, (nl, g) in enumerate(chains):
            s = s_next
            if idx + 1 < len(chains):
                s_next = logits(*chains[idx + 1])
            if g == 0:
                block_state = block_state_of(nl)
            finish(nl, g, s, block_state, head_bias, bounded)

    is_run = run_ref[pl.program_id(0), pl.program_id(1)] == 1
    is_small = small_ref[0] == 1
    for bounded in (True, False):
        mode = is_small if bounded else jnp.logical_not(is_small)
        pl.when(jnp.logical_and(mode, is_run))(
            functools.partial(all_chains, in_window, bias_from_diag, bounded))
        pl.when(jnp.logical_and(mode, jnp.logical_not(is_run)))(
            functools.partial(all_chains, masked_bucket, bias_from_bucket, bounded))


def _win_attention(qt, kn, vt, positions, tbl2, diag2, sink2, small):
    b, _, s, _ = kn.shape
    n_blocks = s // WINDOW
    n_sub = TQ_B // WINDOW
    runs = _consecutive_runs(positions, [(max(0, t * TQ_B - WINDOW), min(s, (t + 1) * TQ_B + WINDOW))
                                         for t in range(s // TQ_B)])
    return pl.pallas_call(
        _win_attn_kernel,
        grid=(b, s // TQ_B),
        in_specs=[
            _smem(), _smem(), _smem(),
            pl.BlockSpec((1, n_sub, N_KV_HEADS, HEAD_DIM, GQA_GROUP * WINDOW), lambda i, j: (i, j, 0, 0, 0)),
            pl.BlockSpec((1, N_KV_HEADS, s, HEAD_DIM), lambda i, j: (i, 0, 0, 0)),
            pl.BlockSpec((1, N_KV_HEADS, n_blocks, V_ROWS, WINDOW), lambda i, j: (i, 0, 0, 0, 0)),
            pl.BlockSpec((1, n_blocks, WINDOW), lambda i, j: (i, 0, 0)),
            pl.BlockSpec((1, s, 1), lambda i, j: (i, 0, 0)),
            pl.BlockSpec((2, N_HEADS, 1, LANES), lambda i, j: (0, 0, 0, 0)),
            pl.BlockSpec((2, N_HEADS, DIAG_ROWS, LANES), lambda i, j: (0, 0, 0, 0), pipeline_mode=pl.Buffered(1)),
        ],
        out_specs=pl.BlockSpec((1, TQ_B, MIX_WIDTH), lambda i, j: (i, j, 0)),
        out_shape=jax.ShapeDtypeStruct((b, s, MIX_WIDTH), BF16),
        compiler_params=_cparams(("parallel", "parallel")),
        name="win_attn",
    )(small.astype(I32).reshape(1), sink2, runs, qt, kn, vt, positions.reshape(b, n_blocks, WINDOW),
      positions.reshape(b, s, 1), tbl2, diag2)


def _mem_attn_kernel(q_ref, k_ref, v_ref, o_ref):
    for h in range(N_MEM_HEADS):
        sl = slice(h * HEAD_DIM, (h + 1) * HEAD_DIM)
        s = lax.dot_general(q_ref[0, :, sl], k_ref[0, :, sl], (((1,), (1,)), ((), ())), preferred_element_type=F32)
        e = jnp.exp2(s - jnp.max(s, axis=-1, keepdims=True))
        p = (e / jnp.sum(e, axis=-1, keepdims=True)).astype(BF16)
        o_ref[0, :, sl] = jnp.dot(p, v_ref[0, :, sl], preferred_element_type=F32).astype(BF16)


def _mem_attention(qmn, k_m, v_m):
    b, s, _ = qmn.shape
    return pl.pallas_call(
        _mem_attn_kernel,
        grid=(b, s // TQ_MEM),
        in_specs=[
            pl.BlockSpec((1, TQ_MEM, MEM_WIDTH), lambda i, j: (i, j, 0)),
            pl.BlockSpec((1, MEM_LEN, MEM_WIDTH), lambda i, j: (i, 0, 0)),
            pl.BlockSpec((1, MEM_LEN, MEM_WIDTH), lambda i, j: (i, 0, 0)),
        ],
        out_specs=pl.BlockSpec((1, TQ_MEM, MEM_WIDTH), lambda i, j: (i, j, 0)),
        out_shape=jax.ShapeDtypeStruct((b, s, MEM_WIDTH), BF16),
        compiler_params=_cparams(("parallel", "parallel")),
        name="mem_attn",
    )(qmn, k_m, v_m)


def _out_proj_kernel(x_ref, o_ref, om_ref, wo_ref, wm_ref, y_ref):
    y_ref[...] = (x_ref[...]
                  + jnp.dot(o_ref[...], wo_ref[...].astype(BF16), preferred_element_type=F32)
                  + jnp.dot(om_ref[...], wm_ref[...].astype(BF16), preferred_element_type=F32))


def _out_proj(x2d, o2d, om2d, w_stack, layer):
    m, d = x2d.shape
    return pl.pallas_call(
        _out_proj_kernel,
        grid=(m // TM_OUT, d // TN_OUT),
        in_specs=[
            pl.BlockSpec((TM_OUT, TN_OUT), lambda i, j: (i, j)),
            pl.BlockSpec((TM_OUT, MIX_WIDTH), lambda i, j: (i, 0)),
            pl.BlockSpec((TM_OUT, MEM_WIDTH), lambda i, j: (i, 0)),
            pl.BlockSpec((None, MIX_WIDTH, TN_OUT), lambda i, j: (layer, 0, j)),
            pl.BlockSpec((None, MEM_WIDTH, TN_OUT), lambda i, j: (layer, MIX_WIDTH // MEM_WIDTH, j)),
        ],
        out_specs=pl.BlockSpec((TM_OUT, TN_OUT), lambda i, j: (i, j)),
        out_shape=jax.ShapeDtypeStruct((m, d), F32),
        compiler_params=_cparams(("parallel", "parallel")),
        name="out_proj",
    )(x2d, o2d, om2d, w_stack, w_stack)


def _mlp_kernel(x_ref, g_ref, wu_ref, wd_ref, y_ref, h_ref):
    f = pl.program_id(1)

    def mlp_chunk(h):
        u = jnp.maximum(jnp.dot(h, wu_ref[...].astype(BF16), preferred_element_type=F32), 0.0)
        return jnp.dot((u * u).astype(BF16), wd_ref[...].astype(BF16), preferred_element_type=F32)

    @pl.when(f == 0)
    def _():
        half = x_ref.shape[0] // 2
        for rows in (slice(0, half), slice(half, 2 * half)):
            x = x_ref[rows, :]
            r = lax.rsqrt(jnp.mean(x * x, axis=-1, keepdims=True) + RMS_EPS)
            h = ((x * r) * g_ref[...]).astype(BF16)
            h_ref[rows, :] = h
            y_ref[rows, :] = x + mlp_chunk(h)

    @pl.when(f > 0)
    def _():
        y_ref[...] += mlp_chunk(h_ref[...])


def _mlp(x2d, gain, wu_stack, wd_stack, layer):
    m, d = x2d.shape
    return pl.pallas_call(
        _mlp_kernel,
        grid=(m // TM_MLP, D_FF // TF_MLP),
        in_specs=[
            pl.BlockSpec((TM_MLP, d), lambda i, f: (i, 0)),
            pl.BlockSpec((1, d), lambda i, f: (0, 0)),
            pl.BlockSpec((None, d, TF_MLP), lambda i, f: (layer, 0, f)),
            pl.BlockSpec((None, TF_MLP, d), lambda i, f: (layer, f, 0)),
        ],
        out_specs=pl.BlockSpec((TM_MLP, d), lambda i, f: (i, 0)),
        out_shape=jax.ShapeDtypeStruct((m, d), F32),
        scratch_shapes=[pltpu.VMEM((TM_MLP, d), BF16)],
        compiler_params=_cparams(("parallel", "arbitrary")),
        name="mlp",
    )(x2d, gain.reshape(1, d), wu_stack, wd_stack)


def _diag_table_kernel(tbl_ref, o_ref):
    r = lax.broadcasted_iota(I32, (DIAG_ROWS, LANES), 0)
    l = lax.broadcasted_iota(I32, (DIAG_ROWS, LANES), 1)
    bias = _table_lookup(tbl_ref[0], _t5_bucket(r - l - DIAG_CENTER))
    o_ref[0, 0] = bias
    o_ref[1, 0] = jnp.exp2(bias)


def _diag_bias_table(tbl_log2):
    return pl.pallas_call(
        _diag_table_kernel,
        grid=(N_HEADS,),
        in_specs=[pl.BlockSpec((1, 1, LANES), lambda h: (h, 0, 0))],
        out_specs=pl.BlockSpec((2, 1, DIAG_ROWS, LANES), lambda h: (0, h, 0, 0)),
        out_shape=jax.ShapeDtypeStruct((2, N_HEADS, DIAG_ROWS, LANES), F32),
        compiler_params=_cparams(("parallel",)),
        name="diag_bias_table",
    )(tbl_log2)


def _consecutive_runs(positions, ranges):
    b = positions.shape[0]
    step_ok = (positions[:, 1:] - positions[:, :-1] == 1).astype(I32)
    c = jnp.concatenate([jnp.zeros((b, 1), I32), jnp.cumsum(step_ok, axis=1)], axis=1)
    return jnp.stack([(c[:, hi - 1] - c[:, lo] == hi - 1 - lo).astype(I32) for lo, hi in ranges], axis=1)


def _bias_tables(rel_bias):
    t = (rel_bias.astype(F32) * LOG2E).T
    tbl = jnp.zeros((N_HEADS, 1, LANES), F32).at[:, 0, :N_BUCKETS].set(t).at[:, 0, N_BUCKETS].set(NEG_BIG)
    far = jnp.stack([t[:, HALF_BUCKETS - 1], t[:, N_BUCKETS - 1]])
    return tbl, far


def _logit_bound(gq, gk, rel_bias, dim):
    qk = 1.01 * dim * jnp.max(jnp.abs(gq)) * jnp.max(jnp.abs(gk)) * (dim ** -0.5 * LOG2E)
    return jnp.ceil(qk + jnp.max(jnp.abs(rel_bias)) * LOG2E).astype(F32)


def kernel(x, mem, positions, rel_bias, norm_attn, norm_mem, norm_mlp, w_in_a, a_q_norm, a_k_norm, a_lambda_q1, a_lambda_k1, a_lambda_q2, a_lambda_k2, a_subln, w_in_b, b_q_norm, b_k_norm, b_sink, w_mem_kv, m_q_norm, m_k_norm, w_out, w_up, w_down):
    b, s, d = x.shape
    depth = norm_attn.shape[0]
    tbl_log2, far_log2 = _bias_tables(rel_bias)
    tbl_mult = jnp.exp2(tbl_log2)
    diag2 = _diag_bias_table(tbl_log2)
    x2d = x.reshape(b * s, d)
    later_w = (w_out, w_up, w_down, w_in_b)
    for i in range(depth):
        j = i // 2
        k_m, v_m = _mem_kv(mem, norm_mem[i], w_mem_kv, i, m_k_norm[i])
        if i % 2 == 0:
            proj = _norm_proj(x2d, norm_attn[i], w_in_a, j).reshape(b, s, -1)
            shift = _logit_bound(a_q_norm[j], a_k_norm[j], rel_bias, DIFF_QK_DIM)
            bounded = shift <= SHIFT_LIMIT
            qx, kx, vtx, qmn = _prep_a(proj, a_q_norm[j], a_k_norm[j], m_q_norm[i], jnp.where(bounded, -shift, 0.0))
            lam_init = 0.8 - 0.6 * math.exp(-0.3 * i)
            cast_srcs = tuple(w.reshape(-1, w.shape[-1]) for w in later_w) if i == 0 else ()
            attn = functools.partial(
                _diff_attention, qx, kx, vtx, positions,
                lam_vecs=(a_lambda_q1[j], a_lambda_k1[j], a_lambda_q2[j], a_lambda_k2[j]), gsub=a_subln[j],
                lam_init=lam_init, cast_srcs=cast_srcs)
            o, casts = lax.cond(
                bounded,
                lambda: attn(tbl_mult, diag2, far_consts=jnp.exp2(far_log2), bounded=True),
                lambda: attn(tbl_log2, diag2, far_consts=far_log2, bounded=False))
            if i == 0:
                later_w = tuple(c.reshape(w.shape) for c, w in zip(casts, later_w))
                w_out, w_up, w_down, w_in_b = later_w
        else:
            proj = _norm_proj(x2d, norm_attn[i], w_in_b, j).reshape(b, s, -1)
            qt, kn, vt, qmn = _prep_b(proj, b_q_norm[j], b_k_norm[j], m_q_norm[i])
            sink_log2 = b_sink[j].astype(F32) * LOG2E
            small = jnp.logical_and(_logit_bound(b_q_norm[j], b_k_norm[j], rel_bias, HEAD_DIM) <= SHIFT_LIMIT,
                                    jnp.max(jnp.abs(sink_log2)) <= SHIFT_LIMIT)
            o = _win_attention(qt, kn, vt, positions, jnp.stack([tbl_log2, tbl_mult]), diag2,
                               jnp.stack([sink_log2, jnp.exp2(sink_log2)]), small)
        o_m = _mem_attention(qmn, k_m, v_m)
        x2d = _out_proj(x2d, o.reshape(b * s, MIX_WIDTH), o_m.reshape(b * s, MEM_WIDTH), w_out, i)
        x2d = _mlp(x2d, norm_mlp[i], w_up, w_down, i)
    return x2d.reshape(b, s, d)
```

```python
import functools
import math

import jax
import jax.numpy as jnp
from jax import lax
from jax.experimental import pallas as pl
from jax.experimental.pallas import tpu as pltpu

F32 = jnp.float32
BF16 = jnp.bfloat16
I32 = jnp.int32

D_MODEL = 2048
N_HEADS = 12
HEAD_DIM = 128
DIFF_QK_DIM = 64
N_KV_HEADS = 4
GQA_GROUP = 3
MIX_WIDTH = N_HEADS * HEAD_DIM
KV_WIDTH = N_KV_HEADS * HEAD_DIM
WINDOW = 128
N_MEM_HEADS = 4
MEM_WIDTH = N_MEM_HEADS * HEAD_DIM
MEM_LEN = 256
D_FF = 4 * D_MODEL
N_BUCKETS = 32
MAX_DISTANCE = 128
RMS_EPS = 1e-6
NEG_BIG = -1e30
LOG2E = math.log2(math.e)

HALF_BUCKETS = N_BUCKETS // 2
MAX_EXACT = HALF_BUCKETS // 2
FAR_DIST = 91

ONE_COL = DIFF_QK_DIM
V_ROWS = HEAD_DIM + 16
SHIFT_LIMIT = 50.0

LANES = 128

DIAG_ROWS_MAX = 512
DIAG_OFF_MIN = 5
DIAG_CENTER = DIAG_OFF_MIN + DIAG_ROWS_MAX + FAR_DIST
DIAG_OFF_MAX = DIAG_CENTER + FAR_DIST + LANES - 1
DIAG_ROWS = -(-(DIAG_OFF_MAX + DIAG_ROWS_MAX) // 16) * 16

VMEM_LIMIT = 56 * 1024 * 1024

TM_PROJ, TN_PROJ = 1024, (1536, 1280, 1024)
TS_PREP = 512
TQ_A, TK_A = 512, 512
TQ_B = 512
TQ_MEM = 2048
TM_OUT, TN_OUT = 512, 2048
TM_MLP, TF_MLP = 1024, 512
FIRST_STEP_CHUNKS = 4


def _cparams(sem):
    return pltpu.CompilerParams(dimension_semantics=sem, vmem_limit_bytes=VMEM_LIMIT)


def _smem():
    return pl.BlockSpec(memory_space=pltpu.SMEM)


def _t5_bucket(rel):
    side = jnp.where(rel > 0, HALF_BUCKETS, 0)
    n = jnp.abs(rel)
    n_f = jnp.maximum(n, 1).astype(F32)
    large = MAX_EXACT + (jnp.log(n_f / MAX_EXACT) / math.log(MAX_DISTANCE / MAX_EXACT)
                         * (HALF_BUCKETS - MAX_EXACT)).astype(I32)
    large = jnp.minimum(large, HALF_BUCKETS - 1)
    return side + jnp.where(n < MAX_EXACT, n, large)


def _table_lookup(tbl_row, bucket):
    rows, cols = bucket.shape
    tb = jnp.broadcast_to(tbl_row, (rows, LANES))
    parts = [jnp.take_along_axis(tb, bucket[:, c:c + LANES], axis=1) for c in range(0, cols, LANES)]
    return parts[0] if len(parts) == 1 else jnp.concatenate(parts, axis=1)


def _group_rms_scale(x, group):
    t = x * x
    if group == LANES:
        return lax.rsqrt(jnp.mean(t, axis=-1, keepdims=True) + RMS_EPS)
    lane = lax.broadcasted_iota(I32, x.shape, 1)
    lo = lane < group
    s_lo = jnp.sum(jnp.where(lo, t, 0.0), axis=-1, keepdims=True)
    s_hi = jnp.sum(jnp.where(lo, 0.0, t), axis=-1, keepdims=True)
    return jnp.where(lo, lax.rsqrt(s_lo / group + RMS_EPS), lax.rsqrt(s_hi / group + RMS_EPS))


def _proj_kernel(x_ref, g_ref, w_ref, o_ref, h_ref):
    def project(h):
        return jnp.dot(h, w_ref[...].astype(BF16), preferred_element_type=F32).astype(BF16)

    @pl.when(pl.program_id(1) == 0)
    def _():
        chunk = x_ref.shape[0] // FIRST_STEP_CHUNKS
        for rows in (slice(c * chunk, (c + 1) * chunk) for c in range(FIRST_STEP_CHUNKS)):
            x = x_ref[rows, :]
            r = lax.rsqrt(jnp.mean(x * x, axis=-1, keepdims=True) + RMS_EPS)
            h = ((x * r) * g_ref[...]).astype(BF16)
            h_ref[rows, :] = h
            o_ref[rows, :] = project(h)

    @pl.when(pl.program_id(1) > 0)
    def _():
        o_ref[...] = project(h_ref[...])


def _norm_proj(x2d, gain, w_stack, layer):
    m, d = x2d.shape
    n = w_stack.shape[2]
    tn = next(t for t in TN_PROJ if n % t == 0)
    return pl.pallas_call(
        _proj_kernel,
        grid=(m // TM_PROJ, n // tn),
        in_specs=[
            pl.BlockSpec((TM_PROJ, d), lambda i, j: (i, 0)),
            pl.BlockSpec((1, d), lambda i, j: (0, 0)),
            pl.BlockSpec((None, d, tn), lambda i, j: (layer, 0, j)),
        ],
        out_specs=pl.BlockSpec((TM_PROJ, tn), lambda i, j: (i, j)),
        out_shape=jax.ShapeDtypeStruct((m, n), BF16),
        scratch_shapes=[pltpu.VMEM((TM_PROJ, d), BF16)],
        compiler_params=_cparams(("parallel", "arbitrary")),
        name="norm_proj",
    )(x2d, gain.reshape(1, d), w_stack)


def _mem_kv_kernel(mem_ref, g_ref, w_ref, gk_ref, k_ref, v_ref):
    x = mem_ref[0]
    r = lax.rsqrt(jnp.mean(x * x, axis=-1, keepdims=True) + RMS_EPS)
    mn = ((x * r) * g_ref[...]).astype(BF16)
    mkv = jnp.dot(mn, w_ref[...].astype(BF16), preferred_element_type=F32)
    for h in range(N_MEM_HEADS):
        kh = mkv[:, h * HEAD_DIM:(h + 1) * HEAD_DIM]
        k_ref[0, :, h * HEAD_DIM:(h + 1) * HEAD_DIM] = ((kh * _group_rms_scale(kh, HEAD_DIM)) * gk_ref[...]).astype(BF16)
    v_ref[0] = mkv[:, MEM_WIDTH:].astype(BF16)


def _mem_kv(mem, gain, w_stack, layer, gk):
    b = mem.shape[0]
    shp = jax.ShapeDtypeStruct((b, MEM_LEN, MEM_WIDTH), BF16)
    return pl.pallas_call(
        _mem_kv_kernel,
        grid=(b,),
        in_specs=[
            pl.BlockSpec((1, MEM_LEN, D_MODEL), lambda i: (i, 0, 0)),
            pl.BlockSpec((1, D_MODEL), lambda i: (0, 0)),
            pl.BlockSpec((None, D_MODEL, 2 * MEM_WIDTH), lambda i: (layer, 0, 0)),
            pl.BlockSpec((1, HEAD_DIM), lambda i: (0, 0)),
        ],
        out_specs=[pl.BlockSpec((1, MEM_LEN, MEM_WIDTH), lambda i: (i, 0, 0))] * 2,
        out_shape=[shp, shp],
        compiler_params=_cparams(("parallel",)),
        name="mem_kv",
    )(mem, gain.reshape(1, D_MODEL), w_stack, gk.reshape(1, HEAD_DIM))


def _prep_a_kernel(negm_ref, q_ref, k_ref, v_ref, qm_ref, gq_ref, gk_ref, gm_ref, qx_ref, kx_ref, vtx_ref, qmn_ref):
    q_scale = DIFF_QK_DIM ** -0.5 * LOG2E
    m_scale = HEAD_DIM ** -0.5 * LOG2E
    lane = lax.broadcasted_iota(I32, (TS_PREP, HEAD_DIM), 1)
    row = lax.broadcasted_iota(I32, (HEAD_DIM, TS_PREP), 0)
    lo_lanes, lo_rows = lane < DIFF_QK_DIM, row < DIFF_QK_DIM
    neg_shift = negm_ref[0]
    ones_rows = jnp.where(lax.broadcasted_iota(I32, (V_ROWS - HEAD_DIM, TS_PREP), 0) == 0, 1.0, 0.0).astype(BF16)
    q_fill = (jnp.where(row == ONE_COL, neg_shift, 0.0), jnp.where(row == ONE_COL - 1, neg_shift, 0.0))
    k_fill = (jnp.where(lane == ONE_COL, 1.0, 0.0), jnp.where(lane == ONE_COL - 1, 1.0, 0.0))
    q_gain = gq_ref[...] * q_scale
    for h in range(N_HEADS):
        sl = slice(h * HEAD_DIM, (h + 1) * HEAD_DIM)
        qt = q_ref[0, :, sl].T.astype(F32)
        t = qt * qt
        r_lo = lax.rsqrt(jnp.mean(t[:DIFF_QK_DIM], axis=0, keepdims=True) + RMS_EPS)
        r_hi = lax.rsqrt(jnp.mean(t[DIFF_QK_DIM:], axis=0, keepdims=True) + RMS_EPS)
        qn = (qt * jnp.where(lo_rows, r_lo, r_hi)) * q_gain
        qx_ref[0, 2 * h] = jnp.where(lo_rows, qn, q_fill[0]).astype(BF16)
        qx_ref[0, 2 * h + 1] = jnp.where(lo_rows, q_fill[1], qn).astype(BF16)
        k = k_ref[0, :, sl].astype(F32)
        kn = (k * _group_rms_scale(k, DIFF_QK_DIM)) * gk_ref[...]
        kx_ref[0, 2 * h] = jnp.where(lo_lanes, kn, k_fill[0]).astype(BF16)
        kx_ref[0, 2 * h + 1] = jnp.where(lo_lanes, k_fill[1], kn).astype(BF16)
        vt = v_ref[0, :, sl].T
        vtx_ref[0, h, 0] = jnp.concatenate([vt, ones_rows], axis=0)
    for h in range(N_MEM_HEADS):
        sl = slice(h * HEAD_DIM, (h + 1) * HEAD_DIM)
        qm = qm_ref[0, :, sl].astype(F32)
        qmn_ref[0, :, sl] = ((qm * _group_rms_scale(qm, HEAD_DIM)) * (gm_ref[...] * m_scale)).astype(BF16)


def _prep_a(proj, gq, gk, gm, neg_shift):
    assert TS_PREP == TK_A
    b, s, _ = proj.shape
    gq2 = jnp.concatenate([gq, gq]).reshape(HEAD_DIM, 1)
    gk2 = jnp.concatenate([gk, gk]).reshape(1, HEAD_DIM)
    wblk = MIX_WIDTH // MEM_WIDTH
    return pl.pallas_call(
        _prep_a_kernel,
        grid=(b, s // TS_PREP),
        in_specs=[
            _smem(),
            pl.BlockSpec((1, TS_PREP, MIX_WIDTH), lambda i, j: (i, j, 0)),
            pl.BlockSpec((1, TS_PREP, MIX_WIDTH), lambda i, j: (i, j, 1)),
            pl.BlockSpec((1, TS_PREP, MIX_WIDTH), lambda i, j: (i, j, 2)),
            pl.BlockSpec((1, TS_PREP, MEM_WIDTH), lambda i, j: (i, j, 3 * wblk)),
            pl.BlockSpec((HEAD_DIM, 1), lambda i, j: (0, 0)),
            pl.BlockSpec((1, HEAD_DIM), lambda i, j: (0, 0)),
            pl.BlockSpec((1, HEAD_DIM), lambda i, j: (0, 0)),
        ],
        out_specs=[
            pl.BlockSpec((1, 2 * N_HEADS, HEAD_DIM, TS_PREP), lambda i, j: (i, 0, 0, j)),
            pl.BlockSpec((1, 2 * N_HEADS, TS_PREP, HEAD_DIM), lambda i, j: (i, 0, j, 0)),
            pl.BlockSpec((1, N_HEADS, 1, V_ROWS, TK_A), lambda i, j: (i, 0, j, 0, 0)),
            pl.BlockSpec((1, TS_PREP, MEM_WIDTH), lambda i, j: (i, j, 0)),
        ],
        out_shape=[
            jax.ShapeDtypeStruct((b, 2 * N_HEADS, HEAD_DIM, s), BF16),
            jax.ShapeDtypeStruct((b, 2 * N_HEADS, s, HEAD_DIM), BF16),
            jax.ShapeDtypeStruct((b, N_HEADS, s // TK_A, V_ROWS, TK_A), BF16),
            jax.ShapeDtypeStruct((b, s, MEM_WIDTH), BF16),
        ],
        compiler_params=_cparams(("parallel", "parallel")),
        name="prep_a",
    )(neg_shift.reshape(1), proj, proj, proj, proj, gq2, gk2, gm.reshape(1, HEAD_DIM))


def _diff_attn_kernel(qlo_ref, qhi_ref, klo_ref, khi_ref, qrun_ref, krun_ref,
                      far_ref, lq1_ref, lk1_ref, lq2_ref, lk2_ref,
                      qx_ref, kx_ref, vtx_ref, posq_ref, posk_ref, tbl_ref, diag_ref, gsub_ref,
                      *rest, lam_init, bounded, n_cast):
    w_f32_refs, o_ref, w_bf16_refs = rest[:n_cast], rest[n_cast], rest[n_cast + 1:2 * n_cast + 1]
    acc_ref, m_scratch = rest[2 * n_cast + 1], rest[2 * n_cast + 2:]
    b, iq, j = pl.program_id(0), pl.program_id(1), pl.program_id(2)
    pos_far = klo_ref[b, j] - qhi_ref[b, iq] >= FAR_DIST
    neg_far = khi_ref[b, j] - qlo_ref[b, iq] <= -FAR_DIST
    far = jnp.logical_or(pos_far, neg_far)

    @pl.when(j == 0)
    def _():
        acc_ref[...] = jnp.zeros(acc_ref.shape, F32)
        if not bounded:
            m_scratch[0][...] = jnp.full(m_scratch[0].shape, NEG_BIG, F32)

    def logits(hc):
        return jnp.dot(kx_ref[0, hc], qx_ref[0, hc], preferred_element_type=F32)

    def accumulate(hc, s, far_const, bias):
        h = hc // 2
        if bounded:
            e = jnp.exp2(s)
            if bias is not None:
                e = e * bias
            pv = jnp.dot(vtx_ref[0, h, 0], e.astype(BF16), preferred_element_type=F32)
            acc_ref[hc] += pv if far_const is None else far_const * pv
        else:
            if bias is not None:
                s = s + bias
            m_ref = m_scratch[0]
            off = 0.0 if far_const is None else far_const
            m_old = m_ref[hc]
            m_new = jnp.maximum(m_old, jnp.max(s, axis=0, keepdims=True) + off)
            p = jnp.exp2(s - (m_new - off)).astype(BF16)
            acc_ref[hc] = (jnp.exp2(m_old - m_new) * acc_ref[hc]
                           + jnp.dot(vtx_ref[0, h, 0], p, preferred_element_type=F32))
            m_ref[hc] = m_new

    def all_heads(bias_of_head, far_const_of_head):
        for src, dst in zip(w_f32_refs, w_bf16_refs):
            dst[...] = src[...].astype(BF16)
        s_next = logits(0)
        bias = None
        bias_next = None if bias_of_head is None else bias_of_head(0)
        for hc in range(2 * N_HEADS):
            h, c = divmod(hc, 2)
            s = s_next
            if hc + 1 < 2 * N_HEADS:
                s_next = logits(hc + 1)
            if bias_of_head is not None and c == 0:
                bias = bias_next
                if h + 1 < N_HEADS:
                    bias_next = bias_of_head(h + 1)
            accumulate(hc, s, None if far_const_of_head is None else far_const_of_head(h), bias)

    @pl.when(far)
    def _():
        all_heads(None, lambda h: jnp.where(pos_far, far_ref[1, h], far_ref[0, h]))

    runs = jnp.logical_and(qrun_ref[b, iq] == 1, krun_ref[b, j] == 1)

    @pl.when(jnp.logical_and(jnp.logical_not(far), runs))
    def _():
        d0 = klo_ref[b, j] - qlo_ref[b, iq]

        def bias_of_head(h):
            offs = [jnp.clip(d0 - c + DIAG_CENTER, DIAG_OFF_MIN, DIAG_OFF_MAX) for c in range(0, TQ_A, LANES)]
            return jnp.concatenate([diag_ref[h, pl.ds(off, TK_A), :] for off in offs], axis=1)

        all_heads(bias_of_head, None)

    @pl.when(jnp.logical_and(jnp.logical_not(far), jnp.logical_not(runs)))
    def _():
        bucket = []

        def bias_of_head(h):
            if not bucket:
                bucket.append(_t5_bucket(posk_ref[0] - posq_ref[0]))
            return _table_lookup(tbl_ref[h], bucket[0])

        all_heads(bias_of_head, None)

    @pl.when(j == pl.num_programs(2) - 1)
    def _():
        lam = (jnp.exp(jnp.sum(lq1_ref[...] * lk1_ref[...], axis=-1, keepdims=True))
               - jnp.exp(jnp.sum(lq2_ref[...] * lk2_ref[...], axis=-1, keepdims=True)) + lam_init)
        for h in range(N_HEADS):
            a0, a1 = acc_ref[2 * h], acc_ref[2 * h + 1]
            o_t = (a0[:HEAD_DIM] / a0[HEAD_DIM:HEAD_DIM + 1]
                   - lam * (a1[:HEAD_DIM] / a1[HEAD_DIM:HEAD_DIM + 1]))
            r = lax.rsqrt(jnp.mean(o_t * o_t, axis=0, keepdims=True) + RMS_EPS)
            o_t = ((o_t * r) * gsub_ref[...]) * (1.0 - lam_init)
            o_ref[0, :, h * HEAD_DIM:(h + 1) * HEAD_DIM] = o_t.T.astype(BF16)


def _diff_attention(qx, kx, vtx, positions, tbl, diag2, far_consts, lam_vecs, gsub, lam_init, bounded, cast_srcs):
    assert TK_A <= DIAG_ROWS_MAX
    b, _, s, _ = kx.shape
    nq, nk = s // TQ_A, s // TK_A
    pq = positions.reshape(b, nq, TQ_A)
    pk = positions.reshape(b, nk, TK_A)
    ranges = (pq.min(-1), pq.max(-1), pk.min(-1), pk.max(-1),
              _consecutive_runs(positions, [(t * TQ_A, (t + 1) * TQ_A) for t in range(nq)]),
              _consecutive_runs(positions, [(t * TK_A, (t + 1) * TK_A) for t in range(nk)]))
    lam_spec = pl.BlockSpec((1, DIFF_QK_DIM), lambda i, q, k, *_: (0, 0))
    scratch = [pltpu.VMEM((2 * N_HEADS, V_ROWS, TQ_A), F32)]
    if not bounded:
        scratch.append(pltpu.VMEM((2 * N_HEADS, 1, TQ_A), F32))
    n_steps = b * nq * nk
    assert all(w.shape[0] % (16 * n_steps) == 0 for w in cast_srcs)
    cast_specs = [pl.BlockSpec((w.shape[0] // n_steps, w.shape[1]), lambda i, q, k, *_: ((i * nq + q) * nk + k, 0))
                  for w in cast_srcs]
    grid_spec = pltpu.PrefetchScalarGridSpec(
        num_scalar_prefetch=len(ranges),
        grid=(b, nq, nk),
        in_specs=[
            _smem(), lam_spec, lam_spec, lam_spec, lam_spec,
            pl.BlockSpec((1, 2 * N_HEADS, HEAD_DIM, TQ_A), lambda i, q, k, *_: (i, 0, 0, q)),
            pl.BlockSpec((1, 2 * N_HEADS, TK_A, HEAD_DIM), lambda i, q, k, *_: (i, 0, k, 0)),
            pl.BlockSpec((1, N_HEADS, 1, V_ROWS, TK_A), lambda i, q, k, *_: (i, 0, k, 0, 0)),
            pl.BlockSpec((1, 1, TQ_A), lambda i, q, k, *_: (i, 0, q)),
            pl.BlockSpec((1, TK_A, 1), lambda i, q, k, *_: (i, k, 0)),
            pl.BlockSpec((N_HEADS, 1, LANES), lambda i, q, k, *_: (0, 0, 0)),
            pl.BlockSpec((None, N_HEADS, DIAG_ROWS, LANES), lambda i, q, k, *_: (int(bounded), 0, 0, 0),
                         pipeline_mode=pl.Buffered(1)),
            pl.BlockSpec((HEAD_DIM, 1), lambda i, q, k, *_: (0, 0)),
            *cast_specs,
        ],
        out_specs=[pl.BlockSpec((1, TQ_A, MIX_WIDTH), lambda i, q, k, *_: (i, q, 0)), *cast_specs],
        scratch_shapes=scratch,
    )
    outs = pl.pallas_call(
        functools.partial(_diff_attn_kernel, lam_init=lam_init, bounded=bounded, n_cast=len(cast_srcs)),
        grid_spec=grid_spec,
        out_shape=[jax.ShapeDtypeStruct((b, s, MIX_WIDTH), BF16),
                   *[jax.ShapeDtypeStruct(w.shape, BF16) for w in cast_srcs]],
        compiler_params=_cparams(("arbitrary", "arbitrary", "arbitrary")),
        name="diff_attn_bounded" if bounded else "diff_attn_running_max",
    )(*ranges, far_consts, *[v.reshape(1, DIFF_QK_DIM) for v in lam_vecs],
      qx, kx, vtx, positions.reshape(b, 1, s), positions.reshape(b, s, 1), tbl, diag2,
      gsub.reshape(HEAD_DIM, 1), *cast_srcs)
    return outs[0], tuple(outs[1:])


def _prep_b_kernel(q_ref, k_ref, v_ref, qm_ref, gq_ref, gk_ref, gm_ref, qt_ref, kn_ref, vt_ref, qmn_ref):
    scale = HEAD_DIM ** -0.5 * LOG2E
    n_blk = TS_PREP // WINDOW
    ones_rows = jnp.where(lax.broadcasted_iota(I32, (V_ROWS - HEAD_DIM, TS_PREP), 0) == 0, 1.0, 0.0).astype(BF16)
    for h in range(N_HEADS):
        g, hg = divmod(h, GQA_GROUP)
        qt = q_ref[0, :, h * HEAD_DIM:(h + 1) * HEAD_DIM].T.astype(F32)
        r = lax.rsqrt(jnp.mean(qt * qt, axis=0, keepdims=True) + RMS_EPS)
        qt = ((qt * r) * (gq_ref[...] * scale)).astype(BF16)
        for n in range(n_blk):
            qt_ref[0, n, g, :, hg * WINDOW:(hg + 1) * WINDOW] = qt[:, n * WINDOW:(n + 1) * WINDOW]
    for g in range(N_KV_HEADS):
        sl = slice(g * HEAD_DIM, (g + 1) * HEAD_DIM)
        k = k_ref[0, :, sl].astype(F32)
        kn_ref[0, g] = ((k * _group_rms_scale(k, HEAD_DIM)) * gk_ref[...]).astype(BF16)
        vt = jnp.concatenate([v_ref[0, :, sl].T, ones_rows], axis=0)
        for n in range(n_blk):
            vt_ref[0, g, n] = vt[:, n * WINDOW:(n + 1) * WINDOW]
        qm = qm_ref[0, :, sl].astype(F32)
        qmn_ref[0, :, sl] = ((qm * _group_rms_scale(qm, HEAD_DIM)) * (gm_ref[...] * scale)).astype(BF16)


def _prep_b(proj, gq, gk, gm):
    b, s, _ = proj.shape
    kblk = MIX_WIDTH // KV_WIDTH
    n_blk = TS_PREP // WINDOW
    g = lambda v: v.reshape(1, HEAD_DIM)
    return pl.pallas_call(
        _prep_b_kernel,
        grid=(b, s // TS_PREP),
        in_specs=[
            pl.BlockSpec((1, TS_PREP, MIX_WIDTH), lambda i, j: (i, j, 0)),
            pl.BlockSpec((1, TS_PREP, KV_WIDTH), lambda i, j: (i, j, kblk)),
            pl.BlockSpec((1, TS_PREP, KV_WIDTH), lambda i, j: (i, j, kblk + 1)),
            pl.BlockSpec((1, TS_PREP, MEM_WIDTH), lambda i, j: (i, j, kblk + 2)),
            pl.BlockSpec((HEAD_DIM, 1), lambda i, j: (0, 0)),
            pl.BlockSpec((1, HEAD_DIM), lambda i, j: (0, 0)),
            pl.BlockSpec((1, HEAD_DIM), lambda i, j: (0, 0)),
        ],
        out_specs=[
            pl.BlockSpec((1, n_blk, N_KV_HEADS, HEAD_DIM, GQA_GROUP * WINDOW), lambda i, j: (i, j, 0, 0, 0)),
            pl.BlockSpec((1, N_KV_HEADS, TS_PREP, HEAD_DIM), lambda i, j: (i, 0, j, 0)),
            pl.BlockSpec((1, N_KV_HEADS, n_blk, V_ROWS, WINDOW), lambda i, j: (i, 0, j, 0, 0)),
            pl.BlockSpec((1, TS_PREP, MEM_WIDTH), lambda i, j: (i, j, 0)),
        ],
        out_shape=[
            jax.ShapeDtypeStruct((b, s // WINDOW, N_KV_HEADS, HEAD_DIM, GQA_GROUP * WINDOW), BF16),
            jax.ShapeDtypeStruct((b, N_KV_HEADS, s, HEAD_DIM), BF16),
            jax.ShapeDtypeStruct((b, N_KV_HEADS, s // WINDOW, V_ROWS, WINDOW), BF16),
            jax.ShapeDtypeStruct((b, s, MEM_WIDTH), BF16),
        ],
        compiler_params=_cparams(("parallel", "parallel")),
        name="prep_b",
    )(proj, proj, proj, proj, gq.reshape(HEAD_DIM, 1), g(gk), g(gm))


def _win_attn_kernel(small_ref, sink_ref, run_ref, qt_ref, k_ref, vt_ref, posq_ref, posk_ref, tbl_ref, diag_ref,
                     o_ref):
    n_blocks = posq_ref.shape[1]
    n_sub = o_ref.shape[1] // WINDOW
    kw = 3 * WINDOW
    lane3 = lax.broadcasted_iota(I32, (1, GQA_GROUP * WINDOW), 1)

    def window(nl):
        n = pl.program_id(1) * n_sub + nl
        nb = jnp.clip(n - 1, 0, n_blocks - 3)
        return n, nb, pl.multiple_of(nb * WINDOW, WINDOW)

    def logits(nl, g):
        start = window(nl)[2]
        return jnp.dot(k_ref[0, g, pl.ds(start, kw), :], qt_ref[0, nl, g], preferred_element_type=F32)

    def in_window(nl):
        n, _, start = window(nl)
        ki = start + lax.broadcasted_iota(I32, (kw, WINDOW), 0)
        qi = n * WINDOW + lax.broadcasted_iota(I32, (kw, WINDOW), 1)
        return jnp.abs(ki - qi) <= WINDOW

    def masked_bucket(nl):
        n, _, start = window(nl)
        rel = posk_ref[0, pl.ds(start, kw), :] - posq_ref[0, pl.ds(n, 1), :]
        return jnp.where(in_window(nl), _t5_bucket(rel), N_BUCKETS)

    def bias_from_bucket(bucket, nl, h, bounded):
        return _table_lookup(tbl_ref[int(bounded), h], bucket)

    def bias_from_diag(valid, nl, h, bounded):
        n, _, start = window(nl)
        off = pl.multiple_of(start - n * WINDOW + DIAG_CENTER, 16)
        return jnp.where(valid, diag_ref[int(bounded), h, pl.ds(off, kw), :], 0.0 if bounded else NEG_BIG)

    def finish(nl, g, s, block_state, head_bias, bounded):
        nb = window(nl)[1]
        heads = range(g * GQA_GROUP, (g + 1) * GQA_GROUP)
        bias = jnp.concatenate([head_bias(block_state, nl, h, bounded) for h in heads], axis=1)
        sinks = [sink_ref[int(bounded), h] for h in heads]
        sink = jnp.where(lane3 < WINDOW, sinks[0], jnp.where(lane3 < 2 * WINDOW, sinks[1], sinks[2]))
        if bounded:
            p = (jnp.exp2(s) * bias).astype(BF16)
            sink_term = sink
        else:
            z = s + bias
            m = jnp.maximum(jnp.max(z, axis=0, keepdims=True), sink)
            p = jnp.exp2(z - m).astype(BF16)
            sink_term = jnp.exp2(sink - m)
        vt = jnp.concatenate([vt_ref[0, g, nb + t] for t in range(3)], axis=1)
        acc = jnp.dot(vt, p, preferred_element_type=F32)
        o_t = acc[:HEAD_DIM] / (acc[HEAD_DIM:HEAD_DIM + 1] + sink_term)
        for hg, h in enumerate(heads):
            o_ref[0, nl * WINDOW:(nl + 1) * WINDOW, h * HEAD_DIM:(h + 1) * HEAD_DIM] = (
                o_t[:, hg * WINDOW:(hg + 1) * WINDOW].T.astype(BF16))

    def all_chains(block_state_of, head_bias, bounded):
        chains = [(nl, g) for nl in range(n_sub) for g in range(N_KV_HEADS)]
        s_next = logits(*chains[0])
        block_state = None
        for idx, (nl, g) in enumerate(chains):
            s = s_next
            if idx + 1 < len(chains):
                s_next = logits(*chains[idx + 1])
            if g == 0:
                block_state = block_state_of(nl)
            finish(nl, g, s, block_state, head_bias, bounded)

    is_run = run_ref[pl.program_id(0), pl.program_id(1)] == 1
    is_small = small_ref[0] == 1
    for bounded in (True, False):
        mode = is_small if bounded else jnp.logical_not(is_small)
        pl.when(jnp.logical_and(mode, is_run))(
            functools.partial(all_chains, in_window, bias_from_diag, bounded))
        pl.when(jnp.logical_and(mode, jnp.logical_not(is_run)))(
            functools.partial(all_chains, masked_bucket, bias_from_bucket, bounded))


def _win_attention(qt, kn, vt, positions, tbl2, diag2, sink2, small):
    b, _, s, _ = kn.shape
    n_blocks = s // WINDOW
    n_sub = TQ_B // WINDOW
    runs = _consecutive_runs(positions, [(max(0, t * TQ_B - WINDOW), min(s, (t + 1) * TQ_B + WINDOW))
                                         for t in range(s // TQ_B)])
    return pl.pallas_call(
        _win_attn_kernel,
        grid=(b, s // TQ_B),
        in_specs=[
            _smem(), _smem(), _smem(),
            pl.BlockSpec((1, n_sub, N_KV_HEADS, HEAD_DIM, GQA_GROUP * WINDOW), lambda i, j: (i, j, 0, 0, 0)),
            pl.BlockSpec((1, N_KV_HEADS, s, HEAD_DIM), lambda i, j: (i, 0, 0, 0)),
            pl.BlockSpec((1, N_KV_HEADS, n_blocks, V_ROWS, WINDOW), lambda i, j: (i, 0, 0, 0, 0)),
            pl.BlockSpec((1, n_blocks, WINDOW), lambda i, j: (i, 0, 0)),
            pl.BlockSpec((1, s, 1), lambda i, j: (i, 0, 0)),
            pl.BlockSpec((2, N_HEADS, 1, LANES), lambda i, j: (0, 0, 0, 0)),
            pl.BlockSpec((2, N_HEADS, DIAG_ROWS, LANES), lambda i, j: (0, 0, 0, 0), pipeline_mode=pl.Buffered(1)),
        ],
        out_specs=pl.BlockSpec((1, TQ_B, MIX_WIDTH), lambda i, j: (i, j, 0)),
        out_shape=jax.ShapeDtypeStruct((b, s, MIX_WIDTH), BF16),
        compiler_params=_cparams(("parallel", "parallel")),
        name="win_attn",
    )(small.astype(I32).reshape(1), sink2, runs, qt, kn, vt, positions.reshape(b, n_blocks, WINDOW),
      positions.reshape(b, s, 1), tbl2, diag2)


def _mem_attn_kernel(q_ref, k_ref, v_ref, o_ref):
    for h in range(N_MEM_HEADS):
        sl = slice(h * HEAD_DIM, (h + 1) * HEAD_DIM)
        s = lax.dot_general(q_ref[0, :, sl], k_ref[0, :, sl], (((1,), (1,)), ((), ())), preferred_element_type=F32)
        e = jnp.exp2(s - jnp.max(s, axis=-1, keepdims=True))
        p = (e / jnp.sum(e, axis=-1, keepdims=True)).astype(BF16)
        o_ref[0, :, sl] = jnp.dot(p, v_ref[0, :, sl], preferred_element_type=F32).astype(BF16)


def _mem_attention(qmn, k_m, v_m):
    b, s, _ = qmn.shape
    return pl.pallas_call(
        _mem_attn_kernel,
        grid=(b, s // TQ_MEM),
        in_specs=[
            pl.BlockSpec((1, TQ_MEM, MEM_WIDTH), lambda i, j: (i, j, 0)),
            pl.BlockSpec((1, MEM_LEN, MEM_WIDTH), lambda i, j: (i, 0, 0)),
            pl.BlockSpec((1, MEM_LEN, MEM_WIDTH), lambda i, j: (i, 0, 0)),
        ],
        out_specs=pl.BlockSpec((1, TQ_MEM, MEM_WIDTH), lambda i, j: (i, j, 0)),
        out_shape=jax.ShapeDtypeStruct((b, s, MEM_WIDTH), BF16),
        compiler_params=_cparams(("parallel", "parallel")),
        name="mem_attn",
    )(qmn, k_m, v_m)


def _out_proj_kernel(x_ref, o_ref, om_ref, wo_ref, wm_ref, y_ref):
    y_ref[...] = (x_ref[...]
                  + jnp.dot(o_ref[...], wo_ref[...].astype(BF16), preferred_element_type=F32)
                  + jnp.dot(om_ref[...], wm_ref[...].astype(BF16), preferred_element_type=F32))


def _out_proj(x2d, o2d, om2d, w_stack, layer):
    m, d = x2d.shape
    return pl.pallas_call(
        _out_proj_kernel,
        grid=(m // TM_OUT, d // TN_OUT),
        in_specs=[
            pl.BlockSpec((TM_OUT, TN_OUT), lambda i, j: (i, j)),
            pl.BlockSpec((TM_OUT, MIX_WIDTH), lambda i, j: (i, 0)),
            pl.BlockSpec((TM_OUT, MEM_WIDTH), lambda i, j: (i, 0)),
            pl.BlockSpec((None, MIX_WIDTH, TN_OUT), lambda i, j: (layer, 0, j)),
            pl.BlockSpec((None, MEM_WIDTH, TN_OUT), lambda i, j: (layer, MIX_WIDTH // MEM_WIDTH, j)),
        ],
        out_specs=pl.BlockSpec((TM_OUT, TN_OUT), lambda i, j: (i, j)),
        out_shape=jax.ShapeDtypeStruct((m, d), F32),
        compiler_params=_cparams(("parallel", "parallel")),
        name="out_proj",
    )(x2d, o2d, om2d, w_stack, w_stack)


def _mlp_kernel(x_ref, g_ref, wu_ref, wd_ref, y_ref, h_ref):
    f = pl.program_id(1)

    def mlp_chunk(h):
        u = jnp.maximum(jnp.dot(h, wu_ref[...].astype(BF16), preferred_element_type=F32), 0.0)
        return jnp.dot((u * u).astype(BF16), wd_ref[...].astype(BF16), preferred_element_type=F32)

    @pl.when(f == 0)
    def _():
        chunk = x_ref.shape[0] // FIRST_STEP_CHUNKS
        for rows in (slice(c * chunk, (c + 1) * chunk) for c in range(FIRST_STEP_CHUNKS)):
            x = x_ref[rows, :]
            r = lax.rsqrt(jnp.mean(x * x, axis=-1, keepdims=True) + RMS_EPS)
            h = ((x * r) * g_ref[...]).astype(BF16)
            h_ref[rows, :] = h
            y_ref[rows, :] = x + mlp_chunk(h)

    @pl.when(f > 0)
    def _():
        y_ref[...] += mlp_chunk(h_ref[...])


def _mlp(x2d, gain, wu_stack, wd_stack, layer):
    m, d = x2d.shape
    return pl.pallas_call(
        _mlp_kernel,
        grid=(m // TM_MLP, D_FF // TF_MLP),
        in_specs=[
            pl.BlockSpec((TM_MLP, d), lambda i, f: (i, 0)),
            pl.BlockSpec((1, d), lambda i, f: (0, 0)),
            pl.BlockSpec((None, d, TF_MLP), lambda i, f: (layer, 0, f)),
            pl.BlockSpec((None, TF_MLP, d), lambda i, f: (layer, f, 0)),
        ],
        out_specs=pl.BlockSpec((TM_MLP, d), lambda i, f: (i, 0)),
        out_shape=jax.ShapeDtypeStruct((m, d), F32),
        scratch_shapes=[pltpu.VMEM((TM_MLP, d), BF16)],
        compiler_params=_cparams(("parallel", "arbitrary")),
        name="mlp",
    )(x2d, gain.reshape(1, d), wu_stack, wd_stack)


def _diag_table_kernel(tbl_ref, o_ref):
    r = lax.broadcasted_iota(I32, (DIAG_ROWS, LANES), 0)
    l = lax.broadcasted_iota(I32, (DIAG_ROWS, LANES), 1)
    bias = _table_lookup(tbl_ref[0], _t5_bucket(r - l - DIAG_CENTER))
    o_ref[0, 0] = bias
    o_ref[1, 0] = jnp.exp2(bias)


def _diag_bias_table(tbl_log2):
    return pl.pallas_call(
        _diag_table_kernel,
        grid=(N_HEADS,),
        in_specs=[pl.BlockSpec((1, 1, LANES), lambda h: (h, 0, 0))],
        out_specs=pl.BlockSpec((2, 1, DIAG_ROWS, LANES), lambda h: (0, h, 0, 0)),
        out_shape=jax.ShapeDtypeStruct((2, N_HEADS, DIAG_ROWS, LANES), F32),
        compiler_params=_cparams(("parallel",)),
        name="diag_bias_table",
    )(tbl_log2)


def _consecutive_runs(positions, ranges):
    b = positions.shape[0]
    step_ok = (positions[:, 1:] - positions[:, :-1] == 1).astype(I32)
    c = jnp.concatenate([jnp.zeros((b, 1), I32), jnp.cumsum(step_ok, axis=1)], axis=1)
    return jnp.stack([(c[:, hi - 1] - c[:, lo] == hi - 1 - lo).astype(I32) for lo, hi in ranges], axis=1)


def _bias_tables(rel_bias):
    t = (rel_bias.astype(F32) * LOG2E).T
    tbl = jnp.zeros((N_HEADS, 1, LANES), F32).at[:, 0, :N_BUCKETS].set(t).at[:, 0, N_BUCKETS].set(NEG_BIG)
    far = jnp.stack([t[:, HALF_BUCKETS - 1], t[:, N_BUCKETS - 1]])
    return tbl, far


def _logit_bound(gq, gk, rel_bias, dim):
    qk = 1.01 * dim * jnp.max(jnp.abs(gq)) * jnp.max(jnp.abs(gk)) * (dim ** -0.5 * LOG2E)
    return jnp.ceil(qk + jnp.max(jnp.abs(rel_bias)) * LOG2E).astype(F32)


def kernel(x, mem, positions, rel_bias, norm_attn, norm_mem, norm_mlp, w_in_a, a_q_norm, a_k_norm, a_lambda_q1, a_lambda_k1, a_lambda_q2, a_lambda_k2, a_subln, w_in_b, b_q_norm, b_k_norm, b_sink, w_mem_kv, m_q_norm, m_k_norm, w_out, w_up, w_down):
    b, s, d = x.shape
    depth = norm_attn.shape[0]
    tbl_log2, far_log2 = _bias_tables(rel_bias)
    tbl_mult = jnp.exp2(tbl_log2)
    diag2 = _diag_bias_table(tbl_log2)
    x2d = x.reshape(b * s, d)
    later_w = (w_out, w_up, w_down, w_in_b)
    for i in range(depth):
        j = i // 2
        k_m, v_m = _mem_kv(mem, norm_mem[i], w_mem_kv, i, m_k_norm[i])
        if i % 2 == 0:
            proj = _norm_proj(x2d, norm_attn[i], w_in_a, j).reshape(b, s, -1)
            shift = _logit_bound(a_q_norm[j], a_k_norm[j], rel_bias, DIFF_QK_DIM)
            bounded = shift <= SHIFT_LIMIT
            qx, kx, vtx, qmn = _prep_a(proj, a_q_norm[j], a_k_norm[j], m_q_norm[i], jnp.where(bounded, -shift, 0.0))
            lam_init = 0.8 - 0.6 * math.exp(-0.3 * i)
            cast_srcs = tuple(w.reshape(-1, w.shape[-1]) for w in later_w) if i == 0 else ()
            attn = functools.partial(
                _diff_attention, qx, kx, vtx, positions,
                lam_vecs=(a_lambda_q1[j], a_lambda_k1[j], a_lambda_q2[j], a_lambda_k2[j]), gsub=a_subln[j],
                lam_init=lam_init, cast_srcs=cast_srcs)
            o, casts = lax.cond(
                bounded,
                lambda: attn(tbl_mult, diag2, far_consts=jnp.exp2(far_log2), bounded=True),
                lambda: attn(tbl_log2, diag2, far_consts=far_log2, bounded=False))
            if i == 0:
                later_w = tuple(c.reshape(w.shape) for c, w in zip(casts, later_w))
                w_out, w_up, w_down, w_in_b = later_w
        else:
            proj = _norm_proj(x2d, norm_attn[i], w_in_b, j).reshape(b, s, -1)
            qt, kn, vt, qmn = _prep_b(proj, b_q_norm[j], b_k_norm[j], m_q_norm[i])
            sink_log2 = b_sink[j].astype(F32) * LOG2E
            small = jnp.logical_and(_logit_bound(b_q_norm[j], b_k_norm[j], rel_bias, HEAD_DIM) <= SHIFT_LIMIT,
                                    jnp.max(jnp.abs(sink_log2)) <= SHIFT_LIMIT)
            o = _win_attention(qt, kn, vt, positions, jnp.stack([tbl_log2, tbl_mult]), diag2,
                               jnp.stack([sink_log2, jnp.exp2(sink_log2)]), small)
        o_m = _mem_attention(qmn, k_m, v_m)
        x2d = _out_proj(x2d, o.reshape(b * s, MIX_WIDTH), o_m.reshape(b * s, MEM_WIDTH), w_out, i)
        x2d = _mlp(x2d, norm_mlp[i], w_up, w_down, i)
    return x2d.reshape(b, s, d)
```
